```python
import jax, jax.numpy as jnp
from jax import lax
import numpy as np


D_MODEL = 1024
BATCH = 8
SEQ = 2048
DEPTH = 2
DEC_BATCH = 128
DEC_SEQ = 4
PAST_LEN = 16384
PAGE_SIZE = 128

N_RET_HEADS = 8
RET_QK_DIM = 64
RET_V_DIM = D_MODEL // N_RET_HEADS
RET_CHUNK = 128
ROPE_BASE = 10000.0
QK_W = N_RET_HEADS * RET_QK_DIM
V_W = N_RET_HEADS * RET_V_DIM
POOL_WINDOWS = (2, 4, 8, 16)
N_POOL_GROUPS = 4
POOL_IN = D_MODEL // 2
POOL_GROUP_IN = POOL_IN // N_POOL_GROUPS
POOL_GROUP_OUT = D_MODEL // N_POOL_GROUPS
POOL_BUF = 15
D_IN = 2 * QK_W + 2 * V_W + POOL_IN + 2 * D_MODEL
D_FF = ((8 * D_MODEL // 3 + 127) // 128) * 128
EPS = 1e-6

kernel_name = 'hybrid_retention_pool_macaron_step'


def _rmsnorm(x, g):
    xf = x.astype(jnp.float32)
    y = xf * lax.rsqrt(jnp.mean(xf * xf, axis=-1, keepdims=True) + EPS)
    return (y * g.astype(jnp.float32)).astype(x.dtype)


def _swiglu(h, w_in, w_out):
    a = h @ w_in
    gate, up = jnp.split(a, 2, axis=-1)
    return (jax.nn.silu(gate) * up) @ w_out


def _rotate(x, pos):
    half = x.shape[-1] // 2
    inv = jnp.power(jnp.float32(ROPE_BASE), -jnp.linspace(0.0, 1.0, half, dtype=jnp.float32))
    ang = pos[:, None] * inv[None, :]
    cos = jnp.cos(ang)[None, :, None, :]
    sin = jnp.sin(ang)[None, :, None, :]
    x1, x2 = x[..., :half], x[..., half:]
    return jnp.concatenate([x1 * cos - x2 * sin, x1 * sin + x2 * cos], axis=-1)


def _log_gamma():
    return jnp.log1p(-jnp.exp2(-5.0 - jnp.arange(N_RET_HEADS, dtype=jnp.float32)))


def _retention(q, k, v, s0, chunk):
    B, T, H, _ = q.shape
    n = T // chunk
    lg = _log_gamma()
    i = jnp.arange(chunk, dtype=jnp.float32)
    dist = i[:, None] - i[None, :]
    dmask = jnp.where(dist[None] >= 0, jnp.exp(lg[:, None, None] * jnp.maximum(dist, 0.0)[None]), 0.0)
    xi = jnp.exp(lg[:, None] * (i[None, :] + 1.0))
    wk = jnp.exp(lg[:, None] * (chunk - 1.0 - i[None, :]))
    g_c = jnp.exp(lg * chunk)

    def to_chunks(a):
        return a.reshape(B, n, chunk, H, a.shape[-1]).transpose(1, 0, 3, 2, 4)

    def step(s, qkv):
        qc, kc, vc = qkv
        sc = jnp.einsum('bhik,bhjk->bhij', qc, kc) * dmask[None]
        o = (jnp.einsum('bhij,bhjv->bhiv', sc, vc)
             + jnp.einsum('bhik,bhkv->bhiv', qc, s) * xi[None, :, :, None])
        s = s * g_c[None, :, None, None] + jnp.einsum('bhjk,bhjv->bhkv', kc * wk[None, :, :, None], vc)
        return s, o

    s, o = lax.scan(step, s0, (to_chunks(q), to_chunks(k), to_chunks(v)))
    o = o.transpose(1, 0, 3, 2, 4).reshape(B, T, H, v.shape[-1])
    return o, s


def _pool_branch(u, buf, pos0, w_pool, scale):
    B, T, _ = u.shape
    ext = jnp.concatenate([buf.astype(u.dtype), u], axis=1)
    cs = jnp.cumsum(ext.astype(jnp.float32), axis=1)
    cs = jnp.pad(cs, ((0, 0), (1, 0), (0, 0)))
    end = cs[:, POOL_BUF + 1:]
    pos = pos0 + jnp.arange(T, dtype=jnp.int32)
    means = []
    for g, w in enumerate(POOL_WINDOWS):
        sl = slice(g * POOL_GROUP_IN, (g + 1) * POOL_GROUP_IN)
        start = cs[:, POOL_BUF + 1 - w: POOL_BUF + 1 - w + T, sl]
        cnt = jnp.minimum(pos + 1, w).astype(jnp.float32)[None, :, None]
        means.append((end[..., sl] - start) / cnt)
    mean = jnp.concatenate(means, axis=-1)
    d = (mean - u.astype(jnp.float32)).astype(u.dtype).reshape(B, T, N_POOL_GROUPS, POOL_GROUP_IN)
    y = jnp.einsum('btgc,gcd->btgd', d, w_pool).reshape(B, T, D_MODEL) * scale
    return y, ext[:, -POOL_BUF:]


def _mixer(h, s0, buf0, pos0, chunk, w_in, w_pool, pool_scale, w_out):
    B, T, _ = h.shape
    z = h @ w_in
    idx = [QK_W, 2 * QK_W, 2 * QK_W + V_W, 2 * QK_W + 2 * V_W,
           2 * QK_W + 2 * V_W + POOL_IN, 2 * QK_W + 2 * V_W + POOL_IN + D_MODEL]
    q, k, v, g_ret, u, gate_a, gate_b = jnp.split(z, idx, axis=-1)
    pos = (pos0 + jnp.arange(T, dtype=jnp.int32)).astype(jnp.float32)
    q = _rotate(q.astype(jnp.float32).reshape(B, T, N_RET_HEADS, RET_QK_DIM), pos)
    k = _rotate(k.astype(jnp.float32).reshape(B, T, N_RET_HEADS, RET_QK_DIM), pos) * (RET_QK_DIM ** -0.5)
    v = v.astype(jnp.float32).reshape(B, T, N_RET_HEADS, RET_V_DIM)
    o, s_new = _retention(q, k, v, s0.astype(jnp.float32), chunk)
    o = o * lax.rsqrt(jnp.mean(o * o, axis=-1, keepdims=True) + EPS)
    ret = (o.reshape(B, T, V_W) * jax.nn.silu(g_ret.astype(jnp.float32))).astype(h.dtype)
    pool, buf_new = _pool_branch(u, buf0, pos0, w_pool, pool_scale)
    merged = jax.nn.sigmoid(gate_a) * ret + jax.nn.sigmoid(gate_b) * pool
    return merged @ w_out, s_new.astype(h.dtype), buf_new


def _trunk(x, ret0, pool0, pos0, chunk, norm_ffn1, w_ffn1_in, w_ffn1_out, norm_mix, w_in, w_pool,
           pool_scale, w_out, norm_ffn2, w_ffn2_in, w_ffn2_out, norm_final):
    ret_states, pool_states = [], []
    for l in range(DEPTH):
        x = x + 0.5 * _swiglu(_rmsnorm(x, norm_ffn1[l]), w_ffn1_in[l], w_ffn1_out[l])
        m, s_new, b_new = _mixer(_rmsnorm(x, norm_mix[l]), ret0[l], pool0[l], pos0, chunk,
                                 w_in[l], w_pool[l], pool_scale[l], w_out[l])
        x = x + m
        x = x + 0.5 * _swiglu(_rmsnorm(x, norm_ffn2[l]), w_ffn2_in[l], w_ffn2_out[l])
        ret_states.append(s_new)
        pool_states.append(b_new)
    return _rmsnorm(x, norm_final), jnp.stack(ret_states), jnp.stack(pool_states)


def setup_inputs(seed: int = 0) -> dict:
    key = jax.random.key(seed)
    ks = jax.random.split(key, 20)
    f32 = jnp.float32
    nrm = lambda k, shape, s: jax.random.normal(k, shape, f32) * s
    return {
        'x_prompt': nrm(ks[0], (BATCH, SEQ, D_MODEL), 1.0),
        'x_sample': nrm(ks[1], (DEC_BATCH, DEC_SEQ, D_MODEL), 1.0),
        'state_ret': nrm(ks[2], (DEPTH, DEC_BATCH, N_RET_HEADS, RET_QK_DIM, RET_V_DIM), 0.5),
        'state_pool': nrm(ks[3], (DEPTH, DEC_BATCH, POOL_BUF, POOL_IN), 1.0),
        'norm_ffn1': 1.0 + nrm(ks[4], (DEPTH, D_MODEL), 0.02),
        'w_ffn1_in': nrm(ks[5], (DEPTH, D_MODEL, 2 * D_FF), D_MODEL ** -0.5),
        'w_ffn1_out': nrm(ks[6], (DEPTH, D_FF, D_MODEL), D_FF ** -0.5),
        'norm_mix': 1.0 + nrm(ks[7], (DEPTH, D_MODEL), 0.02),
        'w_in': nrm(ks[8], (DEPTH, D_MODEL, D_IN), D_MODEL ** -0.5),
        'w_pool': nrm(ks[9], (DEPTH, N_POOL_GROUPS, POOL_GROUP_IN, POOL_GROUP_OUT), POOL_GROUP_IN ** -0.5),
        'pool_scale': 1.0 + nrm(ks[10], (DEPTH, D_MODEL), 0.02),
        'w_out': nrm(ks[11], (DEPTH, D_MODEL, D_MODEL), D_MODEL ** -0.5),
        'norm_ffn2': 1.0 + nrm(ks[12], (DEPTH, D_MODEL), 0.02),
        'w_ffn2_in': nrm(ks[13], (DEPTH, D_MODEL, 2 * D_FF), D_MODEL ** -0.5),
        'w_ffn2_out': nrm(ks[14], (DEPTH, D_FF, D_MODEL), D_FF ** -0.5),
        'norm_final': 1.0 + nrm(ks[15], (D_MODEL,), 0.02),
    }


def reference(x_prompt, x_sample, state_ret, state_pool, norm_ffn1, w_ffn1_in, w_ffn1_out, norm_mix,
              w_in, w_pool, pool_scale, w_out, norm_ffn2, w_ffn2_in, w_ffn2_out, norm_final):
    T_p = x_prompt.shape[1]
    T_s = x_sample.shape[1]
    B_p = x_prompt.shape[0]
    chunk_p = RET_CHUNK if T_p % RET_CHUNK == 0 else T_p
    ret0_p = jnp.zeros((DEPTH, B_p, N_RET_HEADS, RET_QK_DIM, RET_V_DIM), x_prompt.dtype)
    pool0_p = jnp.zeros((DEPTH, B_p, POOL_BUF, POOL_IN), x_prompt.dtype)
    y_prompt, state_ret_prompt, state_pool_prompt = _trunk(
        x_prompt, ret0_p, pool0_p, 0, chunk_p, norm_ffn1, w_ffn1_in, w_ffn1_out, norm_mix, w_in,
        w_pool, pool_scale, w_out, norm_ffn2, w_ffn2_in, w_ffn2_out, norm_final)
    y_sample, state_ret_sample, state_pool_sample = _trunk(
        x_sample, state_ret, state_pool, PAST_LEN, T_s, norm_ffn1, w_ffn1_in, w_ffn1_out, norm_mix,
        w_in, w_pool, pool_scale, w_out, norm_ffn2, w_ffn2_in, w_ffn2_out, norm_final)
    return (y_prompt, y_sample, state_ret_prompt, state_ret_sample, state_pool_prompt, state_pool_sample)
```

```python
import functools

import jax
import jax.numpy as jnp
from jax import lax
from jax.experimental import pallas as pl
from jax.experimental.pallas import tpu as pltpu

D_MODEL = 1024
N_RET_HEADS = 8
RET_QK_DIM = 64
RET_V_DIM = 128
RET_CHUNK = 128
ROPE_BASE = 10000.0
QK_W = N_RET_HEADS * RET_QK_DIM
V_W = N_RET_HEADS * RET_V_DIM
POOL_WINDOWS = (2, 4, 8, 16)
N_POOL_GROUPS = 4
POOL_IN = 512
POOL_GROUP_IN = 128
POOL_GROUP_OUT = 256
POOL_BUF = 15
D_IN = 2 * QK_W + 2 * V_W + POOL_IN + 2 * D_MODEL
D_FF = 2816
EPS = 1e-6
PAST_LEN = 16384

Q_OFF = 0
K_OFF = QK_W
V_OFF = 2 * QK_W
G_OFF = 2 * QK_W + V_W
U_OFF = 2 * QK_W + 2 * V_W
GA_OFF = U_OFF + POOL_IN
GB_OFF = GA_OFF + D_MODEL

LANES = 128
N_PAIRS = N_RET_HEADS // 2
POOL_PAD = 16
VMEM_LIMIT = 60 * 1024 * 1024

F32 = jnp.float32
BF16 = jnp.bfloat16


def _dot(a, b):
    return jnp.dot(a, b, preferred_element_type=F32)


def _rms(x, g):
    return x * lax.rsqrt(jnp.mean(x * x, axis=-1, keepdims=True) + EPS) * g


def _const_spec(shape):
    nd = len(shape)
    return pl.BlockSpec(shape, lambda *_: (0,) * nd)


FF_CHUNKS = ((0, 1024), (1024, 1024), (2048, 768))


def _ffn_body(*refs, final):
    if final:
        x_ref, g_ref, win_ref, wout_ref, gf_ref, o_ref = refs
    else:
        x_ref, g_ref, win_ref, wout_ref, o_ref = refs
    x = x_ref[...]
    h = _rms(x, g_ref[...]).astype(BF16)
    acc = None
    for off, width in FF_CHUNKS:
        gate = _dot(h, win_ref[:, off:off + width])
        up = _dot(h, win_ref[:, D_FF + off:D_FF + off + width])
        a = (gate * jax.nn.sigmoid(gate) * up).astype(BF16)
        part = _dot(a, wout_ref[off:off + width, :])
        acc = part if acc is None else acc + part
    y = x + 0.5 * acc
    if final:
        y = _rms(y, gf_ref[...])
    o_ref[...] = y


def _layer_spec(shape, layer):
    nd = len(shape) - 1
    return pl.BlockSpec((None,) + tuple(shape[1:]), lambda *_: (layer,) + (0,) * nd)


def _ffn(x2d, g, w_in, w_out, layer, g_final=None, *, tm):
    n = x2d.shape[0]
    assert n % tm == 0
    final = g_final is not None
    row_spec = pl.BlockSpec((tm, D_MODEL), lambda i: (i, 0))
    in_specs = [row_spec, _layer_spec(g.shape, layer), _layer_spec(w_in.shape, layer),
                _layer_spec(w_out.shape, layer)]
    args = [x2d, g, w_in, w_out]
    if final:
        in_specs.append(_const_spec((1, D_MODEL)))
        args.append(g_final.reshape(1, D_MODEL))
    return pl.pallas_call(
        functools.partial(_ffn_body, final=final),
        grid=(n // tm,),
        in_specs=in_specs,
        out_specs=row_spec,
        out_shape=jax.ShapeDtypeStruct((n, D_MODEL), F32),
        compiler_params=pltpu.CompilerParams(
            dimension_semantics=("arbitrary",), vmem_limit_bytes=VMEM_LIMIT),
        name="ffn_final" if final else "ffn",
    )(*args)


IN_CHUNKS = tuple((o, 1024) for o in range(0, 5120, 1024)) + ((5120, 512),)


def _project_in(x, g_ref, win_ref, z_ref):
    h = _rms(x, g_ref[...]).astype(BF16)
    for off, width in IN_CHUNKS:
        z_ref[:, off:off + width] = _dot(h, win_ref[:, off:off + width])


def _rotate_inplace(z_ref, cos, sin_signed, first_half):
    for blk in range(2 * QK_W // LANES):
        cols = slice(blk * LANES, (blk + 1) * LANES)
        xb = z_ref[:, cols]
        partner = jnp.where(first_half, pltpu.roll(xb, LANES - 32, 1), pltpu.roll(xb, 32, 1))
        r = xb * cos + partner * sin_signed
        if blk >= QK_W // LANES:
            r = r * (RET_QK_DIM ** -0.5)
        z_ref[:, cols] = r


def _merge_out(x, z_ref, ret, pool, wout_ref):
    ga = z_ref[:, GA_OFF:GA_OFF + D_MODEL]
    gb = z_ref[:, GB_OFF:GB_OFF + D_MODEL]
    merged = jax.nn.sigmoid(ga) * ret + jax.nn.sigmoid(gb) * pool
    return x + _dot(merged.astype(BF16), wout_ref[...])


def _swish_gate_norm(o, gate):
    o = o * lax.rsqrt(jnp.mean(o * o, axis=-1, keepdims=True) + EPS)
    return o * (gate * jax.nn.sigmoid(gate))


def _mixer_prompt_body(*refs, tm, chunk, n_alias):
    (x_ref, g_ref, win_ref, wpool_ref, pscale_ref, wout_ref, cos_ref, sin_ref,
     dmask_ref, xi_ref, wk_ref, gs_ref) = refs[:12]
    (o_ref, sret_ref, spool_ref, z_ref, s_ref, ext_ref, ret_ref, pool_ref) = refs[12 + n_alias:]
    t = pl.program_id(1)
    nt = pl.num_programs(1)

    @pl.when(t == 0)
    def _():
        s_ref[...] = jnp.zeros_like(s_ref)
        ext_ref[0:POOL_PAD, :] = jnp.zeros((POOL_PAD, POOL_IN), F32)

    x = x_ref[...]
    _project_in(x, g_ref, win_ref, z_ref)

    lane = lax.broadcasted_iota(jnp.int32, (1, LANES), 1)
    first_half = (lane % RET_QK_DIM) < (RET_QK_DIM // 2)
    _rotate_inplace(z_ref, cos_ref[...], sin_ref[...], first_half)

    head0 = (lane < RET_QK_DIM).astype(F32)
    head1 = 1.0 - head0

    for pair in range(N_PAIRS):
        qcols = slice(Q_OFF + pair * LANES, Q_OFF + (pair + 1) * LANES)
        kcols = slice(K_OFF + pair * LANES, K_OFF + (pair + 1) * LANES)
        for c in range(tm // chunk):
            rows = slice(c * chunk, (c + 1) * chunk)
            qb = z_ref[rows, qcols]
            kb = z_ref[rows, kcols]
            vb = z_ref[rows, V_OFF + pair * 2 * RET_V_DIM:V_OFF + (pair + 1) * 2 * RET_V_DIM]
            vb16 = vb.astype(BF16)
            s_pair = s_ref[pair]
            s16 = s_pair.astype(BF16)
            k_stack = jnp.concatenate([kb * head0, kb * head1], axis=0).astype(BF16)
            sc = lax.dot_general(qb.astype(BF16), k_stack, (((1,), (1,)), ((), ())),
                                 preferred_element_type=F32)
            qx = qb * xi_ref[pair]
            for j, hmask in enumerate((head0, head1)):
                h = 2 * pair + j
                scj = sc[:, j * chunk:(j + 1) * chunk] * dmask_ref[h]
                lhs = jnp.concatenate([scj, qx * hmask], axis=1).astype(BF16)
                rhs = jnp.concatenate([vb16[:, j * RET_V_DIM:(j + 1) * RET_V_DIM], s16], axis=0)
                o = _dot(lhs, rhs)
                gate = z_ref[rows, G_OFF + h * RET_V_DIM:G_OFF + (h + 1) * RET_V_DIM]
                ret_ref[rows, h * RET_V_DIM:(h + 1) * RET_V_DIM] = _swish_gate_norm(o, gate)
            kw = (kb * wk_ref[pair]).astype(BF16)
            upd = lax.dot_general(kw, vb16, (((0,), (0,)), ((), ())),
                                  preferred_element_type=F32)
            new_s = jnp.concatenate([upd[0:RET_QK_DIM, 0:RET_V_DIM],
                                     upd[RET_QK_DIM:, RET_V_DIM:]], axis=0)
            s_ref[pair] = s_pair * gs_ref[pair] + new_s

    ext_ref[POOL_PAD:POOL_PAD + tm, :] = z_ref[:, U_OFF:U_OFF + POOL_IN]
    pos = t * tm + lax.broadcasted_iota(jnp.int32, (tm, 1), 0)
    for g, w in enumerate(POOL_WINDOWS):
        cols = slice(g * POOL_GROUP_IN, (g + 1) * POOL_GROUP_IN)
        u = ext_ref[POOL_PAD:POOL_PAD + tm, cols]
        tot = u
        for j in range(1, w):
            tot = tot + ext_ref[POOL_PAD - j:POOL_PAD - j + tm, cols]
        cnt = jnp.minimum(pos + 1, w).astype(F32)
        d = (tot / cnt - u).astype(BF16)
        y = _dot(d, wpool_ref[g])
        ocols = slice(g * POOL_GROUP_OUT, (g + 1) * POOL_GROUP_OUT)
        pool_ref[:, ocols] = y * pscale_ref[:, ocols]
    tail = ext_ref[tm:tm + POOL_PAD, :]
    ext_ref[0:POOL_PAD, :] = tail

    o_ref[...] = _merge_out(x, z_ref, ret_ref[...], pool_ref[...], wout_ref)

    @pl.when(t == nt - 1)
    def _():
        sret_ref[...] = s_ref[...].reshape(N_PAIRS * LANES, RET_V_DIM)
        spool_ref[...] = tail[POOL_PAD - POOL_BUF:, :]


def _state_alias(prev, n_in, first_out):
    if prev is None:
        return [], [], {}
    specs = [pl.BlockSpec(memory_space=pl.ANY) for _ in prev]
    aliases = {n_in + k: first_out + k for k in range(len(prev))}
    return list(prev), specs, aliases


def _mixer_prompt(x, g, w_in, w_pool, pool_scale, w_out, tabs, layer, prev_states, *, tm):
    b, t_len, _ = x.shape
    depth = w_in.shape[0]
    chunk = RET_CHUNK
    assert t_len % tm == 0 and tm % chunk == 0
    cos, sin, dmask, xi, wk, gs = tabs
    row_spec = pl.BlockSpec((None, tm, D_MODEL), lambda i, j: (i, j, 0))
    tab_spec = pl.BlockSpec((tm, LANES), lambda i, j: (j, 0))
    in_specs = [
        row_spec, _layer_spec(g.shape, layer), _layer_spec(w_in.shape, layer),
        _layer_spec(w_pool.shape, layer), _layer_spec(pool_scale.shape, layer),
        _layer_spec(w_out.shape, layer), tab_spec, tab_spec,
        _const_spec(dmask.shape), _const_spec(xi.shape), _const_spec(wk.shape), _const_spec(gs.shape),
    ]
    args = [x, g, w_in, w_pool, pool_scale, w_out, cos, sin, dmask, xi, wk, gs]
    extra, extra_specs, aliases = _state_alias(prev_states, len(args), 1)
    out_specs = [
        row_spec,
        pl.BlockSpec((None, None, N_PAIRS * LANES, RET_V_DIM), lambda i, j: (layer, i, 0, 0)),
        pl.BlockSpec((None, None, POOL_BUF, POOL_IN), lambda i, j: (layer, i, 0, 0)),
    ]
    out_shape = [
        jax.ShapeDtypeStruct((b, t_len, D_MODEL), F32),
        jax.ShapeDtypeStruct((depth, b, N_PAIRS * LANES, RET_V_DIM), F32),
        jax.ShapeDtypeStruct((depth, b, POOL_BUF, POOL_IN), F32),
    ]
    scratch = [
        pltpu.VMEM((tm, D_IN), F32),
        pltpu.VMEM((N_PAIRS, LANES, RET_V_DIM), F32),
        pltpu.VMEM((POOL_PAD + tm, POOL_IN), F32),
        pltpu.VMEM((tm, V_W), F32),
        pltpu.VMEM((tm, D_MODEL), F32),
    ]
    y, sret, spool = pl.pallas_call(
        functools.partial(_mixer_prompt_body, tm=tm, chunk=chunk, n_alias=len(extra)),
        grid=(b, t_len // tm),
        in_specs=in_specs + extra_specs,
        out_specs=out_specs,
        out_shape=out_shape,
        scratch_shapes=scratch,
        input_output_aliases=aliases,
        compiler_params=pltpu.CompilerParams(
            dimension_semantics=("arbitrary", "arbitrary"), vmem_limit_bytes=VMEM_LIMIT),
        name="mixer_prompt",
    )(*args, *extra)
    return y, (sret, spool)


def _log_gamma():
    return jnp.log1p(-jnp.exp2(-5.0 - jnp.arange(N_RET_HEADS, dtype=F32)))


def _rope_tables(pos):
    half = RET_QK_DIM // 2
    inv = jnp.power(jnp.float32(ROPE_BASE), -jnp.linspace(0.0, 1.0, half, dtype=F32))
    ang = pos[:, None] * inv[None, :]
    cos, sin = jnp.cos(ang), jnp.sin(ang)
    cos128 = jnp.tile(cos, (1, LANES // half))
    sin128 = jnp.tile(jnp.concatenate([-sin, sin], axis=1), (1, LANES // RET_QK_DIM))
    return cos128, sin128


def _pair_lanes(per_head):
    h, c = per_head.shape
    x = jnp.broadcast_to(per_head[:, :, None], (h, c, RET_QK_DIM))
    return x.reshape(N_PAIRS, 2, c, RET_QK_DIM).transpose(0, 2, 1, 3).reshape(N_PAIRS, c, LANES)


def _state_decay(chunk):
    g_c = jnp.exp(_log_gamma() * chunk)
    gs = jnp.broadcast_to(g_c[:, None, None], (N_RET_HEADS, RET_QK_DIM, RET_V_DIM))
    return gs.reshape(N_PAIRS, LANES, RET_V_DIM)


def _decay_tables(chunk):
    lg = _log_gamma()
    i = jnp.arange(chunk, dtype=F32)
    dist = i[:, None] - i[None, :]
    dmask = jnp.where(dist[None] >= 0, jnp.exp(lg[:, None, None] * jnp.maximum(dist, 0.0)[None]), 0.0)
    xi = jnp.exp(lg[:, None] * (i[None, :] + 1.0))
    wk = jnp.exp(lg[:, None] * (chunk - 1.0 - i[None, :]))
    return dmask, _pair_lanes(xi), _pair_lanes(wk), _state_decay(chunk)


def _mixer_sample_body(*refs, bb, ts, past, n_alias):
    (x_ref, g_ref, win_ref, wpool_ref, pscale_ref, wout_ref, cos_ref, sin_ref,
     xi_ref, wk_ref, lag_ref, gs_ref, ind_ref, s0_ref, pb_ref) = refs[:15]
    (o_ref, snew_ref, pnew_ref, z_ref, inter_ref, ublk_ref, pblk_ref) = refs[15 + n_alias:]
    rows = bb * ts
    x = x_ref[...]
    _project_in(x, g_ref, win_ref, z_ref)
    lane = lax.broadcasted_iota(jnp.int32, (1, LANES), 1)
    first_half = (lane % RET_QK_DIM) < (RET_QK_DIM // 2)
    _rotate_inplace(z_ref, cos_ref[...], sin_ref[...], first_half)

    q = z_ref[:, Q_OFF:Q_OFF + QK_W]
    k = z_ref[:, K_OFF:K_OFF + QK_W]
    v = z_ref[:, V_OFF:V_OFF + V_W]
    step = lax.broadcasted_iota(jnp.int32, (rows, 1), 0) % ts
    intra = None
    for d in range(ts):
        kd = k if d == 0 else pltpu.roll(k, d, 0)
        vd = v if d == 0 else pltpu.roll(v, d, 0)
        prod = jnp.where(step >= d, q * kd, 0.0).astype(BF16)
        score = _dot(prod, ind_ref[...])
        term = score * lag_ref[d:d + 1, :] * jnp.where(step >= d, vd, 0.0)
        intra = term if intra is None else intra + term

    row8 = lax.broadcasted_iota(jnp.int32, (2 * ts, 1), 0)
    top = (lax.broadcasted_iota(jnp.int32, (LANES, 1), 0) < RET_QK_DIM).astype(F32)
    bot = 1.0 - top

    def seq_pair(p, carry):
        r0 = pl.multiple_of(p * 2 * ts, 2 * ts)
        rsl = pl.ds(r0, 2 * ts)
        for pair in range(N_PAIRS):
            lanes = slice(pair * LANES, (pair + 1) * LANES)
            q8 = (z_ref[rsl, Q_OFF + pair * LANES:Q_OFF + (pair + 1) * LANES] * xi_ref[:, lanes]).astype(BF16)
            kw8 = z_ref[rsl, K_OFF + pair * LANES:K_OFF + (pair + 1) * LANES] * wk_ref[:, lanes]
            v8 = z_ref[rsl, V_OFF + pair * 2 * RET_V_DIM:V_OFF + (pair + 1) * 2 * RET_V_DIM].astype(BF16)
            outs = []
            for j in range(2):
                b = 2 * p + j
                s_pair = s0_ref[b, lanes, :]
                bd = jnp.concatenate([s_pair * top, s_pair * bot], axis=1).astype(BF16)
                outs.append(_dot(q8, bd))
                mine = (row8 >= j * ts) & (row8 < (j + 1) * ts)
                kwj = jnp.where(mine, kw8, 0.0).astype(BF16)
                upd = lax.dot_general(kwj, v8, (((0,), (0,)), ((), ())), preferred_element_type=F32)
                new_s = jnp.concatenate([upd[0:RET_QK_DIM, 0:RET_V_DIM],
                                         upd[RET_QK_DIM:, RET_V_DIM:]], axis=0)
                snew_ref[b, lanes, :] = s_pair * gs_ref[pair] + new_s
            inter_ref[rsl, pair * 2 * RET_V_DIM:(pair + 1) * 2 * RET_V_DIM] = jnp.where(
                row8 < ts, outs[0], outs[1])
        return carry

    lax.fori_loop(0, bb // 2, seq_pair, 0)

    keep = POOL_BUF - ts
    pnew_ref[:, 0:keep * POOL_IN] = pb_ref[:, ts * POOL_IN:]
    for g, w in enumerate(POOL_WINDOWS):
        ublk_ref[g] = z_ref[:, U_OFF + g * POOL_GROUP_IN:U_OFF + (g + 1) * POOL_GROUP_IN]
        u_steps = [ublk_ref[g, pl.ds(t, bb, stride=ts), :] for t in range(ts)]
        for t in range(ts):
            c0 = (keep + t) * POOL_IN + g * POOL_GROUP_IN
            pnew_ref[:, c0:c0 + POOL_GROUP_IN] = u_steps[t]
        ds_ = []
        for t in range(ts):
            tot = None
            for j in range(w):
                idx = POOL_BUF + t - j
                if idx >= POOL_BUF:
                    term = u_steps[idx - POOL_BUF]
                else:
                    term = pb_ref[:, idx * POOL_IN + g * POOL_GROUP_IN:idx * POOL_IN + (g + 1) * POOL_GROUP_IN]
                tot = term if tot is None else tot + term
            cnt = float(min(past + t + 1, w))
            ds_.append(tot / cnt - u_steps[t])
        y = _dot(jnp.concatenate(ds_, axis=0).astype(BF16), wpool_ref[g])
        y = y * pscale_ref[:, g * POOL_GROUP_OUT:(g + 1) * POOL_GROUP_OUT]
        for half in range(POOL_GROUP_OUT // LANES):
            blk = g * (POOL_GROUP_OUT // LANES) + half
            for t in range(ts):
                pblk_ref[blk, pl.ds(t, bb, stride=ts), :] = y[t * bb:(t + 1) * bb, half * LANES:(half + 1) * LANES]
    pool = jnp.concatenate([pblk_ref[blk] for blk in range(D_MODEL // LANES)], axis=1)

    o_all = intra + inter_ref[...]
    rets = []
    for h in range(N_RET_HEADS):
        hc = slice(h * RET_V_DIM, (h + 1) * RET_V_DIM)
        rets.append(_swish_gate_norm(o_all[:, hc], z_ref[:, G_OFF + h * RET_V_DIM:G_OFF + (h + 1) * RET_V_DIM]))
    ret = jnp.concatenate(rets, axis=1)
    o_ref[...] = _merge_out(x, z_ref, ret, pool, wout_ref)


def _sample_tables(ts, past, rows):
    lg = _log_gamma()
    i = jnp.arange(ts, dtype=F32)
    lag = jnp.exp(lg[:, None] * i[None, :])
    lag = jnp.repeat(lag.T, RET_V_DIM, axis=1)
    xi = jnp.exp(lg[:, None] * (i[None, :] + 1.0))
    wk = jnp.exp(lg[:, None] * (ts - 1.0 - i[None, :]))
    expand = lambda a: jnp.tile(jnp.repeat(a.T, RET_QK_DIM, axis=1), (2, 1))
    pos = (past + jnp.arange(ts, dtype=jnp.int32)).astype(F32)
    cos, sin = _rope_tables(pos)
    reps = rows // ts
    head_of_k = jnp.arange(QK_W) // RET_QK_DIM
    head_of_v = jnp.arange(V_W) // RET_V_DIM
    ind = (head_of_k[:, None] == head_of_v[None, :]).astype(BF16)
    return (jnp.tile(cos, (reps, 1)), jnp.tile(sin, (reps, 1)), expand(xi), expand(wk), lag,
            _state_decay(ts), ind)


def _mixer_sample(x2d, s0, pbuf, g, w_in, w_pool, pool_scale, w_out, tabs, layer, prev_states, *, bb, ts, past):
    n = x2d.shape[0]
    depth = w_in.shape[0]
    nb = n // ts
    rows = bb * ts
    assert nb % bb == 0 and bb % 2 == 0 and (2 * ts) % 8 == 0 and ts <= POOL_BUF
    cos, sin, xi, wk, lag, gs, ind = tabs
    row_spec = pl.BlockSpec((rows, D_MODEL), lambda i: (i, 0))
    state_spec = pl.BlockSpec((None, bb, N_PAIRS * LANES, RET_V_DIM), lambda i: (layer, i, 0, 0))
    pool_spec = pl.BlockSpec((None, bb, POOL_BUF * POOL_IN), lambda i: (layer, i, 0))
    in_specs = [
        row_spec, _layer_spec(g.shape, layer), _layer_spec(w_in.shape, layer),
        _layer_spec(w_pool.shape, layer), _layer_spec(pool_scale.shape, layer),
        _layer_spec(w_out.shape, layer), _const_spec(cos.shape), _const_spec(sin.shape),
        _const_spec(xi.shape), _const_spec(wk.shape), _const_spec(lag.shape), _const_spec(gs.shape),
        _const_spec(ind.shape), state_spec, pool_spec,
    ]
    args = [x2d, g, w_in, w_pool, pool_scale, w_out, cos, sin, xi, wk, lag, gs, ind, s0, pbuf]
    extra, extra_specs, aliases = _state_alias(prev_states, len(args), 1)
    out_specs = [row_spec, state_spec, pool_spec]
    out_shape = [
        jax.ShapeDtypeStruct((n, D_MODEL), F32),
        jax.ShapeDtypeStruct((depth, nb, N_PAIRS * LANES, RET_V_DIM), F32),
        jax.ShapeDtypeStruct((depth, nb, POOL_BUF * POOL_IN), F32),
    ]
    scratch = [
        pltpu.VMEM((rows, D_IN), F32),
        pltpu.VMEM((rows, V_W), F32),
        pltpu.VMEM((N_POOL_GROUPS, rows, LANES), F32),
        pltpu.VMEM((D_MODEL // LANES, rows, LANES), F32),
    ]
    y, snew, pnew = pl.pallas_call(
        functools.partial(_mixer_sample_body, bb=bb, ts=ts, past=past, n_alias=len(extra)),
        grid=(nb // bb,),
        in_specs=in_specs + extra_specs,
        out_specs=out_specs,
        out_shape=out_shape,
        scratch_shapes=scratch,
        input_output_aliases=aliases,
        compiler_params=pltpu.CompilerParams(
            dimension_semantics=("arbitrary",), vmem_limit_bytes=VMEM_LIMIT),
        name="mixer_sample",
    )(*args, *extra)
    return y, (snew, pnew)


def kernel(x_prompt, x_sample, state_ret, state_pool, norm_ffn1, w_ffn1_in, w_ffn1_out, norm_mix,
           w_in, w_pool, pool_scale, w_out, norm_ffn2, w_ffn2_in, w_ffn2_out, norm_final):
    depth = w_in.shape[0]
    bp, tp, _ = x_prompt.shape
    bs, ts, _ = x_sample.shape
    to16 = lambda w: w.astype(BF16)
    w1i, w1o, wi, wp, wo, w2i, w2o = map(to16, (w_ffn1_in, w_ffn1_out, w_in, w_pool, w_out,
                                                w_ffn2_in, w_ffn2_out))

    row = lambda a: a.reshape(depth, 1, D_MODEL)
    n1, nm, n2, ps = map(row, (norm_ffn1, norm_mix, norm_ffn2, pool_scale))
    gf = norm_final.reshape(1, D_MODEL)
    last = depth - 1

    tm = 512
    pos_p = jnp.arange(tp, dtype=jnp.int32).astype(F32)
    tabs_p = _rope_tables(pos_p) + _decay_tables(RET_CHUNK)

    x = x_prompt
    states_p = None
    for l in range(depth):
        x = _ffn(x.reshape(bp * tp, D_MODEL), n1, w1i, w1o, l, tm=tm).reshape(bp, tp, D_MODEL)
        x, states_p = _mixer_prompt(x, nm, wi, wp, ps, wo, tabs_p, l, states_p, tm=tm)
        x = _ffn(x.reshape(bp * tp, D_MODEL), n2, w2i, w2o, l, gf if l == last else None,
                 tm=tm).reshape(bp, tp, D_MODEL)
    y_prompt = x
    state_ret_prompt = states_p[0].reshape(depth, bp, N_RET_HEADS, RET_QK_DIM, RET_V_DIM)
    state_pool_prompt = states_p[1]

    bb = 16
    tabs_s = _sample_tables(ts, PAST_LEN, bb * ts)
    s0_all = state_ret.reshape(depth, bs, N_PAIRS * LANES, RET_V_DIM)
    pb_all = state_pool.reshape(depth, bs, POOL_BUF * POOL_IN)
    x = x_sample.reshape(bs * ts, D_MODEL)
    states_s = None
    for l in range(depth):
        x = _ffn(x, n1, w1i, w1o, l, tm=tm)
        x, states_s = _mixer_sample(x, s0_all, pb_all, nm, wi, wp, ps, wo, tabs_s, l, states_s,
                                    bb=bb, ts=ts, past=PAST_LEN)
        x = _ffn(x, n2, w2i, w2o, l, gf if l == last else None, tm=tm)
    y_sample = x.reshape(bs, ts, D_MODEL)
    state_ret_sample = states_s[0].reshape(depth, bs, N_RET_HEADS, RET_QK_DIM, RET_V_DIM)
    state_pool_sample = states_s[1].reshape(depth, bs, POOL_BUF, POOL_IN)
    return (y_prompt, y_sample, state_ret_prompt, state_ret_sample, state_pool_prompt, state_pool_sample)
```

```python
import functools

import jax
import jax.numpy as jnp
from jax import lax
from jax.experimental import pallas as pl
from jax.experimental.pallas import tpu as pltpu

D_MODEL = 1024
N_RET_HEADS = 8
RET_QK_DIM = 64
RET_V_DIM = 128
RET_CHUNK = 128
ROPE_BASE = 10000.0
QK_W = N_RET_HEADS * RET_QK_DIM
V_W = N_RET_HEADS * RET_V_DIM
POOL_WINDOWS = (2, 4, 8, 16)
N_POOL_GROUPS = 4
POOL_IN = 512
POOL_GROUP_IN = 128
POOL_GROUP_OUT = 256
POOL_BUF = 15
D_IN = 2 * QK_W + 2 * V_W + POOL_IN + 2 * D_MODEL
D_FF = 2816
EPS = 1e-6
PAST_LEN = 16384

Q_OFF = 0
K_OFF = QK_W
V_OFF = 2 * QK_W
G_OFF = 2 * QK_W + V_W
U_OFF = 2 * QK_W + 2 * V_W
GA_OFF = U_OFF + POOL_IN
GB_OFF = GA_OFF + D_MODEL

LANES = 128
N_PAIRS = N_RET_HEADS // 2
POOL_PAD = 16
VMEM_LIMIT = 60 * 1024 * 1024

F32 = jnp.float32
BF16 = jnp.bfloat16


def _dot(a, b):
    return jnp.dot(a, b, preferred_element_type=F32)


def _rms(x, g):
    return x * lax.rsqrt(jnp.mean(x * x, axis=-1, keepdims=True) + EPS) * g


def _const_spec(shape):
    nd = len(shape)
    return pl.BlockSpec(shape, lambda *_: (0,) * nd)


FF_CHUNKS = ((0, 1024), (1024, 1024), (2048, 768))


def _ffn_body(*refs, final):
    if final:
        x_ref, g_ref, win_ref, wout_ref, gf_ref, o_ref = refs
    else:
        x_ref, g_ref, win_ref, wout_ref, o_ref = refs
    x = x_ref[...]
    h = _rms(x, g_ref[...]).astype(BF16)
    acc = None
    for off, width in FF_CHUNKS:
        gate = _dot(h, win_ref[:, off:off + width])
        up = _dot(h, win_ref[:, D_FF + off:D_FF + off + width])
        a = (gate * jax.nn.sigmoid(gate) * up).astype(BF16)
        part = _dot(a, wout_ref[off:off + width, :])
        acc = part if acc is None else acc + part
    y = x + 0.5 * acc
    if final:
        y = _rms(y, gf_ref[...])
    o_ref[...] = y


def _layer_spec(shape, layer):
    nd = len(shape) - 1
    return pl.BlockSpec((None,) + tuple(shape[1:]), lambda *_: (layer,) + (0,) * nd)


def _ffn(x2d, g, w_in, w_out, layer, g_final=None, *, tm):
    n = x2d.shape[0]
    assert n % tm == 0
    final = g_final is not None
    row_spec = pl.BlockSpec((tm, D_MODEL), lambda i: (i, 0))
    in_specs = [row_spec, _layer_spec(g.shape, layer), _layer_spec(w_in.shape, layer),
                _layer_spec(w_out.shape, layer)]
    args = [x2d, g, w_in, w_out]
    if final:
        in_specs.append(_const_spec((1, D_MODEL)))
        args.append(g_final.reshape(1, D_MODEL))
    return pl.pallas_call(
        functools.partial(_ffn_body, final=final),
        grid=(n // tm,),
        in_specs=in_specs,
        out_specs=row_spec,
        out_shape=jax.ShapeDtypeStruct((n, D_MODEL), F32),
        compiler_params=pltpu.CompilerParams(
            dimension_semantics=("arbitrary",), vmem_limit_bytes=VMEM_LIMIT),
        name="ffn_final" if final else "ffn",
    )(*args)


IN_CHUNKS = tuple((o, 1024) for o in range(0, 5120, 1024)) + ((5120, 512),)


def _project_in(x, g_ref, win_ref, z_ref):
    h = _rms(x, g_ref[...]).astype(BF16)
    for off, width in IN_CHUNKS:
        z_ref[:, off:off + width] = _dot(h, win_ref[:, off:off + width])


def _rotate_inplace(z_ref, cos, sin_signed, first_half):
    for blk in range(2 * QK_W // LANES):
        cols = slice(blk * LANES, (blk + 1) * LANES)
        xb = z_ref[:, cols]
        partner = jnp.where(first_half, pltpu.roll(xb, LANES - 32, 1), pltpu.roll(xb, 32, 1))
        r = xb * cos + partner * sin_signed
        if blk >= QK_W // LANES:
            r = r * (RET_QK_DIM ** -0.5)
        z_ref[:, cols] = r


def _merge_out(x, z_ref, ret, pool, wout_ref):
    ga = z_ref[:, GA_OFF:GA_OFF + D_MODEL]
    gb = z_ref[:, GB_OFF:GB_OFF + D_MODEL]
    merged = jax.nn.sigmoid(ga) * ret + jax.nn.sigmoid(gb) * pool
    return x + _dot(merged.astype(BF16), wout_ref[...])


def _swish_gate_norm(o, gate):
    o = o * lax.rsqrt(jnp.mean(o * o, axis=-1, keepdims=True) + EPS)
    return o * (gate * jax.nn.sigmoid(gate))


def _mixer_prompt_body(*refs, tm, chunk, n_alias):
    (x_ref, g_ref, win_ref, wpool_ref, pscale_ref, wout_ref, cos_ref, sin_ref,
     dmask_ref, xi_ref, wk_ref, gs_ref) = refs[:12]
    (o_ref, sret_ref, spool_ref, z_ref, s_ref, ext_ref, ret_ref, pool_ref, h_ref, d_ref) = refs[12 + n_alias:]
    t = pl.program_id(1)
    nt = pl.num_programs(1)

    @pl.when(t == 0)
    def _():
        s_ref[...] = jnp.zeros_like(s_ref)
        ext_ref[0:POOL_PAD, :] = jnp.zeros((POOL_PAD, POOL_IN), F32)

    h_ref[...] = _rms(x_ref[...], g_ref[...]).astype(BF16)

    def project(off, width):
        z_ref[:, off:off + width] = _dot(h_ref[...], win_ref[:, off:off + width])

    project(Q_OFF, 2 * QK_W)
    project(V_OFF, V_W)
    project(U_OFF, POOL_IN)

    lane = lax.broadcasted_iota(jnp.int32, (1, LANES), 1)
    first_half = (lane % RET_QK_DIM) < (RET_QK_DIM // 2)
    _rotate_inplace(z_ref, cos_ref[...], sin_ref[...], first_half)

    ext_ref[POOL_PAD:POOL_PAD + tm, :] = z_ref[:, U_OFF:U_OFF + POOL_IN]
    pos = t * tm + lax.broadcasted_iota(jnp.int32, (tm, 1), 0)
    for g, w in enumerate(POOL_WINDOWS):
        cols = slice(g * POOL_GROUP_IN, (g + 1) * POOL_GROUP_IN)
        u = ext_ref[POOL_PAD:POOL_PAD + tm, cols]
        tot = u
        for j in range(1, w):
            tot = tot + ext_ref[POOL_PAD - j:POOL_PAD - j + tm, cols]
        cnt = jnp.minimum(pos + 1, w).astype(F32)
        d_ref[:, cols] = (tot / cnt - u).astype(BF16)
    tail = ext_ref[tm:tm + POOL_PAD, :]
    ext_ref[0:POOL_PAD, :] = tail

    def pool_matmul(g):
        ocols = slice(g * POOL_GROUP_OUT, (g + 1) * POOL_GROUP_OUT)
        y = _dot(d_ref[:, g * POOL_GROUP_IN:(g + 1) * POOL_GROUP_IN], wpool_ref[g])
        pool_ref[:, ocols] = y * pscale_ref[:, ocols]

    fill_width = 2 * LANES
    fillers = [functools.partial(project, off, fill_width) for off in range(G_OFF, G_OFF + V_W, fill_width)]
    fillers += [functools.partial(project, off, fill_width) for off in range(GA_OFF, D_IN, fill_width)]
    fillers += [functools.partial(pool_matmul, g) for g in range(N_POOL_GROUPS)]
    n_units = N_PAIRS * (tm // chunk)

    head0 = (lane < RET_QK_DIM).astype(F32)
    head1 = 1.0 - head0

    for c in range(tm // chunk):
        rows = slice(c * chunk, (c + 1) * chunk)
        for pair in range(N_PAIRS):
            qcols = slice(Q_OFF + pair * LANES, Q_OFF + (pair + 1) * LANES)
            kcols = slice(K_OFF + pair * LANES, K_OFF + (pair + 1) * LANES)
            qb = z_ref[rows, qcols]
            kb = z_ref[rows, kcols]
            vb = z_ref[rows, V_OFF + pair * 2 * RET_V_DIM:V_OFF + (pair + 1) * 2 * RET_V_DIM]
            vb16 = vb.astype(BF16)
            s_pair = s_ref[pair]
            s16 = s_pair.astype(BF16)
            k_stack = jnp.concatenate([kb * head0, kb * head1], axis=0).astype(BF16)
            sc = lax.dot_general(qb.astype(BF16), k_stack, (((1,), (1,)), ((), ())),
                                 preferred_element_type=F32)
            qx = qb * xi_ref[pair]
            for j, hmask in enumerate((head0, head1)):
                h = 2 * pair + j
                scj = sc[:, j * chunk:(j + 1) * chunk] * dmask_ref[h]
                lhs = jnp.concatenate([scj, qx * hmask], axis=1).astype(BF16)
                rhs = jnp.concatenate([vb16[:, j * RET_V_DIM:(j + 1) * RET_V_DIM], s16], axis=0)
                o = _dot(lhs, rhs)
                o = o * lax.rsqrt(jnp.mean(o * o, axis=-1, keepdims=True) + EPS)
                ret_ref[rows, h * RET_V_DIM:(h + 1) * RET_V_DIM] = o
            kw = (kb * wk_ref[pair]).astype(BF16)
            upd = lax.dot_general(kw, vb16, (((0,), (0,)), ((), ())),
                                  preferred_element_type=F32)
            new_s = jnp.concatenate([upd[0:RET_QK_DIM, 0:RET_V_DIM],
                                     upd[RET_QK_DIM:, RET_V_DIM:]], axis=0)
            s_ref[pair] = s_pair * gs_ref[pair] + new_s
            unit = c * N_PAIRS + pair
            for f in fillers[unit * len(fillers) // n_units:(unit + 1) * len(fillers) // n_units]:
                f()

    n_halves = 2
    for r in range(n_halves):
        rows = slice(r * tm // n_halves, (r + 1) * tm // n_halves)
        gate = z_ref[rows, G_OFF:G_OFF + V_W]
        ret = ret_ref[rows, :] * (gate * jax.nn.sigmoid(gate))
        merged = (jax.nn.sigmoid(z_ref[rows, GA_OFF:GA_OFF + D_MODEL]) * ret
                  + jax.nn.sigmoid(z_ref[rows, GB_OFF:GB_OFF + D_MODEL]) * pool_ref[rows, :])
        o_ref[rows, :] = x_ref[rows, :] + _dot(merged.astype(BF16), wout_ref[...])

    @pl.when(t == nt - 1)
    def _():
        sret_ref[...] = s_ref[...].reshape(N_PAIRS * LANES, RET_V_DIM)
        spool_ref[...] = tail[POOL_PAD - POOL_BUF:, :]


def _state_alias(prev, n_in, first_out):
    if prev is None:
        return [], [], {}
    specs = [pl.BlockSpec(memory_space=pl.ANY) for _ in prev]
    aliases = {n_in + k: first_out + k for k in range(len(prev))}
    return list(prev), specs, aliases


def _mixer_prompt(x, g, w_in, w_pool, pool_scale, w_out, tabs, layer, prev_states, *, tm):
    b, t_len, _ = x.shape
    depth = w_in.shape[0]
    chunk = RET_CHUNK
    assert t_len % tm == 0 and tm % chunk == 0
    cos, sin, dmask, xi, wk, gs = tabs
    row_spec = pl.BlockSpec((None, tm, D_MODEL), lambda i, j: (i, j, 0))
    tab_spec = pl.BlockSpec((tm, LANES), lambda i, j: (j, 0))
    in_specs = [
        row_spec, _layer_spec(g.shape, layer), _layer_spec(w_in.shape, layer),
        _layer_spec(w_pool.shape, layer), _layer_spec(pool_scale.shape, layer),
        _layer_spec(w_out.shape, layer), tab_spec, tab_spec,
        _const_spec(dmask.shape), _const_spec(xi.shape), _const_spec(wk.shape), _const_spec(gs.shape),
    ]
    args = [x, g, w_in, w_pool, pool_scale, w_out, cos, sin, dmask, xi, wk, gs]
    extra, extra_specs, aliases = _state_alias(prev_states, len(args), 1)
    out_specs = [
        row_spec,
        pl.BlockSpec((None, None, N_PAIRS * LANES, RET_V_DIM), lambda i, j: (layer, i, 0, 0)),
        pl.BlockSpec((None, None, POOL_BUF, POOL_IN), lambda i, j: (layer, i, 0, 0)),
    ]
    out_shape = [
        jax.ShapeDtypeStruct((b, t_len, D_MODEL), F32),
        jax.ShapeDtypeStruct((depth, b, N_PAIRS * LANES, RET_V_DIM), F32),
        jax.ShapeDtypeStruct((depth, b, POOL_BUF, POOL_IN), F32),
    ]
    scratch = [
        pltpu.VMEM((tm, D_IN), F32),
        pltpu.VMEM((N_PAIRS, LANES, RET_V_DIM), F32),
        pltpu.VMEM((POOL_PAD + tm, POOL_IN), F32),
        pltpu.VMEM((tm, V_W), F32),
        pltpu.VMEM((tm, D_MODEL), F32),
        pltpu.VMEM((tm, D_MODEL), BF16),
        pltpu.VMEM((tm, POOL_IN), BF16),
    ]
    y, sret, spool = pl.pallas_call(
        functools.partial(_mixer_prompt_body, tm=tm, chunk=chunk, n_alias=len(extra)),
        grid=(b, t_len // tm),
        in_specs=in_specs + extra_specs,
        out_specs=out_specs,
        out_shape=out_shape,
        scratch_shapes=scratch,
        input_output_aliases=aliases,
        compiler_params=pltpu.CompilerParams(
            dimension_semantics=("arbitrary", "arbitrary"), vmem_limit_bytes=VMEM_LIMIT),
        name="mixer_prompt",
    )(*args, *extra)
    return y, (sret, spool)


def _log_gamma():
    return jnp.log1p(-jnp.exp2(-5.0 - jnp.arange(N_RET_HEADS, dtype=F32)))


def _rope_tables(pos):
    half = RET_QK_DIM // 2
    inv = jnp.power(jnp.float32(ROPE_BASE), -jnp.linspace(0.0, 1.0, half, dtype=F32))
    ang = pos[:, None] * inv[None, :]
    cos, sin = jnp.cos(ang), jnp.sin(ang)
    cos128 = jnp.tile(cos, (1, LANES // half))
    sin128 = jnp.tile(jnp.concatenate([-sin, sin], axis=1), (1, LANES // RET_QK_DIM))
    return cos128, sin128


def _pair_lanes(per_head):
    h, c = per_head.shape
    x = jnp.broadcast_to(per_head[:, :, None], (h, c, RET_QK_DIM))
    return x.reshape(N_PAIRS, 2, c, RET_QK_DIM).transpose(0, 2, 1, 3).reshape(N_PAIRS, c, LANES)


def _state_decay(chunk):
    g_c = jnp.exp(_log_gamma() * chunk)
    gs = jnp.broadcast_to(g_c[:, None, None], (N_RET_HEADS, RET_QK_DIM, RET_V_DIM))
    return gs.reshape(N_PAIRS, LANES, RET_V_DIM)


def _decay_tables(chunk):
    lg = _log_gamma()
    i = jnp.arange(chunk, dtype=F32)
    dist = i[:, None] - i[None, :]
    dmask = jnp.where(dist[None] >= 0, jnp.exp(lg[:, None, None] * jnp.maximum(dist, 0.0)[None]), 0.0)
    xi = jnp.exp(lg[:, None] * (i[None, :] + 1.0))
    wk = jnp.exp(lg[:, None] * (chunk - 1.0 - i[None, :]))
    return dmask, _pair_lanes(xi), _pair_lanes(wk), _state_decay(chunk)


def _mixer_sample_body(*refs, bb, ts, past, n_alias):
    (x_ref, g_ref, win_ref, wpool_ref, pscale_ref, wout_ref, cos_ref, sin_ref,
     xi_ref, wk_ref, lag_ref, gs_ref, ind_ref, s0_ref, pb_ref) = refs[:15]
    (o_ref, snew_ref, pnew_ref, z_ref, inter_ref, ublk_ref, pblk_ref) = refs[15 + n_alias:]
    rows = bb * ts
    x = x_ref[...]
    _project_in(x, g_ref, win_ref, z_ref)
    lane = lax.broadcasted_iota(jnp.int32, (1, LANES), 1)
    first_half = (lane % RET_QK_DIM) < (RET_QK_DIM // 2)
    _rotate_inplace(z_ref, cos_ref[...], sin_ref[...], first_half)

    q = z_ref[:, Q_OFF:Q_OFF + QK_W]
    k = z_ref[:, K_OFF:K_OFF + QK_W]
    v = z_ref[:, V_OFF:V_OFF + V_W]
    step = lax.broadcasted_iota(jnp.int32, (rows, 1), 0) % ts
    intra = None
    for d in range(ts):
        kd = k if d == 0 else pltpu.roll(k, d, 0)
        vd = v if d == 0 else pltpu.roll(v, d, 0)
        prod = jnp.where(step >= d, q * kd, 0.0).astype(BF16)
        score = _dot(prod, ind_ref[...])
        term = score * lag_ref[d:d + 1, :] * jnp.where(step >= d, vd, 0.0)
        intra = term if intra is None else intra + term

    row8 = lax.broadcasted_iota(jnp.int32, (2 * ts, 1), 0)
    top = (lax.broadcasted_iota(jnp.int32, (LANES, 1), 0) < RET_QK_DIM).astype(F32)
    bot = 1.0 - top

    def seq_pair(p, carry):
        r0 = pl.multiple_of(p * 2 * ts, 2 * ts)
        rsl = pl.ds(r0, 2 * ts)
        for pair in range(N_PAIRS):
            lanes = slice(pair * LANES, (pair + 1) * LANES)
            q8 = (z_ref[rsl, Q_OFF + pair * LANES:Q_OFF + (pair + 1) * LANES] * xi_ref[:, lanes]).astype(BF16)
            kw8 = z_ref[rsl, K_OFF + pair * LANES:K_OFF + (pair + 1) * LANES] * wk_ref[:, lanes]
            v8 = z_ref[rsl, V_OFF + pair * 2 * RET_V_DIM:V_OFF + (pair + 1) * 2 * RET_V_DIM].astype(BF16)
            outs = []
            for j in range(2):
                b = 2 * p + j
                s_pair = s0_ref[b, lanes, :]
                bd = jnp.concatenate([s_pair * top, s_pair * bot], axis=1).astype(BF16)
                outs.append(_dot(q8, bd))
                mine = (row8 >= j * ts) & (row8 < (j + 1) * ts)
                kwj = jnp.where(mine, kw8, 0.0).astype(BF16)
                upd = lax.dot_general(kwj, v8, (((0,), (0,)), ((), ())), preferred_element_type=F32)
                new_s = jnp.concatenate([upd[0:RET_QK_DIM, 0:RET_V_DIM],
                                         upd[RET_QK_DIM:, RET_V_DIM:]], axis=0)
                snew_ref[b, lanes, :] = s_pair * gs_ref[pair] + new_s
            inter_ref[rsl, pair * 2 * RET_V_DIM:(pair + 1) * 2 * RET_V_DIM] = jnp.where(
                row8 < ts, outs[0], outs[1])
        return carry

    lax.fori_loop(0, bb // 2, seq_pair, 0)

    keep = POOL_BUF - ts
    pnew_ref[:, 0:keep * POOL_IN] = pb_ref[:, ts * POOL_IN:]
    for g, w in enumerate(POOL_WINDOWS):
        ublk_ref[g] = z_ref[:, U_OFF + g * POOL_GROUP_IN:U_OFF + (g + 1) * POOL_GROUP_IN]
        u_steps = [ublk_ref[g, pl.ds(t, bb, stride=ts), :] for t in range(ts)]
        for t in range(ts):
            c0 = (keep + t) * POOL_IN + g * POOL_GROUP_IN
            pnew_ref[:, c0:c0 + POOL_GROUP_IN] = u_steps[t]
        ds_ = []
        for t in range(ts):
            tot = None
            for j in range(w):
                idx = POOL_BUF + t - j
                if idx >= POOL_BUF:
                    term = u_steps[idx - POOL_BUF]
                else:
                    term = pb_ref[:, idx * POOL_IN + g * POOL_GROUP_IN:idx * POOL_IN + (g + 1) * POOL_GROUP_IN]
                tot = term if tot is None else tot + term
            cnt = float(min(past + t + 1, w))
            ds_.append(tot / cnt - u_steps[t])
        y = _dot(jnp.concatenate(ds_, axis=0).astype(BF16), wpool_ref[g])
        y = y * pscale_ref[:, g * POOL_GROUP_OUT:(g + 1) * POOL_GROUP_OUT]
        for half in range(POOL_GROUP_OUT // LANES):
            blk = g * (POOL_GROUP_OUT // LANES) + half
            for t in range(ts):
                pblk_ref[blk, pl.ds(t, bb, stride=ts), :] = y[t * bb:(t + 1) * bb, half * LANES:(half + 1) * LANES]
    pool = jnp.concatenate([pblk_ref[blk] for blk in range(D_MODEL // LANES)], axis=1)

    o_all = intra + inter_ref[...]
    rets = []
    for h in range(N_RET_HEADS):
        hc = slice(h * RET_V_DIM, (h + 1) * RET_V_DIM)
        rets.append(_swish_gate_norm(o_all[:, hc], z_ref[:, G_OFF + h * RET_V_DIM:G_OFF + (h + 1) * RET_V_DIM]))
    ret = jnp.concatenate(rets, axis=1)
    o_ref[...] = _merge_out(x, z_ref, ret, pool, wout_ref)


def _sample_tables(ts, past, rows):
    lg = _log_gamma()
    i = jnp.arange(ts, dtype=F32)
    lag = jnp.exp(lg[:, None] * i[None, :])
    lag = jnp.repeat(lag.T, RET_V_DIM, axis=1)
    xi = jnp.exp(lg[:, None] * (i[None, :] + 1.0))
    wk = jnp.exp(lg[:, None] * (ts - 1.0 - i[None, :]))
    expand = lambda a: jnp.tile(jnp.repeat(a.T, RET_QK_DIM, axis=1), (2, 1))
    pos = (past + jnp.arange(ts, dtype=jnp.int32)).astype(F32)
    cos, sin = _rope_tables(pos)
    reps = rows // ts
    head_of_k = jnp.arange(QK_W) // RET_QK_DIM
    head_of_v = jnp.arange(V_W) // RET_V_DIM
    ind = (head_of_k[:, None] == head_of_v[None, :]).astype(BF16)
    return (jnp.tile(cos, (reps, 1)), jnp.tile(sin, (reps, 1)), expand(xi), expand(wk), lag,
            _state_decay(ts), ind)


def _mixer_sample(x2d, s0, pbuf, g, w_in, w_pool, pool_scale, w_out, tabs, layer, prev_states, *, bb, ts, past):
    n = x2d.shape[0]
    depth = w_in.shape[0]
    nb = n // ts
    rows = bb * ts
    assert nb % bb == 0 and bb % 2 == 0 and (2 * ts) % 8 == 0 and ts <= POOL_BUF
    cos, sin, xi, wk, lag, gs, ind = tabs
    row_spec = pl.BlockSpec((rows, D_MODEL), lambda i: (i, 0))
    state_spec = pl.BlockSpec((None, bb, N_PAIRS * LANES, RET_V_DIM), lambda i: (layer, i, 0, 0))
    pool_spec = pl.BlockSpec((None, bb, POOL_BUF * POOL_IN), lambda i: (layer, i, 0))
    in_specs = [
        row_spec, _layer_spec(g.shape, layer), _layer_spec(w_in.shape, layer),
        _layer_spec(w_pool.shape, layer), _layer_spec(pool_scale.shape, layer),
        _layer_spec(w_out.shape, layer), _const_spec(cos.shape), _const_spec(sin.shape),
        _const_spec(xi.shape), _const_spec(wk.shape), _const_spec(lag.shape), _const_spec(gs.shape),
        _const_spec(ind.shape), state_spec, pool_spec,
    ]
    args = [x2d, g, w_in, w_pool, pool_scale, w_out, cos, sin, xi, wk, lag, gs, ind, s0, pbuf]
    extra, extra_specs, aliases = _state_alias(prev_states, len(args), 1)
    out_specs = [row_spec, state_spec, pool_spec]
    out_shape = [
        jax.ShapeDtypeStruct((n, D_MODEL), F32),
        jax.ShapeDtypeStruct((depth, nb, N_PAIRS * LANES, RET_V_DIM), F32),
        jax.ShapeDtypeStruct((depth, nb, POOL_BUF * POOL_IN), F32),
    ]
    scratch = [
        pltpu.VMEM((rows, D_IN), F32),
        pltpu.VMEM((rows, V_W), F32),
        pltpu.VMEM((N_POOL_GROUPS, rows, LANES), F32),
        pltpu.VMEM((D_MODEL // LANES, rows, LANES), F32),
    ]
    y, snew, pnew = pl.pallas_call(
        functools.partial(_mixer_sample_body, bb=bb, ts=ts, past=past, n_alias=len(extra)),
        grid=(nb // bb,),
        in_specs=in_specs + extra_specs,
        out_specs=out_specs,
        out_shape=out_shape,
        scratch_shapes=scratch,
        input_output_aliases=aliases,
        compiler_params=pltpu.CompilerParams(
            dimension_semantics=("arbitrary",), vmem_limit_bytes=VMEM_LIMIT),
        name="mixer_sample",
    )(*args, *extra)
    return y, (snew, pnew)


def kernel(x_prompt, x_sample, state_ret, state_pool, norm_ffn1, w_ffn1_in, w_ffn1_out, norm_mix,
           w_in, w_pool, pool_scale, w_out, norm_ffn2, w_ffn2_in, w_ffn2_out, norm_final):
    depth = w_in.shape[0]
    bp, tp, _ = x_prompt.shape
    bs, ts, _ = x_sample.shape
    to16 = lambda w: w.astype(BF16)
    w1i, w1o, wi, wp, wo, w2i, w2o = map(to16, (w_ffn1_in, w_ffn1_out, w_in, w_pool, w_out,
                                                w_ffn2_in, w_ffn2_out))

    row = lambda a: a.reshape(depth, 1, D_MODEL)
    n1, nm, n2, ps = map(row, (norm_ffn1, norm_mix, norm_ffn2, pool_scale))
    gf = norm_final.reshape(1, D_MODEL)
    last = depth - 1

    tm = 512
    pos_p = jnp.arange(tp, dtype=jnp.int32).astype(F32)
    tabs_p = _rope_tables(pos_p) + _decay_tables(RET_CHUNK)

    x = x_prompt
    states_p = None
    for l in range(depth):
        x = _ffn(x.reshape(bp * tp, D_MODEL), n1, w1i, w1o, l, tm=tm).reshape(bp, tp, D_MODEL)
        x, states_p = _mixer_prompt(x, nm, wi, wp, ps, wo, tabs_p, l, states_p, tm=tm)
        x = _ffn(x.reshape(bp * tp, D_MODEL), n2, w2i, w2o, l, gf if l == last else None,
                 tm=tm).reshape(bp, tp, D_MODEL)
    y_prompt = x
    state_ret_prompt = states_p[0].reshape(depth, bp, N_RET_HEADS, RET_QK_DIM, RET_V_DIM)
    state_pool_prompt = states_p[1]

    bb = 16
    tabs_s = _sample_tables(ts, PAST_LEN, bb * ts)
    s0_all = state_ret.reshape(depth, bs, N_PAIRS * LANES, RET_V_DIM)
    pb_all = state_pool.reshape(depth, bs, POOL_BUF * POOL_IN)
    x = x_sample.reshape(bs * ts, D_MODEL)
    states_s = None
    for l in range(depth):
        x = _ffn(x, n1, w1i, w1o, l, tm=tm)
        x, states_s = _mixer_sample(x, s0_all, pb_all, nm, wi, wp, ps, wo, tabs_s, l, states_s,
                                    bb=bb, ts=ts, past=PAST_LEN)
        x = _ffn(x, n2, w2i, w2o, l, gf if l == last else None, tm=tm)
    y_sample = x.reshape(bs, ts, D_MODEL)
    state_ret_sample = states_s[0].reshape(depth, bs, N_RET_HEADS, RET_QK_DIM, RET_V_DIM)
    state_pool_sample = states_s[1].reshape(depth, bs, POOL_BUF, POOL_IN)
    return (y_prompt, y_sample, state_ret_prompt, state_ret_sample, state_pool_prompt, state_pool_sample)
```

```python
import functools

import jax
import jax.numpy as jnp
from jax import lax
from jax.experimental import pallas as pl
from jax.experimental.pallas import tpu as pltpu

D_MODEL = 1024
N_RET_HEADS = 8
RET_QK_DIM = 64
RET_V_DIM = 128
RET_CHUNK = 128
ROPE_BASE = 10000.0
QK_W = N_RET_HEADS * RET_QK_DIM
V_W = N_RET_HEADS * RET_V_DIM
POOL_WINDOWS = (2, 4, 8, 16)
N_POOL_GROUPS = 4
POOL_IN = 512
POOL_GROUP_IN = 128
POOL_GROUP_OUT = 256
POOL_BUF = 15
D_IN = 2 * QK_W + 2 * V_W + POOL_IN + 2 * D_MODEL
D_FF = 2816
EPS = 1e-6
PAST_LEN = 16384

Q_OFF = 0
K_OFF = QK_W
V_OFF = 2 * QK_W
G_OFF = 2 * QK_W + V_W
U_OFF = 2 * QK_W + 2 * V_W
GA_OFF = U_OFF + POOL_IN
GB_OFF = GA_OFF + D_MODEL

LANES = 128
N_PAIRS = N_RET_HEADS // 2
POOL_PAD = 16
VMEM_LIMIT = 60 * 1024 * 1024

F32 = jnp.float32
BF16 = jnp.bfloat16


def _dot(a, b):
    return jnp.dot(a, b, preferred_element_type=F32)


def _rms(x, g):
    return x * lax.rsqrt(jnp.mean(x * x, axis=-1, keepdims=True) + EPS) * g


def _const_spec(shape):
    nd = len(shape)
    return pl.BlockSpec(shape, lambda *_: (0,) * nd)


FF_CHUNKS = ((0, 1024), (1024, 1024), (2048, 768))


def _ffn_body(*refs, final):
    if final:
        x_ref, g_ref, win_ref, wout_ref, gf_ref, o_ref = refs
    else:
        x_ref, g_ref, win_ref, wout_ref, o_ref = refs
    x = x_ref[...]
    h = _rms(x, g_ref[...]).astype(BF16)
    acc = None
    for off, width in FF_CHUNKS:
        gate = _dot(h, win_ref[:, off:off + width])
        up = _dot(h, win_ref[:, D_FF + off:D_FF + off + width])
        a = (gate * jax.nn.sigmoid(gate) * up).astype(BF16)
        part = _dot(a, wout_ref[off:off + width, :])
        acc = part if acc is None else acc + part
    y = x + 0.5 * acc
    if final:
        y = _rms(y, gf_ref[...])
    o_ref[...] = y


def _layer_spec(shape, layer):
    nd = len(shape) - 1
    return pl.BlockSpec((None,) + tuple(shape[1:]), lambda *_: (layer,) + (0,) * nd,
                        pipeline_mode=pl.Buffered(1))


def _ffn(x2d, g, w_in, w_out, layer, g_final=None, *, tm):
    n = x2d.shape[0]
    assert n % tm == 0
    final = g_final is not None
    row_spec = pl.BlockSpec((tm, D_MODEL), lambda i: (i, 0))
    in_specs = [row_spec, _layer_spec(g.shape, layer), _layer_spec(w_in.shape, layer),
                _layer_spec(w_out.shape, layer)]
    args = [x2d, g, w_in, w_out]
    if final:
        in_specs.append(_const_spec((1, D_MODEL)))
        args.append(g_final.reshape(1, D_MODEL))
    return pl.pallas_call(
        functools.partial(_ffn_body, final=final),
        grid=(n // tm,),
        in_specs=in_specs,
        out_specs=row_spec,
        out_shape=jax.ShapeDtypeStruct((n, D_MODEL), F32),
        compiler_params=pltpu.CompilerParams(
            dimension_semantics=("arbitrary",), vmem_limit_bytes=VMEM_LIMIT),
        name="ffn_final" if final else "ffn",
    )(*args)


IN_CHUNKS = tuple((o, 1024) for o in range(0, 5120, 1024)) + ((5120, 512),)


def _project_in(x, g_ref, win_ref, z_ref):
    h = _rms(x, g_ref[...]).astype(BF16)
    for off, width in IN_CHUNKS:
        z_ref[:, off:off + width] = _dot(h, win_ref[:, off:off + width])


def _rotate_inplace(z_ref, cos, sin_signed, first_half):
    for blk in range(2 * QK_W // LANES):
        cols = slice(blk * LANES, (blk + 1) * LANES)
        xb = z_ref[:, cols]
        partner = jnp.where(first_half, pltpu.roll(xb, LANES - 32, 1), pltpu.roll(xb, 32, 1))
        r = xb * cos + partner * sin_signed
        if blk >= QK_W // LANES:
            r = r * (RET_QK_DIM ** -0.5)
        z_ref[:, cols] = r


def _merge_out(x, z_ref, ret, pool, wout_ref):
    ga = z_ref[:, GA_OFF:GA_OFF + D_MODEL]
    gb = z_ref[:, GB_OFF:GB_OFF + D_MODEL]
    merged = jax.nn.sigmoid(ga) * ret + jax.nn.sigmoid(gb) * pool
    return x + _dot(merged.astype(BF16), wout_ref[...])


def _swish_gate_norm(o, gate):
    o = o * lax.rsqrt(jnp.mean(o * o, axis=-1, keepdims=True) + EPS)
    return o * (gate * jax.nn.sigmoid(gate))


def _mixer_prompt_body(*refs, tm, chunk, n_alias):
    (x_ref, g_ref, win_ref, wpool_ref, pscale_ref, wout_ref, cos_ref, sin_ref,
     dmask_ref, xi_ref, wk_ref, gs_ref) = refs[:12]
    (o_ref, sret_ref, spool_ref, z_ref, s_ref, ext_ref, ret_ref, pool_ref, h_ref, d_ref) = refs[12 + n_alias:]
    t = pl.program_id(1)
    nt = pl.num_programs(1)

    @pl.when(t == 0)
    def _():
        s_ref[...] = jnp.zeros_like(s_ref)
        ext_ref[0:POOL_PAD, :] = jnp.zeros((POOL_PAD, POOL_IN), F32)

    h_ref[...] = _rms(x_ref[...], g_ref[...]).astype(BF16)

    def project(off, width):
        z_ref[:, off:off + width] = _dot(h_ref[...], win_ref[:, off:off + width])

    project(Q_OFF, 2 * QK_W)
    project(V_OFF, V_W)
    project(U_OFF, POOL_IN)

    lane = lax.broadcasted_iota(jnp.int32, (1, LANES), 1)
    first_half = (lane % RET_QK_DIM) < (RET_QK_DIM // 2)
    _rotate_inplace(z_ref, cos_ref[...], sin_ref[...], first_half)

    ext_ref[POOL_PAD:POOL_PAD + tm, :] = z_ref[:, U_OFF:U_OFF + POOL_IN]
    pos = t * tm + lax.broadcasted_iota(jnp.int32, (tm, 1), 0)
    for g, w in enumerate(POOL_WINDOWS):
        cols = slice(g * POOL_GROUP_IN, (g + 1) * POOL_GROUP_IN)
        u = ext_ref[POOL_PAD:POOL_PAD + tm, cols]
        tot = u
        for j in range(1, w):
            tot = tot + ext_ref[POOL_PAD - j:POOL_PAD - j + tm, cols]
        cnt = jnp.minimum(pos + 1, w).astype(F32)
        d_ref[:, cols] = (tot / cnt - u).astype(BF16)
    tail = ext_ref[tm:tm + POOL_PAD, :]
    ext_ref[0:POOL_PAD, :] = tail

    def pool_matmul(g):
        ocols = slice(g * POOL_GROUP_OUT, (g + 1) * POOL_GROUP_OUT)
        y = _dot(d_ref[:, g * POOL_GROUP_IN:(g + 1) * POOL_GROUP_IN], wpool_ref[g])
        pool_ref[:, ocols] = y * pscale_ref[:, ocols]

    fill_width = 2 * LANES
    fillers = [functools.partial(project, off, fill_width) for off in range(G_OFF, G_OFF + V_W, fill_width)]
    fillers += [functools.partial(project, off, fill_width) for off in range(GA_OFF, D_IN, fill_width)]
    fillers += [functools.partial(pool_matmul, g) for g in range(N_POOL_GROUPS)]
    n_units = N_PAIRS * (tm // chunk)

    head0 = (lane < RET_QK_DIM).astype(F32)
    head1 = 1.0 - head0

    for c in range(tm // chunk):
        rows = slice(c * chunk, (c + 1) * chunk)
        for pair in range(N_PAIRS):
            qcols = slice(Q_OFF + pair * LANES, Q_OFF + (pair + 1) * LANES)
            kcols = slice(K_OFF + pair * LANES, K_OFF + (pair + 1) * LANES)
            qb = z_ref[rows, qcols]
            kb = z_ref[rows, kcols]
            vb = z_ref[rows, V_OFF + pair * 2 * RET_V_DIM:V_OFF + (pair + 1) * 2 * RET_V_DIM]
            vb16 = vb.astype(BF16)
            s_pair = s_ref[pair]
            s16 = s_pair.astype(BF16)
            k_stack = jnp.concatenate([kb * head0, kb * head1], axis=0).astype(BF16)
            sc = lax.dot_general(qb.astype(BF16), k_stack, (((1,), (1,)), ((), ())),
                                 preferred_element_type=F32)
            qx = qb * xi_ref[pair]
            for j, hmask in enumerate((head0, head1)):
                h = 2 * pair + j
                scj = sc[:, j * chunk:(j + 1) * chunk] * dmask_ref[h]
                lhs = jnp.concatenate([scj, qx * hmask], axis=1).astype(BF16)
                rhs = jnp.concatenate([vb16[:, j * RET_V_DIM:(j + 1) * RET_V_DIM], s16], axis=0)
                o = _dot(lhs, rhs)
                o = o * lax.rsqrt(jnp.mean(o * o, axis=-1, keepdims=True) + EPS)
                ret_ref[rows, h * RET_V_DIM:(h + 1) * RET_V_DIM] = o
            kw = (kb * wk_ref[pair]).astype(BF16)
            upd = lax.dot_general(kw, vb16, (((0,), (0,)), ((), ())),
                                  preferred_element_type=F32)
            new_s = jnp.concatenate([upd[0:RET_QK_DIM, 0:RET_V_DIM],
                                     upd[RET_QK_DIM:, RET_V_DIM:]], axis=0)
            s_ref[pair] = s_pair * gs_ref[pair] + new_s
            unit = c * N_PAIRS + pair
            for f in fillers[unit * len(fillers) // n_units:(unit + 1) * len(fillers) // n_units]:
                f()

    n_halves = 2
    for r in range(n_halves):
        rows = slice(r * tm // n_halves, (r + 1) * tm // n_halves)
        gate = z_ref[rows, G_OFF:G_OFF + V_W]
        ret = ret_ref[rows, :] * (gate * jax.nn.sigmoid(gate))
        merged = (jax.nn.sigmoid(z_ref[rows, GA_OFF:GA_OFF + D_MODEL]) * ret
                  + jax.nn.sigmoid(z_ref[rows, GB_OFF:GB_OFF + D_MODEL]) * pool_ref[rows, :])
        o_ref[rows, :] = x_ref[rows, :] + _dot(merged.astype(BF16), wout_ref[...])

    @pl.when(t == nt - 1)
    def _():
        sret_ref[...] = s_ref[...].reshape(N_PAIRS * LANES, RET_V_DIM)
        spool_ref[...] = tail[POOL_PAD - POOL_BUF:, :]


def _state_alias(prev, n_in, first_out):
    if prev is None:
        return [], [], {}
    specs = [pl.BlockSpec(memory_space=pl.ANY) for _ in prev]
    aliases = {n_in + k: first_out + k for k in range(len(prev))}
    return list(prev), specs, aliases


def _mixer_prompt(x, g, w_in, w_pool, pool_scale, w_out, tabs, layer, prev_states, *, tm):
    b, t_len, _ = x.shape
    depth = w_in.shape[0]
    chunk = RET_CHUNK
    assert t_len % tm == 0 and tm % chunk == 0
    cos, sin, dmask, xi, wk, gs = tabs
    row_spec = pl.BlockSpec((None, tm, D_MODEL), lambda i, j: (i, j, 0))
    tab_spec = pl.BlockSpec((tm, LANES), lambda i, j: (j, 0))
    in_specs = [
        row_spec, _layer_spec(g.shape, layer), _layer_spec(w_in.shape, layer),
        _layer_spec(w_pool.shape, layer), _layer_spec(pool_scale.shape, layer),
        _layer_spec(w_out.shape, layer), tab_spec, tab_spec,
        _const_spec(dmask.shape), _const_spec(xi.shape), _const_spec(wk.shape), _const_spec(gs.shape),
    ]
    args = [x, g, w_in, w_pool, pool_scale, w_out, cos, sin, dmask, xi, wk, gs]
    extra, extra_specs, aliases = _state_alias(prev_states, len(args), 1)
    out_specs = [
        row_spec,
        pl.BlockSpec((None, None, N_PAIRS * LANES, RET_V_DIM), lambda i, j: (layer, i, 0, 0)),
        pl.BlockSpec((None, None, POOL_BUF, POOL_IN), lambda i, j: (layer, i, 0, 0)),
    ]
    out_shape = [
        jax.ShapeDtypeStruct((b, t_len, D_MODEL), F32),
        jax.ShapeDtypeStruct((depth, b, N_PAIRS * LANES, RET_V_DIM), F32),
        jax.ShapeDtypeStruct((depth, b, POOL_BUF, POOL_IN), F32),
    ]
    scratch = [
        pltpu.VMEM((tm, D_IN), F32),
        pltpu.VMEM((N_PAIRS, LANES, RET_V_DIM), F32),
        pltpu.VMEM((POOL_PAD + tm, POOL_IN), F32),
        pltpu.VMEM((tm, V_W), F32),
        pltpu.VMEM((tm, D_MODEL), F32),
        pltpu.VMEM((tm, D_MODEL), BF16),
        pltpu.VMEM((tm, POOL_IN), BF16),
    ]
    y, sret, spool = pl.pallas_call(
        functools.partial(_mixer_prompt_body, tm=tm, chunk=chunk, n_alias=len(extra)),
        grid=(b, t_len // tm),
        in_specs=in_specs + extra_specs,
        out_specs=out_specs,
        out_shape=out_shape,
        scratch_shapes=scratch,
        input_output_aliases=aliases,
        compiler_params=pltpu.CompilerParams(
            dimension_semantics=("arbitrary", "arbitrary"), vmem_limit_bytes=VMEM_LIMIT),
        name="mixer_prompt",
    )(*args, *extra)
    return y, (sret, spool)


def _log_gamma():
    return jnp.log1p(-jnp.exp2(-5.0 - jnp.arange(N_RET_HEADS, dtype=F32)))


def _rope_tables(pos):
    half = RET_QK_DIM // 2
    inv = jnp.power(jnp.float32(ROPE_BASE), -jnp.linspace(0.0, 1.0, half, dtype=F32))
    ang = pos[:, None] * inv[None, :]
    cos, sin = jnp.cos(ang), jnp.sin(ang)
    cos128 = jnp.tile(cos, (1, LANES // half))
    sin128 = jnp.tile(jnp.concatenate([-sin, sin], axis=1), (1, LANES // RET_QK_DIM))
    return cos128, sin128


def _pair_lanes(per_head):
    h, c = per_head.shape
    x = jnp.broadcast_to(per_head[:, :, None], (h, c, RET_QK_DIM))
    return x.reshape(N_PAIRS, 2, c, RET_QK_DIM).transpose(0, 2, 1, 3).reshape(N_PAIRS, c, LANES)


def _state_decay(chunk):
    g_c = jnp.exp(_log_gamma() * chunk)
    gs = jnp.broadcast_to(g_c[:, None, None], (N_RET_HEADS, RET_QK_DIM, RET_V_DIM))
    return gs.reshape(N_PAIRS, LANES, RET_V_DIM)


def _decay_tables(chunk):
    lg = _log_gamma()
    i = jnp.arange(chunk, dtype=F32)
    dist = i[:, None] - i[None, :]
    dmask = jnp.where(dist[None] >= 0, jnp.exp(lg[:, None, None] * jnp.maximum(dist, 0.0)[None]), 0.0)
    xi = jnp.exp(lg[:, None] * (i[None, :] + 1.0))
    wk = jnp.exp(lg[:, None] * (chunk - 1.0 - i[None, :]))
    return dmask, _pair_lanes(xi), _pair_lanes(wk), _state_decay(chunk)


def _mixer_sample_body(*refs, bb, ts, past, n_alias):
    (x_ref, g_ref, win_ref, wpool_ref, pscale_ref, wout_ref, cos_ref, sin_ref,
     xi_ref, wk_ref, lag_ref, gs_ref, ind_ref, s0_ref, pb_ref) = refs[:15]
    (o_ref, snew_ref, pnew_ref, z_ref, inter_ref, ublk_ref, pblk_ref) = refs[15 + n_alias:]
    rows = bb * ts
    x = x_ref[...]
    _project_in(x, g_ref, win_ref, z_ref)
    lane = lax.broadcasted_iota(jnp.int32, (1, LANES), 1)
    first_half = (lane % RET_QK_DIM) < (RET_QK_DIM // 2)
    _rotate_inplace(z_ref, cos_ref[...], sin_ref[...], first_half)

    q = z_ref[:, Q_OFF:Q_OFF + QK_W]
    k = z_ref[:, K_OFF:K_OFF + QK_W]
    v = z_ref[:, V_OFF:V_OFF + V_W]
    step = lax.broadcasted_iota(jnp.int32, (rows, 1), 0) % ts
    intra = None
    for d in range(ts):
        kd = k if d == 0 else pltpu.roll(k, d, 0)
        vd = v if d == 0 else pltpu.roll(v, d, 0)
        prod = jnp.where(step >= d, q * kd, 0.0).astype(BF16)
        score = _dot(prod, ind_ref[...])
        term = score * lag_ref[d:d + 1, :] * jnp.where(step >= d, vd, 0.0)
        intra = term if intra is None else intra + term

    row8 = lax.broadcasted_iota(jnp.int32, (2 * ts, 1), 0)
    top = (lax.broadcasted_iota(jnp.int32, (LANES, 1), 0) < RET_QK_DIM).astype(F32)
    bot = 1.0 - top

    def seq_pair(p, carry):
        r0 = pl.multiple_of(p * 2 * ts, 2 * ts)
        rsl = pl.ds(r0, 2 * ts)
        for pair in range(N_PAIRS):
            lanes = slice(pair * LANES, (pair + 1) * LANES)
            q8 = (z_ref[rsl, Q_OFF + pair * LANES:Q_OFF + (pair + 1) * LANES] * xi_ref[:, lanes]).astype(BF16)
            kw8 = z_ref[rsl, K_OFF + pair * LANES:K_OFF + (pair + 1) * LANES] * wk_ref[:, lanes]
            v8 = z_ref[rsl, V_OFF + pair * 2 * RET_V_DIM:V_OFF + (pair + 1) * 2 * RET_V_DIM].astype(BF16)
            outs = []
            for j in range(2):
                b = 2 * p + j
                s_pair = s0_ref[b, lanes, :]
                bd = jnp.concatenate([s_pair * top, s_pair * bot], axis=1).astype(BF16)
                outs.append(_dot(q8, bd))
                mine = (row8 >= j * ts) & (row8 < (j + 1) * ts)
                kwj = jnp.where(mine, kw8, 0.0).astype(BF16)
                upd = lax.dot_general(kwj, v8, (((0,), (0,)), ((), ())), preferred_element_type=F32)
                new_s = jnp.concatenate([upd[0:RET_QK_DIM, 0:RET_V_DIM],
                                         upd[RET_QK_DIM:, RET_V_DIM:]], axis=0)
                snew_ref[b, lanes, :] = s_pair * gs_ref[pair] + new_s
            inter_ref[rsl, pair * 2 * RET_V_DIM:(pair + 1) * 2 * RET_V_DIM] = jnp.where(
                row8 < ts, outs[0], outs[1])
        return carry

    lax.fori_loop(0, bb // 2, seq_pair, 0)

    keep = POOL_BUF - ts
    pnew_ref[:, 0:keep * POOL_IN] = pb_ref[:, ts * POOL_IN:]
    for g, w in enumerate(POOL_WINDOWS):
        ublk_ref[g] = z_ref[:, U_OFF + g * POOL_GROUP_IN:U_OFF + (g + 1) * POOL_GROUP_IN]
        u_steps = [ublk_ref[g, pl.ds(t, bb, stride=ts), :] for t in range(ts)]
        for t in range(ts):
            c0 = (keep + t) * POOL_IN + g * POOL_GROUP_IN
            pnew_ref[:, c0:c0 + POOL_GROUP_IN] = u_steps[t]
        ds_ = []
        for t in range(ts):
            tot = None
            for j in range(w):
                idx = POOL_BUF + t - j
                if idx >= POOL_BUF:
                    term = u_steps[idx - POOL_BUF]
                else:
                    term = pb_ref[:, idx * POOL_IN + g * POOL_GROUP_IN:idx * POOL_IN + (g + 1) * POOL_GROUP_IN]
                tot = term if tot is None else tot + term
            cnt = float(min(past + t + 1, w))
            ds_.append(tot / cnt - u_steps[t])
        y = _dot(jnp.concatenate(ds_, axis=0).astype(BF16), wpool_ref[g])
        y = y * pscale_ref[:, g * POOL_GROUP_OUT:(g + 1) * POOL_GROUP_OUT]
        for half in range(POOL_GROUP_OUT // LANES):
            blk = g * (POOL_GROUP_OUT // LANES) + half
            for t in range(ts):
                pblk_ref[blk, pl.ds(t, bb, stride=ts), :] = y[t * bb:(t + 1) * bb, half * LANES:(half + 1) * LANES]
    pool = jnp.concatenate([pblk_ref[blk] for blk in range(D_MODEL // LANES)], axis=1)

    o_all = intra + inter_ref[...]
    rets = []
    for h in range(N_RET_HEADS):
        hc = slice(h * RET_V_DIM, (h + 1) * RET_V_DIM)
        rets.append(_swish_gate_norm(o_all[:, hc], z_ref[:, G_OFF + h * RET_V_DIM:G_OFF + (h + 1) * RET_V_DIM]))
    ret = jnp.concatenate(rets, axis=1)
    o_ref[...] = _merge_out(x, z_ref, ret, pool, wout_ref)


def _sample_tables(ts, past, rows):
    lg = _log_gamma()
    i = jnp.arange(ts, dtype=F32)
    lag = jnp.exp(lg[:, None] * i[None, :])
    lag = jnp.repeat(lag.T, RET_V_DIM, axis=1)
    xi = jnp.exp(lg[:, None] * (i[None, :] + 1.0))
    wk = jnp.exp(lg[:, None] * (ts - 1.0 - i[None, :]))
    expand = lambda a: jnp.tile(jnp.repeat(a.T, RET_QK_DIM, axis=1), (2, 1))
    pos = (past + jnp.arange(ts, dtype=jnp.int32)).astype(F32)
    cos, sin = _rope_tables(pos)
    reps = rows // ts
    head_of_k = jnp.arange(QK_W) // RET_QK_DIM
    head_of_v = jnp.arange(V_W) // RET_V_DIM
    ind = (head_of_k[:, None] == head_of_v[None, :]).astype(BF16)
    return (jnp.tile(cos, (reps, 1)), jnp.tile(sin, (reps, 1)), expand(xi), expand(wk), lag,
            _state_decay(ts), ind)


def _mixer_sample(x2d, s0, pbuf, g, w_in, w_pool, pool_scale, w_out, tabs, layer, prev_states, *, bb, ts, past):
    n = x2d.shape[0]
    depth = w_in.shape[0]
    nb = n // ts
    rows = bb * ts
    assert nb % bb == 0 and bb % 2 == 0 and (2 * ts) % 8 == 0 and ts <= POOL_BUF
    cos, sin, xi, wk, lag, gs, ind = tabs
    row_spec = pl.BlockSpec((rows, D_MODEL), lambda i: (i, 0))
    state_spec = pl.BlockSpec((None, bb, N_PAIRS * LANES, RET_V_DIM), lambda i: (layer, i, 0, 0))
    pool_spec = pl.BlockSpec((None, bb, POOL_BUF * POOL_IN), lambda i: (layer, i, 0))
    in_specs = [
        row_spec, _layer_spec(g.shape, layer), _layer_spec(w_in.shape, layer),
        _layer_spec(w_pool.shape, layer), _layer_spec(pool_scale.shape, layer),
        _layer_spec(w_out.shape, layer), _const_spec(cos.shape), _const_spec(sin.shape),
        _const_spec(xi.shape), _const_spec(wk.shape), _const_spec(lag.shape), _const_spec(gs.shape),
        _const_spec(ind.shape), state_spec, pool_spec,
    ]
    args = [x2d, g, w_in, w_pool, pool_scale, w_out, cos, sin, xi, wk, lag, gs, ind, s0, pbuf]
    extra, extra_specs, aliases = _state_alias(prev_states, len(args), 1)
    out_specs = [row_spec, state_spec, pool_spec]
    out_shape = [
        jax.ShapeDtypeStruct((n, D_MODEL), F32),
        jax.ShapeDtypeStruct((depth, nb, N_PAIRS * LANES, RET_V_DIM), F32),
        jax.ShapeDtypeStruct((depth, nb, POOL_BUF * POOL_IN), F32),
    ]
    scratch = [
        pltpu.VMEM((rows, D_IN), F32),
        pltpu.VMEM((rows, V_W), F32),
        pltpu.VMEM((N_POOL_GROUPS, rows, LANES), F32),
        pltpu.VMEM((D_MODEL // LANES, rows, LANES), F32),
    ]
    y, snew, pnew = pl.pallas_call(
        functools.partial(_mixer_sample_body, bb=bb, ts=ts, past=past, n_alias=len(extra)),
        grid=(nb // bb,),
        in_specs=in_specs + extra_specs,
        out_specs=out_specs,
        out_shape=out_shape,
        scratch_shapes=scratch,
        input_output_aliases=aliases,
        compiler_params=pltpu.CompilerParams(
            dimension_semantics=("arbitrary",), vmem_limit_bytes=VMEM_LIMIT),
        name="mixer_sample",
    )(*args, *extra)
    return y, (snew, pnew)


def kernel(x_prompt, x_sample, state_ret, state_pool, norm_ffn1, w_ffn1_in, w_ffn1_out, norm_mix,
           w_in, w_pool, pool_scale, w_out, norm_ffn2, w_ffn2_in, w_ffn2_out, norm_final):
    depth = w_in.shape[0]
    bp, tp, _ = x_prompt.shape
    bs, ts, _ = x_sample.shape
    to16 = lambda w: w.astype(BF16)
    w1i, w1o, wi, wp, wo, w2i, w2o = map(to16, (w_ffn1_in, w_ffn1_out, w_in, w_pool, w_out,
                                                w_ffn2_in, w_ffn2_out))

    row = lambda a: a.reshape(depth, 1, D_MODEL)
    n1, nm, n2, ps = map(row, (norm_ffn1, norm_mix, norm_ffn2, pool_scale))
    gf = norm_final.reshape(1, D_MODEL)
    last = depth - 1

    tm = 512
    tm_ffn = 1024
    pos_p = jnp.arange(tp, dtype=jnp.int32).astype(F32)
    tabs_p = _rope_tables(pos_p) + _decay_tables(RET_CHUNK)

    x = x_prompt
    states_p = None
    for l in range(depth):
        x = _ffn(x.reshape(bp * tp, D_MODEL), n1, w1i, w1o, l, tm=tm_ffn).reshape(bp, tp, D_MODEL)
        x, states_p = _mixer_prompt(x, nm, wi, wp, ps, wo, tabs_p, l, states_p, tm=tm)
        x = _ffn(x.reshape(bp * tp, D_MODEL), n2, w2i, w2o, l, gf if l == last else None,
                 tm=tm_ffn).reshape(bp, tp, D_MODEL)
    y_prompt = x
    state_ret_prompt = states_p[0].reshape(depth, bp, N_RET_HEADS, RET_QK_DIM, RET_V_DIM)
    state_pool_prompt = states_p[1]

    bb = 32
    tabs_s = _sample_tables(ts, PAST_LEN, bb * ts)
    s0_all = state_ret.reshape(depth, bs, N_PAIRS * LANES, RET_V_DIM)
    pb_all = state_pool.reshape(depth, bs, POOL_BUF * POOL_IN)
    x = x_sample.reshape(bs * ts, D_MODEL)
    states_s = None
    for l in range(depth):
        x = _ffn(x, n1, w1i, w1o, l, tm=tm)
        x, states_s = _mixer_sample(x, s0_all, pb_all, nm, wi, wp, ps, wo, tabs_s, l, states_s,
                                    bb=bb, ts=ts, past=PAST_LEN)
        x = _ffn(x, n2, w2i, w2o, l, gf if l == last else None, tm=tm)
    y_sample = x.reshape(bs, ts, D_MODEL)
    state_ret_sample = states_s[0].reshape(depth, bs, N_RET_HEADS, RET_QK_DIM, RET_V_DIM)
    state_pool_sample = states_s[1].reshape(depth, bs, POOL_BUF, POOL_IN)
    return (y_prompt, y_sample, state_ret_prompt, state_ret_sample, state_pool_prompt, state_pool_sample)
```

```python
import functools

import jax
import jax.numpy as jnp
from jax import lax
from jax.experimental import pallas as pl
from jax.experimental.pallas import tpu as pltpu

D_MODEL = 1024
N_RET_HEADS = 8
RET_QK_DIM = 64
RET_V_DIM = 128
RET_CHUNK = 128
ROPE_BASE = 10000.0
QK_W = N_RET_HEADS * RET_QK_DIM
V_W = N_RET_HEADS * RET_V_DIM
POOL_WINDOWS = (2, 4, 8, 16)
N_POOL_GROUPS = 4
POOL_IN = 512
POOL_GROUP_IN = 128
POOL_GROUP_OUT = 256
POOL_BUF = 15
D_IN = 2 * QK_W + 2 * V_W + POOL_IN + 2 * D_MODEL
D_FF = 2816
EPS = 1e-6
PAST_LEN = 16384

Q_OFF = 0
K_OFF = QK_W
V_OFF = 2 * QK_W
G_OFF = 2 * QK_W + V_W
U_OFF = 2 * QK_W + 2 * V_W
GA_OFF = U_OFF + POOL_IN
GB_OFF = GA_OFF + D_MODEL

LANES = 128
N_PAIRS = N_RET_HEADS // 2
POOL_PAD = 16
VMEM_LIMIT = 60 * 1024 * 1024

F32 = jnp.float32
BF16 = jnp.bfloat16


def _dot(a, b):
    return jnp.dot(a, b, preferred_element_type=F32)


def _rms(x, g):
    return x * lax.rsqrt(jnp.mean(x * x, axis=-1, keepdims=True) + EPS) * g


def _const_spec(shape):
    nd = len(shape)
    return pl.BlockSpec(shape, lambda *_: (0,) * nd)


FF_CHUNKS = ((0, 1024), (1024, 1024), (2048, 768))


STAGE_BUFS = 3
STAGE_BYTES = 3 * 512 * 1024


def _stage_rows(shape):
    r, c = shape
    fits = [n for n in range(16, r + 1, 16) if r % n == 0 and n * c * 4 <= STAGE_BYTES]
    return max(fits)


def _stage_scratch(shape):
    return [pltpu.VMEM(shape, BF16), pltpu.VMEM((STAGE_BUFS, _stage_rows(shape), shape[1]), F32),
            pltpu.SemaphoreType.DMA((STAGE_BUFS,))]


def _fetch_cast(src, dst, stage, sems):
    nbuf, rows, _ = stage.shape
    n = src.shape[0] // rows

    def copy(k):
        return pltpu.make_async_copy(src.at[pl.ds(k * rows, rows), :], stage.at[k % nbuf], sems.at[k % nbuf])

    for k in range(min(nbuf, n)):
        copy(k).start()
    for k in range(n):
        copy(k).wait()
        dst[k * rows:(k + 1) * rows, :] = stage[k % nbuf].astype(BF16)
        if k + nbuf < n:
            copy(k + nbuf).start()


def _convert_weights(first, last, layer, srcs, outs, scratch, emit_sems):
    dsts = scratch[0::3]
    emits = [pltpu.make_async_copy(d, o, emit_sems.at[k]) for k, (d, o) in enumerate(zip(dsts, outs))]

    @pl.when(first)
    def _():
        for k, src in enumerate(srcs):
            _fetch_cast(src.at[layer], *scratch[3 * k:3 * k + 3])
        for e in emits:
            e.start()

    @pl.when(last)
    def _():
        for e in emits:
            e.wait()

    return dsts


def _ffn_body(*refs, final, layer, convert):
    n_in = 5 if final else 4
    x_ref, g_ref, win_ref, wout_ref = refs[:4]
    gf_ref = refs[4] if final else None
    o_ref = refs[n_in]
    if convert:
        i = pl.program_id(0)
        win_ref, wout_ref = _convert_weights(
            i == 0, i == pl.num_programs(0) - 1, layer, (win_ref, wout_ref),
            refs[n_in + 1:n_in + 3], refs[n_in + 3:n_in + 9], refs[n_in + 9])
    x = x_ref[...]
    h = _rms(x, g_ref[...]).astype(BF16)
    acc = None
    for off, width in FF_CHUNKS:
        gate = _dot(h, win_ref[:, off:off + width])
        up = _dot(h, win_ref[:, D_FF + off:D_FF + off + width])
        a = (gate * jax.nn.sigmoid(gate) * up).astype(BF16)
        part = _dot(a, wout_ref[off:off + width, :])
        acc = part if acc is None else acc + part
    y = x + 0.5 * acc
    if final:
        y = _rms(y, gf_ref[...])
    o_ref[...] = y


def _layer_spec(shape, layer):
    nd = len(shape) - 1
    return pl.BlockSpec((None,) + tuple(shape[1:]), lambda *_: (layer,) + (0,) * nd,
                        pipeline_mode=pl.Buffered(1))


def _resident_spec(shape):
    nd = len(shape)
    return pl.BlockSpec(tuple(shape), lambda *_: (0,) * nd, pipeline_mode=pl.Buffered(1))


ANY_SPEC = pl.BlockSpec(memory_space=pl.ANY)


def _ffn(x2d, g, w_in, w_out, layer, g_final=None, *, tm):
    n = x2d.shape[0]
    assert n % tm == 0
    final = g_final is not None
    convert = w_in.dtype == F32
    row_spec = pl.BlockSpec((tm, D_MODEL), lambda i: (i, 0))
    w_specs = [ANY_SPEC, ANY_SPEC] if convert else [_resident_spec(w_in.shape), _resident_spec(w_out.shape)]
    in_specs = [row_spec, _layer_spec(g.shape, layer)] + w_specs
    args = [x2d, g, w_in, w_out]
    if final:
        in_specs.append(_const_spec((1, D_MODEL)))
        args.append(g_final.reshape(1, D_MODEL))
    out_specs = [row_spec]
    out_shape = [jax.ShapeDtypeStruct((n, D_MODEL), F32)]
    scratch = []
    if convert:
        shapes = [w_in.shape[1:], w_out.shape[1:]]
        out_specs += [ANY_SPEC] * len(shapes)
        out_shape += [jax.ShapeDtypeStruct(s, BF16) for s in shapes]
        for s in shapes:
            scratch += _stage_scratch(s)
        scratch.append(pltpu.SemaphoreType.DMA((len(shapes),)))
    outs = pl.pallas_call(
        functools.partial(_ffn_body, final=final, layer=layer, convert=convert),
        grid=(n // tm,),
        in_specs=in_specs,
        out_specs=out_specs,
        out_shape=out_shape,
        scratch_shapes=scratch,
        compiler_params=pltpu.CompilerParams(
            dimension_semantics=("arbitrary",), vmem_limit_bytes=VMEM_LIMIT),
        name="ffn_final" if final else "ffn",
    )(*args)
    return outs[0], tuple(outs[1:])


IN_CHUNKS = tuple((o, 1024) for o in range(0, 5120, 1024)) + ((5120, 512),)


def _project_in(x, g_ref, win_ref, z_ref):
    h = _rms(x, g_ref[...]).astype(BF16)
    for off, width in IN_CHUNKS:
        z_ref[:, off:off + width] = _dot(h, win_ref[:, off:off + width])


def _rotate_inplace(z_ref, cos, sin_signed, first_half):
    for blk in range(2 * QK_W // LANES):
        cols = slice(blk * LANES, (blk + 1) * LANES)
        xb = z_ref[:, cols]
        partner = jnp.where(first_half, pltpu.roll(xb, LANES - 32, 1), pltpu.roll(xb, 32, 1))
        r = xb * cos + partner * sin_signed
        if blk >= QK_W // LANES:
            r = r * (RET_QK_DIM ** -0.5)
        z_ref[:, cols] = r


def _merge_out(x, z_ref, ret, pool, wout_ref):
    ga = z_ref[:, GA_OFF:GA_OFF + D_MODEL]
    gb = z_ref[:, GB_OFF:GB_OFF + D_MODEL]
    merged = jax.nn.sigmoid(ga) * ret + jax.nn.sigmoid(gb) * pool
    return x + _dot(merged.astype(BF16), wout_ref[...])


def _swish_gate_norm(o, gate):
    o = o * lax.rsqrt(jnp.mean(o * o, axis=-1, keepdims=True) + EPS)
    return o * (gate * jax.nn.sigmoid(gate))


def _mixer_prompt_body(*refs, tm, chunk, n_alias, layer):
    (x_ref, g_ref, win_ref, wpool_ref, pscale_ref, wout_ref, cos_ref, sin_ref,
     dmask_ref, xi_ref, wk_ref, gs_ref) = refs[:12]
    outs = refs[12 + n_alias:]
    o_ref, sret_ref, spool_ref = outs[:3]
    z_ref, s_ref, ext_ref, ret_ref, pool_ref, h_ref, d_ref = outs[6:13]
    t = pl.program_id(1)
    nt = pl.num_programs(1)
    bi = pl.program_id(0)
    win_ref, wpool_ref, wout_ref = _convert_weights(
        (bi == 0) & (t == 0), (bi == pl.num_programs(0) - 1) & (t == nt - 1), layer,
        (win_ref, wpool_ref, wout_ref), outs[3:6], outs[13:22], outs[22])

    @pl.when(t == 0)
    def _():
        s_ref[...] = jnp.zeros_like(s_ref)
        ext_ref[0:POOL_PAD, :] = jnp.zeros((POOL_PAD, POOL_IN), F32)

    h_ref[...] = _rms(x_ref[...], g_ref[...]).astype(BF16)

    def project(off, width):
        z_ref[:, off:off + width] = _dot(h_ref[...], win_ref[:, off:off + width])

    project(Q_OFF, 2 * QK_W)
    project(V_OFF, V_W)
    project(U_OFF, POOL_IN)

    lane = lax.broadcasted_iota(jnp.int32, (1, LANES), 1)
    first_half = (lane % RET_QK_DIM) < (RET_QK_DIM // 2)
    _rotate_inplace(z_ref, cos_ref[...], sin_ref[...], first_half)

    ext_ref[POOL_PAD:POOL_PAD + tm, :] = z_ref[:, U_OFF:U_OFF + POOL_IN]
    pos = t * tm + lax.broadcasted_iota(jnp.int32, (tm, 1), 0)
    for g, w in enumerate(POOL_WINDOWS):
        cols = slice(g * POOL_GROUP_IN, (g + 1) * POOL_GROUP_IN)
        u = ext_ref[POOL_PAD:POOL_PAD + tm, cols]
        tot = u
        for j in range(1, w):
            tot = tot + ext_ref[POOL_PAD - j:POOL_PAD - j + tm, cols]
        cnt = jnp.minimum(pos + 1, w).astype(F32)
        d_ref[:, cols] = (tot / cnt - u).astype(BF16)
    tail = ext_ref[tm:tm + POOL_PAD, :]
    ext_ref[0:POOL_PAD, :] = tail

    def pool_matmul(g):
        ocols = slice(g * POOL_GROUP_OUT, (g + 1) * POOL_GROUP_OUT)
        y = _dot(d_ref[:, g * POOL_GROUP_IN:(g + 1) * POOL_GROUP_IN],
                 wpool_ref[g * POOL_GROUP_IN:(g + 1) * POOL_GROUP_IN, :])
        pool_ref[:, ocols] = y * pscale_ref[:, ocols]

    fill_width = 2 * LANES
    fillers = [functools.partial(project, off, fill_width) for off in range(G_OFF, G_OFF + V_W, fill_width)]
    fillers += [functools.partial(project, off, fill_width) for off in range(GA_OFF, D_IN, fill_width)]
    fillers += [functools.partial(pool_matmul, g) for g in range(N_POOL_GROUPS)]
    n_units = N_PAIRS * (tm // chunk)

    head0 = (lane < RET_QK_DIM).astype(F32)
    head1 = 1.0 - head0

    for c in range(tm // chunk):
        rows = slice(c * chunk, (c + 1) * chunk)
        for pair in range(N_PAIRS):
            qcols = slice(Q_OFF + pair * LANES, Q_OFF + (pair + 1) * LANES)
            kcols = slice(K_OFF + pair * LANES, K_OFF + (pair + 1) * LANES)
            qb = z_ref[rows, qcols]
            kb = z_ref[rows, kcols]
            vb = z_ref[rows, V_OFF + pair * 2 * RET_V_DIM:V_OFF + (pair + 1) * 2 * RET_V_DIM]
            vb16 = vb.astype(BF16)
            s_pair = s_ref[pair]
            s16 = s_pair.astype(BF16)
            k_stack = jnp.concatenate([kb * head0, kb * head1], axis=0).astype(BF16)
            sc = lax.dot_general(qb.astype(BF16), k_stack, (((1,), (1,)), ((), ())),
                                 preferred_element_type=F32)
            qx = qb * xi_ref[pair]
            for j, hmask in enumerate((head0, head1)):
                h = 2 * pair + j
                scj = sc[:, j * chunk:(j + 1) * chunk] * dmask_ref[h]
                lhs = jnp.concatenate([scj, qx * hmask], axis=1).astype(BF16)
                rhs = jnp.concatenate([vb16[:, j * RET_V_DIM:(j + 1) * RET_V_DIM], s16], axis=0)
                o = _dot(lhs, rhs)
                o = o * lax.rsqrt(jnp.mean(o * o, axis=-1, keepdims=True) + EPS)
                ret_ref[rows, h * RET_V_DIM:(h + 1) * RET_V_DIM] = o
            kw = (kb * wk_ref[pair]).astype(BF16)
            upd = lax.dot_general(kw, vb16, (((0,), (0,)), ((), ())),
                                  preferred_element_type=F32)
            new_s = jnp.concatenate([upd[0:RET_QK_DIM, 0:RET_V_DIM],
                                     upd[RET_QK_DIM:, RET_V_DIM:]], axis=0)
            s_ref[pair] = s_pair * gs_ref[pair] + new_s
            unit = c * N_PAIRS + pair
            for f in fillers[unit * len(fillers) // n_units:(unit + 1) * len(fillers) // n_units]:
                f()

    n_halves = 2
    for r in range(n_halves):
        rows = slice(r * tm // n_halves, (r + 1) * tm // n_halves)
        gate = z_ref[rows, G_OFF:G_OFF + V_W]
        ret = ret_ref[rows, :] * (gate * jax.nn.sigmoid(gate))
        merged = (jax.nn.sigmoid(z_ref[rows, GA_OFF:GA_OFF + D_MODEL]) * ret
                  + jax.nn.sigmoid(z_ref[rows, GB_OFF:GB_OFF + D_MODEL]) * pool_ref[rows, :])
        o_ref[rows, :] = x_ref[rows, :] + _dot(merged.astype(BF16), wout_ref[...])

    @pl.when(t == nt - 1)
    def _():
        sret_ref[...] = s_ref[...].reshape(N_PAIRS * LANES, RET_V_DIM)
        spool_ref[...] = tail[POOL_PAD - POOL_BUF:, :]


def _state_alias(prev, n_in, first_out):
    if prev is None:
        return [], [], {}
    specs = [pl.BlockSpec(memory_space=pl.ANY) for _ in prev]
    aliases = {n_in + k: first_out + k for k in range(len(prev))}
    return list(prev), specs, aliases


def _mixer_prompt(x, g, w_in, w_pool, pool_scale, w_out, tabs, layer, prev_states, *, tm):
    b, t_len, _ = x.shape
    depth = w_in.shape[0]
    chunk = RET_CHUNK
    assert t_len % tm == 0 and tm % chunk == 0
    cos, sin, dmask, xi, wk, gs = tabs
    row_spec = pl.BlockSpec((None, tm, D_MODEL), lambda i, j: (i, j, 0))
    tab_spec = pl.BlockSpec((tm, LANES), lambda i, j: (j, 0))
    in_specs = [
        row_spec, _layer_spec(g.shape, layer), ANY_SPEC, ANY_SPEC, _layer_spec(pool_scale.shape, layer),
        ANY_SPEC, tab_spec, tab_spec,
        _const_spec(dmask.shape), _const_spec(xi.shape), _const_spec(wk.shape), _const_spec(gs.shape),
    ]
    args = [x, g, w_in, w_pool, pool_scale, w_out, cos, sin, dmask, xi, wk, gs]
    extra, extra_specs, aliases = _state_alias(prev_states, len(args), 1)
    w_shapes = [w_in.shape[1:], w_pool.shape[1:], w_out.shape[1:]]
    out_specs = [
        row_spec,
        pl.BlockSpec((None, None, N_PAIRS * LANES, RET_V_DIM), lambda i, j: (layer, i, 0, 0)),
        pl.BlockSpec((None, None, POOL_BUF, POOL_IN), lambda i, j: (layer, i, 0, 0)),
    ] + [ANY_SPEC] * len(w_shapes)
    out_shape = [
        jax.ShapeDtypeStruct((b, t_len, D_MODEL), F32),
        jax.ShapeDtypeStruct((depth, b, N_PAIRS * LANES, RET_V_DIM), F32),
        jax.ShapeDtypeStruct((depth, b, POOL_BUF, POOL_IN), F32),
    ] + [jax.ShapeDtypeStruct(s, BF16) for s in w_shapes]
    scratch = [
        pltpu.VMEM((tm, D_IN), F32),
        pltpu.VMEM((N_PAIRS, LANES, RET_V_DIM), F32),
        pltpu.VMEM((POOL_PAD + tm, POOL_IN), F32),
        pltpu.VMEM((tm, V_W), F32),
        pltpu.VMEM((tm, D_MODEL), F32),
        pltpu.VMEM((tm, D_MODEL), BF16),
        pltpu.VMEM((tm, POOL_IN), BF16),
    ]
    for s in w_shapes:
        scratch += _stage_scratch(s)
    scratch.append(pltpu.SemaphoreType.DMA((len(w_shapes),)))
    y, sret, spool, *w16 = pl.pallas_call(
        functools.partial(_mixer_prompt_body, tm=tm, chunk=chunk, n_alias=len(extra), layer=layer),
        grid=(b, t_len // tm),
        in_specs=in_specs + extra_specs,
        out_specs=out_specs,
        out_shape=out_shape,
        scratch_shapes=scratch,
        input_output_aliases=aliases,
        compiler_params=pltpu.CompilerParams(
            dimension_semantics=("arbitrary", "arbitrary"), vmem_limit_bytes=VMEM_LIMIT),
        name="mixer_prompt",
    )(*args, *extra)
    return y, (sret, spool), tuple(w16)


def _log_gamma():
    return jnp.log1p(-jnp.exp2(-5.0 - jnp.arange(N_RET_HEADS, dtype=F32)))


def _rope_tables(pos):
    half = RET_QK_DIM // 2
    inv = jnp.power(jnp.float32(ROPE_BASE), -jnp.linspace(0.0, 1.0, half, dtype=F32))
    ang = pos[:, None] * inv[None, :]
    cos, sin = jnp.cos(ang), jnp.sin(ang)
    cos128 = jnp.tile(cos, (1, LANES // half))
    sin128 = jnp.tile(jnp.concatenate([-sin, sin], axis=1), (1, LANES // RET_QK_DIM))
    return cos128, sin128


def _pair_lanes(per_head):
    h, c = per_head.shape
    x = jnp.broadcast_to(per_head[:, :, None], (h, c, RET_QK_DIM))
    return x.reshape(N_PAIRS, 2, c, RET_QK_DIM).transpose(0, 2, 1, 3).reshape(N_PAIRS, c, LANES)


def _state_decay(chunk):
    g_c = jnp.exp(_log_gamma() * chunk)
    gs = jnp.broadcast_to(g_c[:, None, None], (N_RET_HEADS, RET_QK_DIM, RET_V_DIM))
    return gs.reshape(N_PAIRS, LANES, RET_V_DIM)


def _decay_tables(chunk):
    lg = _log_gamma()
    i = jnp.arange(chunk, dtype=F32)
    dist = i[:, None] - i[None, :]
    dmask = jnp.where(dist[None] >= 0, jnp.exp(lg[:, None, None] * jnp.maximum(dist, 0.0)[None]), 0.0)
    xi = jnp.exp(lg[:, None] * (i[None, :] + 1.0))
    wk = jnp.exp(lg[:, None] * (chunk - 1.0 - i[None, :]))
    return dmask, _pair_lanes(xi), _pair_lanes(wk), _state_decay(chunk)


def _mixer_sample_body(*refs, bb, ts, past, n_alias):
    (x_ref, g_ref, win_ref, wpool_ref, pscale_ref, wout_ref, cos_ref, sin_ref,
     xi_ref, wk_ref, lag_ref, gs_ref, ind_ref, s0_ref, pb_ref) = refs[:15]
    (o_ref, snew_ref, pnew_ref, z_ref, inter_ref, ublk_ref, pblk_ref) = refs[15 + n_alias:]
    rows = bb * ts
    x = x_ref[...]
    _project_in(x, g_ref, win_ref, z_ref)
    lane = lax.broadcasted_iota(jnp.int32, (1, LANES), 1)
    first_half = (lane % RET_QK_DIM) < (RET_QK_DIM // 2)
    _rotate_inplace(z_ref, cos_ref[...], sin_ref[...], first_half)

    q = z_ref[:, Q_OFF:Q_OFF + QK_W]
    k = z_ref[:, K_OFF:K_OFF + QK_W]
    v = z_ref[:, V_OFF:V_OFF + V_W]
    step = lax.broadcasted_iota(jnp.int32, (rows, 1), 0) % ts
    intra = None
    for d in range(ts):
        kd = k if d == 0 else pltpu.roll(k, d, 0)
        vd = v if d == 0 else pltpu.roll(v, d, 0)
        prod = jnp.where(step >= d, q * kd, 0.0).astype(BF16)
        score = _dot(prod, ind_ref[...])
        term = score * lag_ref[d:d + 1, :] * jnp.where(step >= d, vd, 0.0)
        intra = term if intra is None else intra + term

    row8 = lax.broadcasted_iota(jnp.int32, (2 * ts, 1), 0)
    top = (lax.broadcasted_iota(jnp.int32, (LANES, 1), 0) < RET_QK_DIM).astype(F32)
    bot = 1.0 - top

    def seq_pair(p, carry):
        r0 = pl.multiple_of(p * 2 * ts, 2 * ts)
        rsl = pl.ds(r0, 2 * ts)
        for pair in range(N_PAIRS):
            lanes = slice(pair * LANES, (pair + 1) * LANES)
            q8 = (z_ref[rsl, Q_OFF + pair * LANES:Q_OFF + (pair + 1) * LANES] * xi_ref[:, lanes]).astype(BF16)
            kw8 = z_ref[rsl, K_OFF + pair * LANES:K_OFF + (pair + 1) * LANES] * wk_ref[:, lanes]
            v8 = z_ref[rsl, V_OFF + pair * 2 * RET_V_DIM:V_OFF + (pair + 1) * 2 * RET_V_DIM].astype(BF16)
            outs = []
            for j in range(2):
                b = 2 * p + j
                s_pair = s0_ref[b, lanes, :]
                bd = jnp.concatenate([s_pair * top, s_pair * bot], axis=1).astype(BF16)
                outs.append(_dot(q8, bd))
                mine = (row8 >= j * ts) & (row8 < (j + 1) * ts)
                kwj = jnp.where(mine, kw8, 0.0).astype(BF16)
                upd = lax.dot_general(kwj, v8, (((0,), (0,)), ((), ())), preferred_element_type=F32)
                new_s = jnp.concatenate([upd[0:RET_QK_DIM, 0:RET_V_DIM],
                                         upd[RET_QK_DIM:, RET_V_DIM:]], axis=0)
                snew_ref[b, lanes, :] = s_pair * gs_ref[pair] + new_s
            inter_ref[rsl, pair * 2 * RET_V_DIM:(pair + 1) * 2 * RET_V_DIM] = jnp.where(
                row8 < ts, outs[0], outs[1])
        return carry

    lax.fori_loop(0, bb // 2, seq_pair, 0)

    keep = POOL_BUF - ts
    pnew_ref[:, 0:keep * POOL_IN] = pb_ref[:, ts * POOL_IN:]
    for g, w in enumerate(POOL_WINDOWS):
        ublk_ref[g] = z_ref[:, U_OFF + g * POOL_GROUP_IN:U_OFF + (g + 1) * POOL_GROUP_IN]
        u_steps = [ublk_ref[g, pl.ds(t, bb, stride=ts), :] for t in range(ts)]
        for t in range(ts):
            c0 = (keep + t) * POOL_IN + g * POOL_GROUP_IN
            pnew_ref[:, c0:c0 + POOL_GROUP_IN] = u_steps[t]
        ds_ = []
        for t in range(ts):
            tot = None
            for j in range(w):
                idx = POOL_BUF + t - j
                if idx >= POOL_BUF:
                    term = u_steps[idx - POOL_BUF]
                else:
                    term = pb_ref[:, idx * POOL_IN + g * POOL_GROUP_IN:idx * POOL_IN + (g + 1) * POOL_GROUP_IN]
                tot = term if tot is None else tot + term
            cnt = float(min(past + t + 1, w))
            ds_.append(tot / cnt - u_steps[t])
        y = _dot(jnp.concatenate(ds_, axis=0).astype(BF16),
                 wpool_ref[g * POOL_GROUP_IN:(g + 1) * POOL_GROUP_IN, :])
        y = y * pscale_ref[:, g * POOL_GROUP_OUT:(g + 1) * POOL_GROUP_OUT]
        for half in range(POOL_GROUP_OUT // LANES):
            blk = g * (POOL_GROUP_OUT // LANES) + half
            for t in range(ts):
                pblk_ref[blk, pl.ds(t, bb, stride=ts), :] = y[t * bb:(t + 1) * bb, half * LANES:(half + 1) * LANES]
    pool = jnp.concatenate([pblk_ref[blk] for blk in range(D_MODEL // LANES)], axis=1)

    o_all = intra + inter_ref[...]
    rets = []
    for h in range(N_RET_HEADS):
        hc = slice(h * RET_V_DIM, (h + 1) * RET_V_DIM)
        rets.append(_swish_gate_norm(o_all[:, hc], z_ref[:, G_OFF + h * RET_V_DIM:G_OFF + (h + 1) * RET_V_DIM]))
    ret = jnp.concatenate(rets, axis=1)
    o_ref[...] = _merge_out(x, z_ref, ret, pool, wout_ref)


def _sample_tables(ts, past, rows):
    lg = _log_gamma()
    i = jnp.arange(ts, dtype=F32)
    lag = jnp.exp(lg[:, None] * i[None, :])
    lag = jnp.repeat(lag.T, RET_V_DIM, axis=1)
    xi = jnp.exp(lg[:, None] * (i[None, :] + 1.0))
    wk = jnp.exp(lg[:, None] * (ts - 1.0 - i[None, :]))
    expand = lambda a: jnp.tile(jnp.repeat(a.T, RET_QK_DIM, axis=1), (2, 1))
    pos = (past + jnp.arange(ts, dtype=jnp.int32)).astype(F32)
    cos, sin = _rope_tables(pos)
    reps = rows // ts
    head_of_k = jnp.arange(QK_W) // RET_QK_DIM
    head_of_v = jnp.arange(V_W) // RET_V_DIM
    ind = (head_of_k[:, None] == head_of_v[None, :]).astype(BF16)
    return (jnp.tile(cos, (reps, 1)), jnp.tile(sin, (reps, 1)), expand(xi), expand(wk), lag,
            _state_decay(ts), ind)


def _mixer_sample(x2d, s0, pbuf, g, w_in, w_pool, pool_scale, w_out, tabs, layer, prev_states, *, bb, ts, past):
    n = x2d.shape[0]
    depth = s0.shape[0]
    nb = n // ts
    rows = bb * ts
    assert nb % bb == 0 and bb % 2 == 0 and (2 * ts) % 8 == 0 and ts <= POOL_BUF
    cos, sin, xi, wk, lag, gs, ind = tabs
    row_spec = pl.BlockSpec((rows, D_MODEL), lambda i: (i, 0))
    state_spec = pl.BlockSpec((None, bb, N_PAIRS * LANES, RET_V_DIM), lambda i: (layer, i, 0, 0))
    pool_spec = pl.BlockSpec((None, bb, POOL_BUF * POOL_IN), lambda i: (layer, i, 0))
    in_specs = [
        row_spec, _layer_spec(g.shape, layer), _resident_spec(w_in.shape),
        _resident_spec(w_pool.shape), _layer_spec(pool_scale.shape, layer),
        _resident_spec(w_out.shape), _const_spec(cos.shape), _const_spec(sin.shape),
        _const_spec(xi.shape), _const_spec(wk.shape), _const_spec(lag.shape), _const_spec(gs.shape),
        _const_spec(ind.shape), state_spec, pool_spec,
    ]
    args = [x2d, g, w_in, w_pool, pool_scale, w_out, cos, sin, xi, wk, lag, gs, ind, s0, pbuf]
    extra, extra_specs, aliases = _state_alias(prev_states, len(args), 1)
    out_specs = [row_spec, state_spec, pool_spec]
    out_shape = [
        jax.ShapeDtypeStruct((n, D_MODEL), F32),
        jax.ShapeDtypeStruct((depth, nb, N_PAIRS * LANES, RET_V_DIM), F32),
        jax.ShapeDtypeStruct((depth, nb, POOL_BUF * POOL_IN), F32),
    ]
    scratch = [
        pltpu.VMEM((rows, D_IN), F32),
        pltpu.VMEM((rows, V_W), F32),
        pltpu.VMEM((N_POOL_GROUPS, rows, LANES), F32),
        pltpu.VMEM((D_MODEL // LANES, rows, LANES), F32),
    ]
    y, snew, pnew = pl.pallas_call(
        functools.partial(_mixer_sample_body, bb=bb, ts=ts, past=past, n_alias=len(extra)),
        grid=(nb // bb,),
        in_specs=in_specs + extra_specs,
        out_specs=out_specs,
        out_shape=out_shape,
        scratch_shapes=scratch,
        input_output_aliases=aliases,
        compiler_params=pltpu.CompilerParams(
            dimension_semantics=("arbitrary",), vmem_limit_bytes=VMEM_LIMIT),
        name="mixer_sample",
    )(*args, *extra)
    return y, (snew, pnew)


def kernel(x_prompt, x_sample, state_ret, state_pool, norm_ffn1, w_ffn1_in, w_ffn1_out, norm_mix,
           w_in, w_pool, pool_scale, w_out, norm_ffn2, w_ffn2_in, w_ffn2_out, norm_final):
    depth = w_in.shape[0]
    bp, tp, _ = x_prompt.shape
    bs, ts, _ = x_sample.shape
    wp = w_pool.reshape(depth, N_POOL_GROUPS * POOL_GROUP_IN, POOL_GROUP_OUT)

    row = lambda a: a.reshape(depth, 1, D_MODEL)
    n1, nm, n2, ps = map(row, (norm_ffn1, norm_mix, norm_ffn2, pool_scale))
    gf = norm_final.reshape(1, D_MODEL)
    last = depth - 1

    tm = 512
    tm_ffn = 1024
    pos_p = jnp.arange(tp, dtype=jnp.int32).astype(F32)
    tabs_p = _rope_tables(pos_p) + _decay_tables(RET_CHUNK)

    x = x_prompt.reshape(bp * tp, D_MODEL)
    states_p = None
    w16 = []
    for l in range(depth):
        x, ffn1_w = _ffn(x, n1, w_ffn1_in, w_ffn1_out, l, tm=tm_ffn)
        x, states_p, mix_w = _mixer_prompt(x.reshape(bp, tp, D_MODEL), nm, w_in, wp, ps, w_out, tabs_p, l,
                                           states_p, tm=tm)
        x, ffn2_w = _ffn(x.reshape(bp * tp, D_MODEL), n2, w_ffn2_in, w_ffn2_out, l,
                         gf if l == last else None, tm=tm_ffn)
        w16.append((ffn1_w, mix_w, ffn2_w))
    y_prompt = x.reshape(bp, tp, D_MODEL)
    state_ret_prompt = states_p[0].reshape(depth, bp, N_RET_HEADS, RET_QK_DIM, RET_V_DIM)
    state_pool_prompt = states_p[1]

    bb = 32
    tabs_s = _sample_tables(ts, PAST_LEN, bb * ts)
    s0_all = state_ret.reshape(depth, bs, N_PAIRS * LANES, RET_V_DIM)
    pb_all = state_pool.reshape(depth, bs, POOL_BUF * POOL_IN)
    x = x_sample.reshape(bs * ts, D_MODEL)
    states_s = None
    for l in range(depth):
        ffn1_w, (wi16, wp16, wo16), ffn2_w = w16[l]
        x, _ = _ffn(x, n1, *ffn1_w, l, tm=tm)
        x, states_s = _mixer_sample(x, s0_all, pb_all, nm, wi16, wp16, ps, wo16, tabs_s, l, states_s,
                                    bb=bb, ts=ts, past=PAST_LEN)
        x, _ = _ffn(x, n2, *ffn2_w, l, gf if l == last else None, tm=tm)
    y_sample = x.reshape(bs, ts, D_MODEL)
    state_ret_sample = states_s[0].reshape(depth, bs, N_RET_HEADS, RET_QK_DIM, RET_V_DIM)
    state_pool_sample = states_s[1].reshape(depth, bs, POOL_BUF, POOL_IN)
    return (y_prompt, y_sample, state_ret_prompt, state_ret_sample, state_pool_prompt, state_pool_sample)
```

```python
import functools

import jax
import jax.numpy as jnp
from jax import lax
from jax.experimental import pallas as pl
from jax.experimental.pallas import tpu as pltpu

D_MODEL = 1024
N_RET_HEADS = 8
RET_QK_DIM = 64
RET_V_DIM = 128
RET_CHUNK = 128
ROPE_BASE = 10000.0
QK_W = N_RET_HEADS * RET_QK_DIM
V_W = N_RET_HEADS * RET_V_DIM
POOL_WINDOWS = (2, 4, 8, 16)
N_POOL_GROUPS = 4
POOL_IN = 512
POOL_GROUP_IN = 128
POOL_GROUP_OUT = 256
POOL_BUF = 15
D_IN = 2 * QK_W + 2 * V_W + POOL_IN + 2 * D_MODEL
D_FF = 2816
EPS = 1e-6
PAST_LEN = 16384

Q_OFF = 0
K_OFF = QK_W
V_OFF = 2 * QK_W
G_OFF = 2 * QK_W + V_W
U_OFF = 2 * QK_W + 2 * V_W
GA_OFF = U_OFF + POOL_IN
GB_OFF = GA_OFF + D_MODEL

LANES = 128
N_PAIRS = N_RET_HEADS // 2
POOL_PAD = 16
VMEM_LIMIT = 60 * 1024 * 1024

F32 = jnp.float32
BF16 = jnp.bfloat16


def _dot(a, b):
    return jnp.dot(a, b, preferred_element_type=F32)


def _rms(x, g):
    return x * lax.rsqrt(jnp.mean(x * x, axis=-1, keepdims=True) + EPS) * g


def _const_spec(shape):
    nd = len(shape)
    return pl.BlockSpec(shape, lambda *_: (0,) * nd)


FF_CHUNKS = ((0, 1024), (1024, 1024), (2048, 768))


STAGE_BUFS = 3
STAGE_BYTES = 3 * 512 * 1024


def _stage_rows(shape):
    r, c = shape
    fits = [n for n in range(16, r + 1, 16) if r % n == 0 and n * c * 4 <= STAGE_BYTES]
    return max(fits)


def _stage_scratch(shape):
    return [pltpu.VMEM(shape, BF16), pltpu.VMEM((STAGE_BUFS, _stage_rows(shape), shape[1]), F32),
            pltpu.SemaphoreType.DMA((STAGE_BUFS,))]


def _fetch_cast(src, dst, stage, sems):
    nbuf, rows, _ = stage.shape
    n = src.shape[0] // rows

    def copy(k):
        return pltpu.make_async_copy(src.at[pl.ds(k * rows, rows), :], stage.at[k % nbuf], sems.at[k % nbuf])

    for k in range(min(nbuf, n)):
        copy(k).start()
    for k in range(n):
        copy(k).wait()
        dst[k * rows:(k + 1) * rows, :] = stage[k % nbuf].astype(BF16)
        if k + nbuf < n:
            copy(k + nbuf).start()


def _convert_weights(first, layer, srcs, scratch):
    @pl.when(first)
    def _():
        for k, src in enumerate(srcs):
            _fetch_cast(src.at[layer], *scratch[3 * k:3 * k + 3])

    return scratch[0::3]


def _emit_weights(first, last, srcs, outs, sems):
    copies = [pltpu.make_async_copy(s, o, sems.at[k]) for k, (s, o) in enumerate(zip(srcs, outs))]

    @pl.when(first)
    def _():
        for c in copies:
            c.start()

    @pl.when(last)
    def _():
        for c in copies:
            c.wait()


def _ffn_body(*refs, final, layer):
    n_in = 6 if final else 5
    x_ref, xs_ref, g_ref, win_hbm, wout_hbm = refs[:5]
    gf_ref = refs[5] if final else None
    o_ref, os_ref = refs[n_in:n_in + 2]
    i = pl.program_id(0)
    win_ref, wout_ref = _convert_weights(i == 0, layer, (win_hbm, wout_hbm), refs[n_in + 2:])

    def ffn_rows(x):
        h = _rms(x, g_ref[...]).astype(BF16)
        acc = None
        for off, width in FF_CHUNKS:
            gate = _dot(h, win_ref[:, off:off + width])
            up = _dot(h, win_ref[:, D_FF + off:D_FF + off + width])
            a = (gate * jax.nn.sigmoid(gate) * up).astype(BF16)
            part = _dot(a, wout_ref[off:off + width, :])
            acc = part if acc is None else acc + part
        y = x + 0.5 * acc
        return _rms(y, gf_ref[...]) if final else y

    o_ref[...] = ffn_rows(x_ref[...])

    @pl.when(i == pl.num_programs(0) - 1)
    def _():
        os_ref[...] = ffn_rows(xs_ref[...])


def _layer_spec(shape, layer):
    nd = len(shape) - 1
    return pl.BlockSpec((None,) + tuple(shape[1:]), lambda *_: (layer,) + (0,) * nd,
                        pipeline_mode=pl.Buffered(1))


def _resident_spec(shape):
    nd = len(shape)
    return pl.BlockSpec(tuple(shape), lambda *_: (0,) * nd, pipeline_mode=pl.Buffered(1))


ANY_SPEC = pl.BlockSpec(memory_space=pl.ANY)


def _ffn(x2d, xs2d, g, w_in, w_out, layer, g_final=None, *, tm):
    n = x2d.shape[0]
    assert n % tm == 0
    final = g_final is not None
    row_spec = pl.BlockSpec((tm, D_MODEL), lambda i: (i, 0))
    in_specs = [row_spec, _resident_spec(xs2d.shape), _layer_spec(g.shape, layer), ANY_SPEC, ANY_SPEC]
    args = [x2d, xs2d, g, w_in, w_out]
    if final:
        in_specs.append(_const_spec((1, D_MODEL)))
        args.append(g_final.reshape(1, D_MODEL))
    scratch = _stage_scratch(w_in.shape[1:]) + _stage_scratch(w_out.shape[1:])
    return pl.pallas_call(
        functools.partial(_ffn_body, final=final, layer=layer),
        grid=(n // tm,),
        in_specs=in_specs,
        out_specs=[row_spec, _resident_spec(xs2d.shape)],
        out_shape=[jax.ShapeDtypeStruct(x2d.shape, F32), jax.ShapeDtypeStruct(xs2d.shape, F32)],
        scratch_shapes=scratch,
        compiler_params=pltpu.CompilerParams(
            dimension_semantics=("arbitrary",), vmem_limit_bytes=VMEM_LIMIT),
        name="ffn_final" if final else "ffn",
    )(*args)


IN_CHUNKS = tuple((o, 1024) for o in range(0, 5120, 1024)) + ((5120, 512),)


def _project_in(x, g_ref, win_ref, z_ref):
    h = _rms(x, g_ref[...]).astype(BF16)
    for off, width in IN_CHUNKS:
        z_ref[:, off:off + width] = _dot(h, win_ref[:, off:off + width])


def _rotate_inplace(z_ref, cos, sin_signed, first_half):
    for blk in range(2 * QK_W // LANES):
        cols = slice(blk * LANES, (blk + 1) * LANES)
        xb = z_ref[:, cols]
        partner = jnp.where(first_half, pltpu.roll(xb, LANES - 32, 1), pltpu.roll(xb, 32, 1))
        r = xb * cos + partner * sin_signed
        if blk >= QK_W // LANES:
            r = r * (RET_QK_DIM ** -0.5)
        z_ref[:, cols] = r


def _merge_out(x, z_ref, ret, pool, wout_ref):
    ga = z_ref[:, GA_OFF:GA_OFF + D_MODEL]
    gb = z_ref[:, GB_OFF:GB_OFF + D_MODEL]
    merged = jax.nn.sigmoid(ga) * ret + jax.nn.sigmoid(gb) * pool
    return x + _dot(merged.astype(BF16), wout_ref[...])


def _swish_gate_norm(o, gate):
    o = o * lax.rsqrt(jnp.mean(o * o, axis=-1, keepdims=True) + EPS)
    return o * (gate * jax.nn.sigmoid(gate))


def _mixer_prompt_body(*refs, tm, chunk, n_alias, layer):
    (x_ref, g_ref, win_ref, wpool_ref, pscale_ref, wout_ref, cos_ref, sin_ref,
     dmask_ref, xi_ref, wk_ref, gs_ref) = refs[:12]
    outs = refs[12 + n_alias:]
    o_ref, sret_ref, spool_ref = outs[:3]
    z_ref, s_ref, ext_ref, ret_ref, pool_ref, h_ref, d_ref = outs[6:13]
    t = pl.program_id(1)
    nt = pl.num_programs(1)
    bi = pl.program_id(0)
    first = (bi == 0) & (t == 0)
    win_ref, wpool_ref, wout_ref = _convert_weights(first, layer, (win_ref, wpool_ref, wout_ref), outs[13:22])
    _emit_weights(first, (bi == pl.num_programs(0) - 1) & (t == nt - 1),
                  (win_ref, wpool_ref, wout_ref), outs[3:6], outs[22])

    @pl.when(t == 0)
    def _():
        s_ref[...] = jnp.zeros_like(s_ref)
        ext_ref[0:POOL_PAD, :] = jnp.zeros((POOL_PAD, POOL_IN), F32)

    h_ref[...] = _rms(x_ref[...], g_ref[...]).astype(BF16)

    def project(off, width):
        z_ref[:, off:off + width] = _dot(h_ref[...], win_ref[:, off:off + width])

    project(Q_OFF, 2 * QK_W)
    project(V_OFF, V_W)
    project(U_OFF, POOL_IN)

    lane = lax.broadcasted_iota(jnp.int32, (1, LANES), 1)
    first_half = (lane % RET_QK_DIM) < (RET_QK_DIM // 2)
    _rotate_inplace(z_ref, cos_ref[...], sin_ref[...], first_half)

    ext_ref[POOL_PAD:POOL_PAD + tm, :] = z_ref[:, U_OFF:U_OFF + POOL_IN]
    pos = t * tm + lax.broadcasted_iota(jnp.int32, (tm, 1), 0)
    for g, w in enumerate(POOL_WINDOWS):
        cols = slice(g * POOL_GROUP_IN, (g + 1) * POOL_GROUP_IN)
        u = ext_ref[POOL_PAD:POOL_PAD + tm, cols]
        tot = u
        for j in range(1, w):
            tot = tot + ext_ref[POOL_PAD - j:POOL_PAD - j + tm, cols]
        cnt = jnp.minimum(pos + 1, w).astype(F32)
        d_ref[:, cols] = (tot / cnt - u).astype(BF16)
    tail = ext_ref[tm:tm + POOL_PAD, :]
    ext_ref[0:POOL_PAD, :] = tail

    def pool_matmul(g):
        ocols = slice(g * POOL_GROUP_OUT, (g + 1) * POOL_GROUP_OUT)
        y = _dot(d_ref[:, g * POOL_GROUP_IN:(g + 1) * POOL_GROUP_IN],
                 wpool_ref[g * POOL_GROUP_IN:(g + 1) * POOL_GROUP_IN, :])
        pool_ref[:, ocols] = y * pscale_ref[:, ocols]

    fill_width = 2 * LANES
    fillers = [functools.partial(project, off, fill_width) for off in range(G_OFF, G_OFF + V_W, fill_width)]
    fillers += [functools.partial(project, off, fill_width) for off in range(GA_OFF, D_IN, fill_width)]
    fillers += [functools.partial(pool_matmul, g) for g in range(N_POOL_GROUPS)]
    n_units = N_PAIRS * (tm // chunk)

    head0 = (lane < RET_QK_DIM).astype(F32)
    head1 = 1.0 - head0

    for c in range(tm // chunk):
        rows = slice(c * chunk, (c + 1) * chunk)
        for pair in range(N_PAIRS):
            qcols = slice(Q_OFF + pair * LANES, Q_OFF + (pair + 1) * LANES)
            kcols = slice(K_OFF + pair * LANES, K_OFF + (pair + 1) * LANES)
            qb = z_ref[rows, qcols]
            kb = z_ref[rows, kcols]
            vb = z_ref[rows, V_OFF + pair * 2 * RET_V_DIM:V_OFF + (pair + 1) * 2 * RET_V_DIM]
            vb16 = vb.astype(BF16)
            s_pair = s_ref[pair]
            s16 = s_pair.astype(BF16)
            k_stack = jnp.concatenate([kb * head0, kb * head1], axis=0).astype(BF16)
            sc = lax.dot_general(qb.astype(BF16), k_stack, (((1,), (1,)), ((), ())),
                                 preferred_element_type=F32)
            qx = qb * xi_ref[pair]
            for j, hmask in enumerate((head0, head1)):
                h = 2 * pair + j
                scj = sc[:, j * chunk:(j + 1) * chunk] * dmask_ref[h]
                lhs = jnp.concatenate([scj, qx * hmask], axis=1).astype(BF16)
                rhs = jnp.concatenate([vb16[:, j * RET_V_DIM:(j + 1) * RET_V_DIM], s16], axis=0)
                o = _dot(lhs, rhs)
                o = o * lax.rsqrt(jnp.mean(o * o, axis=-1, keepdims=True) + EPS)
                ret_ref[rows, h * RET_V_DIM:(h + 1) * RET_V_DIM] = o
            kw = (kb * wk_ref[pair]).astype(BF16)
            upd = lax.dot_general(kw, vb16, (((0,), (0,)), ((), ())),
                                  preferred_element_type=F32)
            new_s = jnp.concatenate([upd[0:RET_QK_DIM, 0:RET_V_DIM],
                                     upd[RET_QK_DIM:, RET_V_DIM:]], axis=0)
            s_ref[pair] = s_pair * gs_ref[pair] + new_s
            unit = c * N_PAIRS + pair
            for f in fillers[unit * len(fillers) // n_units:(unit + 1) * len(fillers) // n_units]:
                f()

    n_halves = 2
    for r in range(n_halves):
        rows = slice(r * tm // n_halves, (r + 1) * tm // n_halves)
        gate = z_ref[rows, G_OFF:G_OFF + V_W]
        ret = ret_ref[rows, :] * (gate * jax.nn.sigmoid(gate))
        merged = (jax.nn.sigmoid(z_ref[rows, GA_OFF:GA_OFF + D_MODEL]) * ret
                  + jax.nn.sigmoid(z_ref[rows, GB_OFF:GB_OFF + D_MODEL]) * pool_ref[rows, :])
        o_ref[rows, :] = x_ref[rows, :] + _dot(merged.astype(BF16), wout_ref[...])

    @pl.when(t == nt - 1)
    def _():
        sret_ref[...] = s_ref[...].reshape(N_PAIRS * LANES, RET_V_DIM)
        spool_ref[...] = tail[POOL_PAD - POOL_BUF:, :]


def _state_alias(prev, n_in, first_out):
    if prev is None:
        return [], [], {}
    specs = [pl.BlockSpec(memory_space=pl.ANY) for _ in prev]
    aliases = {n_in + k: first_out + k for k in range(len(prev))}
    return list(prev), specs, aliases


def _mixer_prompt(x, g, w_in, w_pool, pool_scale, w_out, tabs, layer, prev_states, *, tm):
    b, t_len, _ = x.shape
    depth = w_in.shape[0]
    chunk = RET_CHUNK
    assert t_len % tm == 0 and tm % chunk == 0
    cos, sin, dmask, xi, wk, gs = tabs
    row_spec = pl.BlockSpec((None, tm, D_MODEL), lambda i, j: (i, j, 0))
    tab_spec = pl.BlockSpec((tm, LANES), lambda i, j: (j, 0))
    in_specs = [
        row_spec, _layer_spec(g.shape, layer), ANY_SPEC, ANY_SPEC, _layer_spec(pool_scale.shape, layer),
        ANY_SPEC, tab_spec, tab_spec,
        _const_spec(dmask.shape), _const_spec(xi.shape), _const_spec(wk.shape), _const_spec(gs.shape),
    ]
    args = [x, g, w_in, w_pool, pool_scale, w_out, cos, sin, dmask, xi, wk, gs]
    extra, extra_specs, aliases = _state_alias(prev_states, len(args), 1)
    w_shapes = [w_in.shape[1:], w_pool.shape[1:], w_out.shape[1:]]
    out_specs = [
        row_spec,
        pl.BlockSpec((None, None, N_PAIRS * LANES, RET_V_DIM), lambda i, j: (layer, i, 0, 0)),
        pl.BlockSpec((None, None, POOL_BUF, POOL_IN), lambda i, j: (layer, i, 0, 0)),
    ] + [ANY_SPEC] * len(w_shapes)
    out_shape = [
        jax.ShapeDtypeStruct((b, t_len, D_MODEL), F32),
        jax.ShapeDtypeStruct((depth, b, N_PAIRS * LANES, RET_V_DIM), F32),
        jax.ShapeDtypeStruct((depth, b, POOL_BUF, POOL_IN), F32),
    ] + [jax.ShapeDtypeStruct(s, BF16) for s in w_shapes]
    scratch = [
        pltpu.VMEM((tm, D_IN), F32),
        pltpu.VMEM((N_PAIRS, LANES, RET_V_DIM), F32),
        pltpu.VMEM((POOL_PAD + tm, POOL_IN), F32),
        pltpu.VMEM((tm, V_W), F32),
        pltpu.VMEM((tm, D_MODEL), F32),
        pltpu.VMEM((tm, D_MODEL), BF16),
        pltpu.VMEM((tm, POOL_IN), BF16),
    ]
    for s in w_shapes:
        scratch += _stage_scratch(s)
    scratch.append(pltpu.SemaphoreType.DMA((len(w_shapes),)))
    y, sret, spool, *w16 = pl.pallas_call(
        functools.partial(_mixer_prompt_body, tm=tm, chunk=chunk, n_alias=len(extra), layer=layer),
        grid=(b, t_len // tm),
        in_specs=in_specs + extra_specs,
        out_specs=out_specs,
        out_shape=out_shape,
        scratch_shapes=scratch,
        input_output_aliases=aliases,
        compiler_params=pltpu.CompilerParams(
            dimension_semantics=("arbitrary", "arbitrary"), vmem_limit_bytes=VMEM_LIMIT),
        name="mixer_prompt",
    )(*args, *extra)
    return y, (sret, spool), tuple(w16)


def _log_gamma():
    return jnp.log1p(-jnp.exp2(-5.0 - jnp.arange(N_RET_HEADS, dtype=F32)))


def _rope_tables(pos):
    half = RET_QK_DIM // 2
    inv = jnp.power(jnp.float32(ROPE_BASE), -jnp.linspace(0.0, 1.0, half, dtype=F32))
    ang = pos[:, None] * inv[None, :]
    cos, sin = jnp.cos(ang), jnp.sin(ang)
    cos128 = jnp.tile(cos, (1, LANES // half))
    sin128 = jnp.tile(jnp.concatenate([-sin, sin], axis=1), (1, LANES // RET_QK_DIM))
    return cos128, sin128


def _pair_lanes(per_head):
    h, c = per_head.shape
    x = jnp.broadcast_to(per_head[:, :, None], (h, c, RET_QK_DIM))
    return x.reshape(N_PAIRS, 2, c, RET_QK_DIM).transpose(0, 2, 1, 3).reshape(N_PAIRS, c, LANES)


def _state_decay(chunk):
    g_c = jnp.exp(_log_gamma() * chunk)
    gs = jnp.broadcast_to(g_c[:, None, None], (N_RET_HEADS, RET_QK_DIM, RET_V_DIM))
    return gs.reshape(N_PAIRS, LANES, RET_V_DIM)


def _decay_tables(chunk):
    lg = _log_gamma()
    i = jnp.arange(chunk, dtype=F32)
    dist = i[:, None] - i[None, :]
    dmask = jnp.where(dist[None] >= 0, jnp.exp(lg[:, None, None] * jnp.maximum(dist, 0.0)[None]), 0.0)
    xi = jnp.exp(lg[:, None] * (i[None, :] + 1.0))
    wk = jnp.exp(lg[:, None] * (chunk - 1.0 - i[None, :]))
    return dmask, _pair_lanes(xi), _pair_lanes(wk), _state_decay(chunk)


def _mixer_sample_body(*refs, bb, ts, past, n_alias):
    (x_ref, g_ref, win_ref, wpool_ref, pscale_ref, wout_ref, cos_ref, sin_ref,
     xi_ref, wk_ref, lag_ref, gs_ref, ind_ref, s0_ref, pb_ref) = refs[:15]
    (o_ref, snew_ref, pnew_ref, z_ref, inter_ref, ublk_ref, pblk_ref) = refs[15 + n_alias:]
    rows = bb * ts
    x = x_ref[...]
    _project_in(x, g_ref, win_ref, z_ref)
    lane = lax.broadcasted_iota(jnp.int32, (1, LANES), 1)
    first_half = (lane % RET_QK_DIM) < (RET_QK_DIM // 2)
    _rotate_inplace(z_ref, cos_ref[...], sin_ref[...], first_half)

    q = z_ref[:, Q_OFF:Q_OFF + QK_W]
    k = z_ref[:, K_OFF:K_OFF + QK_W]
    v = z_ref[:, V_OFF:V_OFF + V_W]
    step = lax.broadcasted_iota(jnp.int32, (rows, 1), 0) % ts
    intra = None
    for d in range(ts):
        kd = k if d == 0 else pltpu.roll(k, d, 0)
        vd = v if d == 0 else pltpu.roll(v, d, 0)
        prod = jnp.where(step >= d, q * kd, 0.0).astype(BF16)
        score = _dot(prod, ind_ref[...])
        term = score * lag_ref[d:d + 1, :] * jnp.where(step >= d, vd, 0.0)
        intra = term if intra is None else intra + term

    row8 = lax.broadcasted_iota(jnp.int32, (2 * ts, 1), 0)
    top = (lax.broadcasted_iota(jnp.int32, (LANES, 1), 0) < RET_QK_DIM).astype(F32)
    bot = 1.0 - top

    def seq_pair(p, carry):
        r0 = pl.multiple_of(p * 2 * ts, 2 * ts)
        rsl = pl.ds(r0, 2 * ts)
        for pair in range(N_PAIRS):
            lanes = slice(pair * LANES, (pair + 1) * LANES)
            q8 = (z_ref[rsl, Q_OFF + pair * LANES:Q_OFF + (pair + 1) * LANES] * xi_ref[:, lanes]).astype(BF16)
            kw8 = z_ref[rsl, K_OFF + pair * LANES:K_OFF + (pair + 1) * LANES] * wk_ref[:, lanes]
            v8 = z_ref[rsl, V_OFF + pair * 2 * RET_V_DIM:V_OFF + (pair + 1) * 2 * RET_V_DIM].astype(BF16)
            outs = []
            for j in range(2):
                b = 2 * p + j
                s_pair = s0_ref[b, lanes, :]
                bd = jnp.concatenate([s_pair * top, s_pair * bot], axis=1).astype(BF16)
                outs.append(_dot(q8, bd))
                mine = (row8 >= j * ts) & (row8 < (j + 1) * ts)
                kwj = jnp.where(mine, kw8, 0.0).astype(BF16)
                upd = lax.dot_general(kwj, v8, (((0,), (0,)), ((), ())), preferred_element_type=F32)
                new_s = jnp.concatenate([upd[0:RET_QK_DIM, 0:RET_V_DIM],
                                         upd[RET_QK_DIM:, RET_V_DIM:]], axis=0)
                snew_ref[b, lanes, :] = s_pair * gs_ref[pair] + new_s
            inter_ref[rsl, pair * 2 * RET_V_DIM:(pair + 1) * 2 * RET_V_DIM] = jnp.where(
                row8 < ts, outs[0], outs[1])
        return carry

    lax.fori_loop(0, bb // 2, seq_pair, 0)

    keep = POOL_BUF - ts
    pnew_ref[:, 0:keep, :] = pb_ref[:, ts:, :]
    for g, w in enumerate(POOL_WINDOWS):
        ublk_ref[g] = z_ref[:, U_OFF + g * POOL_GROUP_IN:U_OFF + (g + 1) * POOL_GROUP_IN]
        u_steps = [ublk_ref[g, pl.ds(t, bb, stride=ts), :] for t in range(ts)]
        for t in range(ts):
            pnew_ref[:, keep + t, g * POOL_GROUP_IN:(g + 1) * POOL_GROUP_IN] = u_steps[t]
        ds_ = []
        for t in range(ts):
            tot = None
            for j in range(w):
                idx = POOL_BUF + t - j
                if idx >= POOL_BUF:
                    term = u_steps[idx - POOL_BUF]
                else:
                    term = pb_ref[:, idx, g * POOL_GROUP_IN:(g + 1) * POOL_GROUP_IN]
                tot = term if tot is None else tot + term
            cnt = float(min(past + t + 1, w))
            ds_.append(tot / cnt - u_steps[t])
        y = _dot(jnp.concatenate(ds_, axis=0).astype(BF16),
                 wpool_ref[g * POOL_GROUP_IN:(g + 1) * POOL_GROUP_IN, :])
        y = y * pscale_ref[:, g * POOL_GROUP_OUT:(g + 1) * POOL_GROUP_OUT]
        for half in range(POOL_GROUP_OUT // LANES):
            blk = g * (POOL_GROUP_OUT // LANES) + half
            for t in range(ts):
                pblk_ref[blk, pl.ds(t, bb, stride=ts), :] = y[t * bb:(t + 1) * bb, half * LANES:(half + 1) * LANES]
    pool = jnp.concatenate([pblk_ref[blk] for blk in range(D_MODEL // LANES)], axis=1)

    o_all = intra + inter_ref[...]
    rets = []
    for h in range(N_RET_HEADS):
        hc = slice(h * RET_V_DIM, (h + 1) * RET_V_DIM)
        rets.append(_swish_gate_norm(o_all[:, hc], z_ref[:, G_OFF + h * RET_V_DIM:G_OFF + (h + 1) * RET_V_DIM]))
    ret = jnp.concatenate(rets, axis=1)
    o_ref[...] = _merge_out(x, z_ref, ret, pool, wout_ref)


def _sample_tables(ts, past, rows):
    lg = _log_gamma()
    i = jnp.arange(ts, dtype=F32)
    lag = jnp.exp(lg[:, None] * i[None, :])
    lag = jnp.repeat(lag.T, RET_V_DIM, axis=1)
    xi = jnp.exp(lg[:, None] * (i[None, :] + 1.0))
    wk = jnp.exp(lg[:, None] * (ts - 1.0 - i[None, :]))
    expand = lambda a: jnp.tile(jnp.repeat(a.T, RET_QK_DIM, axis=1), (2, 1))
    pos = (past + jnp.arange(ts, dtype=jnp.int32)).astype(F32)
    cos, sin = _rope_tables(pos)
    reps = rows // ts
    head_of_k = jnp.arange(QK_W) // RET_QK_DIM
    head_of_v = jnp.arange(V_W) // RET_V_DIM
    ind = (head_of_k[:, None] == head_of_v[None, :]).astype(BF16)
    return (jnp.tile(cos, (reps, 1)), jnp.tile(sin, (reps, 1)), expand(xi), expand(wk), lag,
            _state_decay(ts), ind)


def _mixer_sample(x2d, s0, pbuf, g, w_in, w_pool, pool_scale, w_out, tabs, layer, prev_states, *, bb, ts, past):
    n = x2d.shape[0]
    depth = s0.shape[0]
    nb = n // ts
    rows = bb * ts
    assert nb % bb == 0 and bb % 2 == 0 and (2 * ts) % 8 == 0 and ts <= POOL_BUF
    cos, sin, xi, wk, lag, gs, ind = tabs
    row_spec = pl.BlockSpec((rows, D_MODEL), lambda i: (i, 0))
    state_spec = pl.BlockSpec((None, bb, N_PAIRS * LANES, RET_V_DIM), lambda i: (layer, i, 0, 0))
    pool_spec = pl.BlockSpec((None, bb, POOL_BUF, POOL_IN), lambda i: (layer, i, 0, 0))
    in_specs = [
        row_spec, _layer_spec(g.shape, layer), _resident_spec(w_in.shape),
        _resident_spec(w_pool.shape), _layer_spec(pool_scale.shape, layer),
        _resident_spec(w_out.shape), _const_spec(cos.shape), _const_spec(sin.shape),
        _const_spec(xi.shape), _const_spec(wk.shape), _const_spec(lag.shape), _const_spec(gs.shape),
        _const_spec(ind.shape), state_spec, pool_spec,
    ]
    args = [x2d, g, w_in, w_pool, pool_scale, w_out, cos, sin, xi, wk, lag, gs, ind, s0, pbuf]
    extra, extra_specs, aliases = _state_alias(prev_states, len(args), 1)
    out_specs = [row_spec, state_spec, pool_spec]
    out_shape = [
        jax.ShapeDtypeStruct((n, D_MODEL), F32),
        jax.ShapeDtypeStruct((depth, nb, N_PAIRS * LANES, RET_V_DIM), F32),
        jax.ShapeDtypeStruct((depth, nb, POOL_BUF, POOL_IN), F32),
    ]
    scratch = [
        pltpu.VMEM((rows, D_IN), F32),
        pltpu.VMEM((rows, V_W), F32),
        pltpu.VMEM((N_POOL_GROUPS, rows, LANES), F32),
        pltpu.VMEM((D_MODEL // LANES, rows, LANES), F32),
    ]
    y, snew, pnew = pl.pallas_call(
        functools.partial(_mixer_sample_body, bb=bb, ts=ts, past=past, n_alias=len(extra)),
        grid=(nb // bb,),
        in_specs=in_specs + extra_specs,
        out_specs=out_specs,
        out_shape=out_shape,
        scratch_shapes=scratch,
        input_output_aliases=aliases,
        compiler_params=pltpu.CompilerParams(
            dimension_semantics=("arbitrary",), vmem_limit_bytes=VMEM_LIMIT),
        name="mixer_sample",
    )(*args, *extra)
    return y, (snew, pnew)


def kernel(x_prompt, x_sample, state_ret, state_pool, norm_ffn1, w_ffn1_in, w_ffn1_out, norm_mix,
           w_in, w_pool, pool_scale, w_out, norm_ffn2, w_ffn2_in, w_ffn2_out, norm_final):
    depth = w_in.shape[0]
    bp, tp, _ = x_prompt.shape
    bs, ts, _ = x_sample.shape
    wp = w_pool.reshape(depth, N_POOL_GROUPS * POOL_GROUP_IN, POOL_GROUP_OUT)

    row = lambda a: a.reshape(depth, 1, D_MODEL)
    n1, nm, n2, ps = map(row, (norm_ffn1, norm_mix, norm_ffn2, pool_scale))
    gf = norm_final.reshape(1, D_MODEL)
    last = depth - 1

    tm = 512
    tm_ffn = 1024
    pos_p = jnp.arange(tp, dtype=jnp.int32).astype(F32)
    tabs_p = _rope_tables(pos_p) + _decay_tables(RET_CHUNK)

    bb = 32
    tabs_s = _sample_tables(ts, PAST_LEN, bb * ts)
    s0_all = state_ret.reshape(depth, bs, N_PAIRS * LANES, RET_V_DIM)

    x = x_prompt.reshape(bp * tp, D_MODEL)
    xs = x_sample.reshape(bs * ts, D_MODEL)
    states_p = states_s = None
    for l in range(depth):
        x, xs = _ffn(x, xs, n1, w_ffn1_in, w_ffn1_out, l, tm=tm_ffn)
        x, states_p, (wi16, wp16, wo16) = _mixer_prompt(x.reshape(bp, tp, D_MODEL), nm, w_in, wp, ps, w_out,
                                                        tabs_p, l, states_p, tm=tm)
        xs, states_s = _mixer_sample(xs, s0_all, state_pool, nm, wi16, wp16, ps, wo16, tabs_s, l, states_s,
                                     bb=bb, ts=ts, past=PAST_LEN)
        x, xs = _ffn(x.reshape(bp * tp, D_MODEL), xs, n2, w_ffn2_in, w_ffn2_out, l,
                     gf if l == last else None, tm=tm_ffn)
    y_prompt = x.reshape(bp, tp, D_MODEL)
    y_sample = xs.reshape(bs, ts, D_MODEL)
    state_ret_prompt = states_p[0].reshape(depth, bp, N_RET_HEADS, RET_QK_DIM, RET_V_DIM)
    state_pool_prompt = states_p[1]
    state_ret_sample = states_s[0].reshape(depth, bs, N_RET_HEADS, RET_QK_DIM, RET_V_DIM)
    state_pool_sample = states_s[1]
    return (y_prompt, y_sample, state_ret_prompt, state_ret_sample, state_pool_prompt, state_pool_sample)
```

```python
import functools

import jax
import jax.numpy as jnp
from jax import lax
from jax.experimental import pallas as pl
from jax.experimental.pallas import tpu as pltpu

D_MODEL = 1024
N_RET_HEADS = 8
RET_QK_DIM = 64
RET_V_DIM = 128
RET_CHUNK = 128
ROPE_BASE = 10000.0
QK_W = N_RET_HEADS * RET_QK_DIM
V_W = N_RET_HEADS * RET_V_DIM
POOL_WINDOWS = (2, 4, 8, 16)
N_POOL_GROUPS = 4
POOL_IN = 512
POOL_GROUP_IN = 128
POOL_GROUP_OUT = 256
POOL_BUF = 15
D_IN = 2 * QK_W + 2 * V_W + POOL_IN + 2 * D_MODEL
D_FF = 2816
EPS = 1e-6
PAST_LEN = 16384

Q_OFF = 0
K_OFF = QK_W
V_OFF = 2 * QK_W
G_OFF = 2 * QK_W + V_W
U_OFF = 2 * QK_W + 2 * V_W
GA_OFF = U_OFF + POOL_IN
GB_OFF = GA_OFF + D_MODEL

LANES = 128
N_PAIRS = N_RET_HEADS // 2
POOL_PAD = 16
VMEM_LIMIT = 60 * 1024 * 1024

F32 = jnp.float32
BF16 = jnp.bfloat16


def _dot(a, b):
    return jnp.dot(a, b, preferred_element_type=F32)


def _rms(x, g):
    return x * lax.rsqrt(jnp.mean(x * x, axis=-1, keepdims=True) + EPS) * g


def _const_spec(shape):
    nd = len(shape)
    return pl.BlockSpec(shape, lambda *_: (0,) * nd)


FF_CHUNKS = ((0, 1024), (1024, 1024), (2048, 768))


STAGE_BUFS = 3
STAGE_BYTES = 3 * 512 * 1024


def _stage_rows(shape):
    r, c = shape
    fits = [n for n in range(16, r + 1, 16) if r % n == 0 and n * c * 4 <= STAGE_BYTES]
    return max(fits)


def _stage_scratch(shape):
    return [pltpu.VMEM(shape, BF16), pltpu.VMEM((STAGE_BUFS, _stage_rows(shape), shape[1]), F32),
            pltpu.SemaphoreType.DMA((STAGE_BUFS,))]


def _fetch_cast(src, dst, stage, sems):
    nbuf, rows, _ = stage.shape
    n = src.shape[0] // rows

    def copy(k):
        return pltpu.make_async_copy(src.at[pl.ds(k * rows, rows), :], stage.at[k % nbuf], sems.at[k % nbuf])

    for k in range(min(nbuf, n)):
        copy(k).start()
    for k in range(n):
        copy(k).wait()
        dst[k * rows:(k + 1) * rows, :] = stage[k % nbuf].astype(BF16)
        if k + nbuf < n:
            copy(k + nbuf).start()


def _convert_weights(first, layer, srcs, scratch):
    @pl.when(first)
    def _():
        for k, src in enumerate(srcs):
            _fetch_cast(src.at[layer], *scratch[3 * k:3 * k + 3])

    return scratch[0::3]


def _emit_weights(first, last, srcs, outs, sems):
    copies = [pltpu.make_async_copy(s, o, sems.at[k]) for k, (s, o) in enumerate(zip(srcs, outs))]

    @pl.when(first)
    def _():
        for c in copies:
            c.start()

    @pl.when(last)
    def _():
        for c in copies:
            c.wait()


def _ffn_body(*refs, final, layer):
    n_in = 6 if final else 5
    x_ref, xs_ref, g_ref, win_hbm, wout_hbm = refs[:5]
    gf_ref = refs[5] if final else None
    o_ref, os_ref = refs[n_in:n_in + 2]
    i = pl.program_id(0)
    win_ref, wout_ref = _convert_weights(i == 0, layer, (win_hbm, wout_hbm), refs[n_in + 2:])

    def ffn_rows(x):
        h = _rms(x, g_ref[...]).astype(BF16)
        acc = None
        for off, width in FF_CHUNKS:
            gate = _dot(h, win_ref[:, off:off + width])
            up = _dot(h, win_ref[:, D_FF + off:D_FF + off + width])
            a = (gate * jax.nn.sigmoid(gate) * up).astype(BF16)
            part = _dot(a, wout_ref[off:off + width, :])
            acc = part if acc is None else acc + part
        y = x + 0.5 * acc
        return _rms(y, gf_ref[...]) if final else y

    o_ref[...] = ffn_rows(x_ref[...])

    @pl.when(i == pl.num_programs(0) - 1)
    def _():
        os_ref[...] = ffn_rows(xs_ref[...])


def _layer_spec(shape, layer):
    nd = len(shape) - 1
    return pl.BlockSpec((None,) + tuple(shape[1:]), lambda *_: (layer,) + (0,) * nd,
                        pipeline_mode=pl.Buffered(1))


def _resident_spec(shape):
    nd = len(shape)
    return pl.BlockSpec(tuple(shape), lambda *_: (0,) * nd, pipeline_mode=pl.Buffered(1))


ANY_SPEC = pl.BlockSpec(memory_space=pl.ANY)


def _ffn(x2d, xs2d, g, w_in, w_out, layer, g_final=None, *, tm):
    n = x2d.shape[0]
    assert n % tm == 0
    final = g_final is not None
    row_spec = pl.BlockSpec((tm, D_MODEL), lambda i: (i, 0))
    in_specs = [row_spec, _resident_spec(xs2d.shape), _layer_spec(g.shape, layer), ANY_SPEC, ANY_SPEC]
    args = [x2d, xs2d, g, w_in, w_out]
    if final:
        in_specs.append(_const_spec((1, D_MODEL)))
        args.append(g_final.reshape(1, D_MODEL))
    scratch = _stage_scratch(w_in.shape[1:]) + _stage_scratch(w_out.shape[1:])
    return pl.pallas_call(
        functools.partial(_ffn_body, final=final, layer=layer),
        grid=(n // tm,),
        in_specs=in_specs,
        out_specs=[row_spec, _resident_spec(xs2d.shape)],
        out_shape=[jax.ShapeDtypeStruct(x2d.shape, F32), jax.ShapeDtypeStruct(xs2d.shape, F32)],
        scratch_shapes=scratch,
        compiler_params=pltpu.CompilerParams(
            dimension_semantics=("arbitrary",), vmem_limit_bytes=VMEM_LIMIT),
        name="ffn_final" if final else "ffn",
    )(*args)


IN_CHUNKS = tuple((o, 1024) for o in range(0, 5120, 1024)) + ((5120, 512),)


def _project_in(x, g_ref, win_ref, z_ref):
    h = _rms(x, g_ref[...]).astype(BF16)
    for off, width in IN_CHUNKS:
        z_ref[:, off:off + width] = _dot(h, win_ref[:, off:off + width])


def _rotate_inplace(z_ref, cos, sin_signed, first_half):
    for blk in range(2 * QK_W // LANES):
        cols = slice(blk * LANES, (blk + 1) * LANES)
        xb = z_ref[:, cols]
        partner = jnp.where(first_half, pltpu.roll(xb, LANES - 32, 1), pltpu.roll(xb, 32, 1))
        r = xb * cos + partner * sin_signed
        if blk >= QK_W // LANES:
            r = r * (RET_QK_DIM ** -0.5)
        z_ref[:, cols] = r


def _merge_out(x, z_ref, ret, pool, wout_ref):
    ga = z_ref[:, GA_OFF:GA_OFF + D_MODEL]
    gb = z_ref[:, GB_OFF:GB_OFF + D_MODEL]
    merged = jax.nn.sigmoid(ga) * ret + jax.nn.sigmoid(gb) * pool
    return x + _dot(merged.astype(BF16), wout_ref[...])


def _swish_gate_norm(o, gate):
    o = o * lax.rsqrt(jnp.mean(o * o, axis=-1, keepdims=True) + EPS)
    return o * (gate * jax.nn.sigmoid(gate))


def _mixer_prompt_body(*refs, tm, chunk, n_alias, layer):
    (x_ref, g_ref, win_ref, wpool_ref, pscale_ref, wout_ref, cos_ref, sin_ref,
     dmask_ref, xi_ref, wk_ref, gs_ref) = refs[:12]
    outs = refs[12 + n_alias:]
    o_ref, sret_ref, spool_ref = outs[:3]
    z_ref, s_ref, ext_ref, ret_ref, pool_ref, h_ref, d_ref = outs[6:13]
    t = pl.program_id(1)
    nt = pl.num_programs(1)
    bi = pl.program_id(0)
    first = (bi == 0) & (t == 0)
    win_ref, wpool_ref, wout_ref = _convert_weights(first, layer, (win_ref, wpool_ref, wout_ref), outs[13:22])
    _emit_weights(first, (bi == pl.num_programs(0) - 1) & (t == nt - 1),
                  (win_ref, wpool_ref, wout_ref), outs[3:6], outs[22])

    @pl.when(t == 0)
    def _():
        s_ref[...] = jnp.zeros_like(s_ref)
        ext_ref[0:POOL_PAD, :] = jnp.zeros((POOL_PAD, POOL_IN), F32)

    h_ref[...] = _rms(x_ref[...], g_ref[...]).astype(BF16)

    def project(off, width, act=None):
        y = _dot(h_ref[...], win_ref[:, off:off + width])
        z_ref[:, off:off + width] = y if act is None else act(y)

    project(Q_OFF, 2 * QK_W)
    project(V_OFF, V_W)
    project(U_OFF, POOL_IN)

    lane = lax.broadcasted_iota(jnp.int32, (1, LANES), 1)
    first_half = (lane % RET_QK_DIM) < (RET_QK_DIM // 2)
    _rotate_inplace(z_ref, cos_ref[...], sin_ref[...], first_half)

    ext_ref[POOL_PAD:POOL_PAD + tm, :] = z_ref[:, U_OFF:U_OFF + POOL_IN]
    pos = t * tm + lax.broadcasted_iota(jnp.int32, (tm, 1), 0)
    for g, w in enumerate(POOL_WINDOWS):
        cols = slice(g * POOL_GROUP_IN, (g + 1) * POOL_GROUP_IN)
        u = ext_ref[POOL_PAD:POOL_PAD + tm, cols]
        tot = u
        for j in range(1, w):
            tot = tot + ext_ref[POOL_PAD - j:POOL_PAD - j + tm, cols]
        cnt = jnp.minimum(pos + 1, w).astype(F32)
        d_ref[:, cols] = (tot / cnt - u).astype(BF16)
    tail = ext_ref[tm:tm + POOL_PAD, :]
    ext_ref[0:POOL_PAD, :] = tail

    def pool_matmul(g):
        ocols = slice(g * POOL_GROUP_OUT, (g + 1) * POOL_GROUP_OUT)
        y = _dot(d_ref[:, g * POOL_GROUP_IN:(g + 1) * POOL_GROUP_IN],
                 wpool_ref[g * POOL_GROUP_IN:(g + 1) * POOL_GROUP_IN, :])
        pool_ref[:, ocols] = y * pscale_ref[:, ocols]

    fill_width = 2 * LANES
    fillers = [functools.partial(project, off, fill_width, jax.nn.silu)
               for off in range(G_OFF, G_OFF + V_W, fill_width)]
    fillers += [functools.partial(project, off, fill_width, jax.nn.sigmoid)
                for off in range(GA_OFF, D_IN, fill_width)]
    fillers += [functools.partial(pool_matmul, g) for g in range(N_POOL_GROUPS)]

    head0 = (lane < RET_QK_DIM).astype(F32)
    head1 = 1.0 - head0

    n_chunks = tm // chunk
    stages = 3 * n_chunks

    def run_fillers(stage):
        for f in fillers[stage * len(fillers) // stages:(stage + 1) * len(fillers) // stages]:
            f()

    def pair_cols(off, pair, width):
        return slice(off + pair * width, off + (pair + 1) * width)

    for c in range(n_chunks):
        rows = slice(c * chunk, (c + 1) * chunk)
        scores = []
        for pair in range(N_PAIRS):
            qb = z_ref[rows, pair_cols(Q_OFF, pair, LANES)]
            kb = z_ref[rows, pair_cols(K_OFF, pair, LANES)]
            k_stack = jnp.concatenate([kb * head0, kb * head1], axis=0).astype(BF16)
            scores.append(lax.dot_general(qb.astype(BF16), k_stack, (((1,), (1,)), ((), ())),
                                          preferred_element_type=F32))
        run_fillers(3 * c)
        for pair in range(N_PAIRS):
            qx = z_ref[rows, pair_cols(Q_OFF, pair, LANES)] * xi_ref[pair]
            vb16 = z_ref[rows, pair_cols(V_OFF, pair, 2 * RET_V_DIM)].astype(BF16)
            s16 = s_ref[pair].astype(BF16)
            for j, hmask in enumerate((head0, head1)):
                h = 2 * pair + j
                scj = scores[pair][:, j * chunk:(j + 1) * chunk] * dmask_ref[h]
                lhs = jnp.concatenate([scj, qx * hmask], axis=1).astype(BF16)
                rhs = jnp.concatenate([vb16[:, j * RET_V_DIM:(j + 1) * RET_V_DIM], s16], axis=0)
                o = _dot(lhs, rhs)
                o = o * lax.rsqrt(jnp.mean(o * o, axis=-1, keepdims=True) + EPS)
                ret_ref[rows, h * RET_V_DIM:(h + 1) * RET_V_DIM] = o
        run_fillers(3 * c + 1)
        for pair in range(N_PAIRS):
            kw = (z_ref[rows, pair_cols(K_OFF, pair, LANES)] * wk_ref[pair]).astype(BF16)
            vb16 = z_ref[rows, pair_cols(V_OFF, pair, 2 * RET_V_DIM)].astype(BF16)
            upd = lax.dot_general(kw, vb16, (((0,), (0,)), ((), ())),
                                  preferred_element_type=F32)
            new_s = jnp.concatenate([upd[0:RET_QK_DIM, 0:RET_V_DIM],
                                     upd[RET_QK_DIM:, RET_V_DIM:]], axis=0)
            s_ref[pair] = s_ref[pair] * gs_ref[pair] + new_s
        run_fillers(3 * c + 2)

    n_halves = 2
    for r in range(n_halves):
        rows = slice(r * tm // n_halves, (r + 1) * tm // n_halves)
        ret = ret_ref[rows, :] * z_ref[rows, G_OFF:G_OFF + V_W]
        merged = (z_ref[rows, GA_OFF:GA_OFF + D_MODEL] * ret
                  + z_ref[rows, GB_OFF:GB_OFF + D_MODEL] * pool_ref[rows, :])
        o_ref[rows, :] = x_ref[rows, :] + _dot(merged.astype(BF16), wout_ref[...])

    @pl.when(t == nt - 1)
    def _():
        sret_ref[...] = s_ref[...].reshape(N_PAIRS * LANES, RET_V_DIM)
        spool_ref[...] = tail[POOL_PAD - POOL_BUF:, :]


def _state_alias(prev, n_in, first_out):
    if prev is None:
        return [], [], {}
    specs = [pl.BlockSpec(memory_space=pl.ANY) for _ in prev]
    aliases = {n_in + k: first_out + k for k in range(len(prev))}
    return list(prev), specs, aliases


def _mixer_prompt(x, g, w_in, w_pool, pool_scale, w_out, tabs, layer, prev_states, *, tm):
    b, t_len, _ = x.shape
    depth = w_in.shape[0]
    chunk = RET_CHUNK
    assert t_len % tm == 0 and tm % chunk == 0
    cos, sin, dmask, xi, wk, gs = tabs
    row_spec = pl.BlockSpec((None, tm, D_MODEL), lambda i, j: (i, j, 0))
    tab_spec = pl.BlockSpec((tm, LANES), lambda i, j: (j, 0))
    in_specs = [
        row_spec, _layer_spec(g.shape, layer), ANY_SPEC, ANY_SPEC, _layer_spec(pool_scale.shape, layer),
        ANY_SPEC, tab_spec, tab_spec,
        _const_spec(dmask.shape), _const_spec(xi.shape), _const_spec(wk.shape), _const_spec(gs.shape),
    ]
    args = [x, g, w_in, w_pool, pool_scale, w_out, cos, sin, dmask, xi, wk, gs]
    extra, extra_specs, aliases = _state_alias(prev_states, len(args), 1)
    w_shapes = [w_in.shape[1:], w_pool.shape[1:], w_out.shape[1:]]
    out_specs = [
        row_spec,
        pl.BlockSpec((None, None, N_PAIRS * LANES, RET_V_DIM), lambda i, j: (layer, i, 0, 0)),
        pl.BlockSpec((None, None, POOL_BUF, POOL_IN), lambda i, j: (layer, i, 0, 0)),
    ] + [ANY_SPEC] * len(w_shapes)
    out_shape = [
        jax.ShapeDtypeStruct((b, t_len, D_MODEL), F32),
        jax.ShapeDtypeStruct((depth, b, N_PAIRS * LANES, RET_V_DIM), F32),
        jax.ShapeDtypeStruct((depth, b, POOL_BUF, POOL_IN), F32),
    ] + [jax.ShapeDtypeStruct(s, BF16) for s in w_shapes]
    scratch = [
        pltpu.VMEM((tm, D_IN), F32),
        pltpu.VMEM((N_PAIRS, LANES, RET_V_DIM), F32),
        pltpu.VMEM((POOL_PAD + tm, POOL_IN), F32),
        pltpu.VMEM((tm, V_W), F32),
        pltpu.VMEM((tm, D_MODEL), F32),
        pltpu.VMEM((tm, D_MODEL), BF16),
        pltpu.VMEM((tm, POOL_IN), BF16),
    ]
    for s in w_shapes:
        scratch += _stage_scratch(s)
    scratch.append(pltpu.SemaphoreType.DMA((len(w_shapes),)))
    y, sret, spool, *w16 = pl.pallas_call(
        functools.partial(_mixer_prompt_body, tm=tm, chunk=chunk, n_alias=len(extra), layer=layer),
        grid=(b, t_len // tm),
        in_specs=in_specs + extra_specs,
        out_specs=out_specs,
        out_shape=out_shape,
        scratch_shapes=scratch,
        input_output_aliases=aliases,
        compiler_params=pltpu.CompilerParams(
            dimension_semantics=("arbitrary", "arbitrary"), vmem_limit_bytes=VMEM_LIMIT),
        name="mixer_prompt",
    )(*args, *extra)
    return y, (sret, spool), tuple(w16)


def _log_gamma():
    return jnp.log1p(-jnp.exp2(-5.0 - jnp.arange(N_RET_HEADS, dtype=F32)))


def _rope_tables(pos):
    half = RET_QK_DIM // 2
    inv = jnp.power(jnp.float32(ROPE_BASE), -jnp.linspace(0.0, 1.0, half, dtype=F32))
    ang = pos[:, None] * inv[None, :]
    cos, sin = jnp.cos(ang), jnp.sin(ang)
    cos128 = jnp.tile(cos, (1, LANES // half))
    sin128 = jnp.tile(jnp.concatenate([-sin, sin], axis=1), (1, LANES // RET_QK_DIM))
    return cos128, sin128


def _pair_lanes(per_head):
    h, c = per_head.shape
    x = jnp.broadcast_to(per_head[:, :, None], (h, c, RET_QK_DIM))
    return x.reshape(N_PAIRS, 2, c, RET_QK_DIM).transpose(0, 2, 1, 3).reshape(N_PAIRS, c, LANES)


def _state_decay(chunk):
    g_c = jnp.exp(_log_gamma() * chunk)
    gs = jnp.broadcast_to(g_c[:, None, None], (N_RET_HEADS, RET_QK_DIM, RET_V_DIM))
    return gs.reshape(N_PAIRS, LANES, RET_V_DIM)


def _decay_tables(chunk):
    lg = _log_gamma()
    i = jnp.arange(chunk, dtype=F32)
    dist = i[:, None] - i[None, :]
    dmask = jnp.where(dist[None] >= 0, jnp.exp(lg[:, None, None] * jnp.maximum(dist, 0.0)[None]), 0.0)
    xi = jnp.exp(lg[:, None] * (i[None, :] + 1.0))
    wk = jnp.exp(lg[:, None] * (chunk - 1.0 - i[None, :]))
    return dmask, _pair_lanes(xi), _pair_lanes(wk), _state_decay(chunk)


def _mixer_sample_body(*refs, bb, ts, past, n_alias):
    (x_ref, g_ref, win_ref, wpool_ref, pscale_ref, wout_ref, cos_ref, sin_ref,
     xi_ref, wk_ref, lag_ref, gs_ref, ind_ref, s0_ref, pb_ref) = refs[:15]
    (o_ref, snew_ref, pnew_ref, z_ref, inter_ref, ublk_ref, pblk_ref) = refs[15 + n_alias:]
    rows = bb * ts
    x = x_ref[...]
    _project_in(x, g_ref, win_ref, z_ref)
    lane = lax.broadcasted_iota(jnp.int32, (1, LANES), 1)
    first_half = (lane % RET_QK_DIM) < (RET_QK_DIM // 2)
    _rotate_inplace(z_ref, cos_ref[...], sin_ref[...], first_half)

    q = z_ref[:, Q_OFF:Q_OFF + QK_W]
    k = z_ref[:, K_OFF:K_OFF + QK_W]
    v = z_ref[:, V_OFF:V_OFF + V_W]
    step = lax.broadcasted_iota(jnp.int32, (rows, 1), 0) % ts
    intra = None
    for d in range(ts):
        kd = k if d == 0 else pltpu.roll(k, d, 0)
        vd = v if d == 0 else pltpu.roll(v, d, 0)
        prod = jnp.where(step >= d, q * kd, 0.0).astype(BF16)
        score = _dot(prod, ind_ref[...])
        term = score * lag_ref[d:d + 1, :] * jnp.where(step >= d, vd, 0.0)
        intra = term if intra is None else intra + term

    row8 = lax.broadcasted_iota(jnp.int32, (2 * ts, 1), 0)
    top = (lax.broadcasted_iota(jnp.int32, (LANES, 1), 0) < RET_QK_DIM).astype(F32)
    bot = 1.0 - top

    def seq_pair(p, carry):
        r0 = pl.multiple_of(p * 2 * ts, 2 * ts)
        rsl = pl.ds(r0, 2 * ts)
        for pair in range(N_PAIRS):
            lanes = slice(pair * LANES, (pair + 1) * LANES)
            q8 = (z_ref[rsl, Q_OFF + pair * LANES:Q_OFF + (pair + 1) * LANES] * xi_ref[:, lanes]).astype(BF16)
            kw8 = z_ref[rsl, K_OFF + pair * LANES:K_OFF + (pair + 1) * LANES] * wk_ref[:, lanes]
            v8 = z_ref[rsl, V_OFF + pair * 2 * RET_V_DIM:V_OFF + (pair + 1) * 2 * RET_V_DIM].astype(BF16)
            outs = []
            for j in range(2):
                b = 2 * p + j
                s_pair = s0_ref[b, lanes, :]
                bd = jnp.concatenate([s_pair * top, s_pair * bot], axis=1).astype(BF16)
                outs.append(_dot(q8, bd))
                mine = (row8 >= j * ts) & (row8 < (j + 1) * ts)
                kwj = jnp.where(mine, kw8, 0.0).astype(BF16)
                upd = lax.dot_general(kwj, v8, (((0,), (0,)), ((), ())), preferred_element_type=F32)
                new_s = jnp.concatenate([upd[0:RET_QK_DIM, 0:RET_V_DIM],
                                         upd[RET_QK_DIM:, RET_V_DIM:]], axis=0)
                snew_ref[b, lanes, :] = s_pair * gs_ref[pair] + new_s
            inter_ref[rsl, pair * 2 * RET_V_DIM:(pair + 1) * 2 * RET_V_DIM] = jnp.where(
                row8 < ts, outs[0], outs[1])
        return carry

    lax.fori_loop(0, bb // 2, seq_pair, 0)

    keep = POOL_BUF - ts
    pnew_ref[:, 0:keep, :] = pb_ref[:, ts:, :]
    for g, w in enumerate(POOL_WINDOWS):
        ublk_ref[g] = z_ref[:, U_OFF + g * POOL_GROUP_IN:U_OFF + (g + 1) * POOL_GROUP_IN]
        u_steps = [ublk_ref[g, pl.ds(t, bb, stride=ts), :] for t in range(ts)]
        for t in range(ts):
            pnew_ref[:, keep + t, g * POOL_GROUP_IN:(g + 1) * POOL_GROUP_IN] = u_steps[t]
        ds_ = []
        for t in range(ts):
            tot = None
            for j in range(w):
                idx = POOL_BUF + t - j
                if idx >= POOL_BUF:
                    term = u_steps[idx - POOL_BUF]
                else:
                    term = pb_ref[:, idx, g * POOL_GROUP_IN:(g + 1) * POOL_GROUP_IN]
                tot = term if tot is None else tot + term
            cnt = float(min(past + t + 1, w))
            ds_.append(tot / cnt - u_steps[t])
        y = _dot(jnp.concatenate(ds_, axis=0).astype(BF16),
                 wpool_ref[g * POOL_GROUP_IN:(g + 1) * POOL_GROUP_IN, :])
        y = y * pscale_ref[:, g * POOL_GROUP_OUT:(g + 1) * POOL_GROUP_OUT]
        for half in range(POOL_GROUP_OUT // LANES):
            blk = g * (POOL_GROUP_OUT // LANES) + half
            for t in range(ts):
                pblk_ref[blk, pl.ds(t, bb, stride=ts), :] = y[t * bb:(t + 1) * bb, half * LANES:(half + 1) * LANES]
    pool = jnp.concatenate([pblk_ref[blk] for blk in range(D_MODEL // LANES)], axis=1)

    o_all = intra + inter_ref[...]
    rets = []
    for h in range(N_RET_HEADS):
        hc = slice(h * RET_V_DIM, (h + 1) * RET_V_DIM)
        rets.append(_swish_gate_norm(o_all[:, hc], z_ref[:, G_OFF + h * RET_V_DIM:G_OFF + (h + 1) * RET_V_DIM]))
    ret = jnp.concatenate(rets, axis=1)
    o_ref[...] = _merge_out(x, z_ref, ret, pool, wout_ref)


def _sample_tables(ts, past, rows):
    lg = _log_gamma()
    i = jnp.arange(ts, dtype=F32)
    lag = jnp.exp(lg[:, None] * i[None, :])
    lag = jnp.repeat(lag.T, RET_V_DIM, axis=1)
    xi = jnp.exp(lg[:, None] * (i[None, :] + 1.0))
    wk = jnp.exp(lg[:, None] * (ts - 1.0 - i[None, :]))
    expand = lambda a: jnp.tile(jnp.repeat(a.T, RET_QK_DIM, axis=1), (2, 1))
    pos = (past + jnp.arange(ts, dtype=jnp.int32)).astype(F32)
    cos, sin = _rope_tables(pos)
    reps = rows // ts
    head_of_k = jnp.arange(QK_W) // RET_QK_DIM
    head_of_v = jnp.arange(V_W) // RET_V_DIM
    ind = (head_of_k[:, None] == head_of_v[None, :]).astype(BF16)
    return (jnp.tile(cos, (reps, 1)), jnp.tile(sin, (reps, 1)), expand(xi), expand(wk), lag,
            _state_decay(ts), ind)


def _mixer_sample(x2d, s0, pbuf, g, w_in, w_pool, pool_scale, w_out, tabs, layer, prev_states, *, bb, ts, past):
    n = x2d.shape[0]
    depth = s0.shape[0]
    nb = n // ts
    rows = bb * ts
    assert nb % bb == 0 and bb % 2 == 0 and (2 * ts) % 8 == 0 and ts <= POOL_BUF
    cos, sin, xi, wk, lag, gs, ind = tabs
    row_spec = pl.BlockSpec((rows, D_MODEL), lambda i: (i, 0))
    state_spec = pl.BlockSpec((None, bb, N_PAIRS * LANES, RET_V_DIM), lambda i: (layer, i, 0, 0))
    pool_spec = pl.BlockSpec((None, bb, POOL_BUF, POOL_IN), lambda i: (layer, i, 0, 0))
    in_specs = [
        row_spec, _layer_spec(g.shape, layer), _resident_spec(w_in.shape),
        _resident_spec(w_pool.shape), _layer_spec(pool_scale.shape, layer),
        _resident_spec(w_out.shape), _const_spec(cos.shape), _const_spec(sin.shape),
        _const_spec(xi.shape), _const_spec(wk.shape), _const_spec(lag.shape), _const_spec(gs.shape),
        _const_spec(ind.shape), state_spec, pool_spec,
    ]
    args = [x2d, g, w_in, w_pool, pool_scale, w_out, cos, sin, xi, wk, lag, gs, ind, s0, pbuf]
    extra, extra_specs, aliases = _state_alias(prev_states, len(args), 1)
    out_specs = [row_spec, state_spec, pool_spec]
    out_shape = [
        jax.ShapeDtypeStruct((n, D_MODEL), F32),
        jax.ShapeDtypeStruct((depth, nb, N_PAIRS * LANES, RET_V_DIM), F32),
        jax.ShapeDtypeStruct((depth, nb, POOL_BUF, POOL_IN), F32),
    ]
    scratch = [
        pltpu.VMEM((rows, D_IN), F32),
        pltpu.VMEM((rows, V_W), F32),
        pltpu.VMEM((N_POOL_GROUPS, rows, LANES), F32),
        pltpu.VMEM((D_MODEL // LANES, rows, LANES), F32),
    ]
    y, snew, pnew = pl.pallas_call(
        functools.partial(_mixer_sample_body, bb=bb, ts=ts, past=past, n_alias=len(extra)),
        grid=(nb // bb,),
        in_specs=in_specs + extra_specs,
        out_specs=out_specs,
        out_shape=out_shape,
        scratch_shapes=scratch,
        input_output_aliases=aliases,
        compiler_params=pltpu.CompilerParams(
            dimension_semantics=("arbitrary",), vmem_limit_bytes=VMEM_LIMIT),
        name="mixer_sample",
    )(*args, *extra)
    return y, (snew, pnew)


def kernel(x_prompt, x_sample, state_ret, state_pool, norm_ffn1, w_ffn1_in, w_ffn1_out, norm_mix,
           w_in, w_pool, pool_scale, w_out, norm_ffn2, w_ffn2_in, w_ffn2_out, norm_final):
    depth = w_in.shape[0]
    bp, tp, _ = x_prompt.shape
    bs, ts, _ = x_sample.shape
    wp = w_pool.reshape(depth, N_POOL_GROUPS * POOL_GROUP_IN, POOL_GROUP_OUT)

    row = lambda a: a.reshape(depth, 1, D_MODEL)
    n1, nm, n2, ps = map(row, (norm_ffn1, norm_mix, norm_ffn2, pool_scale))
    gf = norm_final.reshape(1, D_MODEL)
    last = depth - 1

    tm = 512
    tm_ffn = 1024
    pos_p = jnp.arange(tp, dtype=jnp.int32).astype(F32)
    tabs_p = _rope_tables(pos_p) + _decay_tables(RET_CHUNK)

    bb = 32
    tabs_s = _sample_tables(ts, PAST_LEN, bb * ts)
    s0_all = state_ret.reshape(depth, bs, N_PAIRS * LANES, RET_V_DIM)

    x = x_prompt.reshape(bp * tp, D_MODEL)
    xs = x_sample.reshape(bs * ts, D_MODEL)
    states_p = states_s = None
    for l in range(depth):
        x, xs = _ffn(x, xs, n1, w_ffn1_in, w_ffn1_out, l, tm=tm_ffn)
        x, states_p, (wi16, wp16, wo16) = _mixer_prompt(x.reshape(bp, tp, D_MODEL), nm, w_in, wp, ps, w_out,
                                                        tabs_p, l, states_p, tm=tm)
        xs, states_s = _mixer_sample(xs, s0_all, state_pool, nm, wi16, wp16, ps, wo16, tabs_s, l, states_s,
                                     bb=bb, ts=ts, past=PAST_LEN)
        x, xs = _ffn(x.reshape(bp * tp, D_MODEL), xs, n2, w_ffn2_in, w_ffn2_out, l,
                     gf if l == last else None, tm=tm_ffn)
    y_prompt = x.reshape(bp, tp, D_MODEL)
    y_sample = xs.reshape(bs, ts, D_MODEL)
    state_ret_prompt = states_p[0].reshape(depth, bp, N_RET_HEADS, RET_QK_DIM, RET_V_DIM)
    state_pool_prompt = states_p[1]
    state_ret_sample = states_s[0].reshape(depth, bs, N_RET_HEADS, RET_QK_DIM, RET_V_DIM)
    state_pool_sample = states_s[1]
    return (y_prompt, y_sample, state_ret_prompt, state_ret_sample, state_pool_prompt, state_pool_sample)
```

```python
import functools

import jax
import jax.numpy as jnp
from jax import lax
from jax.experimental import pallas as pl
from jax.experimental.pallas import tpu as pltpu

D_MODEL = 1024
N_RET_HEADS = 8
RET_QK_DIM = 64
RET_V_DIM = 128
RET_CHUNK = 128
ROPE_BASE = 10000.0
QK_W = N_RET_HEADS * RET_QK_DIM
V_W = N_RET_HEADS * RET_V_DIM
POOL_WINDOWS = (2, 4, 8, 16)
N_POOL_GROUPS = 4
POOL_IN = 512
POOL_GROUP_IN = 128
POOL_GROUP_OUT = 256
POOL_BUF = 15
D_IN = 2 * QK_W + 2 * V_W + POOL_IN + 2 * D_MODEL
D_FF = 2816
EPS = 1e-6
PAST_LEN = 16384

Q_OFF = 0
K_OFF = QK_W
V_OFF = 2 * QK_W
G_OFF = 2 * QK_W + V_W
U_OFF = 2 * QK_W + 2 * V_W
GA_OFF = U_OFF + POOL_IN
GB_OFF = GA_OFF + D_MODEL

LANES = 128
N_PAIRS = N_RET_HEADS // 2
POOL_PAD = 16
VMEM_LIMIT = 60 * 1024 * 1024

F32 = jnp.float32
BF16 = jnp.bfloat16


def _dot(a, b):
    return jnp.dot(a, b, preferred_element_type=F32)


def _rms(x, g):
    return x * lax.rsqrt(jnp.mean(x * x, axis=-1, keepdims=True) + EPS) * g


def _const_spec(shape):
    nd = len(shape)
    return pl.BlockSpec(shape, lambda *_: (0,) * nd)


FF_CHUNKS = ((0, 1024), (1024, 1024), (2048, 768))


STAGE_BUFS = 3
STAGE_BYTES = 3 * 512 * 1024


def _stage_rows(shape):
    r, c = shape
    fits = [n for n in range(16, r + 1, 16) if r % n == 0 and n * c * 4 <= STAGE_BYTES]
    return max(fits)


def _stage_scratch(shape):
    return [pltpu.VMEM(shape, BF16), pltpu.VMEM((STAGE_BUFS, _stage_rows(shape), shape[1]), F32),
            pltpu.SemaphoreType.DMA((STAGE_BUFS,))]


def _fetch_cast(src, dst, stage, sems):
    nbuf, rows, _ = stage.shape
    n = src.shape[0] // rows

    def copy(k):
        return pltpu.make_async_copy(src.at[pl.ds(k * rows, rows), :], stage.at[k % nbuf], sems.at[k % nbuf])

    for k in range(min(nbuf, n)):
        copy(k).start()
    for k in range(n):
        copy(k).wait()
        dst[k * rows:(k + 1) * rows, :] = stage[k % nbuf].astype(BF16)
        if k + nbuf < n:
            copy(k + nbuf).start()


def _convert_weights(first, layer, srcs, scratch):
    @pl.when(first)
    def _():
        for k, src in enumerate(srcs):
            _fetch_cast(src.at[layer], *scratch[3 * k:3 * k + 3])

    return scratch[0::3]


def _emit_weights(first, last, srcs, outs, sems):
    copies = [pltpu.make_async_copy(s, o, sems.at[k]) for k, (s, o) in enumerate(zip(srcs, outs))]

    @pl.when(first)
    def _():
        for c in copies:
            c.start()

    @pl.when(last)
    def _():
        for c in copies:
            c.wait()


def _ffn_body(*refs, final, layer):
    n_in = 6 if final else 5
    x_ref, xs_ref, g_ref, win_hbm, wout_hbm = refs[:5]
    gf_ref = refs[5] if final else None
    o_ref, os_ref = refs[n_in:n_in + 2]
    i = pl.program_id(0)
    win_ref, wout_ref = _convert_weights(i == 0, layer, (win_hbm, wout_hbm), refs[n_in + 2:])

    def ffn_rows(x):
        h = _rms(x, g_ref[...]).astype(BF16)
        acc = None
        for off, width in FF_CHUNKS:
            gate = _dot(h, win_ref[:, off:off + width])
            up = _dot(h, win_ref[:, D_FF + off:D_FF + off + width])
            a = (gate * jax.nn.sigmoid(gate) * up).astype(BF16)
            part = _dot(a, wout_ref[off:off + width, :])
            acc = part if acc is None else acc + part
        y = x + 0.5 * acc
        return _rms(y, gf_ref[...]) if final else y

    o_ref[...] = ffn_rows(x_ref[...])

    @pl.when(i == pl.num_programs(0) - 1)
    def _():
        os_ref[...] = ffn_rows(xs_ref[...])


def _layer_spec(shape, layer):
    nd = len(shape) - 1
    return pl.BlockSpec((None,) + tuple(shape[1:]), lambda *_: (layer,) + (0,) * nd,
                        pipeline_mode=pl.Buffered(1))


def _resident_spec(shape):
    nd = len(shape)
    return pl.BlockSpec(tuple(shape), lambda *_: (0,) * nd, pipeline_mode=pl.Buffered(1))


ANY_SPEC = pl.BlockSpec(memory_space=pl.ANY)


def _ffn(x2d, xs2d, g, w_in, w_out, layer, g_final=None, *, tm):
    n = x2d.shape[0]
    assert n % tm == 0
    final = g_final is not None
    row_spec = pl.BlockSpec((tm, D_MODEL), lambda i: (i, 0))
    in_specs = [row_spec, _resident_spec(xs2d.shape), _layer_spec(g.shape, layer), ANY_SPEC, ANY_SPEC]
    args = [x2d, xs2d, g, w_in, w_out]
    if final:
        in_specs.append(_const_spec((1, D_MODEL)))
        args.append(g_final.reshape(1, D_MODEL))
    scratch = _stage_scratch(w_in.shape[1:]) + _stage_scratch(w_out.shape[1:])
    return pl.pallas_call(
        functools.partial(_ffn_body, final=final, layer=layer),
        grid=(n // tm,),
        in_specs=in_specs,
        out_specs=[row_spec, _resident_spec(xs2d.shape)],
        out_shape=[jax.ShapeDtypeStruct(x2d.shape, F32), jax.ShapeDtypeStruct(xs2d.shape, F32)],
        scratch_shapes=scratch,
        compiler_params=pltpu.CompilerParams(
            dimension_semantics=("arbitrary",), vmem_limit_bytes=VMEM_LIMIT),
        name="ffn_final" if final else "ffn",
    )(*args)


IN_CHUNKS = tuple((o, 1024) for o in range(0, 5120, 1024)) + ((5120, 512),)


def _project_in(x, g_ref, win_ref, z_ref):
    h = _rms(x, g_ref[...]).astype(BF16)
    for off, width in IN_CHUNKS:
        z_ref[:, off:off + width] = _dot(h, win_ref[:, off:off + width])


def _rotate_inplace(z_ref, cos, sin_signed, first_half):
    for blk in range(2 * QK_W // LANES):
        cols = slice(blk * LANES, (blk + 1) * LANES)
        xb = z_ref[:, cols]
        partner = jnp.where(first_half, pltpu.roll(xb, LANES - 32, 1), pltpu.roll(xb, 32, 1))
        r = xb * cos + partner * sin_signed
        if blk >= QK_W // LANES:
            r = r * (RET_QK_DIM ** -0.5)
        z_ref[:, cols] = r


def _merge_out(x, z_ref, ret, pool, wout_ref):
    ga = z_ref[:, GA_OFF:GA_OFF + D_MODEL]
    gb = z_ref[:, GB_OFF:GB_OFF + D_MODEL]
    merged = jax.nn.sigmoid(ga) * ret + jax.nn.sigmoid(gb) * pool
    return x + _dot(merged.astype(BF16), wout_ref[...])


def _swish_gate_norm(o, gate):
    o = o * lax.rsqrt(jnp.mean(o * o, axis=-1, keepdims=True) + EPS)
    return o * (gate * jax.nn.sigmoid(gate))


def _mixer_prompt_body(*refs, tm, chunk, n_alias, layer):
    (x_ref, g_ref, win_ref, wpool_ref, pscale_ref, wout_ref, cos_ref, sin_ref,
     dmask_ref, xi_ref, wk_ref, gs_ref) = refs[:12]
    outs = refs[12 + n_alias:]
    o_ref, sret_ref, spool_ref = outs[:3]
    z_ref, s_ref, ext_ref, ret_ref, pool_ref, h_ref, d_ref = outs[6:13]
    t = pl.program_id(1)
    nt = pl.num_programs(1)
    bi = pl.program_id(0)
    first = (bi == 0) & (t == 0)
    win_ref, wpool_ref, wout_ref = _convert_weights(first, layer, (win_ref, wpool_ref, wout_ref), outs[13:22])
    _emit_weights(first, (bi == pl.num_programs(0) - 1) & (t == nt - 1),
                  (win_ref, wpool_ref, wout_ref), outs[3:6], outs[22])

    @pl.when(t == 0)
    def _():
        s_ref[...] = jnp.zeros_like(s_ref)
        ext_ref[0:POOL_PAD, :] = jnp.zeros((POOL_PAD, POOL_IN), F32)

    h_ref[...] = _rms(x_ref[...], g_ref[...]).astype(BF16)

    def project(off, width, act=None):
        y = _dot(h_ref[...], win_ref[:, off:off + width])
        z_ref[:, off:off + width] = y if act is None else act(y)

    project(Q_OFF, 2 * QK_W)
    project(V_OFF, V_W)
    project(U_OFF, POOL_IN)

    lane = lax.broadcasted_iota(jnp.int32, (1, LANES), 1)
    first_half = (lane % RET_QK_DIM) < (RET_QK_DIM // 2)
    _rotate_inplace(z_ref, cos_ref[...], sin_ref[...], first_half)

    ext_ref[POOL_PAD:POOL_PAD + tm, :] = z_ref[:, U_OFF:U_OFF + POOL_IN]
    pos = t * tm + lax.broadcasted_iota(jnp.int32, (tm, 1), 0)
    for g, w in enumerate(POOL_WINDOWS):
        cols = slice(g * POOL_GROUP_IN, (g + 1) * POOL_GROUP_IN)
        u = ext_ref[POOL_PAD:POOL_PAD + tm, cols]
        tot = u
        for j in range(1, w):
            tot = tot + ext_ref[POOL_PAD - j:POOL_PAD - j + tm, cols]
        cnt = jnp.minimum(pos + 1, w).astype(F32)
        d_ref[:, cols] = (tot / cnt - u).astype(BF16)
    tail = ext_ref[tm:tm + POOL_PAD, :]
    ext_ref[0:POOL_PAD, :] = tail

    def pool_matmul(g):
        ocols = slice(g * POOL_GROUP_OUT, (g + 1) * POOL_GROUP_OUT)
        y = _dot(d_ref[:, g * POOL_GROUP_IN:(g + 1) * POOL_GROUP_IN],
                 wpool_ref[g * POOL_GROUP_IN:(g + 1) * POOL_GROUP_IN, :])
        pool_ref[:, ocols] = y * pscale_ref[:, ocols]

    fill_width = 2 * LANES
    fillers = [functools.partial(project, off, fill_width, jax.nn.silu)
               for off in range(G_OFF, G_OFF + V_W, fill_width)]
    fillers += [functools.partial(project, off, fill_width, jax.nn.sigmoid)
                for off in range(GA_OFF, D_IN, fill_width)]
    fillers += [functools.partial(pool_matmul, g) for g in range(N_POOL_GROUPS)]

    head0 = (lane < RET_QK_DIM).astype(F32)
    head1 = 1.0 - head0

    n_chunks = tm // chunk
    stages = 3 * n_chunks

    def run_fillers(stage):
        for f in fillers[stage * len(fillers) // stages:(stage + 1) * len(fillers) // stages]:
            f()

    def pair_cols(off, pair, width):
        return slice(off + pair * width, off + (pair + 1) * width)

    for c in range(n_chunks):
        rows = slice(c * chunk, (c + 1) * chunk)
        scores = []
        for pair in range(N_PAIRS):
            qb = z_ref[rows, pair_cols(Q_OFF, pair, LANES)]
            kb = z_ref[rows, pair_cols(K_OFF, pair, LANES)]
            k_stack = jnp.concatenate([kb * head0, kb * head1], axis=0).astype(BF16)
            scores.append(lax.dot_general(qb.astype(BF16), k_stack, (((1,), (1,)), ((), ())),
                                          preferred_element_type=F32))
        run_fillers(3 * c)
        for pair in range(N_PAIRS):
            qx = z_ref[rows, pair_cols(Q_OFF, pair, LANES)] * xi_ref[pair]
            vb16 = z_ref[rows, pair_cols(V_OFF, pair, 2 * RET_V_DIM)].astype(BF16)
            s16 = s_ref[pair].astype(BF16)
            for j, hmask in enumerate((head0, head1)):
                h = 2 * pair + j
                scj = scores[pair][:, j * chunk:(j + 1) * chunk] * dmask_ref[h]
                lhs = jnp.concatenate([scj, qx * hmask], axis=1).astype(BF16)
                rhs = jnp.concatenate([vb16[:, j * RET_V_DIM:(j + 1) * RET_V_DIM], s16], axis=0)
                o = _dot(lhs, rhs)
                o = o * lax.rsqrt(jnp.mean(o * o, axis=-1, keepdims=True) + EPS)
                ret_ref[rows, h * RET_V_DIM:(h + 1) * RET_V_DIM] = o
        run_fillers(3 * c + 1)
        for pair in range(N_PAIRS):
            kw = (z_ref[rows, pair_cols(K_OFF, pair, LANES)] * wk_ref[pair]).astype(BF16)
            vb16 = z_ref[rows, pair_cols(V_OFF, pair, 2 * RET_V_DIM)].astype(BF16)
            upd = lax.dot_general(kw, vb16, (((0,), (0,)), ((), ())),
                                  preferred_element_type=F32)
            new_s = jnp.concatenate([upd[0:RET_QK_DIM, 0:RET_V_DIM],
                                     upd[RET_QK_DIM:, RET_V_DIM:]], axis=0)
            s_ref[pair] = s_ref[pair] * gs_ref[pair] + new_s
        run_fillers(3 * c + 2)

    n_halves = 2
    for r in range(n_halves):
        rows = slice(r * tm // n_halves, (r + 1) * tm // n_halves)
        ret = ret_ref[rows, :] * z_ref[rows, G_OFF:G_OFF + V_W]
        merged = (z_ref[rows, GA_OFF:GA_OFF + D_MODEL] * ret
                  + z_ref[rows, GB_OFF:GB_OFF + D_MODEL] * pool_ref[rows, :])
        o_ref[rows, :] = x_ref[rows, :] + _dot(merged.astype(BF16), wout_ref[...])

    @pl.when(t == nt - 1)
    def _():
        sret_ref[...] = s_ref[...].reshape(N_PAIRS * LANES, RET_V_DIM)
        spool_ref[...] = tail[POOL_PAD - POOL_BUF:, :]


def _state_alias(prev, n_in, first_out):
    if prev is None:
        return [], [], {}
    specs = [pl.BlockSpec(memory_space=pl.ANY) for _ in prev]
    aliases = {n_in + k: first_out + k for k in range(len(prev))}
    return list(prev), specs, aliases


def _mixer_prompt(x, g, w_in, w_pool, pool_scale, w_out, tabs, layer, prev_states, *, tm):
    b, t_len, _ = x.shape
    depth = w_in.shape[0]
    chunk = RET_CHUNK
    assert t_len % tm == 0 and tm % chunk == 0
    cos, sin, dmask, xi, wk, gs = tabs
    row_spec = pl.BlockSpec((None, tm, D_MODEL), lambda i, j: (i, j, 0))
    tab_spec = pl.BlockSpec((tm, LANES), lambda i, j: (j, 0))
    in_specs = [
        row_spec, _layer_spec(g.shape, layer), ANY_SPEC, ANY_SPEC, _layer_spec(pool_scale.shape, layer),
        ANY_SPEC, tab_spec, tab_spec,
        _const_spec(dmask.shape), _const_spec(xi.shape), _const_spec(wk.shape), _const_spec(gs.shape),
    ]
    args = [x, g, w_in, w_pool, pool_scale, w_out, cos, sin, dmask, xi, wk, gs]
    extra, extra_specs, aliases = _state_alias(prev_states, len(args), 1)
    w_shapes = [w_in.shape[1:], w_pool.shape[1:], w_out.shape[1:]]
    out_specs = [
        row_spec,
        pl.BlockSpec((None, None, N_PAIRS * LANES, RET_V_DIM), lambda i, j: (layer, i, 0, 0)),
        pl.BlockSpec((None, None, POOL_BUF, POOL_IN), lambda i, j: (layer, i, 0, 0)),
    ] + [ANY_SPEC] * len(w_shapes)
    out_shape = [
        jax.ShapeDtypeStruct((b, t_len, D_MODEL), F32),
        jax.ShapeDtypeStruct((depth, b, N_PAIRS * LANES, RET_V_DIM), F32),
        jax.ShapeDtypeStruct((depth, b, POOL_BUF, POOL_IN), F32),
    ] + [jax.ShapeDtypeStruct(s, BF16) for s in w_shapes]
    scratch = [
        pltpu.VMEM((tm, D_IN), F32),
        pltpu.VMEM((N_PAIRS, LANES, RET_V_DIM), F32),
        pltpu.VMEM((POOL_PAD + tm, POOL_IN), F32),
        pltpu.VMEM((tm, V_W), F32),
        pltpu.VMEM((tm, D_MODEL), F32),
        pltpu.VMEM((tm, D_MODEL), BF16),
        pltpu.VMEM((tm, POOL_IN), BF16),
    ]
    for s in w_shapes:
        scratch += _stage_scratch(s)
    scratch.append(pltpu.SemaphoreType.DMA((len(w_shapes),)))
    y, sret, spool, *w16 = pl.pallas_call(
        functools.partial(_mixer_prompt_body, tm=tm, chunk=chunk, n_alias=len(extra), layer=layer),
        grid=(b, t_len // tm),
        in_specs=in_specs + extra_specs,
        out_specs=out_specs,
        out_shape=out_shape,
        scratch_shapes=scratch,
        input_output_aliases=aliases,
        compiler_params=pltpu.CompilerParams(
            dimension_semantics=("arbitrary", "arbitrary"), vmem_limit_bytes=VMEM_LIMIT),
        name="mixer_prompt",
    )(*args, *extra)
    return y, (sret, spool), tuple(w16)


def _log_gamma():
    return jnp.log1p(-jnp.exp2(-5.0 - jnp.arange(N_RET_HEADS, dtype=F32)))


def _rope_tables(pos):
    half = RET_QK_DIM // 2
    inv = jnp.power(jnp.float32(ROPE_BASE), -jnp.linspace(0.0, 1.0, half, dtype=F32))
    ang = pos[:, None] * inv[None, :]
    cos, sin = jnp.cos(ang), jnp.sin(ang)
    cos128 = jnp.tile(cos, (1, LANES // half))
    sin128 = jnp.tile(jnp.concatenate([-sin, sin], axis=1), (1, LANES // RET_QK_DIM))
    return cos128, sin128


def _pair_lanes(per_head):
    h, c = per_head.shape
    x = jnp.broadcast_to(per_head[:, :, None], (h, c, RET_QK_DIM))
    return x.reshape(N_PAIRS, 2, c, RET_QK_DIM).transpose(0, 2, 1, 3).reshape(N_PAIRS, c, LANES)


def _state_decay(chunk):
    g_c = jnp.exp(_log_gamma() * chunk)
    gs = jnp.broadcast_to(g_c[:, None, None], (N_RET_HEADS, RET_QK_DIM, RET_V_DIM))
    return gs.reshape(N_PAIRS, LANES, RET_V_DIM)


def _decay_tables(chunk):
    lg = _log_gamma()
    i = jnp.arange(chunk, dtype=F32)
    dist = i[:, None] - i[None, :]
    dmask = jnp.where(dist[None] >= 0, jnp.exp(lg[:, None, None] * jnp.maximum(dist, 0.0)[None]), 0.0)
    xi = jnp.exp(lg[:, None] * (i[None, :] + 1.0))
    wk = jnp.exp(lg[:, None] * (chunk - 1.0 - i[None, :]))
    return dmask, _pair_lanes(xi), _pair_lanes(wk), _state_decay(chunk)


def _mixer_sample_body(*refs, bb, ts, past, n_alias):
    (x_ref, g_ref, win_ref, wpool_ref, pscale_ref, wout_ref, cos_ref, sin_ref,
     xi_ref, wk_ref, lag_ref, gs_ref, ind_ref, s0_ref, pb_ref) = refs[:15]
    (o_ref, snew_ref, pnew_ref, z_ref, inter_ref, ublk_ref, pblk_ref) = refs[15 + n_alias:]
    rows = bb * ts
    x = x_ref[...]
    _project_in(x, g_ref, win_ref, z_ref)
    lane = lax.broadcasted_iota(jnp.int32, (1, LANES), 1)
    first_half = (lane % RET_QK_DIM) < (RET_QK_DIM // 2)
    _rotate_inplace(z_ref, cos_ref[...], sin_ref[...], first_half)

    q = z_ref[:, Q_OFF:Q_OFF + QK_W]
    k = z_ref[:, K_OFF:K_OFF + QK_W]
    v = z_ref[:, V_OFF:V_OFF + V_W]
    step = lax.broadcasted_iota(jnp.int32, (rows, 1), 0) % ts
    intra = None
    for d in range(ts):
        kd = k if d == 0 else pltpu.roll(k, d, 0)
        vd = v if d == 0 else pltpu.roll(v, d, 0)
        prod = jnp.where(step >= d, q * kd, 0.0).astype(BF16)
        score = _dot(prod, ind_ref[...])
        term = score * lag_ref[d:d + 1, :] * jnp.where(step >= d, vd, 0.0)
        intra = term if intra is None else intra + term

    row8 = lax.broadcasted_iota(jnp.int32, (2 * ts, 1), 0)
    top = (lax.broadcasted_iota(jnp.int32, (LANES, 1), 0) < RET_QK_DIM).astype(F32)
    bot = 1.0 - top

    def seq_pair(p, carry):
        r0 = pl.multiple_of(p * 2 * ts, 2 * ts)
        rsl = pl.ds(r0, 2 * ts)
        for pair in range(N_PAIRS):
            lanes = slice(pair * LANES, (pair + 1) * LANES)
            q8 = (z_ref[rsl, Q_OFF + pair * LANES:Q_OFF + (pair + 1) * LANES] * xi_ref[:, lanes]).astype(BF16)
            kw8 = z_ref[rsl, K_OFF + pair * LANES:K_OFF + (pair + 1) * LANES] * wk_ref[:, lanes]
            v8 = z_ref[rsl, V_OFF + pair * 2 * RET_V_DIM:V_OFF + (pair + 1) * 2 * RET_V_DIM].astype(BF16)
            outs = []
            for j in range(2):
                b = 2 * p + j
                s_pair = s0_ref[b, lanes, :]
                bd = jnp.concatenate([s_pair * top, s_pair * bot], axis=1).astype(BF16)
                outs.append(_dot(q8, bd))
                mine = (row8 >= j * ts) & (row8 < (j + 1) * ts)
                kwj = jnp.where(mine, kw8, 0.0).astype(BF16)
                upd = lax.dot_general(kwj, v8, (((0,), (0,)), ((), ())), preferred_element_type=F32)
                new_s = jnp.concatenate([upd[0:RET_QK_DIM, 0:RET_V_DIM],
                                         upd[RET_QK_DIM:, RET_V_DIM:]], axis=0)
                snew_ref[b, lanes, :] = s_pair * gs_ref[pair] + new_s
            inter_ref[rsl, pair * 2 * RET_V_DIM:(pair + 1) * 2 * RET_V_DIM] = jnp.where(
                row8 < ts, outs[0], outs[1])
        return carry

    lax.fori_loop(0, bb // 2, seq_pair, 0, unroll=4)

    keep = POOL_BUF - ts
    pnew_ref[:, 0:keep, :] = pb_ref[:, ts:, :]
    for g, w in enumerate(POOL_WINDOWS):
        ublk_ref[g] = z_ref[:, U_OFF + g * POOL_GROUP_IN:U_OFF + (g + 1) * POOL_GROUP_IN]
        u_steps = [ublk_ref[g, pl.ds(t, bb, stride=ts), :] for t in range(ts)]
        for t in range(ts):
            pnew_ref[:, keep + t, g * POOL_GROUP_IN:(g + 1) * POOL_GROUP_IN] = u_steps[t]
        ds_ = []
        for t in range(ts):
            tot = None
            for j in range(w):
                idx = POOL_BUF + t - j
                if idx >= POOL_BUF:
                    term = u_steps[idx - POOL_BUF]
                else:
                    term = pb_ref[:, idx, g * POOL_GROUP_IN:(g + 1) * POOL_GROUP_IN]
                tot = term if tot is None else tot + term
            cnt = float(min(past + t + 1, w))
            ds_.append(tot / cnt - u_steps[t])
        y = _dot(jnp.concatenate(ds_, axis=0).astype(BF16),
                 wpool_ref[g * POOL_GROUP_IN:(g + 1) * POOL_GROUP_IN, :])
        y = y * pscale_ref[:, g * POOL_GROUP_OUT:(g + 1) * POOL_GROUP_OUT]
        for half in range(POOL_GROUP_OUT // LANES):
            blk = g * (POOL_GROUP_OUT // LANES) + half
            for t in range(ts):
                pblk_ref[blk, pl.ds(t, bb, stride=ts), :] = y[t * bb:(t + 1) * bb, half * LANES:(half + 1) * LANES]
    pool = jnp.concatenate([pblk_ref[blk] for blk in range(D_MODEL // LANES)], axis=1)

    o_all = intra + inter_ref[...]
    rets = []
    for h in range(N_RET_HEADS):
        hc = slice(h * RET_V_DIM, (h + 1) * RET_V_DIM)
        rets.append(_swish_gate_norm(o_all[:, hc], z_ref[:, G_OFF + h * RET_V_DIM:G_OFF + (h + 1) * RET_V_DIM]))
    ret = jnp.concatenate(rets, axis=1)
    o_ref[...] = _merge_out(x, z_ref, ret, pool, wout_ref)


def _sample_tables(ts, past, rows):
    lg = _log_gamma()
    i = jnp.arange(ts, dtype=F32)
    lag = jnp.exp(lg[:, None] * i[None, :])
    lag = jnp.repeat(lag.T, RET_V_DIM, axis=1)
    xi = jnp.exp(lg[:, None] * (i[None, :] + 1.0))
    wk = jnp.exp(lg[:, None] * (ts - 1.0 - i[None, :]))
    expand = lambda a: jnp.tile(jnp.repeat(a.T, RET_QK_DIM, axis=1), (2, 1))
    pos = (past + jnp.arange(ts, dtype=jnp.int32)).astype(F32)
    cos, sin = _rope_tables(pos)
    reps = rows // ts
    head_of_k = jnp.arange(QK_W) // RET_QK_DIM
    head_of_v = jnp.arange(V_W) // RET_V_DIM
    ind = (head_of_k[:, None] == head_of_v[None, :]).astype(BF16)
    return (jnp.tile(cos, (reps, 1)), jnp.tile(sin, (reps, 1)), expand(xi), expand(wk), lag,
            _state_decay(ts), ind)


def _mixer_sample(x2d, s0, pbuf, g, w_in, w_pool, pool_scale, w_out, tabs, layer, prev_states, *, bb, ts, past):
    n = x2d.shape[0]
    depth = s0.shape[0]
    nb = n // ts
    rows = bb * ts
    assert nb % bb == 0 and bb % 2 == 0 and (2 * ts) % 8 == 0 and ts <= POOL_BUF
    cos, sin, xi, wk, lag, gs, ind = tabs
    row_spec = pl.BlockSpec((rows, D_MODEL), lambda i: (i, 0))
    state_spec = pl.BlockSpec((None, bb, N_PAIRS * LANES, RET_V_DIM), lambda i: (layer, i, 0, 0))
    pool_spec = pl.BlockSpec((None, bb, POOL_BUF, POOL_IN), lambda i: (layer, i, 0, 0))
    in_specs = [
        row_spec, _layer_spec(g.shape, layer), _resident_spec(w_in.shape),
        _resident_spec(w_pool.shape), _layer_spec(pool_scale.shape, layer),
        _resident_spec(w_out.shape), _const_spec(cos.shape), _const_spec(sin.shape),
        _const_spec(xi.shape), _const_spec(wk.shape), _const_spec(lag.shape), _const_spec(gs.shape),
        _const_spec(ind.shape), state_spec, pool_spec,
    ]
    args = [x2d, g, w_in, w_pool, pool_scale, w_out, cos, sin, xi, wk, lag, gs, ind, s0, pbuf]
    extra, extra_specs, aliases = _state_alias(prev_states, len(args), 1)
    out_specs = [row_spec, state_spec, pool_spec]
    out_shape = [
        jax.ShapeDtypeStruct((n, D_MODEL), F32),
        jax.ShapeDtypeStruct((depth, nb, N_PAIRS * LANES, RET_V_DIM), F32),
        jax.ShapeDtypeStruct((depth, nb, POOL_BUF, POOL_IN), F32),
    ]
    scratch = [
        pltpu.VMEM((rows, D_IN), F32),
        pltpu.VMEM((rows, V_W), F32),
        pltpu.VMEM((N_POOL_GROUPS, rows, LANES), F32),
        pltpu.VMEM((D_MODEL // LANES, rows, LANES), F32),
    ]
    y, snew, pnew = pl.pallas_call(
        functools.partial(_mixer_sample_body, bb=bb, ts=ts, past=past, n_alias=len(extra)),
        grid=(nb // bb,),
        in_specs=in_specs + extra_specs,
        out_specs=out_specs,
        out_shape=out_shape,
        scratch_shapes=scratch,
        input_output_aliases=aliases,
        compiler_params=pltpu.CompilerParams(
            dimension_semantics=("arbitrary",), vmem_limit_bytes=VMEM_LIMIT),
        name="mixer_sample",
    )(*args, *extra)
    return y, (snew, pnew)


def kernel(x_prompt, x_sample, state_ret, state_pool, norm_ffn1, w_ffn1_in, w_ffn1_out, norm_mix,
           w_in, w_pool, pool_scale, w_out, norm_ffn2, w_ffn2_in, w_ffn2_out, norm_final):
    depth = w_in.shape[0]
    bp, tp, _ = x_prompt.shape
    bs, ts, _ = x_sample.shape
    wp = w_pool.reshape(depth, N_POOL_GROUPS * POOL_GROUP_IN, POOL_GROUP_OUT)

    row = lambda a: a.reshape(depth, 1, D_MODEL)
    n1, nm, n2, ps = map(row, (norm_ffn1, norm_mix, norm_ffn2, pool_scale))
    gf = norm_final.reshape(1, D_MODEL)
    last = depth - 1

    tm = 512
    tm_ffn = 1024
    pos_p = jnp.arange(tp, dtype=jnp.int32).astype(F32)
    tabs_p = _rope_tables(pos_p) + _decay_tables(RET_CHUNK)

    bb = 32
    tabs_s = _sample_tables(ts, PAST_LEN, bb * ts)
    s0_all = state_ret.reshape(depth, bs, N_PAIRS * LANES, RET_V_DIM)

    x = x_prompt.reshape(bp * tp, D_MODEL)
    xs = x_sample.reshape(bs * ts, D_MODEL)
    states_p = states_s = None
    for l in range(depth):
        x, xs = _ffn(x, xs, n1, w_ffn1_in, w_ffn1_out, l, tm=tm_ffn)
        x, states_p, (wi16, wp16, wo16) = _mixer_prompt(x.reshape(bp, tp, D_MODEL), nm, w_in, wp, ps, w_out,
                                                        tabs_p, l, states_p, tm=tm)
        xs, states_s = _mixer_sample(xs, s0_all, state_pool, nm, wi16, wp16, ps, wo16, tabs_s, l, states_s,
                                     bb=bb, ts=ts, past=PAST_LEN)
        x, xs = _ffn(x.reshape(bp * tp, D_MODEL), xs, n2, w_ffn2_in, w_ffn2_out, l,
                     gf if l == last else None, tm=tm_ffn)
    y_prompt = x.reshape(bp, tp, D_MODEL)
    y_sample = xs.reshape(bs, ts, D_MODEL)
    state_ret_prompt = states_p[0].reshape(depth, bp, N_RET_HEADS, RET_QK_DIM, RET_V_DIM)
    state_pool_prompt = states_p[1]
    state_ret_sample = states_s[0].reshape(depth, bs, N_RET_HEADS, RET_QK_DIM, RET_V_DIM)
    state_pool_sample = states_s[1]
    return (y_prompt, y_sample, state_ret_prompt, state_ret_sample, state_pool_prompt, state_pool_sample)
```

```python
import functools

import jax
import jax.numpy as jnp
from jax import lax
from jax.experimental import pallas as pl
from jax.experimental.pallas import tpu as pltpu

D_MODEL = 1024
N_RET_HEADS = 8
RET_QK_DIM = 64
RET_V_DIM = 128
RET_CHUNK = 128
ROPE_BASE = 10000.0
QK_W = N_RET_HEADS * RET_QK_DIM
V_W = N_RET_HEADS * RET_V_DIM
POOL_WINDOWS = (2, 4, 8, 16)
N_POOL_GROUPS = 4
POOL_IN = 512
POOL_GROUP_IN = 128
POOL_GROUP_OUT = 256
POOL_BUF = 15
D_IN = 2 * QK_W + 2 * V_W + POOL_IN + 2 * D_MODEL
D_FF = 2816
EPS = 1e-6
PAST_LEN = 16384

Q_OFF = 0
K_OFF = QK_W
V_OFF = 2 * QK_W
G_OFF = 2 * QK_W + V_W
U_OFF = 2 * QK_W + 2 * V_W
GA_OFF = U_OFF + POOL_IN
GB_OFF = GA_OFF + D_MODEL

LANES = 128
N_PAIRS = N_RET_HEADS // 2
POOL_PAD = 16
VMEM_LIMIT = 60 * 1024 * 1024

F32 = jnp.float32
BF16 = jnp.bfloat16


def _dot(a, b):
    return jnp.dot(a, b, preferred_element_type=F32)


def _rms(x, g):
    return x * lax.rsqrt(jnp.mean(x * x, axis=-1, keepdims=True) + EPS) * g


def _const_spec(shape):
    nd = len(shape)
    return pl.BlockSpec(shape, lambda *_: (0,) * nd)


FF_CHUNKS = ((0, 1024), (1024, 1024), (2048, 768))


STAGE_BUFS = 3
STAGE_BYTES = 3 * 512 * 1024


def _stage_rows(shape):
    r, c = shape
    fits = [n for n in range(16, r + 1, 16) if r % n == 0 and n * c * 4 <= STAGE_BYTES]
    return max(fits)


def _stage_scratch(shape):
    return [pltpu.VMEM(shape, BF16), pltpu.VMEM((STAGE_BUFS, _stage_rows(shape), shape[1]), F32),
            pltpu.SemaphoreType.DMA((STAGE_BUFS,))]


def _fetch_cast(src, dst, stage, sems):
    nbuf, rows, _ = stage.shape
    n = src.shape[0] // rows

    def copy(k):
        return pltpu.make_async_copy(src.at[pl.ds(k * rows, rows), :], stage.at[k % nbuf], sems.at[k % nbuf])

    for k in range(min(nbuf, n)):
        copy(k).start()
    for k in range(n):
        copy(k).wait()
        dst[k * rows:(k + 1) * rows, :] = stage[k % nbuf].astype(BF16)
        if k + nbuf < n:
            copy(k + nbuf).start()


def _convert_weights(first, layer, srcs, scratch):
    @pl.when(first)
    def _():
        for k, src in enumerate(srcs):
            _fetch_cast(src.at[layer], *scratch[3 * k:3 * k + 3])

    return scratch[0::3]


def _emit_weights(first, last, srcs, outs, sems):
    copies = [pltpu.make_async_copy(s, o, sems.at[k]) for k, (s, o) in enumerate(zip(srcs, outs))]

    @pl.when(first)
    def _():
        for c in copies:
            c.start()

    @pl.when(last)
    def _():
        for c in copies:
            c.wait()


def _ffn_body(*refs, final, layer):
    n_in = 6 if final else 5
    x_ref, xs_ref, g_ref, win_hbm, wout_hbm = refs[:5]
    gf_ref = refs[5] if final else None
    o_ref, os_ref = refs[n_in:n_in + 2]
    i = pl.program_id(0)
    win_ref, wout_ref = _convert_weights(i == 0, layer, (win_hbm, wout_hbm), refs[n_in + 2:])

    def ffn_rows(x):
        h = _rms(x, g_ref[...]).astype(BF16)
        acc = None
        for off, width in FF_CHUNKS:
            gate = _dot(h, win_ref[:, off:off + width])
            up = _dot(h, win_ref[:, D_FF + off:D_FF + off + width])
            a = (gate * jax.nn.sigmoid(gate) * up).astype(BF16)
            part = _dot(a, wout_ref[off:off + width, :])
            acc = part if acc is None else acc + part
        y = x + 0.5 * acc
        return _rms(y, gf_ref[...]) if final else y

    o_ref[...] = ffn_rows(x_ref[...])

    @pl.when(i == pl.num_programs(0) - 1)
    def _():
        os_ref[...] = ffn_rows(xs_ref[...])


def _layer_spec(shape, layer):
    nd = len(shape) - 1
    return pl.BlockSpec((None,) + tuple(shape[1:]), lambda *_: (layer,) + (0,) * nd,
                        pipeline_mode=pl.Buffered(1))


def _resident_spec(shape):
    nd = len(shape)
    return pl.BlockSpec(tuple(shape), lambda *_: (0,) * nd, pipeline_mode=pl.Buffered(1))


ANY_SPEC = pl.BlockSpec(memory_space=pl.ANY)


def _ffn(x2d, xs2d, g, w_in, w_out, layer, g_final=None, *, tm, inplace):
    n = x2d.shape[0]
    assert n % tm == 0
    final = g_final is not None
    row_spec = pl.BlockSpec((tm, D_MODEL), lambda i: (i, 0))
    in_specs = [row_spec, _resident_spec(xs2d.shape), _layer_spec(g.shape, layer), ANY_SPEC, ANY_SPEC]
    args = [x2d, xs2d, g, w_in, w_out]
    if final:
        in_specs.append(_const_spec((1, D_MODEL)))
        args.append(g_final.reshape(1, D_MODEL))
    scratch = _stage_scratch(w_in.shape[1:]) + _stage_scratch(w_out.shape[1:])
    return pl.pallas_call(
        functools.partial(_ffn_body, final=final, layer=layer),
        grid=(n // tm,),
        in_specs=in_specs,
        out_specs=[row_spec, _resident_spec(xs2d.shape)],
        out_shape=[jax.ShapeDtypeStruct(x2d.shape, F32), jax.ShapeDtypeStruct(xs2d.shape, F32)],
        scratch_shapes=scratch,
        input_output_aliases={0: 0, 1: 1} if inplace else {},
        compiler_params=pltpu.CompilerParams(
            dimension_semantics=("arbitrary",), vmem_limit_bytes=VMEM_LIMIT),
        name="ffn_final" if final else "ffn",
    )(*args)


IN_CHUNKS = tuple((o, 1024) for o in range(0, 5120, 1024)) + ((5120, 512),)


def _project_in(x, g_ref, win_ref, z_ref):
    h = _rms(x, g_ref[...]).astype(BF16)
    for off, width in IN_CHUNKS:
        z_ref[:, off:off + width] = _dot(h, win_ref[:, off:off + width])


def _rotate_inplace(z_ref, cos, sin_signed, first_half):
    for blk in range(2 * QK_W // LANES):
        cols = slice(blk * LANES, (blk + 1) * LANES)
        xb = z_ref[:, cols]
        partner = jnp.where(first_half, pltpu.roll(xb, LANES - 32, 1), pltpu.roll(xb, 32, 1))
        r = xb * cos + partner * sin_signed
        if blk >= QK_W // LANES:
            r = r * (RET_QK_DIM ** -0.5)
        z_ref[:, cols] = r


def _merge_out(x, z_ref, ret, pool, wout_ref):
    ga = z_ref[:, GA_OFF:GA_OFF + D_MODEL]
    gb = z_ref[:, GB_OFF:GB_OFF + D_MODEL]
    merged = jax.nn.sigmoid(ga) * ret + jax.nn.sigmoid(gb) * pool
    return x + _dot(merged.astype(BF16), wout_ref[...])


def _swish_gate_norm(o, gate):
    o = o * lax.rsqrt(jnp.mean(o * o, axis=-1, keepdims=True) + EPS)
    return o * (gate * jax.nn.sigmoid(gate))


def _mixer_prompt_body(*refs, tm, chunk, n_alias, layer):
    (x_ref, g_ref, win_ref, wpool_ref, pscale_ref, wout_ref, cos_ref, sin_ref,
     dmask_ref, xi_ref, wk_ref, gs_ref) = refs[:12]
    outs = refs[12 + n_alias:]
    o_ref, sret_ref, spool_ref = outs[:3]
    z_ref, s_ref, ext_ref, ret_ref, pool_ref, h_ref, d_ref = outs[6:13]
    t = pl.program_id(1)
    nt = pl.num_programs(1)
    bi = pl.program_id(0)
    first = (bi == 0) & (t == 0)
    win_ref, wpool_ref, wout_ref = _convert_weights(first, layer, (win_ref, wpool_ref, wout_ref), outs[13:22])
    _emit_weights(first, (bi == pl.num_programs(0) - 1) & (t == nt - 1),
                  (win_ref, wpool_ref, wout_ref), outs[3:6], outs[22])

    @pl.when(t == 0)
    def _():
        s_ref[...] = jnp.zeros_like(s_ref)
        ext_ref[0:POOL_PAD, :] = jnp.zeros((POOL_PAD, POOL_IN), F32)

    h_ref[...] = _rms(x_ref[...], g_ref[...]).astype(BF16)

    def project(off, width, act=None):
        y = _dot(h_ref[...], win_ref[:, off:off + width])
        z_ref[:, off:off + width] = y if act is None else act(y)

    project(Q_OFF, 2 * QK_W)
    project(V_OFF, V_W)
    project(U_OFF, POOL_IN)

    lane = lax.broadcasted_iota(jnp.int32, (1, LANES), 1)
    first_half = (lane % RET_QK_DIM) < (RET_QK_DIM // 2)
    _rotate_inplace(z_ref, cos_ref[...], sin_ref[...], first_half)

    ext_ref[POOL_PAD:POOL_PAD + tm, :] = z_ref[:, U_OFF:U_OFF + POOL_IN]
    pos = t * tm + lax.broadcasted_iota(jnp.int32, (tm, 1), 0)
    for g, w in enumerate(POOL_WINDOWS):
        cols = slice(g * POOL_GROUP_IN, (g + 1) * POOL_GROUP_IN)
        u = ext_ref[POOL_PAD:POOL_PAD + tm, cols]
        tot = u
        for j in range(1, w):
            tot = tot + ext_ref[POOL_PAD - j:POOL_PAD - j + tm, cols]
        cnt = jnp.minimum(pos + 1, w).astype(F32)
        d_ref[:, cols] = (tot / cnt - u).astype(BF16)
    tail = ext_ref[tm:tm + POOL_PAD, :]
    ext_ref[0:POOL_PAD, :] = tail

    def pool_matmul(g):
        ocols = slice(g * POOL_GROUP_OUT, (g + 1) * POOL_GROUP_OUT)
        y = _dot(d_ref[:, g * POOL_GROUP_IN:(g + 1) * POOL_GROUP_IN],
                 wpool_ref[g * POOL_GROUP_IN:(g + 1) * POOL_GROUP_IN, :])
        pool_ref[:, ocols] = y * pscale_ref[:, ocols]

    fill_width = 2 * LANES
    fillers = [functools.partial(project, off, fill_width, jax.nn.silu)
               for off in range(G_OFF, G_OFF + V_W, fill_width)]
    fillers += [functools.partial(project, off, fill_width, jax.nn.sigmoid)
                for off in range(GA_OFF, D_IN, fill_width)]
    fillers += [functools.partial(pool_matmul, g) for g in range(N_POOL_GROUPS)]

    head0 = (lane < RET_QK_DIM).astype(F32)
    head1 = 1.0 - head0

    n_chunks = tm // chunk
    stages = 3 * n_chunks

    def run_fillers(stage):
        for f in fillers[stage * len(fillers) // stages:(stage + 1) * len(fillers) // stages]:
            f()

    def pair_cols(off, pair, width):
        return slice(off + pair * width, off + (pair + 1) * width)

    for c in range(n_chunks):
        rows = slice(c * chunk, (c + 1) * chunk)
        scores = []
        for pair in range(N_PAIRS):
            qb = z_ref[rows, pair_cols(Q_OFF, pair, LANES)]
            kb = z_ref[rows, pair_cols(K_OFF, pair, LANES)]
            k_stack = jnp.concatenate([kb * head0, kb * head1], axis=0).astype(BF16)
            scores.append(lax.dot_general(qb.astype(BF16), k_stack, (((1,), (1,)), ((), ())),
                                          preferred_element_type=F32))
        run_fillers(3 * c)
        for pair in range(N_PAIRS):
            qx = z_ref[rows, pair_cols(Q_OFF, pair, LANES)] * xi_ref[pair]
            vb16 = z_ref[rows, pair_cols(V_OFF, pair, 2 * RET_V_DIM)].astype(BF16)
            s16 = s_ref[pair].astype(BF16)
            for j, hmask in enumerate((head0, head1)):
                h = 2 * pair + j
                scj = scores[pair][:, j * chunk:(j + 1) * chunk] * dmask_ref[h]
                lhs = jnp.concatenate([scj, qx * hmask], axis=1).astype(BF16)
                rhs = jnp.concatenate([vb16[:, j * RET_V_DIM:(j + 1) * RET_V_DIM], s16], axis=0)
                o = _dot(lhs, rhs)
                o = o * lax.rsqrt(jnp.mean(o * o, axis=-1, keepdims=True) + EPS)
                ret_ref[rows, h * RET_V_DIM:(h + 1) * RET_V_DIM] = o
        run_fillers(3 * c + 1)
        for pair in range(N_PAIRS):
            kw = (z_ref[rows, pair_cols(K_OFF, pair, LANES)] * wk_ref[pair]).astype(BF16)
            vb16 = z_ref[rows, pair_cols(V_OFF, pair, 2 * RET_V_DIM)].astype(BF16)
            upd = lax.dot_general(kw, vb16, (((0,), (0,)), ((), ())),
                                  preferred_element_type=F32)
            new_s = jnp.concatenate([upd[0:RET_QK_DIM, 0:RET_V_DIM],
                                     upd[RET_QK_DIM:, RET_V_DIM:]], axis=0)
            s_ref[pair] = s_ref[pair] * gs_ref[pair] + new_s
        run_fillers(3 * c + 2)

    n_halves = 2
    for r in range(n_halves):
        rows = slice(r * tm // n_halves, (r + 1) * tm // n_halves)
        ret = ret_ref[rows, :] * z_ref[rows, G_OFF:G_OFF + V_W]
        merged = (z_ref[rows, GA_OFF:GA_OFF + D_MODEL] * ret
                  + z_ref[rows, GB_OFF:GB_OFF + D_MODEL] * pool_ref[rows, :])
        o_ref[rows, :] = x_ref[rows, :] + _dot(merged.astype(BF16), wout_ref[...])

    @pl.when(t == nt - 1)
    def _():
        sret_ref[...] = s_ref[...].reshape(N_PAIRS * LANES, RET_V_DIM)
        spool_ref[...] = tail[POOL_PAD - POOL_BUF:, :]


def _state_alias(prev, n_in, first_out):
    if prev is None:
        return [], [], {}
    specs = [pl.BlockSpec(memory_space=pl.ANY) for _ in prev]
    aliases = {n_in + k: first_out + k for k in range(len(prev))}
    return list(prev), specs, aliases


def _mixer_prompt(x, g, w_in, w_pool, pool_scale, w_out, tabs, layer, prev_states, *, tm):
    b, t_len, _ = x.shape
    depth = w_in.shape[0]
    chunk = RET_CHUNK
    assert t_len % tm == 0 and tm % chunk == 0
    cos, sin, dmask, xi, wk, gs = tabs
    row_spec = pl.BlockSpec((None, tm, D_MODEL), lambda i, j: (i, j, 0))
    tab_spec = pl.BlockSpec((tm, LANES), lambda i, j: (j, 0))
    in_specs = [
        row_spec, _layer_spec(g.shape, layer), ANY_SPEC, ANY_SPEC, _layer_spec(pool_scale.shape, layer),
        ANY_SPEC, tab_spec, tab_spec,
        _const_spec(dmask.shape), _const_spec(xi.shape), _const_spec(wk.shape), _const_spec(gs.shape),
    ]
    args = [x, g, w_in, w_pool, pool_scale, w_out, cos, sin, dmask, xi, wk, gs]
    extra, extra_specs, aliases = _state_alias(prev_states, len(args), 1)
    aliases[0] = 0
    w_shapes = [w_in.shape[1:], w_pool.shape[1:], w_out.shape[1:]]
    out_specs = [
        row_spec,
        pl.BlockSpec((None, None, N_PAIRS * LANES, RET_V_DIM), lambda i, j: (layer, i, 0, 0)),
        pl.BlockSpec((None, None, POOL_BUF, POOL_IN), lambda i, j: (layer, i, 0, 0)),
    ] + [ANY_SPEC] * len(w_shapes)
    out_shape = [
        jax.ShapeDtypeStruct((b, t_len, D_MODEL), F32),
        jax.ShapeDtypeStruct((depth, b, N_PAIRS * LANES, RET_V_DIM), F32),
        jax.ShapeDtypeStruct((depth, b, POOL_BUF, POOL_IN), F32),
    ] + [jax.ShapeDtypeStruct(s, BF16) for s in w_shapes]
    scratch = [
        pltpu.VMEM((tm, D_IN), F32),
        pltpu.VMEM((N_PAIRS, LANES, RET_V_DIM), F32),
        pltpu.VMEM((POOL_PAD + tm, POOL_IN), F32),
        pltpu.VMEM((tm, V_W), F32),
        pltpu.VMEM((tm, D_MODEL), F32),
        pltpu.VMEM((tm, D_MODEL), BF16),
        pltpu.VMEM((tm, POOL_IN), BF16),
    ]
    for s in w_shapes:
        scratch += _stage_scratch(s)
    scratch.append(pltpu.SemaphoreType.DMA((len(w_shapes),)))
    y, sret, spool, *w16 = pl.pallas_call(
        functools.partial(_mixer_prompt_body, tm=tm, chunk=chunk, n_alias=len(extra), layer=layer),
        grid=(b, t_len // tm),
        in_specs=in_specs + extra_specs,
        out_specs=out_specs,
        out_shape=out_shape,
        scratch_shapes=scratch,
        input_output_aliases=aliases,
        compiler_params=pltpu.CompilerParams(
            dimension_semantics=("arbitrary", "arbitrary"), vmem_limit_bytes=VMEM_LIMIT),
        name="mixer_prompt",
    )(*args, *extra)
    return y, (sret, spool), tuple(w16)


def _log_gamma():
    return jnp.log1p(-jnp.exp2(-5.0 - jnp.arange(N_RET_HEADS, dtype=F32)))


def _rope_tables(pos):
    half = RET_QK_DIM // 2
    inv = jnp.power(jnp.float32(ROPE_BASE), -jnp.linspace(0.0, 1.0, half, dtype=F32))
    ang = pos[:, None] * inv[None, :]
    cos, sin = jnp.cos(ang), jnp.sin(ang)
    cos128 = jnp.tile(cos, (1, LANES // half))
    sin128 = jnp.tile(jnp.concatenate([-sin, sin], axis=1), (1, LANES // RET_QK_DIM))
    return cos128, sin128


def _pair_lanes(per_head):
    h, c = per_head.shape
    x = jnp.broadcast_to(per_head[:, :, None], (h, c, RET_QK_DIM))
    return x.reshape(N_PAIRS, 2, c, RET_QK_DIM).transpose(0, 2, 1, 3).reshape(N_PAIRS, c, LANES)


def _state_decay(chunk):
    g_c = jnp.exp(_log_gamma() * chunk)
    gs = jnp.broadcast_to(g_c[:, None, None], (N_RET_HEADS, RET_QK_DIM, RET_V_DIM))
    return gs.reshape(N_PAIRS, LANES, RET_V_DIM)


def _decay_tables(chunk):
    lg = _log_gamma()
    i = jnp.arange(chunk, dtype=F32)
    dist = i[:, None] - i[None, :]
    dmask = jnp.where(dist[None] >= 0, jnp.exp(lg[:, None, None] * jnp.maximum(dist, 0.0)[None]), 0.0)
    xi = jnp.exp(lg[:, None] * (i[None, :] + 1.0))
    wk = jnp.exp(lg[:, None] * (chunk - 1.0 - i[None, :]))
    return dmask, _pair_lanes(xi), _pair_lanes(wk), _state_decay(chunk)


def _mixer_sample_body(*refs, bb, ts, past, n_alias):
    (x_ref, g_ref, win_ref, wpool_ref, pscale_ref, wout_ref, cos_ref, sin_ref,
     xi_ref, wk_ref, lag_ref, gs_ref, ind_ref, s0_ref, pb_ref) = refs[:15]
    (o_ref, snew_ref, pnew_ref, z_ref, inter_ref, ublk_ref, pblk_ref) = refs[15 + n_alias:]
    rows = bb * ts
    x = x_ref[...]
    _project_in(x, g_ref, win_ref, z_ref)
    lane = lax.broadcasted_iota(jnp.int32, (1, LANES), 1)
    first_half = (lane % RET_QK_DIM) < (RET_QK_DIM // 2)
    _rotate_inplace(z_ref, cos_ref[...], sin_ref[...], first_half)

    q = z_ref[:, Q_OFF:Q_OFF + QK_W]
    k = z_ref[:, K_OFF:K_OFF + QK_W]
    v = z_ref[:, V_OFF:V_OFF + V_W]
    step = lax.broadcasted_iota(jnp.int32, (rows, 1), 0) % ts
    intra = None
    for d in range(ts):
        kd = k if d == 0 else pltpu.roll(k, d, 0)
        vd = v if d == 0 else pltpu.roll(v, d, 0)
        prod = jnp.where(step >= d, q * kd, 0.0).astype(BF16)
        score = _dot(prod, ind_ref[...])
        term = score * lag_ref[d:d + 1, :] * jnp.where(step >= d, vd, 0.0)
        intra = term if intra is None else intra + term

    row8 = lax.broadcasted_iota(jnp.int32, (2 * ts, 1), 0)
    top = (lax.broadcasted_iota(jnp.int32, (LANES, 1), 0) < RET_QK_DIM).astype(F32)
    bot = 1.0 - top

    def seq_pair(p, carry):
        r0 = pl.multiple_of(p * 2 * ts, 2 * ts)
        rsl = pl.ds(r0, 2 * ts)
        for pair in range(N_PAIRS):
            lanes = slice(pair * LANES, (pair + 1) * LANES)
            q8 = (z_ref[rsl, Q_OFF + pair * LANES:Q_OFF + (pair + 1) * LANES] * xi_ref[:, lanes]).astype(BF16)
            kw8 = z_ref[rsl, K_OFF + pair * LANES:K_OFF + (pair + 1) * LANES] * wk_ref[:, lanes]
            v8 = z_ref[rsl, V_OFF + pair * 2 * RET_V_DIM:V_OFF + (pair + 1) * 2 * RET_V_DIM].astype(BF16)
            outs = []
            for j in range(2):
                b = 2 * p + j
                s_pair = s0_ref[b, lanes, :]
                bd = jnp.concatenate([s_pair * top, s_pair * bot], axis=1).astype(BF16)
                outs.append(_dot(q8, bd))
                mine = (row8 >= j * ts) & (row8 < (j + 1) * ts)
                kwj = jnp.where(mine, kw8, 0.0).astype(BF16)
                upd = lax.dot_general(kwj, v8, (((0,), (0,)), ((), ())), preferred_element_type=F32)
                new_s = jnp.concatenate([upd[0:RET_QK_DIM, 0:RET_V_DIM],
                                         upd[RET_QK_DIM:, RET_V_DIM:]], axis=0)
                snew_ref[b, lanes, :] = s_pair * gs_ref[pair] + new_s
            inter_ref[rsl, pair * 2 * RET_V_DIM:(pair + 1) * 2 * RET_V_DIM] = jnp.where(
                row8 < ts, outs[0], outs[1])
        return carry

    lax.fori_loop(0, bb // 2, seq_pair, 0, unroll=4)

    keep = POOL_BUF - ts
    pnew_ref[:, 0:keep, :] = pb_ref[:, ts:, :]
    for g, w in enumerate(POOL_WINDOWS):
        ublk_ref[g] = z_ref[:, U_OFF + g * POOL_GROUP_IN:U_OFF + (g + 1) * POOL_GROUP_IN]
        u_steps = [ublk_ref[g, pl.ds(t, bb, stride=ts), :] for t in range(ts)]
        for t in range(ts):
            pnew_ref[:, keep + t, g * POOL_GROUP_IN:(g + 1) * POOL_GROUP_IN] = u_steps[t]
        ds_ = []
        for t in range(ts):
            tot = None
            for j in range(w):
                idx = POOL_BUF + t - j
                if idx >= POOL_BUF:
                    term = u_steps[idx - POOL_BUF]
                else:
                    term = pb_ref[:, idx, g * POOL_GROUP_IN:(g + 1) * POOL_GROUP_IN]
                tot = term if tot is None else tot + term
            cnt = float(min(past + t + 1, w))
            ds_.append(tot / cnt - u_steps[t])
        y = _dot(jnp.concatenate(ds_, axis=0).astype(BF16),
                 wpool_ref[g * POOL_GROUP_IN:(g + 1) * POOL_GROUP_IN, :])
        y = y * pscale_ref[:, g * POOL_GROUP_OUT:(g + 1) * POOL_GROUP_OUT]
        for half in range(POOL_GROUP_OUT // LANES):
            blk = g * (POOL_GROUP_OUT // LANES) + half
            for t in range(ts):
                pblk_ref[blk, pl.ds(t, bb, stride=ts), :] = y[t * bb:(t + 1) * bb, half * LANES:(half + 1) * LANES]
    pool = jnp.concatenate([pblk_ref[blk] for blk in range(D_MODEL // LANES)], axis=1)

    o_all = intra + inter_ref[...]
    rets = []
    for h in range(N_RET_HEADS):
        hc = slice(h * RET_V_DIM, (h + 1) * RET_V_DIM)
        rets.append(_swish_gate_norm(o_all[:, hc], z_ref[:, G_OFF + h * RET_V_DIM:G_OFF + (h + 1) * RET_V_DIM]))
    ret = jnp.concatenate(rets, axis=1)
    o_ref[...] = _merge_out(x, z_ref, ret, pool, wout_ref)


def _sample_tables(ts, past, rows):
    lg = _log_gamma()
    i = jnp.arange(ts, dtype=F32)
    lag = jnp.exp(lg[:, None] * i[None, :])
    lag = jnp.repeat(lag.T, RET_V_DIM, axis=1)
    xi = jnp.exp(lg[:, None] * (i[None, :] + 1.0))
    wk = jnp.exp(lg[:, None] * (ts - 1.0 - i[None, :]))
    expand = lambda a: jnp.tile(jnp.repeat(a.T, RET_QK_DIM, axis=1), (2, 1))
    pos = (past + jnp.arange(ts, dtype=jnp.int32)).astype(F32)
    cos, sin = _rope_tables(pos)
    reps = rows // ts
    head_of_k = jnp.arange(QK_W) // RET_QK_DIM
    head_of_v = jnp.arange(V_W) // RET_V_DIM
    ind = (head_of_k[:, None] == head_of_v[None, :]).astype(BF16)
    return (jnp.tile(cos, (reps, 1)), jnp.tile(sin, (reps, 1)), expand(xi), expand(wk), lag,
            _state_decay(ts), ind)


def _mixer_sample(x2d, s0, pbuf, g, w_in, w_pool, pool_scale, w_out, tabs, layer, prev_states, *, bb, ts, past):
    n = x2d.shape[0]
    depth = s0.shape[0]
    nb = n // ts
    rows = bb * ts
    assert nb % bb == 0 and bb % 2 == 0 and (2 * ts) % 8 == 0 and ts <= POOL_BUF
    cos, sin, xi, wk, lag, gs, ind = tabs
    row_spec = pl.BlockSpec((rows, D_MODEL), lambda i: (i, 0))
    state_spec = pl.BlockSpec((None, bb, N_PAIRS * LANES, RET_V_DIM), lambda i: (layer, i, 0, 0))
    pool_spec = pl.BlockSpec((None, bb, POOL_BUF, POOL_IN), lambda i: (layer, i, 0, 0))
    in_specs = [
        row_spec, _layer_spec(g.shape, layer), _resident_spec(w_in.shape),
        _resident_spec(w_pool.shape), _layer_spec(pool_scale.shape, layer),
        _resident_spec(w_out.shape), _const_spec(cos.shape), _const_spec(sin.shape),
        _const_spec(xi.shape), _const_spec(wk.shape), _const_spec(lag.shape), _const_spec(gs.shape),
        _const_spec(ind.shape), state_spec, pool_spec,
    ]
    args = [x2d, g, w_in, w_pool, pool_scale, w_out, cos, sin, xi, wk, lag, gs, ind, s0, pbuf]
    extra, extra_specs, aliases = _state_alias(prev_states, len(args), 1)
    aliases[0] = 0
    out_specs = [row_spec, state_spec, pool_spec]
    out_shape = [
        jax.ShapeDtypeStruct((n, D_MODEL), F32),
        jax.ShapeDtypeStruct((depth, nb, N_PAIRS * LANES, RET_V_DIM), F32),
        jax.ShapeDtypeStruct((depth, nb, POOL_BUF, POOL_IN), F32),
    ]
    scratch = [
        pltpu.VMEM((rows, D_IN), F32),
        pltpu.VMEM((rows, V_W), F32),
        pltpu.VMEM((N_POOL_GROUPS, rows, LANES), F32),
        pltpu.VMEM((D_MODEL // LANES, rows, LANES), F32),
    ]
    y, snew, pnew = pl.pallas_call(
        functools.partial(_mixer_sample_body, bb=bb, ts=ts, past=past, n_alias=len(extra)),
        grid=(nb // bb,),
        in_specs=in_specs + extra_specs,
        out_specs=out_specs,
        out_shape=out_shape,
        scratch_shapes=scratch,
        input_output_aliases=aliases,
        compiler_params=pltpu.CompilerParams(
            dimension_semantics=("arbitrary",), vmem_limit_bytes=VMEM_LIMIT),
        name="mixer_sample",
    )(*args, *extra)
    return y, (snew, pnew)


def kernel(x_prompt, x_sample, state_ret, state_pool, norm_ffn1, w_ffn1_in, w_ffn1_out, norm_mix,
           w_in, w_pool, pool_scale, w_out, norm_ffn2, w_ffn2_in, w_ffn2_out, norm_final):
    depth = w_in.shape[0]
    bp, tp, _ = x_prompt.shape
    bs, ts, _ = x_sample.shape
    wp = w_pool.reshape(depth, N_POOL_GROUPS * POOL_GROUP_IN, POOL_GROUP_OUT)

    row = lambda a: a.reshape(depth, 1, D_MODEL)
    n1, nm, n2, ps = map(row, (norm_ffn1, norm_mix, norm_ffn2, pool_scale))
    gf = norm_final.reshape(1, D_MODEL)
    last = depth - 1

    tm = 512
    tm_ffn = 1024
    pos_p = jnp.arange(tp, dtype=jnp.int32).astype(F32)
    tabs_p = _rope_tables(pos_p) + _decay_tables(RET_CHUNK)

    bb = 32
    tabs_s = _sample_tables(ts, PAST_LEN, bb * ts)
    s0_all = state_ret.reshape(depth, bs, N_PAIRS * LANES, RET_V_DIM)

    x = x_prompt.reshape(bp * tp, D_MODEL)
    xs = x_sample.reshape(bs * ts, D_MODEL)
    states_p = states_s = None
    for l in range(depth):
        x, xs = _ffn(x, xs, n1, w_ffn1_in, w_ffn1_out, l, tm=tm_ffn, inplace=l > 0)
        x, states_p, (wi16, wp16, wo16) = _mixer_prompt(x.reshape(bp, tp, D_MODEL), nm, w_in, wp, ps, w_out,
                                                        tabs_p, l, states_p, tm=tm)
        xs, states_s = _mixer_sample(xs, s0_all, state_pool, nm, wi16, wp16, ps, wo16, tabs_s, l, states_s,
                                     bb=bb, ts=ts, past=PAST_LEN)
        x, xs = _ffn(x.reshape(bp * tp, D_MODEL), xs, n2, w_ffn2_in, w_ffn2_out, l,
                     gf if l == last else None, tm=tm_ffn, inplace=True)
    y_prompt = x.reshape(bp, tp, D_MODEL)
    y_sample = xs.reshape(bs, ts, D_MODEL)
    state_ret_prompt = states_p[0].reshape(depth, bp, N_RET_HEADS, RET_QK_DIM, RET_V_DIM)
    state_pool_prompt = states_p[1]
    state_ret_sample = states_s[0].reshape(depth, bs, N_RET_HEADS, RET_QK_DIM, RET_V_DIM)
    state_pool_sample = states_s[1]
    return (y_prompt, y_sample, state_ret_prompt, state_ret_sample, state_pool_prompt, state_pool_sample)
```

```python
import functools

import jax
import jax.numpy as jnp
from jax import lax
from jax.experimental import pallas as pl
from jax.experimental.pallas import tpu as pltpu

D_MODEL = 1024
N_RET_HEADS = 8
RET_QK_DIM = 64
RET_V_DIM = 128
RET_CHUNK = 128
ROPE_BASE = 10000.0
QK_W = N_RET_HEADS * RET_QK_DIM
V_W = N_RET_HEADS * RET_V_DIM
POOL_WINDOWS = (2, 4, 8, 16)
N_POOL_GROUPS = 4
POOL_IN = 512
POOL_GROUP_IN = 128
POOL_GROUP_OUT = 256
POOL_BUF = 15
D_IN = 2 * QK_W + 2 * V_W + POOL_IN + 2 * D_MODEL
D_FF = 2816
EPS = 1e-6
PAST_LEN = 16384

Q_OFF = 0
K_OFF = QK_W
V_OFF = 2 * QK_W
G_OFF = 2 * QK_W + V_W
U_OFF = 2 * QK_W + 2 * V_W
GA_OFF = U_OFF + POOL_IN
GB_OFF = GA_OFF + D_MODEL

LANES = 128
N_PAIRS = N_RET_HEADS // 2
POOL_PAD = 16
VMEM_LIMIT = 60 * 1024 * 1024

F32 = jnp.float32
BF16 = jnp.bfloat16


def _dot(a, b):
    return jnp.dot(a, b, preferred_element_type=F32)


def _rms(x, g):
    return x * lax.rsqrt(jnp.mean(x * x, axis=-1, keepdims=True) + EPS) * g


def _const_spec(shape):
    nd = len(shape)
    return pl.BlockSpec(shape, lambda *_: (0,) * nd)


FF_CHUNKS = ((0, 1024), (1024, 1024), (2048, 768))


STAGE_BUFS = 3
STAGE_BYTES = 3 * 512 * 1024


def _stage_rows(shape):
    r, c = shape
    fits = [n for n in range(16, r + 1, 16) if r % n == 0 and n * c * 4 <= STAGE_BYTES]
    return max(fits)


def _stage_scratch(shape):
    return [pltpu.VMEM(shape, BF16), pltpu.VMEM((STAGE_BUFS, _stage_rows(shape), shape[1]), F32),
            pltpu.SemaphoreType.DMA((STAGE_BUFS,))]


def _fetch_cast(src, dst, stage, sems):
    nbuf, rows, _ = stage.shape
    n = src.shape[0] // rows

    def copy(k):
        return pltpu.make_async_copy(src.at[pl.ds(k * rows, rows), :], stage.at[k % nbuf], sems.at[k % nbuf])

    for k in range(min(nbuf, n)):
        copy(k).start()
    for k in range(n):
        copy(k).wait()
        dst[k * rows:(k + 1) * rows, :] = stage[k % nbuf].astype(BF16)
        if k + nbuf < n:
            copy(k + nbuf).start()


def _convert_weights(first, layer, srcs, scratch):
    @pl.when(first)
    def _():
        for k, src in enumerate(srcs):
            _fetch_cast(src.at[layer], *scratch[3 * k:3 * k + 3])

    return scratch[0::3]


def _emit_weights(first, last, srcs, outs, sems):
    copies = [pltpu.make_async_copy(s, o, sems.at[k]) for k, (s, o) in enumerate(zip(srcs, outs))]

    @pl.when(first)
    def _():
        for c in copies:
            c.start()

    @pl.when(last)
    def _():
        for c in copies:
            c.wait()


def _ffn_body(*refs, final, layer):
    n_in = 6 if final else 5
    x_ref, xs_ref, g_ref, win_hbm, wout_hbm = refs[:5]
    gf_ref = refs[5] if final else None
    o_ref, os_ref = refs[n_in:n_in + 2]
    i = pl.program_id(0)
    win_ref, wout_ref = _convert_weights(i == 0, layer, (win_hbm, wout_hbm), refs[n_in + 2:])

    def ffn_rows(x):
        h = _rms(x, g_ref[...]).astype(BF16)
        acc = None
        for off, width in FF_CHUNKS:
            gate = _dot(h, win_ref[:, off:off + width])
            up = _dot(h, win_ref[:, D_FF + off:D_FF + off + width])
            a = (gate * jax.nn.sigmoid(gate) * up).astype(BF16)
            part = _dot(a, wout_ref[off:off + width, :])
            acc = part if acc is None else acc + part
        y = x + 0.5 * acc
        return _rms(y, gf_ref[...]) if final else y

    o_ref[...] = ffn_rows(x_ref[...])

    @pl.when(i == pl.num_programs(0) - 1)
    def _():
        os_ref[...] = ffn_rows(xs_ref[...])


def _layer_spec(shape, layer):
    nd = len(shape) - 1
    return pl.BlockSpec((None,) + tuple(shape[1:]), lambda *_: (layer,) + (0,) * nd,
                        pipeline_mode=pl.Buffered(1))


def _resident_spec(shape):
    nd = len(shape)
    return pl.BlockSpec(tuple(shape), lambda *_: (0,) * nd, pipeline_mode=pl.Buffered(1))


ANY_SPEC = pl.BlockSpec(memory_space=pl.ANY)


def _ffn(x2d, xs2d, g, w_in, w_out, layer, g_final=None, *, tm):
    n = x2d.shape[0]
    assert n % tm == 0
    final = g_final is not None
    row_spec = pl.BlockSpec((tm, D_MODEL), lambda i: (i, 0))
    in_specs = [row_spec, _resident_spec(xs2d.shape), _layer_spec(g.shape, layer), ANY_SPEC, ANY_SPEC]
    args = [x2d, xs2d, g, w_in, w_out]
    if final:
        in_specs.append(_const_spec((1, D_MODEL)))
        args.append(g_final.reshape(1, D_MODEL))
    scratch = _stage_scratch(w_in.shape[1:]) + _stage_scratch(w_out.shape[1:])
    return pl.pallas_call(
        functools.partial(_ffn_body, final=final, layer=layer),
        grid=(n // tm,),
        in_specs=in_specs,
        out_specs=[row_spec, _resident_spec(xs2d.shape)],
        out_shape=[jax.ShapeDtypeStruct(x2d.shape, F32), jax.ShapeDtypeStruct(xs2d.shape, F32)],
        scratch_shapes=scratch,
        compiler_params=pltpu.CompilerParams(
            dimension_semantics=("arbitrary",), vmem_limit_bytes=VMEM_LIMIT),
        name="ffn_final" if final else "ffn",
    )(*args)


IN_CHUNKS = tuple((o, 1024) for o in range(0, 5120, 1024)) + ((5120, 512),)


def _project_in(x, g_ref, win_ref, z_ref):
    h = _rms(x, g_ref[...]).astype(BF16)
    for off, width in IN_CHUNKS:
        z_ref[:, off:off + width] = _dot(h, win_ref[:, off:off + width])


def _rotate_inplace(z_ref, cos, sin_signed, first_half):
    for blk in range(2 * QK_W // LANES):
        cols = slice(blk * LANES, (blk + 1) * LANES)
        xb = z_ref[:, cols]
        partner = jnp.where(first_half, pltpu.roll(xb, LANES - 32, 1), pltpu.roll(xb, 32, 1))
        r = xb * cos + partner * sin_signed
        if blk >= QK_W // LANES:
            r = r * (RET_QK_DIM ** -0.5)
        z_ref[:, cols] = r


def _merge_out(x, z_ref, ret, pool, wout_ref):
    ga = z_ref[:, GA_OFF:GA_OFF + D_MODEL]
    gb = z_ref[:, GB_OFF:GB_OFF + D_MODEL]
    merged = jax.nn.sigmoid(ga) * ret + jax.nn.sigmoid(gb) * pool
    return x + _dot(merged.astype(BF16), wout_ref[...])


def _swish_gate_norm(o, gate):
    o = o * lax.rsqrt(jnp.mean(o * o, axis=-1, keepdims=True) + EPS)
    return o * (gate * jax.nn.sigmoid(gate))


def _mixer_prompt_body(*refs, tm, chunk, n_alias, layer):
    (x_ref, g_ref, win_ref, wpool_ref, pscale_ref, wout_ref, cos_ref, sin_ref,
     dmask_ref, xi_ref, wk_ref, gs_ref) = refs[:12]
    outs = refs[12 + n_alias:]
    o_ref, sret_ref, spool_ref = outs[:3]
    z_ref, s_ref, ext_ref, ret_ref, pool_ref, h_ref, d_ref = outs[6:13]
    t = pl.program_id(1)
    nt = pl.num_programs(1)
    bi = pl.program_id(0)
    first = (bi == 0) & (t == 0)
    win_ref, wpool_ref, wout_ref = _convert_weights(first, layer, (win_ref, wpool_ref, wout_ref), outs[13:22])
    _emit_weights(first, (bi == pl.num_programs(0) - 1) & (t == nt - 1),
                  (win_ref, wpool_ref, wout_ref), outs[3:6], outs[22])

    @pl.when(t == 0)
    def _():
        s_ref[...] = jnp.zeros_like(s_ref)
        ext_ref[0:POOL_PAD, :] = jnp.zeros((POOL_PAD, POOL_IN), F32)

    h_ref[...] = _rms(x_ref[...], g_ref[...]).astype(BF16)

    def project(off, width, act=None):
        y = _dot(h_ref[...], win_ref[:, off:off + width])
        z_ref[:, off:off + width] = y if act is None else act(y)

    project(Q_OFF, 2 * QK_W)
    project(V_OFF, V_W)
    project(U_OFF, POOL_IN)

    lane = lax.broadcasted_iota(jnp.int32, (1, LANES), 1)
    first_half = (lane % RET_QK_DIM) < (RET_QK_DIM // 2)
    _rotate_inplace(z_ref, cos_ref[...], sin_ref[...], first_half)

    ext_ref[POOL_PAD:POOL_PAD + tm, :] = z_ref[:, U_OFF:U_OFF + POOL_IN]
    pos = t * tm + lax.broadcasted_iota(jnp.int32, (tm, 1), 0)
    for g, w in enumerate(POOL_WINDOWS):
        cols = slice(g * POOL_GROUP_IN, (g + 1) * POOL_GROUP_IN)
        u = ext_ref[POOL_PAD:POOL_PAD + tm, cols]
        tot = u
        for j in range(1, w):
            tot = tot + ext_ref[POOL_PAD - j:POOL_PAD - j + tm, cols]
        cnt = jnp.minimum(pos + 1, w).astype(F32)
        d_ref[:, cols] = (tot / cnt - u).astype(BF16)
    tail = ext_ref[tm:tm + POOL_PAD, :]
    ext_ref[0:POOL_PAD, :] = tail

    def pool_matmul(g):
        ocols = slice(g * POOL_GROUP_OUT, (g + 1) * POOL_GROUP_OUT)
        y = _dot(d_ref[:, g * POOL_GROUP_IN:(g + 1) * POOL_GROUP_IN],
                 wpool_ref[g * POOL_GROUP_IN:(g + 1) * POOL_GROUP_IN, :])
        pool_ref[:, ocols] = y * pscale_ref[:, ocols]

    fill_width = 2 * LANES
    fillers = [functools.partial(project, off, fill_width, jax.nn.silu)
               for off in range(G_OFF, G_OFF + V_W, fill_width)]
    fillers += [functools.partial(project, off, fill_width, jax.nn.sigmoid)
                for off in range(GA_OFF, D_IN, fill_width)]
    fillers += [functools.partial(pool_matmul, g) for g in range(N_POOL_GROUPS)]

    head0 = (lane < RET_QK_DIM).astype(F32)
    head1 = 1.0 - head0

    n_chunks = tm // chunk
    stages = 3 * n_chunks

    def run_fillers(stage):
        for f in fillers[stage * len(fillers) // stages:(stage + 1) * len(fillers) // stages]:
            f()

    def pair_cols(off, pair, width):
        return slice(off + pair * width, off + (pair + 1) * width)

    for c in range(n_chunks):
        rows = slice(c * chunk, (c + 1) * chunk)
        scores = []
        for pair in range(N_PAIRS):
            qb = z_ref[rows, pair_cols(Q_OFF, pair, LANES)]
            kb = z_ref[rows, pair_cols(K_OFF, pair, LANES)]
            k_stack = jnp.concatenate([kb * head0, kb * head1], axis=0).astype(BF16)
            scores.append(lax.dot_general(qb.astype(BF16), k_stack, (((1,), (1,)), ((), ())),
                                          preferred_element_type=F32))
        run_fillers(3 * c)
        for pair in range(N_PAIRS):
            qx = z_ref[rows, pair_cols(Q_OFF, pair, LANES)] * xi_ref[pair]
            vb16 = z_ref[rows, pair_cols(V_OFF, pair, 2 * RET_V_DIM)].astype(BF16)
            s16 = s_ref[pair].astype(BF16)
            for j, hmask in enumerate((head0, head1)):
                h = 2 * pair + j
                scj = scores[pair][:, j * chunk:(j + 1) * chunk] * dmask_ref[h]
                lhs = jnp.concatenate([scj, qx * hmask], axis=1).astype(BF16)
                rhs = jnp.concatenate([vb16[:, j * RET_V_DIM:(j + 1) * RET_V_DIM], s16], axis=0)
                o = _dot(lhs, rhs)
                o = o * lax.rsqrt(jnp.mean(o * o, axis=-1, keepdims=True) + EPS)
                ret_ref[rows, h * RET_V_DIM:(h + 1) * RET_V_DIM] = o
        run_fillers(3 * c + 1)
        for pair in range(N_PAIRS):
            kw = (z_ref[rows, pair_cols(K_OFF, pair, LANES)] * wk_ref[pair]).astype(BF16)
            vb16 = z_ref[rows, pair_cols(V_OFF, pair, 2 * RET_V_DIM)].astype(BF16)
            upd = lax.dot_general(kw, vb16, (((0,), (0,)), ((), ())),
                                  preferred_element_type=F32)
            new_s = jnp.concatenate([upd[0:RET_QK_DIM, 0:RET_V_DIM],
                                     upd[RET_QK_DIM:, RET_V_DIM:]], axis=0)
            s_ref[pair] = s_ref[pair] * gs_ref[pair] + new_s
        run_fillers(3 * c + 2)

    n_halves = 2
    for r in range(n_halves):
        rows = slice(r * tm // n_halves, (r + 1) * tm // n_halves)
        ret = ret_ref[rows, :] * z_ref[rows, G_OFF:G_OFF + V_W]
        merged = (z_ref[rows, GA_OFF:GA_OFF + D_MODEL] * ret
                  + z_ref[rows, GB_OFF:GB_OFF + D_MODEL] * pool_ref[rows, :])
        o_ref[rows, :] = x_ref[rows, :] + _dot(merged.astype(BF16), wout_ref[...])

    @pl.when(t == nt - 1)
    def _():
        sret_ref[...] = s_ref[...].reshape(N_PAIRS * LANES, RET_V_DIM)
        spool_ref[...] = tail[POOL_PAD - POOL_BUF:, :]


def _state_alias(prev, n_in, first_out):
    if prev is None:
        return [], [], {}
    specs = [pl.BlockSpec(memory_space=pl.ANY) for _ in prev]
    aliases = {n_in + k: first_out + k for k in range(len(prev))}
    return list(prev), specs, aliases


def _mixer_prompt(x, g, w_in, w_pool, pool_scale, w_out, tabs, layer, prev_states, *, tm):
    b, t_len, _ = x.shape
    depth = w_in.shape[0]
    chunk = RET_CHUNK
    assert t_len % tm == 0 and tm % chunk == 0
    cos, sin, dmask, xi, wk, gs = tabs
    row_spec = pl.BlockSpec((None, tm, D_MODEL), lambda i, j: (i, j, 0))
    tab_spec = pl.BlockSpec((tm, LANES), lambda i, j: (j, 0))
    in_specs = [
        row_spec, _layer_spec(g.shape, layer), ANY_SPEC, ANY_SPEC, _layer_spec(pool_scale.shape, layer),
        ANY_SPEC, tab_spec, tab_spec,
        _const_spec(dmask.shape), _const_spec(xi.shape), _const_spec(wk.shape), _const_spec(gs.shape),
    ]
    args = [x, g, w_in, w_pool, pool_scale, w_out, cos, sin, dmask, xi, wk, gs]
    extra, extra_specs, aliases = _state_alias(prev_states, len(args), 1)
    w_shapes = [w_in.shape[1:], w_pool.shape[1:], w_out.shape[1:]]
    out_specs = [
        row_spec,
        pl.BlockSpec((None, None, N_PAIRS * LANES, RET_V_DIM), lambda i, j: (layer, i, 0, 0)),
        pl.BlockSpec((None, None, POOL_BUF, POOL_IN), lambda i, j: (layer, i, 0, 0)),
    ] + [ANY_SPEC] * len(w_shapes)
    out_shape = [
        jax.ShapeDtypeStruct((b, t_len, D_MODEL), F32),
        jax.ShapeDtypeStruct((depth, b, N_PAIRS * LANES, RET_V_DIM), F32),
        jax.ShapeDtypeStruct((depth, b, POOL_BUF, POOL_IN), F32),
    ] + [jax.ShapeDtypeStruct(s, BF16) for s in w_shapes]
    scratch = [
        pltpu.VMEM((tm, D_IN), F32),
        pltpu.VMEM((N_PAIRS, LANES, RET_V_DIM), F32),
        pltpu.VMEM((POOL_PAD + tm, POOL_IN), F32),
        pltpu.VMEM((tm, V_W), F32),
        pltpu.VMEM((tm, D_MODEL), F32),
        pltpu.VMEM((tm, D_MODEL), BF16),
        pltpu.VMEM((tm, POOL_IN), BF16),
    ]
    for s in w_shapes:
        scratch += _stage_scratch(s)
    scratch.append(pltpu.SemaphoreType.DMA((len(w_shapes),)))
    y, sret, spool, *w16 = pl.pallas_call(
        functools.partial(_mixer_prompt_body, tm=tm, chunk=chunk, n_alias=len(extra), layer=layer),
        grid=(b, t_len // tm),
        in_specs=in_specs + extra_specs,
        out_specs=out_specs,
        out_shape=out_shape,
        scratch_shapes=scratch,
        input_output_aliases=aliases,
        compiler_params=pltpu.CompilerParams(
            dimension_semantics=("arbitrary", "arbitrary"), vmem_limit_bytes=VMEM_LIMIT),
        name="mixer_prompt",
    )(*args, *extra)
    return y, (sret, spool), tuple(w16)


def _log_gamma():
    return jnp.log1p(-jnp.exp2(-5.0 - jnp.arange(N_RET_HEADS, dtype=F32)))


def _rope_tables(pos):
    half = RET_QK_DIM // 2
    inv = jnp.power(jnp.float32(ROPE_BASE), -jnp.linspace(0.0, 1.0, half, dtype=F32))
    ang = pos[:, None] * inv[None, :]
    cos, sin = jnp.cos(ang), jnp.sin(ang)
    cos128 = jnp.tile(cos, (1, LANES // half))
    sin128 = jnp.tile(jnp.concatenate([-sin, sin], axis=1), (1, LANES // RET_QK_DIM))
    return cos128, sin128


def _pair_lanes(per_head):
    h, c = per_head.shape
    x = jnp.broadcast_to(per_head[:, :, None], (h, c, RET_QK_DIM))
    return x.reshape(N_PAIRS, 2, c, RET_QK_DIM).transpose(0, 2, 1, 3).reshape(N_PAIRS, c, LANES)


def _state_decay(chunk):
    g_c = jnp.exp(_log_gamma() * chunk)
    gs = jnp.broadcast_to(g_c[:, None, None], (N_RET_HEADS, RET_QK_DIM, RET_V_DIM))
    return gs.reshape(N_PAIRS, LANES, RET_V_DIM)


def _decay_tables(chunk):
    lg = _log_gamma()
    i = jnp.arange(chunk, dtype=F32)
    dist = i[:, None] - i[None, :]
    dmask = jnp.where(dist[None] >= 0, jnp.exp(lg[:, None, None] * jnp.maximum(dist, 0.0)[None]), 0.0)
    xi = jnp.exp(lg[:, None] * (i[None, :] + 1.0))
    wk = jnp.exp(lg[:, None] * (chunk - 1.0 - i[None, :]))
    return dmask, _pair_lanes(xi), _pair_lanes(wk), _state_decay(chunk)


def _mixer_sample_body(*refs, bb, ts, past, n_alias):
    (x_ref, g_ref, win_ref, wpool_ref, pscale_ref, wout_ref, cos_ref, sin_ref,
     xi_ref, wk_ref, lag_ref, gs_ref, ind_ref, s0_ref, pb_ref) = refs[:15]
    (o_ref, snew_ref, pnew_ref, z_ref, inter_ref, ublk_ref, pblk_ref) = refs[15 + n_alias:]
    rows = bb * ts
    x = x_ref[...]
    _project_in(x, g_ref, win_ref, z_ref)
    lane = lax.broadcasted_iota(jnp.int32, (1, LANES), 1)
    first_half = (lane % RET_QK_DIM) < (RET_QK_DIM // 2)
    _rotate_inplace(z_ref, cos_ref[...], sin_ref[...], first_half)

    q = z_ref[:, Q_OFF:Q_OFF + QK_W]
    k = z_ref[:, K_OFF:K_OFF + QK_W]
    v = z_ref[:, V_OFF:V_OFF + V_W]
    step = lax.broadcasted_iota(jnp.int32, (rows, 1), 0) % ts
    intra = None
    for d in range(ts):
        kd = k if d == 0 else pltpu.roll(k, d, 0)
        vd = v if d == 0 else pltpu.roll(v, d, 0)
        prod = jnp.where(step >= d, q * kd, 0.0).astype(BF16)
        score = _dot(prod, ind_ref[...])
        term = score * lag_ref[d:d + 1, :] * jnp.where(step >= d, vd, 0.0)
        intra = term if intra is None else intra + term

    row8 = lax.broadcasted_iota(jnp.int32, (2 * ts, 1), 0)
    top = (lax.broadcasted_iota(jnp.int32, (LANES, 1), 0) < RET_QK_DIM).astype(F32)
    bot = 1.0 - top

    def seq_pair(p, carry):
        r0 = pl.multiple_of(p * 2 * ts, 2 * ts)
        rsl = pl.ds(r0, 2 * ts)
        for pair in range(N_PAIRS):
            lanes = slice(pair * LANES, (pair + 1) * LANES)
            q8 = (z_ref[rsl, Q_OFF + pair * LANES:Q_OFF + (pair + 1) * LANES] * xi_ref[:, lanes]).astype(BF16)
            kw8 = z_ref[rsl, K_OFF + pair * LANES:K_OFF + (pair + 1) * LANES] * wk_ref[:, lanes]
            v8 = z_ref[rsl, V_OFF + pair * 2 * RET_V_DIM:V_OFF + (pair + 1) * 2 * RET_V_DIM].astype(BF16)
            outs = []
            for j in range(2):
                b = 2 * p + j
                s_pair = s0_ref[b, lanes, :]
                bd = jnp.concatenate([s_pair * top, s_pair * bot], axis=1).astype(BF16)
                outs.append(_dot(q8, bd))
                mine = (row8 >= j * ts) & (row8 < (j + 1) * ts)
                kwj = jnp.where(mine, kw8, 0.0).astype(BF16)
                upd = lax.dot_general(kwj, v8, (((0,), (0,)), ((), ())), preferred_element_type=F32)
                new_s = jnp.concatenate([upd[0:RET_QK_DIM, 0:RET_V_DIM],
                                         upd[RET_QK_DIM:, RET_V_DIM:]], axis=0)
                snew_ref[b, lanes, :] = s_pair * gs_ref[pair] + new_s
            inter_ref[rsl, pair * 2 * RET_V_DIM:(pair + 1) * 2 * RET_V_DIM] = jnp.where(
                row8 < ts, outs[0], outs[1])
        return carry

    lax.fori_loop(0, bb // 2, seq_pair, 0, unroll=8)

    keep = POOL_BUF - ts
    pnew_ref[0:keep] = pb_ref[ts:]
    for g, w in enumerate(POOL_WINDOWS):
        ublk_ref[g] = z_ref[:, U_OFF + g * POOL_GROUP_IN:U_OFF + (g + 1) * POOL_GROUP_IN]
        u_steps = [ublk_ref[g, pl.ds(t, bb, stride=ts), :] for t in range(ts)]
        for t in range(ts):
            pnew_ref[keep + t, :, g * POOL_GROUP_IN:(g + 1) * POOL_GROUP_IN] = u_steps[t]
        ds_ = []
        for t in range(ts):
            tot = None
            for j in range(w):
                idx = POOL_BUF + t - j
                if idx >= POOL_BUF:
                    term = u_steps[idx - POOL_BUF]
                else:
                    term = pb_ref[idx, :, g * POOL_GROUP_IN:(g + 1) * POOL_GROUP_IN]
                tot = term if tot is None else tot + term
            cnt = float(min(past + t + 1, w))
            ds_.append(tot / cnt - u_steps[t])
        y = _dot(jnp.concatenate(ds_, axis=0).astype(BF16),
                 wpool_ref[g * POOL_GROUP_IN:(g + 1) * POOL_GROUP_IN, :])
        y = y * pscale_ref[:, g * POOL_GROUP_OUT:(g + 1) * POOL_GROUP_OUT]
        for half in range(POOL_GROUP_OUT // LANES):
            blk = g * (POOL_GROUP_OUT // LANES) + half
            for t in range(ts):
                pblk_ref[blk, pl.ds(t, bb, stride=ts), :] = y[t * bb:(t + 1) * bb, half * LANES:(half + 1) * LANES]
    pool = jnp.concatenate([pblk_ref[blk] for blk in range(D_MODEL // LANES)], axis=1)

    o_all = intra + inter_ref[...]
    rets = []
    for h in range(N_RET_HEADS):
        hc = slice(h * RET_V_DIM, (h + 1) * RET_V_DIM)
        rets.append(_swish_gate_norm(o_all[:, hc], z_ref[:, G_OFF + h * RET_V_DIM:G_OFF + (h + 1) * RET_V_DIM]))
    ret = jnp.concatenate(rets, axis=1)
    o_ref[...] = _merge_out(x, z_ref, ret, pool, wout_ref)


def _sample_tables(ts, past, rows):
    lg = _log_gamma()
    i = jnp.arange(ts, dtype=F32)
    lag = jnp.exp(lg[:, None] * i[None, :])
    lag = jnp.repeat(lag.T, RET_V_DIM, axis=1)
    xi = jnp.exp(lg[:, None] * (i[None, :] + 1.0))
    wk = jnp.exp(lg[:, None] * (ts - 1.0 - i[None, :]))
    expand = lambda a: jnp.tile(jnp.repeat(a.T, RET_QK_DIM, axis=1), (2, 1))
    pos = (past + jnp.arange(ts, dtype=jnp.int32)).astype(F32)
    cos, sin = _rope_tables(pos)
    reps = rows // ts
    head_of_k = jnp.arange(QK_W) // RET_QK_DIM
    head_of_v = jnp.arange(V_W) // RET_V_DIM
    ind = (head_of_k[:, None] == head_of_v[None, :]).astype(BF16)
    return (jnp.tile(cos, (reps, 1)), jnp.tile(sin, (reps, 1)), expand(xi), expand(wk), lag,
            _state_decay(ts), ind)


def _mixer_sample(x2d, s0, pbuf, g, w_in, w_pool, pool_scale, w_out, tabs, layer, prev_states, *, bb, ts, past):
    n = x2d.shape[0]
    depth = s0.shape[0]
    nb = n // ts
    rows = bb * ts
    assert nb % bb == 0 and bb % 2 == 0 and (2 * ts) % 8 == 0 and ts <= POOL_BUF
    cos, sin, xi, wk, lag, gs, ind = tabs
    row_spec = pl.BlockSpec((rows, D_MODEL), lambda i: (i, 0))
    state_spec = pl.BlockSpec((None, bb, N_PAIRS * LANES, RET_V_DIM), lambda i: (layer, i, 0, 0))
    pool_spec = pl.BlockSpec((None, POOL_BUF, bb, POOL_IN), lambda i: (layer, 0, i, 0))
    in_specs = [
        row_spec, _layer_spec(g.shape, layer), _resident_spec(w_in.shape),
        _resident_spec(w_pool.shape), _layer_spec(pool_scale.shape, layer),
        _resident_spec(w_out.shape), _const_spec(cos.shape), _const_spec(sin.shape),
        _const_spec(xi.shape), _const_spec(wk.shape), _const_spec(lag.shape), _const_spec(gs.shape),
        _const_spec(ind.shape), state_spec, pool_spec,
    ]
    args = [x2d, g, w_in, w_pool, pool_scale, w_out, cos, sin, xi, wk, lag, gs, ind, s0, pbuf]
    extra, extra_specs, aliases = _state_alias(prev_states, len(args), 1)
    out_specs = [row_spec, state_spec, pool_spec]
    out_shape = [
        jax.ShapeDtypeStruct((n, D_MODEL), F32),
        jax.ShapeDtypeStruct((depth, nb, N_PAIRS * LANES, RET_V_DIM), F32),
        jax.ShapeDtypeStruct((depth, POOL_BUF, nb, POOL_IN), F32),
    ]
    scratch = [
        pltpu.VMEM((rows, D_IN), F32),
        pltpu.VMEM((rows, V_W), F32),
        pltpu.VMEM((N_POOL_GROUPS, rows, LANES), F32),
        pltpu.VMEM((D_MODEL // LANES, rows, LANES), F32),
    ]
    y, snew, pnew = pl.pallas_call(
        functools.partial(_mixer_sample_body, bb=bb, ts=ts, past=past, n_alias=len(extra)),
        grid=(nb // bb,),
        in_specs=in_specs + extra_specs,
        out_specs=out_specs,
        out_shape=out_shape,
        scratch_shapes=scratch,
        input_output_aliases=aliases,
        compiler_params=pltpu.CompilerParams(
            dimension_semantics=("arbitrary",), vmem_limit_bytes=VMEM_LIMIT),
        name="mixer_sample",
    )(*args, *extra)
    return y, (snew, pnew)


def kernel(x_prompt, x_sample, state_ret, state_pool, norm_ffn1, w_ffn1_in, w_ffn1_out, norm_mix,
           w_in, w_pool, pool_scale, w_out, norm_ffn2, w_ffn2_in, w_ffn2_out, norm_final):
    depth = w_in.shape[0]
    bp, tp, _ = x_prompt.shape
    bs, ts, _ = x_sample.shape
    wp = w_pool.reshape(depth, N_POOL_GROUPS * POOL_GROUP_IN, POOL_GROUP_OUT)

    row = lambda a: a.reshape(depth, 1, D_MODEL)
    n1, nm, n2, ps = map(row, (norm_ffn1, norm_mix, norm_ffn2, pool_scale))
    gf = norm_final.reshape(1, D_MODEL)
    last = depth - 1

    tm = 512
    tm_ffn = 1024
    pos_p = jnp.arange(tp, dtype=jnp.int32).astype(F32)
    tabs_p = _rope_tables(pos_p) + _decay_tables(RET_CHUNK)

    bb = 32
    tabs_s = _sample_tables(ts, PAST_LEN, bb * ts)
    s0_all = state_ret.reshape(depth, bs, N_PAIRS * LANES, RET_V_DIM)
    pb_all = jnp.swapaxes(state_pool, 1, 2)

    x = x_prompt.reshape(bp * tp, D_MODEL)
    xs = x_sample.reshape(bs * ts, D_MODEL)
    states_p = states_s = None
    for l in range(depth):
        x, xs = _ffn(x, xs, n1, w_ffn1_in, w_ffn1_out, l, tm=tm_ffn)
        x, states_p, (wi16, wp16, wo16) = _mixer_prompt(x.reshape(bp, tp, D_MODEL), nm, w_in, wp, ps, w_out,
                                                        tabs_p, l, states_p, tm=tm)
        xs, states_s = _mixer_sample(xs, s0_all, pb_all, nm, wi16, wp16, ps, wo16, tabs_s, l, states_s,
                                     bb=bb, ts=ts, past=PAST_LEN)
        x, xs = _ffn(x.reshape(bp * tp, D_MODEL), xs, n2, w_ffn2_in, w_ffn2_out, l,
                     gf if l == last else None, tm=tm_ffn)
    y_prompt = x.reshape(bp, tp, D_MODEL)
    y_sample = xs.reshape(bs, ts, D_MODEL)
    state_ret_prompt = states_p[0].reshape(depth, bp, N_RET_HEADS, RET_QK_DIM, RET_V_DIM)
    state_pool_prompt = states_p[1]
    state_ret_sample = states_s[0].reshape(depth, bs, N_RET_HEADS, RET_QK_DIM, RET_V_DIM)
    state_pool_sample = jnp.swapaxes(states_s[1], 1, 2)
    return (y_prompt, y_sample, state_ret_prompt, state_ret_sample, state_pool_prompt, state_pool_sample)
```

```python
import functools

import jax
import jax.numpy as jnp
from jax import lax
from jax.experimental import pallas as pl
from jax.experimental.pallas import tpu as pltpu

D_MODEL = 1024
N_RET_HEADS = 8
RET_QK_DIM = 64
RET_V_DIM = 128
RET_CHUNK = 128
ROPE_BASE = 10000.0
QK_W = N_RET_HEADS * RET_QK_DIM
V_W = N_RET_HEADS * RET_V_DIM
POOL_WINDOWS = (2, 4, 8, 16)
N_POOL_GROUPS = 4
POOL_IN = 512
POOL_GROUP_IN = 128
POOL_GROUP_OUT = 256
POOL_BUF = 15
D_IN = 2 * QK_W + 2 * V_W + POOL_IN + 2 * D_MODEL
D_FF = 2816
EPS = 1e-6
PAST_LEN = 16384

Q_OFF = 0
K_OFF = QK_W
V_OFF = 2 * QK_W
G_OFF = 2 * QK_W + V_W
U_OFF = 2 * QK_W + 2 * V_W
GA_OFF = U_OFF + POOL_IN
GB_OFF = GA_OFF + D_MODEL

LANES = 128
N_PAIRS = N_RET_HEADS // 2
POOL_PAD = 16
VMEM_LIMIT = 60 * 1024 * 1024

F32 = jnp.float32
BF16 = jnp.bfloat16


def _dot(a, b):
    return jnp.dot(a, b, preferred_element_type=F32)


def _rms(x, g):
    return x * lax.rsqrt(jnp.mean(x * x, axis=-1, keepdims=True) + EPS) * g


def _const_spec(shape):
    nd = len(shape)
    return pl.BlockSpec(shape, lambda *_: (0,) * nd)


FF_CHUNKS = ((0, 1024), (1024, 1024), (2048, 768))


STAGE_BUFS = 3
STAGE_BYTES = 3 * 512 * 1024


def _stage_rows(shape):
    r, c = shape
    fits = [n for n in range(16, r + 1, 16) if r % n == 0 and n * c * 4 <= STAGE_BYTES]
    return max(fits)


def _stage_scratch(shape):
    return [pltpu.VMEM(shape, BF16), pltpu.VMEM((STAGE_BUFS, _stage_rows(shape), shape[1]), F32),
            pltpu.SemaphoreType.DMA((STAGE_BUFS,))]


def _fetch_cast(src, dst, stage, sems):
    nbuf, rows, _ = stage.shape
    n = src.shape[0] // rows

    def copy(k):
        return pltpu.make_async_copy(src.at[pl.ds(k * rows, rows), :], stage.at[k % nbuf], sems.at[k % nbuf])

    for k in range(min(nbuf, n)):
        copy(k).start()
    for k in range(n):
        copy(k).wait()
        dst[k * rows:(k + 1) * rows, :] = stage[k % nbuf].astype(BF16)
        if k + nbuf < n:
            copy(k + nbuf).start()


def _convert_weights(first, layer, srcs, scratch):
    @pl.when(first)
    def _():
        for k, src in enumerate(srcs):
            _fetch_cast(src.at[layer], *scratch[3 * k:3 * k + 3])

    return scratch[0::3]


def _emit_weights(first, last, srcs, outs, sems):
    copies = [pltpu.make_async_copy(s, o, sems.at[k]) for k, (s, o) in enumerate(zip(srcs, outs))]

    @pl.when(first)
    def _():
        for c in copies:
            c.start()

    @pl.when(last)
    def _():
        for c in copies:
            c.wait()


def _ffn_body(*refs, final, layer):
    n_in = 6 if final else 5
    x_ref, xs_ref, g_ref, win_hbm, wout_hbm = refs[:5]
    gf_ref = refs[5] if final else None
    o_ref, os_ref = refs[n_in:n_in + 2]
    i = pl.program_id(0)
    win_ref, wout_ref = _convert_weights(i == 0, layer, (win_hbm, wout_hbm), refs[n_in + 2:])

    def ffn_rows(x):
        h = _rms(x, g_ref[...]).astype(BF16)
        acc = None
        for off, width in FF_CHUNKS:
            gate = _dot(h, win_ref[:, off:off + width])
            up = _dot(h, win_ref[:, D_FF + off:D_FF + off + width])
            a = (gate * jax.nn.sigmoid(gate) * up).astype(BF16)
            part = _dot(a, wout_ref[off:off + width, :])
            acc = part if acc is None else acc + part
        y = x + 0.5 * acc
        return _rms(y, gf_ref[...]) if final else y

    o_ref[...] = ffn_rows(x_ref[...])

    @pl.when(i == pl.num_programs(0) - 1)
    def _():
        os_ref[...] = ffn_rows(xs_ref[...])


def _layer_spec(shape, layer):
    nd = len(shape) - 1
    return pl.BlockSpec((None,) + tuple(shape[1:]), lambda *_: (layer,) + (0,) * nd,
                        pipeline_mode=pl.Buffered(1))


def _resident_spec(shape):
    nd = len(shape)
    return pl.BlockSpec(tuple(shape), lambda *_: (0,) * nd, pipeline_mode=pl.Buffered(1))


ANY_SPEC = pl.BlockSpec(memory_space=pl.ANY)


def _ffn(x2d, xs2d, g, w_in, w_out, layer, g_final=None, *, tm):
    n = x2d.shape[0]
    assert n % tm == 0
    final = g_final is not None
    row_spec = pl.BlockSpec((tm, D_MODEL), lambda i: (i, 0))
    in_specs = [row_spec, _resident_spec(xs2d.shape), _layer_spec(g.shape, layer), ANY_SPEC, ANY_SPEC]
    args = [x2d, xs2d, g, w_in, w_out]
    if final:
        in_specs.append(_const_spec((1, D_MODEL)))
        args.append(g_final.reshape(1, D_MODEL))
    scratch = _stage_scratch(w_in.shape[1:]) + _stage_scratch(w_out.shape[1:])
    return pl.pallas_call(
        functools.partial(_ffn_body, final=final, layer=layer),
        grid=(n // tm,),
        in_specs=in_specs,
        out_specs=[row_spec, _resident_spec(xs2d.shape)],
        out_shape=[jax.ShapeDtypeStruct(x2d.shape, F32), jax.ShapeDtypeStruct(xs2d.shape, F32)],
        scratch_shapes=scratch,
        compiler_params=pltpu.CompilerParams(
            dimension_semantics=("arbitrary",), vmem_limit_bytes=VMEM_LIMIT),
        name="ffn_final" if final else "ffn",
    )(*args)


IN_CHUNKS = tuple((o, 1024) for o in range(0, 5120, 1024)) + ((5120, 512),)


def _project_in(x, g_ref, win_ref, z_ref):
    h = _rms(x, g_ref[...]).astype(BF16)
    for off, width in IN_CHUNKS:
        z_ref[:, off:off + width] = _dot(h, win_ref[:, off:off + width])


def _rotate_inplace(z_ref, cos, sin_signed, first_half):
    for blk in range(2 * QK_W // LANES):
        cols = slice(blk * LANES, (blk + 1) * LANES)
        xb = z_ref[:, cols]
        partner = jnp.where(first_half, pltpu.roll(xb, LANES - 32, 1), pltpu.roll(xb, 32, 1))
        r = xb * cos + partner * sin_signed
        if blk >= QK_W // LANES:
            r = r * (RET_QK_DIM ** -0.5)
        z_ref[:, cols] = r


def _merge_out(x, z_ref, ret, pool, wout_ref):
    ga = z_ref[:, GA_OFF:GA_OFF + D_MODEL]
    gb = z_ref[:, GB_OFF:GB_OFF + D_MODEL]
    merged = jax.nn.sigmoid(ga) * ret + jax.nn.sigmoid(gb) * pool
    return x + _dot(merged.astype(BF16), wout_ref[...])


def _swish_gate_norm(o, gate):
    o = o * lax.rsqrt(jnp.mean(o * o, axis=-1, keepdims=True) + EPS)
    return o * (gate * jax.nn.sigmoid(gate))


def _mixer_prompt_body(*refs, tm, n_sub, chunk, n_alias, layer):
    (x_ref, g_ref, win_ref, wpool_ref, pscale_ref, wout_ref, cos_ref, sin_ref,
     dmask_ref, xi_ref, wk_ref, gs_ref) = refs[:12]
    outs = refs[12 + n_alias:]
    o_ref, sret_ref, spool_ref = outs[:3]
    z_ref, s_ref, ext_ref, ret_ref, pool_ref, h_ref, d_ref = outs[6:13]
    t = pl.program_id(1)
    nt = pl.num_programs(1)
    bi = pl.program_id(0)
    first = (bi == 0) & (t == 0)
    win_ref, wpool_ref, wout_ref = _convert_weights(first, layer, (win_ref, wpool_ref, wout_ref), outs[13:22])
    _emit_weights(first, (bi == pl.num_programs(0) - 1) & (t == nt - 1),
                  (win_ref, wpool_ref, wout_ref), outs[3:6], outs[22])

    @pl.when(t == 0)
    def _():
        s_ref[...] = jnp.zeros_like(s_ref)
        ext_ref[0:POOL_PAD, :] = jnp.zeros((POOL_PAD, POOL_IN), F32)

    tile_refs = (x_ref, g_ref, win_ref, wpool_ref, pscale_ref, wout_ref, cos_ref, sin_ref, dmask_ref, xi_ref,
                 wk_ref, gs_ref, o_ref, z_ref, s_ref, ext_ref, ret_ref, pool_ref, h_ref, d_ref)
    for sub in range(n_sub):
        tail = _mixer_prompt_tile(tile_refs, sub * tm, (t * n_sub + sub) * tm, tm, chunk)

    @pl.when(t == nt - 1)
    def _():
        sret_ref[...] = s_ref[...].reshape(N_PAIRS * LANES, RET_V_DIM)
        spool_ref[...] = tail[POOL_PAD - POOL_BUF:, :]


def _mixer_prompt_tile(tile_refs, r0, pos0, tm, chunk):
    (x_ref, g_ref, win_ref, wpool_ref, pscale_ref, wout_ref, cos_ref, sin_ref, dmask_ref, xi_ref,
     wk_ref, gs_ref, o_ref, z_ref, s_ref, ext_ref, ret_ref, pool_ref, h_ref, d_ref) = tile_refs
    h_ref[...] = _rms(x_ref[r0:r0 + tm, :], g_ref[...]).astype(BF16)

    def project(off, width, act=None):
        y = _dot(h_ref[...], win_ref[:, off:off + width])
        z_ref[:, off:off + width] = y if act is None else act(y)

    project(Q_OFF, 2 * QK_W)
    project(V_OFF, V_W)
    project(U_OFF, POOL_IN)

    lane = lax.broadcasted_iota(jnp.int32, (1, LANES), 1)
    first_half = (lane % RET_QK_DIM) < (RET_QK_DIM // 2)
    _rotate_inplace(z_ref, cos_ref[r0:r0 + tm, :], sin_ref[r0:r0 + tm, :], first_half)

    ext_ref[POOL_PAD:POOL_PAD + tm, :] = z_ref[:, U_OFF:U_OFF + POOL_IN]
    pos = pos0 + lax.broadcasted_iota(jnp.int32, (tm, 1), 0)
    for g, w in enumerate(POOL_WINDOWS):
        cols = slice(g * POOL_GROUP_IN, (g + 1) * POOL_GROUP_IN)
        u = ext_ref[POOL_PAD:POOL_PAD + tm, cols]
        tot = u
        for j in range(1, w):
            tot = tot + ext_ref[POOL_PAD - j:POOL_PAD - j + tm, cols]
        cnt = jnp.minimum(pos + 1, w).astype(F32)
        d_ref[:, cols] = (tot / cnt - u).astype(BF16)
    tail = ext_ref[tm:tm + POOL_PAD, :]
    ext_ref[0:POOL_PAD, :] = tail

    def pool_matmul(g):
        ocols = slice(g * POOL_GROUP_OUT, (g + 1) * POOL_GROUP_OUT)
        y = _dot(d_ref[:, g * POOL_GROUP_IN:(g + 1) * POOL_GROUP_IN],
                 wpool_ref[g * POOL_GROUP_IN:(g + 1) * POOL_GROUP_IN, :])
        pool_ref[:, ocols] = y * pscale_ref[:, ocols]

    fill_width = 2 * LANES
    fillers = [functools.partial(project, off, fill_width, jax.nn.silu)
               for off in range(G_OFF, G_OFF + V_W, fill_width)]
    fillers += [functools.partial(project, off, fill_width, jax.nn.sigmoid)
                for off in range(GA_OFF, D_IN, fill_width)]
    fillers += [functools.partial(pool_matmul, g) for g in range(N_POOL_GROUPS)]

    head0 = (lane < RET_QK_DIM).astype(F32)
    head1 = 1.0 - head0

    n_chunks = tm // chunk
    stages = 3 * n_chunks

    def run_fillers(stage):
        for f in fillers[stage * len(fillers) // stages:(stage + 1) * len(fillers) // stages]:
            f()

    def pair_cols(off, pair, width):
        return slice(off + pair * width, off + (pair + 1) * width)

    for c in range(n_chunks):
        rows = slice(c * chunk, (c + 1) * chunk)
        scores = []
        for pair in range(N_PAIRS):
            qb = z_ref[rows, pair_cols(Q_OFF, pair, LANES)]
            kb = z_ref[rows, pair_cols(K_OFF, pair, LANES)]
            k_stack = jnp.concatenate([kb * head0, kb * head1], axis=0).astype(BF16)
            scores.append(lax.dot_general(qb.astype(BF16), k_stack, (((1,), (1,)), ((), ())),
                                          preferred_element_type=F32))
        run_fillers(3 * c)
        for pair in range(N_PAIRS):
            qx = z_ref[rows, pair_cols(Q_OFF, pair, LANES)] * xi_ref[pair]
            vb16 = z_ref[rows, pair_cols(V_OFF, pair, 2 * RET_V_DIM)].astype(BF16)
            s16 = s_ref[pair].astype(BF16)
            for j, hmask in enumerate((head0, head1)):
                h = 2 * pair + j
                scj = scores[pair][:, j * chunk:(j + 1) * chunk] * dmask_ref[h]
                lhs = jnp.concatenate([scj, qx * hmask], axis=1).astype(BF16)
                rhs = jnp.concatenate([vb16[:, j * RET_V_DIM:(j + 1) * RET_V_DIM], s16], axis=0)
                o = _dot(lhs, rhs)
                o = o * lax.rsqrt(jnp.mean(o * o, axis=-1, keepdims=True) + EPS)
                ret_ref[rows, h * RET_V_DIM:(h + 1) * RET_V_DIM] = o
        run_fillers(3 * c + 1)
        for pair in range(N_PAIRS):
            kw = (z_ref[rows, pair_cols(K_OFF, pair, LANES)] * wk_ref[pair]).astype(BF16)
            vb16 = z_ref[rows, pair_cols(V_OFF, pair, 2 * RET_V_DIM)].astype(BF16)
            upd = lax.dot_general(kw, vb16, (((0,), (0,)), ((), ())),
                                  preferred_element_type=F32)
            new_s = jnp.concatenate([upd[0:RET_QK_DIM, 0:RET_V_DIM],
                                     upd[RET_QK_DIM:, RET_V_DIM:]], axis=0)
            s_ref[pair] = s_ref[pair] * gs_ref[pair] + new_s
        run_fillers(3 * c + 2)

    n_halves = 2
    for r in range(n_halves):
        rows = slice(r * tm // n_halves, (r + 1) * tm // n_halves)
        ret = ret_ref[rows, :] * z_ref[rows, G_OFF:G_OFF + V_W]
        merged = (z_ref[rows, GA_OFF:GA_OFF + D_MODEL] * ret
                  + z_ref[rows, GB_OFF:GB_OFF + D_MODEL] * pool_ref[rows, :])
        orows = slice(r0 + r * tm // n_halves, r0 + (r + 1) * tm // n_halves)
        o_ref[orows, :] = x_ref[orows, :] + _dot(merged.astype(BF16), wout_ref[...])
    return tail


def _state_alias(prev, n_in, first_out):
    if prev is None:
        return [], [], {}
    specs = [pl.BlockSpec(memory_space=pl.ANY) for _ in prev]
    aliases = {n_in + k: first_out + k for k in range(len(prev))}
    return list(prev), specs, aliases


def _mixer_prompt(x, g, w_in, w_pool, pool_scale, w_out, tabs, layer, prev_states, *, tm, n_sub):
    b, t_len, _ = x.shape
    depth = w_in.shape[0]
    chunk = RET_CHUNK
    blk = n_sub * tm
    assert t_len % blk == 0 and tm % chunk == 0
    cos, sin, dmask, xi, wk, gs = tabs
    row_spec = pl.BlockSpec((None, blk, D_MODEL), lambda i, j: (i, j, 0))
    tab_spec = pl.BlockSpec((blk, LANES), lambda i, j: (j, 0))
    in_specs = [
        row_spec, _layer_spec(g.shape, layer), ANY_SPEC, ANY_SPEC, _layer_spec(pool_scale.shape, layer),
        ANY_SPEC, tab_spec, tab_spec,
        _const_spec(dmask.shape), _const_spec(xi.shape), _const_spec(wk.shape), _const_spec(gs.shape),
    ]
    args = [x, g, w_in, w_pool, pool_scale, w_out, cos, sin, dmask, xi, wk, gs]
    extra, extra_specs, aliases = _state_alias(prev_states, len(args), 1)
    w_shapes = [w_in.shape[1:], w_pool.shape[1:], w_out.shape[1:]]
    out_specs = [
        row_spec,
        pl.BlockSpec((None, None, N_PAIRS * LANES, RET_V_DIM), lambda i, j: (layer, i, 0, 0)),
        pl.BlockSpec((None, None, POOL_BUF, POOL_IN), lambda i, j: (layer, i, 0, 0)),
    ] + [ANY_SPEC] * len(w_shapes)
    out_shape = [
        jax.ShapeDtypeStruct((b, t_len, D_MODEL), F32),
        jax.ShapeDtypeStruct((depth, b, N_PAIRS * LANES, RET_V_DIM), F32),
        jax.ShapeDtypeStruct((depth, b, POOL_BUF, POOL_IN), F32),
    ] + [jax.ShapeDtypeStruct(s, BF16) for s in w_shapes]
    scratch = [
        pltpu.VMEM((tm, D_IN), F32),
        pltpu.VMEM((N_PAIRS, LANES, RET_V_DIM), F32),
        pltpu.VMEM((POOL_PAD + tm, POOL_IN), F32),
        pltpu.VMEM((tm, V_W), F32),
        pltpu.VMEM((tm, D_MODEL), F32),
        pltpu.VMEM((tm, D_MODEL), BF16),
        pltpu.VMEM((tm, POOL_IN), BF16),
    ]
    for s in w_shapes:
        scratch += _stage_scratch(s)
    scratch.append(pltpu.SemaphoreType.DMA((len(w_shapes),)))
    y, sret, spool, *w16 = pl.pallas_call(
        functools.partial(_mixer_prompt_body, tm=tm, n_sub=n_sub, chunk=chunk, n_alias=len(extra), layer=layer),
        grid=(b, t_len // blk),
        in_specs=in_specs + extra_specs,
        out_specs=out_specs,
        out_shape=out_shape,
        scratch_shapes=scratch,
        input_output_aliases=aliases,
        compiler_params=pltpu.CompilerParams(
            dimension_semantics=("arbitrary", "arbitrary"), vmem_limit_bytes=VMEM_LIMIT),
        name="mixer_prompt",
    )(*args, *extra)
    return y, (sret, spool), tuple(w16)


def _log_gamma():
    return jnp.log1p(-jnp.exp2(-5.0 - jnp.arange(N_RET_HEADS, dtype=F32)))


def _rope_tables(pos):
    half = RET_QK_DIM // 2
    inv = jnp.power(jnp.float32(ROPE_BASE), -jnp.linspace(0.0, 1.0, half, dtype=F32))
    ang = pos[:, None] * inv[None, :]
    cos, sin = jnp.cos(ang), jnp.sin(ang)
    cos128 = jnp.tile(cos, (1, LANES // half))
    sin128 = jnp.tile(jnp.concatenate([-sin, sin], axis=1), (1, LANES // RET_QK_DIM))
    return cos128, sin128


def _pair_lanes(per_head):
    h, c = per_head.shape
    x = jnp.broadcast_to(per_head[:, :, None], (h, c, RET_QK_DIM))
    return x.reshape(N_PAIRS, 2, c, RET_QK_DIM).transpose(0, 2, 1, 3).reshape(N_PAIRS, c, LANES)


def _state_decay(chunk):
    g_c = jnp.exp(_log_gamma() * chunk)
    gs = jnp.broadcast_to(g_c[:, None, None], (N_RET_HEADS, RET_QK_DIM, RET_V_DIM))
    return gs.reshape(N_PAIRS, LANES, RET_V_DIM)


def _decay_tables(chunk):
    lg = _log_gamma()
    i = jnp.arange(chunk, dtype=F32)
    dist = i[:, None] - i[None, :]
    dmask = jnp.where(dist[None] >= 0, jnp.exp(lg[:, None, None] * jnp.maximum(dist, 0.0)[None]), 0.0)
    xi = jnp.exp(lg[:, None] * (i[None, :] + 1.0))
    wk = jnp.exp(lg[:, None] * (chunk - 1.0 - i[None, :]))
    return dmask, _pair_lanes(xi), _pair_lanes(wk), _state_decay(chunk)


def _mixer_sample_body(*refs, bb, ts, past, n_alias):
    (x_ref, g_ref, win_ref, wpool_ref, pscale_ref, wout_ref, cos_ref, sin_ref,
     xi_ref, wk_ref, lag_ref, gs_ref, ind_ref, s0_ref, pb_ref) = refs[:15]
    (o_ref, snew_ref, pnew_ref, z_ref, inter_ref, ublk_ref, pblk_ref) = refs[15 + n_alias:]
    rows = bb * ts
    x = x_ref[...]
    _project_in(x, g_ref, win_ref, z_ref)
    lane = lax.broadcasted_iota(jnp.int32, (1, LANES), 1)
    first_half = (lane % RET_QK_DIM) < (RET_QK_DIM // 2)
    _rotate_inplace(z_ref, cos_ref[...], sin_ref[...], first_half)

    q = z_ref[:, Q_OFF:Q_OFF + QK_W]
    k = z_ref[:, K_OFF:K_OFF + QK_W]
    v = z_ref[:, V_OFF:V_OFF + V_W]
    step = lax.broadcasted_iota(jnp.int32, (rows, 1), 0) % ts
    intra = None
    for d in range(ts):
        kd = k if d == 0 else pltpu.roll(k, d, 0)
        vd = v if d == 0 else pltpu.roll(v, d, 0)
        prod = jnp.where(step >= d, q * kd, 0.0).astype(BF16)
        score = _dot(prod, ind_ref[...])
        term = score * lag_ref[d:d + 1, :] * jnp.where(step >= d, vd, 0.0)
        intra = term if intra is None else intra + term

    row8 = lax.broadcasted_iota(jnp.int32, (2 * ts, 1), 0)
    top = (lax.broadcasted_iota(jnp.int32, (LANES, 1), 0) < RET_QK_DIM).astype(F32)
    bot = 1.0 - top

    def seq_pair(p, carry):
        r0 = pl.multiple_of(p * 2 * ts, 2 * ts)
        rsl = pl.ds(r0, 2 * ts)
        for pair in range(N_PAIRS):
            lanes = slice(pair * LANES, (pair + 1) * LANES)
            q8 = (z_ref[rsl, Q_OFF + pair * LANES:Q_OFF + (pair + 1) * LANES] * xi_ref[:, lanes]).astype(BF16)
            kw8 = z_ref[rsl, K_OFF + pair * LANES:K_OFF + (pair + 1) * LANES] * wk_ref[:, lanes]
            v8 = z_ref[rsl, V_OFF + pair * 2 * RET_V_DIM:V_OFF + (pair + 1) * 2 * RET_V_DIM].astype(BF16)
            outs = []
            for j in range(2):
                b = 2 * p + j
                s_pair = s0_ref[b, lanes, :]
                bd = jnp.concatenate([s_pair * top, s_pair * bot], axis=1).astype(BF16)
                outs.append(_dot(q8, bd))
                mine = (row8 >= j * ts) & (row8 < (j + 1) * ts)
                kwj = jnp.where(mine, kw8, 0.0).astype(BF16)
                upd = lax.dot_general(kwj, v8, (((0,), (0,)), ((), ())), preferred_element_type=F32)
                new_s = jnp.concatenate([upd[0:RET_QK_DIM, 0:RET_V_DIM],
                                         upd[RET_QK_DIM:, RET_V_DIM:]], axis=0)
                snew_ref[b, lanes, :] = s_pair * gs_ref[pair] + new_s
            inter_ref[rsl, pair * 2 * RET_V_DIM:(pair + 1) * 2 * RET_V_DIM] = jnp.where(
                row8 < ts, outs[0], outs[1])
        return carry

    lax.fori_loop(0, bb // 2, seq_pair, 0, unroll=8)

    keep = POOL_BUF - ts
    pnew_ref[0:keep] = pb_ref[ts:]
    for g, w in enumerate(POOL_WINDOWS):
        ublk_ref[g] = z_ref[:, U_OFF + g * POOL_GROUP_IN:U_OFF + (g + 1) * POOL_GROUP_IN]
        u_steps = [ublk_ref[g, pl.ds(t, bb, stride=ts), :] for t in range(ts)]
        for t in range(ts):
            pnew_ref[keep + t, :, g * POOL_GROUP_IN:(g + 1) * POOL_GROUP_IN] = u_steps[t]
        ds_ = []
        for t in range(ts):
            tot = None
            for j in range(w):
                idx = POOL_BUF + t - j
                if idx >= POOL_BUF:
                    term = u_steps[idx - POOL_BUF]
                else:
                    term = pb_ref[idx, :, g * POOL_GROUP_IN:(g + 1) * POOL_GROUP_IN]
                tot = term if tot is None else tot + term
            cnt = float(min(past + t + 1, w))
            ds_.append(tot / cnt - u_steps[t])
        y = _dot(jnp.concatenate(ds_, axis=0).astype(BF16),
                 wpool_ref[g * POOL_GROUP_IN:(g + 1) * POOL_GROUP_IN, :])
        y = y * pscale_ref[:, g * POOL_GROUP_OUT:(g + 1) * POOL_GROUP_OUT]
        for half in range(POOL_GROUP_OUT // LANES):
            blk = g * (POOL_GROUP_OUT // LANES) + half
            for t in range(ts):
                pblk_ref[blk, pl.ds(t, bb, stride=ts), :] = y[t * bb:(t + 1) * bb, half * LANES:(half + 1) * LANES]
    pool = jnp.concatenate([pblk_ref[blk] for blk in range(D_MODEL // LANES)], axis=1)

    o_all = intra + inter_ref[...]
    rets = []
    for h in range(N_RET_HEADS):
        hc = slice(h * RET_V_DIM, (h + 1) * RET_V_DIM)
        rets.append(_swish_gate_norm(o_all[:, hc], z_ref[:, G_OFF + h * RET_V_DIM:G_OFF + (h + 1) * RET_V_DIM]))
    ret = jnp.concatenate(rets, axis=1)
    o_ref[...] = _merge_out(x, z_ref, ret, pool, wout_ref)


def _sample_tables(ts, past, rows):
    lg = _log_gamma()
    i = jnp.arange(ts, dtype=F32)
    lag = jnp.exp(lg[:, None] * i[None, :])
    lag = jnp.repeat(lag.T, RET_V_DIM, axis=1)
    xi = jnp.exp(lg[:, None] * (i[None, :] + 1.0))
    wk = jnp.exp(lg[:, None] * (ts - 1.0 - i[None, :]))
    expand = lambda a: jnp.tile(jnp.repeat(a.T, RET_QK_DIM, axis=1), (2, 1))
    pos = (past + jnp.arange(ts, dtype=jnp.int32)).astype(F32)
    cos, sin = _rope_tables(pos)
    reps = rows // ts
    head_of_k = jnp.arange(QK_W) // RET_QK_DIM
    head_of_v = jnp.arange(V_W) // RET_V_DIM
    ind = (head_of_k[:, None] == head_of_v[None, :]).astype(BF16)
    return (jnp.tile(cos, (reps, 1)), jnp.tile(sin, (reps, 1)), expand(xi), expand(wk), lag,
            _state_decay(ts), ind)


def _mixer_sample(x2d, s0, pbuf, g, w_in, w_pool, pool_scale, w_out, tabs, layer, prev_states, *, bb, ts, past):
    n = x2d.shape[0]
    depth = s0.shape[0]
    nb = n // ts
    rows = bb * ts
    assert nb % bb == 0 and bb % 2 == 0 and (2 * ts) % 8 == 0 and ts <= POOL_BUF
    cos, sin, xi, wk, lag, gs, ind = tabs
    row_spec = pl.BlockSpec((rows, D_MODEL), lambda i: (i, 0))
    state_spec = pl.BlockSpec((None, bb, N_PAIRS * LANES, RET_V_DIM), lambda i: (layer, i, 0, 0))
    pool_spec = pl.BlockSpec((None, POOL_BUF, bb, POOL_IN), lambda i: (layer, 0, i, 0))
    in_specs = [
        row_spec, _layer_spec(g.shape, layer), _resident_spec(w_in.shape),
        _resident_spec(w_pool.shape), _layer_spec(pool_scale.shape, layer),
        _resident_spec(w_out.shape), _const_spec(cos.shape), _const_spec(sin.shape),
        _const_spec(xi.shape), _const_spec(wk.shape), _const_spec(lag.shape), _const_spec(gs.shape),
        _const_spec(ind.shape), state_spec, pool_spec,
    ]
    args = [x2d, g, w_in, w_pool, pool_scale, w_out, cos, sin, xi, wk, lag, gs, ind, s0, pbuf]
    extra, extra_specs, aliases = _state_alias(prev_states, len(args), 1)
    out_specs = [row_spec, state_spec, pool_spec]
    out_shape = [
        jax.ShapeDtypeStruct((n, D_MODEL), F32),
        jax.ShapeDtypeStruct((depth, nb, N_PAIRS * LANES, RET_V_DIM), F32),
        jax.ShapeDtypeStruct((depth, POOL_BUF, nb, POOL_IN), F32),
    ]
    scratch = [
        pltpu.VMEM((rows, D_IN), F32),
        pltpu.VMEM((rows, V_W), F32),
        pltpu.VMEM((N_POOL_GROUPS, rows, LANES), F32),
        pltpu.VMEM((D_MODEL // LANES, rows, LANES), F32),
    ]
    y, snew, pnew = pl.pallas_call(
        functools.partial(_mixer_sample_body, bb=bb, ts=ts, past=past, n_alias=len(extra)),
        grid=(nb // bb,),
        in_specs=in_specs + extra_specs,
        out_specs=out_specs,
        out_shape=out_shape,
        scratch_shapes=scratch,
        input_output_aliases=aliases,
        compiler_params=pltpu.CompilerParams(
            dimension_semantics=("arbitrary",), vmem_limit_bytes=VMEM_LIMIT),
        name="mixer_sample",
    )(*args, *extra)
    return y, (snew, pnew)


def kernel(x_prompt, x_sample, state_ret, state_pool, norm_ffn1, w_ffn1_in, w_ffn1_out, norm_mix,
           w_in, w_pool, pool_scale, w_out, norm_ffn2, w_ffn2_in, w_ffn2_out, norm_final):
    depth = w_in.shape[0]
    bp, tp, _ = x_prompt.shape
    bs, ts, _ = x_sample.shape
    wp = w_pool.reshape(depth, N_POOL_GROUPS * POOL_GROUP_IN, POOL_GROUP_OUT)

    row = lambda a: a.reshape(depth, 1, D_MODEL)
    n1, nm, n2, ps = map(row, (norm_ffn1, norm_mix, norm_ffn2, pool_scale))
    gf = norm_final.reshape(1, D_MODEL)
    last = depth - 1

    tm = 512
    tm_ffn = 1024
    pos_p = jnp.arange(tp, dtype=jnp.int32).astype(F32)
    tabs_p = _rope_tables(pos_p) + _decay_tables(RET_CHUNK)

    bb = 32
    tabs_s = _sample_tables(ts, PAST_LEN, bb * ts)
    s0_all = state_ret.reshape(depth, bs, N_PAIRS * LANES, RET_V_DIM)
    pb_all = jnp.swapaxes(state_pool, 1, 2)

    x = x_prompt.reshape(bp * tp, D_MODEL)
    xs = x_sample.reshape(bs * ts, D_MODEL)
    states_p = states_s = None
    for l in range(depth):
        x, xs = _ffn(x, xs, n1, w_ffn1_in, w_ffn1_out, l, tm=tm_ffn)
        x, states_p, (wi16, wp16, wo16) = _mixer_prompt(x.reshape(bp, tp, D_MODEL), nm, w_in, wp, ps, w_out,
                                                        tabs_p, l, states_p, tm=tm, n_sub=2)
        xs, states_s = _mixer_sample(xs, s0_all, pb_all, nm, wi16, wp16, ps, wo16, tabs_s, l, states_s,
                                     bb=bb, ts=ts, past=PAST_LEN)
        x, xs = _ffn(x.reshape(bp * tp, D_MODEL), xs, n2, w_ffn2_in, w_ffn2_out, l,
                     gf if l == last else None, tm=tm_ffn)
    y_prompt = x.reshape(bp, tp, D_MODEL)
    y_sample = xs.reshape(bs, ts, D_MODEL)
    state_ret_prompt = states_p[0].reshape(depth, bp, N_RET_HEADS, RET_QK_DIM, RET_V_DIM)
    state_pool_prompt = states_p[1]
    state_ret_sample = states_s[0].reshape(depth, bs, N_RET_HEADS, RET_QK_DIM, RET_V_DIM)
    state_pool_sample = jnp.swapaxes(states_s[1], 1, 2)
    return (y_prompt, y_sample, state_ret_prompt, state_ret_sample, state_pool_prompt, state_pool_sample)
```

```python
import functools

import jax
import jax.numpy as jnp
from jax import lax
from jax.experimental import pallas as pl
from jax.experimental.pallas import tpu as pltpu

D_MODEL = 1024
N_RET_HEADS = 8
RET_QK_DIM = 64
RET_V_DIM = 128
RET_CHUNK = 128
ROPE_BASE = 10000.0
QK_W = N_RET_HEADS * RET_QK_DIM
V_W = N_RET_HEADS * RET_V_DIM
POOL_WINDOWS = (2, 4, 8, 16)
N_POOL_GROUPS = 4
POOL_IN = 512
POOL_GROUP_IN = 128
POOL_GROUP_OUT = 256
POOL_BUF = 15
D_IN = 2 * QK_W + 2 * V_W + POOL_IN + 2 * D_MODEL
D_FF = 2816
EPS = 1e-6
PAST_LEN = 16384

Q_OFF = 0
K_OFF = QK_W
V_OFF = 2 * QK_W
G_OFF = 2 * QK_W + V_W
U_OFF = 2 * QK_W + 2 * V_W
GA_OFF = U_OFF + POOL_IN
GB_OFF = GA_OFF + D_MODEL

LANES = 128
N_PAIRS = N_RET_HEADS // 2
POOL_PAD = 16
VMEM_LIMIT = 60 * 1024 * 1024

F32 = jnp.float32
BF16 = jnp.bfloat16


def _dot(a, b):
    return jnp.dot(a, b, preferred_element_type=F32)


def _rms(x, g):
    return x * lax.rsqrt(jnp.mean(x * x, axis=-1, keepdims=True) + EPS) * g


def _const_spec(shape):
    nd = len(shape)
    return pl.BlockSpec(shape, lambda *_: (0,) * nd)


FF_CHUNKS = ((0, 1024), (1024, 1024), (2048, 768))


STAGE_BUFS = 6
STAGE_BYTES = 3 * 256 * 1024


def _stage_rows(shape):
    r, c = shape
    fits = [n for n in range(16, r + 1, 16) if r % n == 0 and n * c * 4 <= STAGE_BYTES]
    return max(fits)


def _stage_scratch(shape):
    return [pltpu.VMEM(shape, BF16), pltpu.VMEM((STAGE_BUFS, _stage_rows(shape), shape[1]), F32),
            pltpu.SemaphoreType.DMA((STAGE_BUFS,))]


def _fetch_cast(src, dst, stage, sems):
    nbuf, rows, _ = stage.shape
    n = src.shape[0] // rows

    def copy(k):
        return pltpu.make_async_copy(src.at[pl.ds(k * rows, rows), :], stage.at[k % nbuf], sems.at[k % nbuf])

    for k in range(min(nbuf, n)):
        copy(k).start()
    for k in range(n):
        copy(k).wait()
        dst[k * rows:(k + 1) * rows, :] = stage[k % nbuf].astype(BF16)
        if k + nbuf < n:
            copy(k + nbuf).start()


def _convert_weights(first, layer, srcs, scratch):
    @pl.when(first)
    def _():
        for k, src in enumerate(srcs):
            _fetch_cast(src.at[layer], *scratch[3 * k:3 * k + 3])

    return scratch[0::3]


def _emit_weights(first, last, srcs, outs, sems):
    copies = [pltpu.make_async_copy(s, o, sems.at[k]) for k, (s, o) in enumerate(zip(srcs, outs))]

    @pl.when(first)
    def _():
        for c in copies:
            c.start()

    @pl.when(last)
    def _():
        for c in copies:
            c.wait()


def _ffn_body(*refs, final, layer):
    n_in = 6 if final else 5
    x_ref, xs_ref, g_ref, win_hbm, wout_hbm = refs[:5]
    gf_ref = refs[5] if final else None
    o_ref, os_ref = refs[n_in:n_in + 2]
    i = pl.program_id(0)
    win_ref, wout_ref = _convert_weights(i == 0, layer, (win_hbm, wout_hbm), refs[n_in + 2:])

    def ffn_rows(x):
        h = _rms(x, g_ref[...]).astype(BF16)
        acc = None
        for off, width in FF_CHUNKS:
            gate = _dot(h, win_ref[:, off:off + width])
            up = _dot(h, win_ref[:, D_FF + off:D_FF + off + width])
            a = (gate * jax.nn.sigmoid(gate) * up).astype(BF16)
            part = _dot(a, wout_ref[off:off + width, :])
            acc = part if acc is None else acc + part
        y = x + 0.5 * acc
        return _rms(y, gf_ref[...]) if final else y

    o_ref[...] = ffn_rows(x_ref[...])

    @pl.when(i == pl.num_programs(0) - 1)
    def _():
        os_ref[...] = ffn_rows(xs_ref[...])


def _layer_spec(shape, layer):
    nd = len(shape) - 1
    return pl.BlockSpec((None,) + tuple(shape[1:]), lambda *_: (layer,) + (0,) * nd,
                        pipeline_mode=pl.Buffered(1))


def _resident_spec(shape):
    nd = len(shape)
    return pl.BlockSpec(tuple(shape), lambda *_: (0,) * nd, pipeline_mode=pl.Buffered(1))


ANY_SPEC = pl.BlockSpec(memory_space=pl.ANY)


def _ffn(x2d, xs2d, g, w_in, w_out, layer, g_final=None, *, tm):
    n = x2d.shape[0]
    assert n % tm == 0
    final = g_final is not None
    row_spec = pl.BlockSpec((tm, D_MODEL), lambda i: (i, 0))
    in_specs = [row_spec, _resident_spec(xs2d.shape), _layer_spec(g.shape, layer), ANY_SPEC, ANY_SPEC]
    args = [x2d, xs2d, g, w_in, w_out]
    if final:
        in_specs.append(_const_spec((1, D_MODEL)))
        args.append(g_final.reshape(1, D_MODEL))
    scratch = _stage_scratch(w_in.shape[1:]) + _stage_scratch(w_out.shape[1:])
    return pl.pallas_call(
        functools.partial(_ffn_body, final=final, layer=layer),
        grid=(n // tm,),
        in_specs=in_specs,
        out_specs=[row_spec, _resident_spec(xs2d.shape)],
        out_shape=[jax.ShapeDtypeStruct(x2d.shape, F32), jax.ShapeDtypeStruct(xs2d.shape, F32)],
        scratch_shapes=scratch,
        compiler_params=pltpu.CompilerParams(
            dimension_semantics=("arbitrary",), vmem_limit_bytes=VMEM_LIMIT),
        name="ffn_final" if final else "ffn",
    )(*args)


IN_CHUNKS = tuple((o, 1024) for o in range(0, 5120, 1024)) + ((5120, 512),)


def _project_in(x, g_ref, win_ref, z_ref):
    h = _rms(x, g_ref[...]).astype(BF16)
    for off, width in IN_CHUNKS:
        z_ref[:, off:off + width] = _dot(h, win_ref[:, off:off + width])


def _rotate_inplace(z_ref, cos, sin_signed, first_half):
    for blk in range(2 * QK_W // LANES):
        cols = slice(blk * LANES, (blk + 1) * LANES)
        xb = z_ref[:, cols]
        partner = jnp.where(first_half, pltpu.roll(xb, LANES - 32, 1), pltpu.roll(xb, 32, 1))
        r = xb * cos + partner * sin_signed
        if blk >= QK_W // LANES:
            r = r * (RET_QK_DIM ** -0.5)
        z_ref[:, cols] = r


def _merge_out(x, z_ref, ret, pool, wout_ref):
    ga = z_ref[:, GA_OFF:GA_OFF + D_MODEL]
    gb = z_ref[:, GB_OFF:GB_OFF + D_MODEL]
    merged = jax.nn.sigmoid(ga) * ret + jax.nn.sigmoid(gb) * pool
    return x + _dot(merged.astype(BF16), wout_ref[...])


def _swish_gate_norm(o, gate):
    o = o * lax.rsqrt(jnp.mean(o * o, axis=-1, keepdims=True) + EPS)
    return o * (gate * jax.nn.sigmoid(gate))


def _mixer_prompt_body(*refs, tm, n_sub, chunk, n_alias, layer):
    (x_ref, g_ref, win_ref, wpool_ref, pscale_ref, wout_ref, cos_ref, sin_ref,
     dmask_ref, xi_ref, wk_ref, gs_ref) = refs[:12]
    outs = refs[12 + n_alias:]
    o_ref, sret_ref, spool_ref = outs[:3]
    z_ref, s_ref, ext_ref, ret_ref, pool_ref, h_ref, d_ref = outs[6:13]
    t = pl.program_id(1)
    nt = pl.num_programs(1)
    bi = pl.program_id(0)
    first = (bi == 0) & (t == 0)
    win_ref, wpool_ref, wout_ref = _convert_weights(first, layer, (win_ref, wpool_ref, wout_ref), outs[13:22])
    _emit_weights(first, (bi == pl.num_programs(0) - 1) & (t == nt - 1),
                  (win_ref, wpool_ref, wout_ref), outs[3:6], outs[22])

    @pl.when(t == 0)
    def _():
        s_ref[...] = jnp.zeros_like(s_ref)
        ext_ref[0:POOL_PAD, :] = jnp.zeros((POOL_PAD, POOL_IN), F32)

    tile_refs = (x_ref, g_ref, win_ref, wpool_ref, pscale_ref, wout_ref, cos_ref, sin_ref, dmask_ref, xi_ref,
                 wk_ref, gs_ref, o_ref, z_ref, s_ref, ext_ref, ret_ref, pool_ref, h_ref, d_ref)
    for sub in range(n_sub):
        tail = _mixer_prompt_tile(tile_refs, sub * tm, (t * n_sub + sub) * tm, tm, chunk)

    @pl.when(t == nt - 1)
    def _():
        sret_ref[...] = s_ref[...].reshape(N_PAIRS * LANES, RET_V_DIM)
        spool_ref[...] = tail[POOL_PAD - POOL_BUF:, :]


def _mixer_prompt_tile(tile_refs, r0, pos0, tm, chunk):
    (x_ref, g_ref, win_ref, wpool_ref, pscale_ref, wout_ref, cos_ref, sin_ref, dmask_ref, xi_ref,
     wk_ref, gs_ref, o_ref, z_ref, s_ref, ext_ref, ret_ref, pool_ref, h_ref, d_ref) = tile_refs
    h_ref[...] = _rms(x_ref[r0:r0 + tm, :], g_ref[...]).astype(BF16)

    def project(off, width, act=None):
        y = _dot(h_ref[...], win_ref[:, off:off + width])
        z_ref[:, off:off + width] = y if act is None else act(y)

    project(Q_OFF, 2 * QK_W)
    project(V_OFF, V_W)
    project(U_OFF, POOL_IN)

    lane = lax.broadcasted_iota(jnp.int32, (1, LANES), 1)
    first_half = (lane % RET_QK_DIM) < (RET_QK_DIM // 2)
    _rotate_inplace(z_ref, cos_ref[r0:r0 + tm, :], sin_ref[r0:r0 + tm, :], first_half)

    ext_ref[POOL_PAD:POOL_PAD + tm, :] = z_ref[:, U_OFF:U_OFF + POOL_IN]
    pos = pos0 + lax.broadcasted_iota(jnp.int32, (tm, 1), 0)
    for g, w in enumerate(POOL_WINDOWS):
        cols = slice(g * POOL_GROUP_IN, (g + 1) * POOL_GROUP_IN)
        u = ext_ref[POOL_PAD:POOL_PAD + tm, cols]
        tot = u
        for j in range(1, w):
            tot = tot + ext_ref[POOL_PAD - j:POOL_PAD - j + tm, cols]
        cnt = jnp.minimum(pos + 1, w).astype(F32)
        d_ref[:, cols] = (tot / cnt - u).astype(BF16)
    tail = ext_ref[tm:tm + POOL_PAD, :]
    ext_ref[0:POOL_PAD, :] = tail

    def pool_matmul(g):
        ocols = slice(g * POOL_GROUP_OUT, (g + 1) * POOL_GROUP_OUT)
        y = _dot(d_ref[:, g * POOL_GROUP_IN:(g + 1) * POOL_GROUP_IN],
                 wpool_ref[g * POOL_GROUP_IN:(g + 1) * POOL_GROUP_IN, :])
        pool_ref[:, ocols] = y * pscale_ref[:, ocols]

    fill_width = 2 * LANES
    fillers = [functools.partial(project, off, fill_width, jax.nn.silu)
               for off in range(G_OFF, G_OFF + V_W, fill_width)]
    fillers += [functools.partial(project, off, fill_width, jax.nn.sigmoid)
                for off in range(GA_OFF, D_IN, fill_width)]
    fillers += [functools.partial(pool_matmul, g) for g in range(N_POOL_GROUPS)]

    head0 = (lane < RET_QK_DIM).astype(F32)
    head1 = 1.0 - head0

    n_chunks = tm // chunk
    stages = 3 * n_chunks

    def run_fillers(stage):
        for f in fillers[stage * len(fillers) // stages:(stage + 1) * len(fillers) // stages]:
            f()

    def pair_cols(off, pair, width):
        return slice(off + pair * width, off + (pair + 1) * width)

    for c in range(n_chunks):
        rows = slice(c * chunk, (c + 1) * chunk)
        scores = []
        for pair in range(N_PAIRS):
            qb = z_ref[rows, pair_cols(Q_OFF, pair, LANES)]
            kb = z_ref[rows, pair_cols(K_OFF, pair, LANES)]
            k_stack = jnp.concatenate([kb * head0, kb * head1], axis=0).astype(BF16)
            scores.append(lax.dot_general(qb.astype(BF16), k_stack, (((1,), (1,)), ((), ())),
                                          preferred_element_type=F32))
        run_fillers(3 * c)
        for pair in range(N_PAIRS):
            qx = z_ref[rows, pair_cols(Q_OFF, pair, LANES)] * xi_ref[pair]
            vb16 = z_ref[rows, pair_cols(V_OFF, pair, 2 * RET_V_DIM)].astype(BF16)
            s16 = s_ref[pair].astype(BF16)
            for j, hmask in enumerate((head0, head1)):
                h = 2 * pair + j
                scj = scores[pair][:, j * chunk:(j + 1) * chunk] * dmask_ref[h]
                lhs = jnp.concatenate([scj, qx * hmask], axis=1).astype(BF16)
                rhs = jnp.concatenate([vb16[:, j * RET_V_DIM:(j + 1) * RET_V_DIM], s16], axis=0)
                o = _dot(lhs, rhs)
                o = o * lax.rsqrt(jnp.mean(o * o, axis=-1, keepdims=True) + EPS)
                ret_ref[rows, h * RET_V_DIM:(h + 1) * RET_V_DIM] = o
        run_fillers(3 * c + 1)
        for pair in range(N_PAIRS):
            kw = (z_ref[rows, pair_cols(K_OFF, pair, LANES)] * wk_ref[pair]).astype(BF16)
            vb16 = z_ref[rows, pair_cols(V_OFF, pair, 2 * RET_V_DIM)].astype(BF16)
            upd = lax.dot_general(kw, vb16, (((0,), (0,)), ((), ())),
                                  preferred_element_type=F32)
            new_s = jnp.concatenate([upd[0:RET_QK_DIM, 0:RET_V_DIM],
                                     upd[RET_QK_DIM:, RET_V_DIM:]], axis=0)
            s_ref[pair] = s_ref[pair] * gs_ref[pair] + new_s
        run_fillers(3 * c + 2)

    n_halves = 2
    for r in range(n_halves):
        rows = slice(r * tm // n_halves, (r + 1) * tm // n_halves)
        ret = ret_ref[rows, :] * z_ref[rows, G_OFF:G_OFF + V_W]
        merged = (z_ref[rows, GA_OFF:GA_OFF + D_MODEL] * ret
                  + z_ref[rows, GB_OFF:GB_OFF + D_MODEL] * pool_ref[rows, :])
        orows = slice(r0 + r * tm // n_halves, r0 + (r + 1) * tm // n_halves)
        o_ref[orows, :] = x_ref[orows, :] + _dot(merged.astype(BF16), wout_ref[...])
    return tail


def _state_alias(prev, n_in, first_out):
    if prev is None:
        return [], [], {}
    specs = [pl.BlockSpec(memory_space=pl.ANY) for _ in prev]
    aliases = {n_in + k: first_out + k for k in range(len(prev))}
    return list(prev), specs, aliases


def _mixer_prompt(x, g, w_in, w_pool, pool_scale, w_out, tabs, layer, prev_states, *, tm, n_sub):
    b, t_len, _ = x.shape
    depth = w_in.shape[0]
    chunk = RET_CHUNK
    blk = n_sub * tm
    assert t_len % blk == 0 and tm % chunk == 0
    cos, sin, dmask, xi, wk, gs = tabs
    row_spec = pl.BlockSpec((None, blk, D_MODEL), lambda i, j: (i, j, 0))
    tab_spec = pl.BlockSpec((blk, LANES), lambda i, j: (j, 0))
    in_specs = [
        row_spec, _layer_spec(g.shape, layer), ANY_SPEC, ANY_SPEC, _layer_spec(pool_scale.shape, layer),
        ANY_SPEC, tab_spec, tab_spec,
        _const_spec(dmask.shape), _const_spec(xi.shape), _const_spec(wk.shape), _const_spec(gs.shape),
    ]
    args = [x, g, w_in, w_pool, pool_scale, w_out, cos, sin, dmask, xi, wk, gs]
    extra, extra_specs, aliases = _state_alias(prev_states, len(args), 1)
    w_shapes = [w_in.shape[1:], w_pool.shape[1:], w_out.shape[1:]]
    out_specs = [
        row_spec,
        pl.BlockSpec((None, None, N_PAIRS * LANES, RET_V_DIM), lambda i, j: (layer, i, 0, 0)),
        pl.BlockSpec((None, None, POOL_BUF, POOL_IN), lambda i, j: (layer, i, 0, 0)),
    ] + [ANY_SPEC] * len(w_shapes)
    out_shape = [
        jax.ShapeDtypeStruct((b, t_len, D_MODEL), F32),
        jax.ShapeDtypeStruct((depth, b, N_PAIRS * LANES, RET_V_DIM), F32),
        jax.ShapeDtypeStruct((depth, b, POOL_BUF, POOL_IN), F32),
    ] + [jax.ShapeDtypeStruct(s, BF16) for s in w_shapes]
    scratch = [
        pltpu.VMEM((tm, D_IN), F32),
        pltpu.VMEM((N_PAIRS, LANES, RET_V_DIM), F32),
        pltpu.VMEM((POOL_PAD + tm, POOL_IN), F32),
        pltpu.VMEM((tm, V_W), F32),
        pltpu.VMEM((tm, D_MODEL), F32),
        pltpu.VMEM((tm, D_MODEL), BF16),
        pltpu.VMEM((tm, POOL_IN), BF16),
    ]
    for s in w_shapes:
        scratch += _stage_scratch(s)
    scratch.append(pltpu.SemaphoreType.DMA((len(w_shapes),)))
    y, sret, spool, *w16 = pl.pallas_call(
        functools.partial(_mixer_prompt_body, tm=tm, n_sub=n_sub, chunk=chunk, n_alias=len(extra), layer=layer),
        grid=(b, t_len // blk),
        in_specs=in_specs + extra_specs,
        out_specs=out_specs,
        out_shape=out_shape,
        scratch_shapes=scratch,
        input_output_aliases=aliases,
        compiler_params=pltpu.CompilerParams(
            dimension_semantics=("arbitrary", "arbitrary"), vmem_limit_bytes=VMEM_LIMIT),
        name="mixer_prompt",
    )(*args, *extra)
    return y, (sret, spool), tuple(w16)


def _log_gamma():
    return jnp.log1p(-jnp.exp2(-5.0 - jnp.arange(N_RET_HEADS, dtype=F32)))


def _rope_tables(pos):
    half = RET_QK_DIM // 2
    inv = jnp.power(jnp.float32(ROPE_BASE), -jnp.linspace(0.0, 1.0, half, dtype=F32))
    ang = pos[:, None] * inv[None, :]
    cos, sin = jnp.cos(ang), jnp.sin(ang)
    cos128 = jnp.tile(cos, (1, LANES // half))
    sin128 = jnp.tile(jnp.concatenate([-sin, sin], axis=1), (1, LANES // RET_QK_DIM))
    return cos128, sin128


def _pair_lanes(per_head):
    h, c = per_head.shape
    x = jnp.broadcast_to(per_head[:, :, None], (h, c, RET_QK_DIM))
    return x.reshape(N_PAIRS, 2, c, RET_QK_DIM).transpose(0, 2, 1, 3).reshape(N_PAIRS, c, LANES)


def _state_decay(chunk):
    g_c = jnp.exp(_log_gamma() * chunk)
    gs = jnp.broadcast_to(g_c[:, None, None], (N_RET_HEADS, RET_QK_DIM, RET_V_DIM))
    return gs.reshape(N_PAIRS, LANES, RET_V_DIM)


def _decay_tables(chunk):
    lg = _log_gamma()
    i = jnp.arange(chunk, dtype=F32)
    dist = i[:, None] - i[None, :]
    dmask = jnp.where(dist[None] >= 0, jnp.exp(lg[:, None, None] * jnp.maximum(dist, 0.0)[None]), 0.0)
    xi = jnp.exp(lg[:, None] * (i[None, :] + 1.0))
    wk = jnp.exp(lg[:, None] * (chunk - 1.0 - i[None, :]))
    return dmask, _pair_lanes(xi), _pair_lanes(wk), _state_decay(chunk)


def _mixer_sample_body(*refs, bb, ts, past, n_alias):
    (x_ref, g_ref, win_ref, wpool_ref, pscale_ref, wout_ref, cos_ref, sin_ref,
     xi_ref, wk_ref, lag_ref, gs_ref, ind_ref, s0_ref, pb_ref) = refs[:15]
    (o_ref, snew_ref, pnew_ref, z_ref, inter_ref, ublk_ref, pblk_ref) = refs[15 + n_alias:]
    rows = bb * ts
    x = x_ref[...]
    _project_in(x, g_ref, win_ref, z_ref)
    lane = lax.broadcasted_iota(jnp.int32, (1, LANES), 1)
    first_half = (lane % RET_QK_DIM) < (RET_QK_DIM // 2)
    _rotate_inplace(z_ref, cos_ref[...], sin_ref[...], first_half)

    q = z_ref[:, Q_OFF:Q_OFF + QK_W]
    k = z_ref[:, K_OFF:K_OFF + QK_W]
    v = z_ref[:, V_OFF:V_OFF + V_W]
    step = lax.broadcasted_iota(jnp.int32, (rows, 1), 0) % ts
    intra = None
    for d in range(ts):
        kd = k if d == 0 else pltpu.roll(k, d, 0)
        vd = v if d == 0 else pltpu.roll(v, d, 0)
        prod = jnp.where(step >= d, q * kd, 0.0).astype(BF16)
        score = _dot(prod, ind_ref[...])
        term = score * lag_ref[d:d + 1, :] * jnp.where(step >= d, vd, 0.0)
        intra = term if intra is None else intra + term

    row8 = lax.broadcasted_iota(jnp.int32, (2 * ts, 1), 0)
    top = (lax.broadcasted_iota(jnp.int32, (LANES, 1), 0) < RET_QK_DIM).astype(F32)
    bot = 1.0 - top

    def seq_pair(p, carry):
        r0 = pl.multiple_of(p * 2 * ts, 2 * ts)
        rsl = pl.ds(r0, 2 * ts)
        for pair in range(N_PAIRS):
            lanes = slice(pair * LANES, (pair + 1) * LANES)
            q8 = (z_ref[rsl, Q_OFF + pair * LANES:Q_OFF + (pair + 1) * LANES] * xi_ref[:, lanes]).astype(BF16)
            kw8 = z_ref[rsl, K_OFF + pair * LANES:K_OFF + (pair + 1) * LANES] * wk_ref[:, lanes]
            v8 = z_ref[rsl, V_OFF + pair * 2 * RET_V_DIM:V_OFF + (pair + 1) * 2 * RET_V_DIM].astype(BF16)
            outs = []
            for j in range(2):
                b = 2 * p + j
                s_pair = s0_ref[b, lanes, :]
                bd = jnp.concatenate([s_pair * top, s_pair * bot], axis=1).astype(BF16)
                outs.append(_dot(q8, bd))
                mine = (row8 >= j * ts) & (row8 < (j + 1) * ts)
                kwj = jnp.where(mine, kw8, 0.0).astype(BF16)
                upd = lax.dot_general(kwj, v8, (((0,), (0,)), ((), ())), preferred_element_type=F32)
                new_s = jnp.concatenate([upd[0:RET_QK_DIM, 0:RET_V_DIM],
                                         upd[RET_QK_DIM:, RET_V_DIM:]], axis=0)
                snew_ref[b, lanes, :] = s_pair * gs_ref[pair] + new_s
            inter_ref[rsl, pair * 2 * RET_V_DIM:(pair + 1) * 2 * RET_V_DIM] = jnp.where(
                row8 < ts, outs[0], outs[1])
        return carry

    lax.fori_loop(0, bb // 2, seq_pair, 0, unroll=8)

    keep = POOL_BUF - ts
    pnew_ref[0:keep] = pb_ref[ts:]
    for g, w in enumerate(POOL_WINDOWS):
        ublk_ref[g] = z_ref[:, U_OFF + g * POOL_GROUP_IN:U_OFF + (g + 1) * POOL_GROUP_IN]
        u_steps = [ublk_ref[g, pl.ds(t, bb, stride=ts), :] for t in range(ts)]
        for t in range(ts):
            pnew_ref[keep + t, :, g * POOL_GROUP_IN:(g + 1) * POOL_GROUP_IN] = u_steps[t]
        ds_ = []
        for t in range(ts):
            tot = None
            for j in range(w):
                idx = POOL_BUF + t - j
                if idx >= POOL_BUF:
                    term = u_steps[idx - POOL_BUF]
                else:
                    term = pb_ref[idx, :, g * POOL_GROUP_IN:(g + 1) * POOL_GROUP_IN]
                tot = term if tot is None else tot + term
            cnt = float(min(past + t + 1, w))
            ds_.append(tot / cnt - u_steps[t])
        y = _dot(jnp.concatenate(ds_, axis=0).astype(BF16),
                 wpool_ref[g * POOL_GROUP_IN:(g + 1) * POOL_GROUP_IN, :])
        y = y * pscale_ref[:, g * POOL_GROUP_OUT:(g + 1) * POOL_GROUP_OUT]
        for half in range(POOL_GROUP_OUT // LANES):
            blk = g * (POOL_GROUP_OUT // LANES) + half
            for t in range(ts):
                pblk_ref[blk, pl.ds(t, bb, stride=ts), :] = y[t * bb:(t + 1) * bb, half * LANES:(half + 1) * LANES]
    pool = jnp.concatenate([pblk_ref[blk] for blk in range(D_MODEL // LANES)], axis=1)

    o_all = intra + inter_ref[...]
    rets = []
    for h in range(N_RET_HEADS):
        hc = slice(h * RET_V_DIM, (h + 1) * RET_V_DIM)
        rets.append(_swish_gate_norm(o_all[:, hc], z_ref[:, G_OFF + h * RET_V_DIM:G_OFF + (h + 1) * RET_V_DIM]))
    ret = jnp.concatenate(rets, axis=1)
    o_ref[...] = _merge_out(x, z_ref, ret, pool, wout_ref)


def _sample_tables(ts, past, rows):
    lg = _log_gamma()
    i = jnp.arange(ts, dtype=F32)
    lag = jnp.exp(lg[:, None] * i[None, :])
    lag = jnp.repeat(lag.T, RET_V_DIM, axis=1)
    xi = jnp.exp(lg[:, None] * (i[None, :] + 1.0))
    wk = jnp.exp(lg[:, None] * (ts - 1.0 - i[None, :]))
    expand = lambda a: jnp.tile(jnp.repeat(a.T, RET_QK_DIM, axis=1), (2, 1))
    pos = (past + jnp.arange(ts, dtype=jnp.int32)).astype(F32)
    cos, sin = _rope_tables(pos)
    reps = rows // ts
    head_of_k = jnp.arange(QK_W) // RET_QK_DIM
    head_of_v = jnp.arange(V_W) // RET_V_DIM
    ind = (head_of_k[:, None] == head_of_v[None, :]).astype(BF16)
    return (jnp.tile(cos, (reps, 1)), jnp.tile(sin, (reps, 1)), expand(xi), expand(wk), lag,
            _state_decay(ts), ind)


def _mixer_sample(x2d, s0, pbuf, g, w_in, w_pool, pool_scale, w_out, tabs, layer, prev_states, *, bb, ts, past):
    n = x2d.shape[0]
    depth = s0.shape[0]
    nb = n // ts
    rows = bb * ts
    assert nb % bb == 0 and bb % 2 == 0 and (2 * ts) % 8 == 0 and ts <= POOL_BUF
    cos, sin, xi, wk, lag, gs, ind = tabs
    row_spec = pl.BlockSpec((rows, D_MODEL), lambda i: (i, 0))
    state_spec = pl.BlockSpec((None, bb, N_PAIRS * LANES, RET_V_DIM), lambda i: (layer, i, 0, 0))
    pool_spec = pl.BlockSpec((None, POOL_BUF, bb, POOL_IN), lambda i: (layer, 0, i, 0))
    in_specs = [
        row_spec, _layer_spec(g.shape, layer), _resident_spec(w_in.shape),
        _resident_spec(w_pool.shape), _layer_spec(pool_scale.shape, layer),
        _resident_spec(w_out.shape), _const_spec(cos.shape), _const_spec(sin.shape),
        _const_spec(xi.shape), _const_spec(wk.shape), _const_spec(lag.shape), _const_spec(gs.shape),
        _const_spec(ind.shape), state_spec, pool_spec,
    ]
    args = [x2d, g, w_in, w_pool, pool_scale, w_out, cos, sin, xi, wk, lag, gs, ind, s0, pbuf]
    extra, extra_specs, aliases = _state_alias(prev_states, len(args), 1)
    out_specs = [row_spec, state_spec, pool_spec]
    out_shape = [
        jax.ShapeDtypeStruct((n, D_MODEL), F32),
        jax.ShapeDtypeStruct((depth, nb, N_PAIRS * LANES, RET_V_DIM), F32),
        jax.ShapeDtypeStruct((depth, POOL_BUF, nb, POOL_IN), F32),
    ]
    scratch = [
        pltpu.VMEM((rows, D_IN), F32),
        pltpu.VMEM((rows, V_W), F32),
        pltpu.VMEM((N_POOL_GROUPS, rows, LANES), F32),
        pltpu.VMEM((D_MODEL // LANES, rows, LANES), F32),
    ]
    y, snew, pnew = pl.pallas_call(
        functools.partial(_mixer_sample_body, bb=bb, ts=ts, past=past, n_alias=len(extra)),
        grid=(nb // bb,),
        in_specs=in_specs + extra_specs,
        out_specs=out_specs,
        out_shape=out_shape,
        scratch_shapes=scratch,
        input_output_aliases=aliases,
        compiler_params=pltpu.CompilerParams(
            dimension_semantics=("arbitrary",), vmem_limit_bytes=VMEM_LIMIT),
        name="mixer_sample",
    )(*args, *extra)
    return y, (snew, pnew)


def kernel(x_prompt, x_sample, state_ret, state_pool, norm_ffn1, w_ffn1_in, w_ffn1_out, norm_mix,
           w_in, w_pool, pool_scale, w_out, norm_ffn2, w_ffn2_in, w_ffn2_out, norm_final):
    depth = w_in.shape[0]
    bp, tp, _ = x_prompt.shape
    bs, ts, _ = x_sample.shape
    wp = w_pool.reshape(depth, N_POOL_GROUPS * POOL_GROUP_IN, POOL_GROUP_OUT)

    row = lambda a: a.reshape(depth, 1, D_MODEL)
    n1, nm, n2, ps = map(row, (norm_ffn1, norm_mix, norm_ffn2, pool_scale))
    gf = norm_final.reshape(1, D_MODEL)
    last = depth - 1

    tm = 512
    tm_ffn = 1024
    pos_p = jnp.arange(tp, dtype=jnp.int32).astype(F32)
    tabs_p = _rope_tables(pos_p) + _decay_tables(RET_CHUNK)

    bb = 32
    tabs_s = _sample_tables(ts, PAST_LEN, bb * ts)
    s0_all = state_ret.reshape(depth, bs, N_PAIRS * LANES, RET_V_DIM)
    pb_all = jnp.swapaxes(state_pool, 1, 2)

    x = x_prompt.reshape(bp * tp, D_MODEL)
    xs = x_sample.reshape(bs * ts, D_MODEL)
    states_p = states_s = None
    for l in range(depth):
        x, xs = _ffn(x, xs, n1, w_ffn1_in, w_ffn1_out, l, tm=tm_ffn)
        x, states_p, (wi16, wp16, wo16) = _mixer_prompt(x.reshape(bp, tp, D_MODEL), nm, w_in, wp, ps, w_out,
                                                        tabs_p, l, states_p, tm=tm, n_sub=2)
        xs, states_s = _mixer_sample(xs, s0_all, pb_all, nm, wi16, wp16, ps, wo16, tabs_s, l, states_s,
                                     bb=bb, ts=ts, past=PAST_LEN)
        x, xs = _ffn(x.reshape(bp * tp, D_MODEL), xs, n2, w_ffn2_in, w_ffn2_out, l,
                     gf if l == last else None, tm=tm_ffn)
    y_prompt = x.reshape(bp, tp, D_MODEL)
    y_sample = xs.reshape(bs, ts, D_MODEL)
    state_ret_prompt = states_p[0].reshape(depth, bp, N_RET_HEADS, RET_QK_DIM, RET_V_DIM)
    state_pool_prompt = states_p[1]
    state_ret_sample = states_s[0].reshape(depth, bs, N_RET_HEADS, RET_QK_DIM, RET_V_DIM)
    state_pool_sample = jnp.swapaxes(states_s[1], 1, 2)
    return (y_prompt, y_sample, state_ret_prompt, state_ret_sample, state_pool_prompt, state_pool_sample)
```

```python
import functools

import jax
import jax.numpy as jnp
from jax import lax
from jax.experimental import pallas as pl
from jax.experimental.pallas import tpu as pltpu

D_MODEL = 1024
N_RET_HEADS = 8
RET_QK_DIM = 64
RET_V_DIM = 128
RET_CHUNK = 128
ROPE_BASE = 10000.0
QK_W = N_RET_HEADS * RET_QK_DIM
V_W = N_RET_HEADS * RET_V_DIM
POOL_WINDOWS = (2, 4, 8, 16)
N_POOL_GROUPS = 4
POOL_IN = 512
POOL_GROUP_IN = 128
POOL_GROUP_OUT = 256
POOL_BUF = 15
D_IN = 2 * QK_W + 2 * V_W + POOL_IN + 2 * D_MODEL
D_FF = 2816
EPS = 1e-6
PAST_LEN = 16384

Q_OFF = 0
K_OFF = QK_W
V_OFF = 2 * QK_W
G_OFF = 2 * QK_W + V_W
U_OFF = 2 * QK_W + 2 * V_W
GA_OFF = U_OFF + POOL_IN
GB_OFF = GA_OFF + D_MODEL

LANES = 128
N_PAIRS = N_RET_HEADS // 2
POOL_PAD = 16
VMEM_LIMIT = 60 * 1024 * 1024

F32 = jnp.float32
BF16 = jnp.bfloat16


def _dot(a, b):
    return jnp.dot(a, b, preferred_element_type=F32)


def _rms(x, g):
    return x * lax.rsqrt(jnp.mean(x * x, axis=-1, keepdims=True) + EPS) * g


def _const_spec(shape):
    nd = len(shape)
    return pl.BlockSpec(shape, lambda *_: (0,) * nd)


FF_CHUNKS = ((0, 1024), (1024, 1024), (2048, 768))


STAGE_BUFS = 6
STAGE_BYTES = 3 * 256 * 1024


def _stage_rows(shape):
    r, c = shape
    fits = [n for n in range(16, r + 1, 16) if r % n == 0 and n * c * 4 <= STAGE_BYTES]
    return max(fits)


def _stage_scratch(shape):
    return [pltpu.VMEM(shape, BF16), pltpu.VMEM((STAGE_BUFS, _stage_rows(shape), shape[1]), F32),
            pltpu.SemaphoreType.DMA((STAGE_BUFS,))]


def _fetch_cast(src, dst, stage, sems):
    nbuf, rows, _ = stage.shape
    n = src.shape[0] // rows

    def copy(k):
        return pltpu.make_async_copy(src.at[pl.ds(k * rows, rows), :], stage.at[k % nbuf], sems.at[k % nbuf])

    for k in range(min(nbuf, n)):
        copy(k).start()
    for k in range(n):
        copy(k).wait()
        dst[k * rows:(k + 1) * rows, :] = stage[k % nbuf].astype(BF16)
        if k + nbuf < n:
            copy(k + nbuf).start()


def _convert_weights(first, layer, srcs, scratch):
    @pl.when(first)
    def _():
        for k, src in enumerate(srcs):
            _fetch_cast(src.at[layer], *scratch[3 * k:3 * k + 3])

    return scratch[0::3]


def _emit_weights(first, last, srcs, outs, sems):
    copies = [pltpu.make_async_copy(s, o, sems.at[k]) for k, (s, o) in enumerate(zip(srcs, outs))]

    @pl.when(first)
    def _():
        for c in copies:
            c.start()

    @pl.when(last)
    def _():
        for c in copies:
            c.wait()


def _ffn_body(*refs, final, layer):
    n_in = 6 if final else 5
    x_ref, xs_ref, g_ref, win_hbm, wout_hbm = refs[:5]
    gf_ref = refs[5] if final else None
    o_ref, os_ref = refs[n_in:n_in + 2]
    i = pl.program_id(0)
    win_ref, wout_ref = _convert_weights(i == 0, layer, (win_hbm, wout_hbm), refs[n_in + 2:])

    def ffn_rows(x):
        h = _rms(x, g_ref[...]).astype(BF16)
        acc = None
        for off, width in FF_CHUNKS:
            gate = _dot(h, win_ref[:, off:off + width])
            up = _dot(h, win_ref[:, D_FF + off:D_FF + off + width])
            a = (gate * jax.nn.sigmoid(gate) * up).astype(BF16)
            part = _dot(a, wout_ref[off:off + width, :])
            acc = part if acc is None else acc + part
        y = x + 0.5 * acc
        return _rms(y, gf_ref[...]) if final else y

    o_ref[...] = ffn_rows(x_ref[...])

    @pl.when(i == pl.num_programs(0) - 1)
    def _():
        os_ref[...] = ffn_rows(xs_ref[...])


def _layer_spec(shape, layer):
    nd = len(shape) - 1
    return pl.BlockSpec((None,) + tuple(shape[1:]), lambda *_: (layer,) + (0,) * nd,
                        pipeline_mode=pl.Buffered(1))


def _resident_spec(shape):
    nd = len(shape)
    return pl.BlockSpec(tuple(shape), lambda *_: (0,) * nd, pipeline_mode=pl.Buffered(1))


ANY_SPEC = pl.BlockSpec(memory_space=pl.ANY)


def _ffn(x2d, xs2d, g, w_in, w_out, layer, g_final=None, *, tm):
    n = x2d.shape[0]
    assert n % tm == 0
    final = g_final is not None
    row_spec = pl.BlockSpec((tm, D_MODEL), lambda i: (i, 0))
    in_specs = [row_spec, _resident_spec(xs2d.shape), _layer_spec(g.shape, layer), ANY_SPEC, ANY_SPEC]
    args = [x2d, xs2d, g, w_in, w_out]
    if final:
        in_specs.append(_const_spec((1, D_MODEL)))
        args.append(g_final.reshape(1, D_MODEL))
    scratch = _stage_scratch(w_in.shape[1:]) + _stage_scratch(w_out.shape[1:])
    return pl.pallas_call(
        functools.partial(_ffn_body, final=final, layer=layer),
        grid=(n // tm,),
        in_specs=in_specs,
        out_specs=[row_spec, _resident_spec(xs2d.shape)],
        out_shape=[jax.ShapeDtypeStruct(x2d.shape, F32), jax.ShapeDtypeStruct(xs2d.shape, F32)],
        scratch_shapes=scratch,
        compiler_params=pltpu.CompilerParams(
            dimension_semantics=("arbitrary",), vmem_limit_bytes=VMEM_LIMIT),
        name="ffn_final" if final else "ffn",
    )(*args)


IN_CHUNKS = tuple((o, 1024) for o in range(0, 5120, 1024)) + ((5120, 512),)


def _project_in(x, g_ref, win_ref, z_ref):
    h = _rms(x, g_ref[...]).astype(BF16)
    for off, width in IN_CHUNKS:
        z_ref[:, off:off + width] = _dot(h, win_ref[:, off:off + width])


def _rotate_inplace(z_ref, cos, sin_signed, first_half):
    for blk in range(2 * QK_W // LANES):
        cols = slice(blk * LANES, (blk + 1) * LANES)
        xb = z_ref[:, cols]
        partner = jnp.where(first_half, pltpu.roll(xb, LANES - 32, 1), pltpu.roll(xb, 32, 1))
        r = xb * cos + partner * sin_signed
        if blk >= QK_W // LANES:
            r = r * (RET_QK_DIM ** -0.5)
        z_ref[:, cols] = r


def _merge_out(x, z_ref, ret, pool, wout_ref):
    ga = z_ref[:, GA_OFF:GA_OFF + D_MODEL]
    gb = z_ref[:, GB_OFF:GB_OFF + D_MODEL]
    merged = jax.nn.sigmoid(ga) * ret + jax.nn.sigmoid(gb) * pool
    return x + _dot(merged.astype(BF16), wout_ref[...])


def _swish_gate_norm(o, gate):
    o = o * lax.rsqrt(jnp.mean(o * o, axis=-1, keepdims=True) + EPS)
    return o * (gate * jax.nn.sigmoid(gate))


def _mixer_prompt_body(*refs, tm, n_sub, chunk, n_alias, layer):
    (x_ref, g_ref, win_ref, wpool_ref, pscale_ref, wout_ref, cos_ref, sin_ref,
     dmask_ref, xi_ref, wk_ref, gs_ref) = refs[:12]
    outs = refs[12 + n_alias:]
    o_ref, sret_ref, spool_ref = outs[:3]
    z_ref, s_ref, ext_ref, ret_ref, pool_ref, h_ref, d_ref = outs[6:13]
    t = pl.program_id(1)
    nt = pl.num_programs(1)
    bi = pl.program_id(0)
    first = (bi == 0) & (t == 0)
    win_ref, wpool_ref, wout_ref = _convert_weights(first, layer, (win_ref, wpool_ref, wout_ref), outs[13:22])
    _emit_weights(first, (bi == pl.num_programs(0) - 1) & (t == nt - 1),
                  (win_ref, wpool_ref, wout_ref), outs[3:6], outs[22])

    @pl.when(t == 0)
    def _():
        s_ref[...] = jnp.zeros_like(s_ref)
        ext_ref[0:POOL_PAD, :] = jnp.zeros((POOL_PAD, POOL_IN), F32)

    tile_refs = (x_ref, g_ref, win_ref, wpool_ref, pscale_ref, wout_ref, cos_ref, sin_ref, dmask_ref, xi_ref,
                 wk_ref, gs_ref, o_ref, z_ref, s_ref, ext_ref, ret_ref, pool_ref, h_ref, d_ref)

    def tile(sub, carry):
        _mixer_prompt_tile(tile_refs, pl.multiple_of(sub * tm, tm), (t * n_sub + sub) * tm, tm, chunk)
        return carry

    lax.fori_loop(0, n_sub, tile, 0)

    @pl.when(t == nt - 1)
    def _():
        sret_ref[...] = s_ref[...].reshape(N_PAIRS * LANES, RET_V_DIM)
        spool_ref[...] = ext_ref[POOL_PAD - POOL_BUF:POOL_PAD, :]


def _mixer_prompt_tile(tile_refs, r0, pos0, tm, chunk):
    (x_ref, g_ref, win_ref, wpool_ref, pscale_ref, wout_ref, cos_ref, sin_ref, dmask_ref, xi_ref,
     wk_ref, gs_ref, o_ref, z_ref, s_ref, ext_ref, ret_ref, pool_ref, h_ref, d_ref) = tile_refs
    h_ref[...] = _rms(x_ref[pl.ds(r0, tm), :], g_ref[...]).astype(BF16)

    def project(off, width, act=None):
        y = _dot(h_ref[...], win_ref[:, off:off + width])
        z_ref[:, off:off + width] = y if act is None else act(y)

    project(Q_OFF, 2 * QK_W)
    project(V_OFF, V_W)
    project(U_OFF, POOL_IN)

    lane = lax.broadcasted_iota(jnp.int32, (1, LANES), 1)
    first_half = (lane % RET_QK_DIM) < (RET_QK_DIM // 2)
    _rotate_inplace(z_ref, cos_ref[pl.ds(r0, tm), :], sin_ref[pl.ds(r0, tm), :], first_half)

    ext_ref[POOL_PAD:POOL_PAD + tm, :] = z_ref[:, U_OFF:U_OFF + POOL_IN]
    pos = pos0 + lax.broadcasted_iota(jnp.int32, (tm, 1), 0)
    for g, w in enumerate(POOL_WINDOWS):
        cols = slice(g * POOL_GROUP_IN, (g + 1) * POOL_GROUP_IN)
        u = ext_ref[POOL_PAD:POOL_PAD + tm, cols]
        tot = u
        for j in range(1, w):
            tot = tot + ext_ref[POOL_PAD - j:POOL_PAD - j + tm, cols]
        cnt = jnp.minimum(pos + 1, w).astype(F32)
        d_ref[:, cols] = (tot / cnt - u).astype(BF16)
    tail = ext_ref[tm:tm + POOL_PAD, :]
    ext_ref[0:POOL_PAD, :] = tail

    def pool_matmul(g):
        ocols = slice(g * POOL_GROUP_OUT, (g + 1) * POOL_GROUP_OUT)
        y = _dot(d_ref[:, g * POOL_GROUP_IN:(g + 1) * POOL_GROUP_IN],
                 wpool_ref[g * POOL_GROUP_IN:(g + 1) * POOL_GROUP_IN, :])
        pool_ref[:, ocols] = y * pscale_ref[:, ocols]

    fill_width = 2 * LANES
    fillers = [functools.partial(project, off, fill_width, jax.nn.silu)
               for off in range(G_OFF, G_OFF + V_W, fill_width)]
    fillers += [functools.partial(project, off, fill_width, jax.nn.sigmoid)
                for off in range(GA_OFF, D_IN, fill_width)]
    fillers += [functools.partial(pool_matmul, g) for g in range(N_POOL_GROUPS)]

    head0 = (lane < RET_QK_DIM).astype(F32)
    head1 = 1.0 - head0

    n_chunks = tm // chunk
    stages = 3 * n_chunks

    def run_fillers(stage):
        for f in fillers[stage * len(fillers) // stages:(stage + 1) * len(fillers) // stages]:
            f()

    def pair_cols(off, pair, width):
        return slice(off + pair * width, off + (pair + 1) * width)

    for c in range(n_chunks):
        rows = slice(c * chunk, (c + 1) * chunk)
        scores = []
        for pair in range(N_PAIRS):
            qb = z_ref[rows, pair_cols(Q_OFF, pair, LANES)]
            kb = z_ref[rows, pair_cols(K_OFF, pair, LANES)]
            k_stack = jnp.concatenate([kb * head0, kb * head1], axis=0).astype(BF16)
            scores.append(lax.dot_general(qb.astype(BF16), k_stack, (((1,), (1,)), ((), ())),
                                          preferred_element_type=F32))
        run_fillers(3 * c)
        for pair in range(N_PAIRS):
            qx = z_ref[rows, pair_cols(Q_OFF, pair, LANES)] * xi_ref[pair]
            vb16 = z_ref[rows, pair_cols(V_OFF, pair, 2 * RET_V_DIM)].astype(BF16)
            s16 = s_ref[pair].astype(BF16)
            for j, hmask in enumerate((head0, head1)):
                h = 2 * pair + j
                scj = scores[pair][:, j * chunk:(j + 1) * chunk] * dmask_ref[h]
                lhs = jnp.concatenate([scj, qx * hmask], axis=1).astype(BF16)
                rhs = jnp.concatenate([vb16[:, j * RET_V_DIM:(j + 1) * RET_V_DIM], s16], axis=0)
                o = _dot(lhs, rhs)
                o = o * lax.rsqrt(jnp.mean(o * o, axis=-1, keepdims=True) + EPS)
                ret_ref[rows, h * RET_V_DIM:(h + 1) * RET_V_DIM] = o
        run_fillers(3 * c + 1)
        for pair in range(N_PAIRS):
            kw = (z_ref[rows, pair_cols(K_OFF, pair, LANES)] * wk_ref[pair]).astype(BF16)
            vb16 = z_ref[rows, pair_cols(V_OFF, pair, 2 * RET_V_DIM)].astype(BF16)
            upd = lax.dot_general(kw, vb16, (((0,), (0,)), ((), ())),
                                  preferred_element_type=F32)
            new_s = jnp.concatenate([upd[0:RET_QK_DIM, 0:RET_V_DIM],
                                     upd[RET_QK_DIM:, RET_V_DIM:]], axis=0)
            s_ref[pair] = s_ref[pair] * gs_ref[pair] + new_s
        run_fillers(3 * c + 2)

    n_halves = 2
    for r in range(n_halves):
        rows = slice(r * tm // n_halves, (r + 1) * tm // n_halves)
        ret = ret_ref[rows, :] * z_ref[rows, G_OFF:G_OFF + V_W]
        merged = (z_ref[rows, GA_OFF:GA_OFF + D_MODEL] * ret
                  + z_ref[rows, GB_OFF:GB_OFF + D_MODEL] * pool_ref[rows, :])
        orows = pl.ds(pl.multiple_of(r0 + r * (tm // n_halves), tm // n_halves), tm // n_halves)
        o_ref[orows, :] = x_ref[orows, :] + _dot(merged.astype(BF16), wout_ref[...])


def _state_alias(prev, n_in, first_out):
    if prev is None:
        return [], [], {}
    specs = [pl.BlockSpec(memory_space=pl.ANY) for _ in prev]
    aliases = {n_in + k: first_out + k for k in range(len(prev))}
    return list(prev), specs, aliases


def _mixer_prompt(x, g, w_in, w_pool, pool_scale, w_out, tabs, layer, prev_states, *, tm, n_sub):
    b, t_len, _ = x.shape
    depth = w_in.shape[0]
    chunk = RET_CHUNK
    blk = n_sub * tm
    assert t_len % blk == 0 and tm % chunk == 0
    cos, sin, dmask, xi, wk, gs = tabs
    row_spec = pl.BlockSpec((None, blk, D_MODEL), lambda i, j: (i, j, 0))
    tab_spec = pl.BlockSpec((blk, LANES), lambda i, j: (j, 0))
    in_specs = [
        row_spec, _layer_spec(g.shape, layer), ANY_SPEC, ANY_SPEC, _layer_spec(pool_scale.shape, layer),
        ANY_SPEC, tab_spec, tab_spec,
        _const_spec(dmask.shape), _const_spec(xi.shape), _const_spec(wk.shape), _const_spec(gs.shape),
    ]
    args = [x, g, w_in, w_pool, pool_scale, w_out, cos, sin, dmask, xi, wk, gs]
    extra, extra_specs, aliases = _state_alias(prev_states, len(args), 1)
    w_shapes = [w_in.shape[1:], w_pool.shape[1:], w_out.shape[1:]]
    out_specs = [
        row_spec,
        pl.BlockSpec((None, None, N_PAIRS * LANES, RET_V_DIM), lambda i, j: (layer, i, 0, 0)),
        pl.BlockSpec((None, None, POOL_BUF, POOL_IN), lambda i, j: (layer, i, 0, 0)),
    ] + [ANY_SPEC] * len(w_shapes)
    out_shape = [
        jax.ShapeDtypeStruct((b, t_len, D_MODEL), F32),
        jax.ShapeDtypeStruct((depth, b, N_PAIRS * LANES, RET_V_DIM), F32),
        jax.ShapeDtypeStruct((depth, b, POOL_BUF, POOL_IN), F32),
    ] + [jax.ShapeDtypeStruct(s, BF16) for s in w_shapes]
    scratch = [
        pltpu.VMEM((tm, D_IN), F32),
        pltpu.VMEM((N_PAIRS, LANES, RET_V_DIM), F32),
        pltpu.VMEM((POOL_PAD + tm, POOL_IN), F32),
        pltpu.VMEM((tm, V_W), F32),
        pltpu.VMEM((tm, D_MODEL), F32),
        pltpu.VMEM((tm, D_MODEL), BF16),
        pltpu.VMEM((tm, POOL_IN), BF16),
    ]
    for s in w_shapes:
        scratch += _stage_scratch(s)
    scratch.append(pltpu.SemaphoreType.DMA((len(w_shapes),)))
    y, sret, spool, *w16 = pl.pallas_call(
        functools.partial(_mixer_prompt_body, tm=tm, n_sub=n_sub, chunk=chunk, n_alias=len(extra), layer=layer),
        grid=(b, t_len // blk),
        in_specs=in_specs + extra_specs,
        out_specs=out_specs,
        out_shape=out_shape,
        scratch_shapes=scratch,
        input_output_aliases=aliases,
        compiler_params=pltpu.CompilerParams(
            dimension_semantics=("arbitrary", "arbitrary"), vmem_limit_bytes=VMEM_LIMIT),
        name="mixer_prompt",
    )(*args, *extra)
    return y, (sret, spool), tuple(w16)


def _log_gamma():
    return jnp.log1p(-jnp.exp2(-5.0 - jnp.arange(N_RET_HEADS, dtype=F32)))


def _rope_tables(pos):
    half = RET_QK_DIM // 2
    inv = jnp.power(jnp.float32(ROPE_BASE), -jnp.linspace(0.0, 1.0, half, dtype=F32))
    ang = pos[:, None] * inv[None, :]
    cos, sin = jnp.cos(ang), jnp.sin(ang)
    cos128 = jnp.tile(cos, (1, LANES // half))
    sin128 = jnp.tile(jnp.concatenate([-sin, sin], axis=1), (1, LANES // RET_QK_DIM))
    return cos128, sin128


def _pair_lanes(per_head):
    h, c = per_head.shape
    x = jnp.broadcast_to(per_head[:, :, None], (h, c, RET_QK_DIM))
    return x.reshape(N_PAIRS, 2, c, RET_QK_DIM).transpose(0, 2, 1, 3).reshape(N_PAIRS, c, LANES)


def _state_decay(chunk):
    g_c = jnp.exp(_log_gamma() * chunk)
    gs = jnp.broadcast_to(g_c[:, None, None], (N_RET_HEADS, RET_QK_DIM, RET_V_DIM))
    return gs.reshape(N_PAIRS, LANES, RET_V_DIM)


def _decay_tables(chunk):
    lg = _log_gamma()
    i = jnp.arange(chunk, dtype=F32)
    dist = i[:, None] - i[None, :]
    dmask = jnp.where(dist[None] >= 0, jnp.exp(lg[:, None, None] * jnp.maximum(dist, 0.0)[None]), 0.0)
    xi = jnp.exp(lg[:, None] * (i[None, :] + 1.0))
    wk = jnp.exp(lg[:, None] * (chunk - 1.0 - i[None, :]))
    return dmask, _pair_lanes(xi), _pair_lanes(wk), _state_decay(chunk)


def _mixer_sample_body(*refs, bb, ts, past, n_alias):
    (x_ref, g_ref, win_ref, wpool_ref, pscale_ref, wout_ref, cos_ref, sin_ref,
     xi_ref, wk_ref, lag_ref, gs_ref, ind_ref, s0_ref, pb_ref) = refs[:15]
    (o_ref, snew_ref, pnew_ref, z_ref, inter_ref, ublk_ref, pblk_ref) = refs[15 + n_alias:]
    rows = bb * ts
    x = x_ref[...]
    _project_in(x, g_ref, win_ref, z_ref)
    lane = lax.broadcasted_iota(jnp.int32, (1, LANES), 1)
    first_half = (lane % RET_QK_DIM) < (RET_QK_DIM // 2)
    _rotate_inplace(z_ref, cos_ref[...], sin_ref[...], first_half)

    q = z_ref[:, Q_OFF:Q_OFF + QK_W]
    k = z_ref[:, K_OFF:K_OFF + QK_W]
    v = z_ref[:, V_OFF:V_OFF + V_W]
    step = lax.broadcasted_iota(jnp.int32, (rows, 1), 0) % ts
    intra = None
    for d in range(ts):
        kd = k if d == 0 else pltpu.roll(k, d, 0)
        vd = v if d == 0 else pltpu.roll(v, d, 0)
        prod = jnp.where(step >= d, q * kd, 0.0).astype(BF16)
        score = _dot(prod, ind_ref[...])
        term = score * lag_ref[d:d + 1, :] * jnp.where(step >= d, vd, 0.0)
        intra = term if intra is None else intra + term

    row8 = lax.broadcasted_iota(jnp.int32, (2 * ts, 1), 0)
    top = (lax.broadcasted_iota(jnp.int32, (LANES, 1), 0) < RET_QK_DIM).astype(F32)
    bot = 1.0 - top

    def seq_pair(p, carry):
        r0 = pl.multiple_of(p * 2 * ts, 2 * ts)
        rsl = pl.ds(r0, 2 * ts)
        for pair in range(N_PAIRS):
            lanes = slice(pair * LANES, (pair + 1) * LANES)
            q8 = (z_ref[rsl, Q_OFF + pair * LANES:Q_OFF + (pair + 1) * LANES] * xi_ref[:, lanes]).astype(BF16)
            kw8 = z_ref[rsl, K_OFF + pair * LANES:K_OFF + (pair + 1) * LANES] * wk_ref[:, lanes]
            v8 = z_ref[rsl, V_OFF + pair * 2 * RET_V_DIM:V_OFF + (pair + 1) * 2 * RET_V_DIM].astype(BF16)
            outs = []
            for j in range(2):
                b = 2 * p + j
                s_pair = s0_ref[b, lanes, :]
                bd = jnp.concatenate([s_pair * top, s_pair * bot], axis=1).astype(BF16)
                outs.append(_dot(q8, bd))
                mine = (row8 >= j * ts) & (row8 < (j + 1) * ts)
                kwj = jnp.where(mine, kw8, 0.0).astype(BF16)
                upd = lax.dot_general(kwj, v8, (((0,), (0,)), ((), ())), preferred_element_type=F32)
                new_s = jnp.concatenate([upd[0:RET_QK_DIM, 0:RET_V_DIM],
                                         upd[RET_QK_DIM:, RET_V_DIM:]], axis=0)
                snew_ref[b, lanes, :] = s_pair * gs_ref[pair] + new_s
            inter_ref[rsl, pair * 2 * RET_V_DIM:(pair + 1) * 2 * RET_V_DIM] = jnp.where(
                row8 < ts, outs[0], outs[1])
        return carry

    lax.fori_loop(0, bb // 2, seq_pair, 0, unroll=8)

    keep = POOL_BUF - ts
    pnew_ref[0:keep] = pb_ref[ts:]
    for g, w in enumerate(POOL_WINDOWS):
        ublk_ref[g] = z_ref[:, U_OFF + g * POOL_GROUP_IN:U_OFF + (g + 1) * POOL_GROUP_IN]
        u_steps = [ublk_ref[g, pl.ds(t, bb, stride=ts), :] for t in range(ts)]
        for t in range(ts):
            pnew_ref[keep + t, :, g * POOL_GROUP_IN:(g + 1) * POOL_GROUP_IN] = u_steps[t]
        ds_ = []
        for t in range(ts):
            tot = None
            for j in range(w):
                idx = POOL_BUF + t - j
                if idx >= POOL_BUF:
                    term = u_steps[idx - POOL_BUF]
                else:
                    term = pb_ref[idx, :, g * POOL_GROUP_IN:(g + 1) * POOL_GROUP_IN]
                tot = term if tot is None else tot + term
            cnt = float(min(past + t + 1, w))
            ds_.append(tot / cnt - u_steps[t])
        y = _dot(jnp.concatenate(ds_, axis=0).astype(BF16),
                 wpool_ref[g * POOL_GROUP_IN:(g + 1) * POOL_GROUP_IN, :])
        y = y * pscale_ref[:, g * POOL_GROUP_OUT:(g + 1) * POOL_GROUP_OUT]
        for half in range(POOL_GROUP_OUT // LANES):
            blk = g * (POOL_GROUP_OUT // LANES) + half
            for t in range(ts):
                pblk_ref[blk, pl.ds(t, bb, stride=ts), :] = y[t * bb:(t + 1) * bb, half * LANES:(half + 1) * LANES]
    pool = jnp.concatenate([pblk_ref[blk] for blk in range(D_MODEL // LANES)], axis=1)

    o_all = intra + inter_ref[...]
    rets = []
    for h in range(N_RET_HEADS):
        hc = slice(h * RET_V_DIM, (h + 1) * RET_V_DIM)
        rets.append(_swish_gate_norm(o_all[:, hc], z_ref[:, G_OFF + h * RET_V_DIM:G_OFF + (h + 1) * RET_V_DIM]))
    ret = jnp.concatenate(rets, axis=1)
    o_ref[...] = _merge_out(x, z_ref, ret, pool, wout_ref)


def _sample_tables(ts, past, rows):
    lg = _log_gamma()
    i = jnp.arange(ts, dtype=F32)
    lag = jnp.exp(lg[:, None] * i[None, :])
    lag = jnp.repeat(lag.T, RET_V_DIM, axis=1)
    xi = jnp.exp(lg[:, None] * (i[None, :] + 1.0))
    wk = jnp.exp(lg[:, None] * (ts - 1.0 - i[None, :]))
    expand = lambda a: jnp.tile(jnp.repeat(a.T, RET_QK_DIM, axis=1), (2, 1))
    pos = (past + jnp.arange(ts, dtype=jnp.int32)).astype(F32)
    cos, sin = _rope_tables(pos)
    reps = rows // ts
    head_of_k = jnp.arange(QK_W) // RET_QK_DIM
    head_of_v = jnp.arange(V_W) // RET_V_DIM
    ind = (head_of_k[:, None] == head_of_v[None, :]).astype(BF16)
    return (jnp.tile(cos, (reps, 1)), jnp.tile(sin, (reps, 1)), expand(xi), expand(wk), lag,
            _state_decay(ts), ind)


def _mixer_sample(x2d, s0, pbuf, g, w_in, w_pool, pool_scale, w_out, tabs, layer, prev_states, *, bb, ts, past):
    n = x2d.shape[0]
    depth = s0.shape[0]
    nb = n // ts
    rows = bb * ts
    assert nb % bb == 0 and bb % 2 == 0 and (2 * ts) % 8 == 0 and ts <= POOL_BUF
    cos, sin, xi, wk, lag, gs, ind = tabs
    row_spec = pl.BlockSpec((rows, D_MODEL), lambda i: (i, 0))
    state_spec = pl.BlockSpec((None, bb, N_PAIRS * LANES, RET_V_DIM), lambda i: (layer, i, 0, 0))
    pool_spec = pl.BlockSpec((None, POOL_BUF, bb, POOL_IN), lambda i: (layer, 0, i, 0))
    in_specs = [
        row_spec, _layer_spec(g.shape, layer), _resident_spec(w_in.shape),
        _resident_spec(w_pool.shape), _layer_spec(pool_scale.shape, layer),
        _resident_spec(w_out.shape), _const_spec(cos.shape), _const_spec(sin.shape),
        _const_spec(xi.shape), _const_spec(wk.shape), _const_spec(lag.shape), _const_spec(gs.shape),
        _const_spec(ind.shape), state_spec, pool_spec,
    ]
    args = [x2d, g, w_in, w_pool, pool_scale, w_out, cos, sin, xi, wk, lag, gs, ind, s0, pbuf]
    extra, extra_specs, aliases = _state_alias(prev_states, len(args), 1)
    out_specs = [row_spec, state_spec, pool_spec]
    out_shape = [
        jax.ShapeDtypeStruct((n, D_MODEL), F32),
        jax.ShapeDtypeStruct((depth, nb, N_PAIRS * LANES, RET_V_DIM), F32),
        jax.ShapeDtypeStruct((depth, POOL_BUF, nb, POOL_IN), F32),
    ]
    scratch = [
        pltpu.VMEM((rows, D_IN), F32),
        pltpu.VMEM((rows, V_W), F32),
        pltpu.VMEM((N_POOL_GROUPS, rows, LANES), F32),
        pltpu.VMEM((D_MODEL // LANES, rows, LANES), F32),
    ]
    y, snew, pnew = pl.pallas_call(
        functools.partial(_mixer_sample_body, bb=bb, ts=ts, past=past, n_alias=len(extra)),
        grid=(nb // bb,),
        in_specs=in_specs + extra_specs,
        out_specs=out_specs,
        out_shape=out_shape,
        scratch_shapes=scratch,
        input_output_aliases=aliases,
        compiler_params=pltpu.CompilerParams(
            dimension_semantics=("arbitrary",), vmem_limit_bytes=VMEM_LIMIT),
        name="mixer_sample",
    )(*args, *extra)
    return y, (snew, pnew)


def kernel(x_prompt, x_sample, state_ret, state_pool, norm_ffn1, w_ffn1_in, w_ffn1_out, norm_mix,
           w_in, w_pool, pool_scale, w_out, norm_ffn2, w_ffn2_in, w_ffn2_out, norm_final):
    depth = w_in.shape[0]
    bp, tp, _ = x_prompt.shape
    bs, ts, _ = x_sample.shape
    wp = w_pool.reshape(depth, N_POOL_GROUPS * POOL_GROUP_IN, POOL_GROUP_OUT)

    row = lambda a: a.reshape(depth, 1, D_MODEL)
    n1, nm, n2, ps = map(row, (norm_ffn1, norm_mix, norm_ffn2, pool_scale))
    gf = norm_final.reshape(1, D_MODEL)
    last = depth - 1

    tm = 512
    tm_ffn = 1024
    pos_p = jnp.arange(tp, dtype=jnp.int32).astype(F32)
    tabs_p = _rope_tables(pos_p) + _decay_tables(RET_CHUNK)

    bb = 32
    tabs_s = _sample_tables(ts, PAST_LEN, bb * ts)
    s0_all = state_ret.reshape(depth, bs, N_PAIRS * LANES, RET_V_DIM)
    pb_all = jnp.swapaxes(state_pool, 1, 2)

    x = x_prompt.reshape(bp * tp, D_MODEL)
    xs = x_sample.reshape(bs * ts, D_MODEL)
    states_p = states_s = None
    for l in range(depth):
        x, xs = _ffn(x, xs, n1, w_ffn1_in, w_ffn1_out, l, tm=tm_ffn)
        x, states_p, (wi16, wp16, wo16) = _mixer_prompt(x.reshape(bp, tp, D_MODEL), nm, w_in, wp, ps, w_out,
                                                        tabs_p, l, states_p, tm=tm, n_sub=2)
        xs, states_s = _mixer_sample(xs, s0_all, pb_all, nm, wi16, wp16, ps, wo16, tabs_s, l, states_s,
                                     bb=bb, ts=ts, past=PAST_LEN)
        x, xs = _ffn(x.reshape(bp * tp, D_MODEL), xs, n2, w_ffn2_in, w_ffn2_out, l,
                     gf if l == last else None, tm=tm_ffn)
    y_prompt = x.reshape(bp, tp, D_MODEL)
    y_sample = xs.reshape(bs, ts, D_MODEL)
    state_ret_prompt = states_p[0].reshape(depth, bp, N_RET_HEADS, RET_QK_DIM, RET_V_DIM)
    state_pool_prompt = states_p[1]
    state_ret_sample = states_s[0].reshape(depth, bs, N_RET_HEADS, RET_QK_DIM, RET_V_DIM)
    state_pool_sample = jnp.swapaxes(states_s[1], 1, 2)
    return (y_prompt, y_sample, state_ret_prompt, state_ret_sample, state_pool_prompt, state_pool_sample)
```

```python
import functools

import jax
import jax.numpy as jnp
from jax import lax
from jax.experimental import pallas as pl
from jax.experimental.pallas import tpu as pltpu

D_MODEL = 1024
N_RET_HEADS = 8
RET_QK_DIM = 64
RET_V_DIM = 128
RET_CHUNK = 128
ROPE_BASE = 10000.0
QK_W = N_RET_HEADS * RET_QK_DIM
V_W = N_RET_HEADS * RET_V_DIM
POOL_WINDOWS = (2, 4, 8, 16)
N_POOL_GROUPS = 4
POOL_IN = 512
POOL_GROUP_IN = 128
POOL_GROUP_OUT = 256
POOL_BUF = 15
D_IN = 2 * QK_W + 2 * V_W + POOL_IN + 2 * D_MODEL
D_FF = 2816
EPS = 1e-6
PAST_LEN = 16384

Q_OFF = 0
K_OFF = QK_W
V_OFF = 2 * QK_W
G_OFF = 2 * QK_W + V_W
U_OFF = 2 * QK_W + 2 * V_W
GA_OFF = U_OFF + POOL_IN
GB_OFF = GA_OFF + D_MODEL

LANES = 128
N_PAIRS = N_RET_HEADS // 2
POOL_PAD = 16
VMEM_LIMIT = 60 * 1024 * 1024

F32 = jnp.float32
BF16 = jnp.bfloat16


def _dot(a, b):
    return jnp.dot(a, b, preferred_element_type=F32)


def _rms(x, g):
    return x * lax.rsqrt(jnp.mean(x * x, axis=-1, keepdims=True) + EPS) * g


def _const_spec(shape):
    nd = len(shape)
    return pl.BlockSpec(shape, lambda *_: (0,) * nd)


FF_CHUNKS = ((0, 1024), (1024, 1024), (2048, 768))


STAGE_BUFS = 6
STAGE_BYTES = 3 * 256 * 1024


def _stage_rows(shape):
    r, c = shape
    fits = [n for n in range(16, r + 1, 16) if r % n == 0 and n * c * 4 <= STAGE_BYTES]
    return max(fits)


def _stage_scratch(shape):
    return [pltpu.VMEM(shape, BF16), pltpu.VMEM((STAGE_BUFS, _stage_rows(shape), shape[1]), F32),
            pltpu.SemaphoreType.DMA((STAGE_BUFS,))]


def _fetch_cast(src, dst, stage, sems):
    nbuf, rows, _ = stage.shape
    n = src.shape[0] // rows

    def copy(k):
        return pltpu.make_async_copy(src.at[pl.ds(k * rows, rows), :], stage.at[k % nbuf], sems.at[k % nbuf])

    for k in range(min(nbuf, n)):
        copy(k).start()
    for k in range(n):
        copy(k).wait()
        dst[k * rows:(k + 1) * rows, :] = stage[k % nbuf].astype(BF16)
        if k + nbuf < n:
            copy(k + nbuf).start()


def _convert_weights(first, layer, srcs, scratch):
    @pl.when(first)
    def _():
        for k, src in enumerate(srcs):
            _fetch_cast(src.at[layer], *scratch[3 * k:3 * k + 3])

    return scratch[0::3]


def _emit_weights(first, last, srcs, outs, sems):
    copies = [pltpu.make_async_copy(s, o, sems.at[k]) for k, (s, o) in enumerate(zip(srcs, outs))]

    @pl.when(first)
    def _():
        for c in copies:
            c.start()

    @pl.when(last)
    def _():
        for c in copies:
            c.wait()


def _ffn_body(*refs, final, layer):
    n_in = 6 if final else 5
    x_ref, xs_ref, g_ref, win_hbm, wout_hbm = refs[:5]
    gf_ref = refs[5] if final else None
    o_ref, os_ref = refs[n_in:n_in + 2]
    i = pl.program_id(0)
    win_ref, wout_ref = _convert_weights(i == 0, layer, (win_hbm, wout_hbm), refs[n_in + 2:])

    def ffn_rows(x):
        h = _rms(x, g_ref[...]).astype(BF16)
        acc = None
        for off, width in FF_CHUNKS:
            gate = _dot(h, win_ref[:, off:off + width])
            up = _dot(h, win_ref[:, D_FF + off:D_FF + off + width])
            a = (gate * jax.nn.sigmoid(gate) * up).astype(BF16)
            part = _dot(a, wout_ref[off:off + width, :])
            acc = part if acc is None else acc + part
        y = x + 0.5 * acc
        return _rms(y, gf_ref[...]) if final else y

    o_ref[...] = ffn_rows(x_ref[...])

    @pl.when(i == pl.num_programs(0) - 1)
    def _():
        os_ref[...] = ffn_rows(xs_ref[...])


def _layer_spec(shape, layer):
    nd = len(shape) - 1
    return pl.BlockSpec((None,) + tuple(shape[1:]), lambda *_: (layer,) + (0,) * nd,
                        pipeline_mode=pl.Buffered(1))


def _resident_spec(shape):
    nd = len(shape)
    return pl.BlockSpec(tuple(shape), lambda *_: (0,) * nd, pipeline_mode=pl.Buffered(1))


ANY_SPEC = pl.BlockSpec(memory_space=pl.ANY)


def _ffn(x2d, xs2d, g, w_in, w_out, layer, g_final=None, *, tm):
    n = x2d.shape[0]
    assert n % tm == 0
    final = g_final is not None
    row_spec = pl.BlockSpec((tm, D_MODEL), lambda i: (i, 0))
    in_specs = [row_spec, _resident_spec(xs2d.shape), _layer_spec(g.shape, layer), ANY_SPEC, ANY_SPEC]
    args = [x2d, xs2d, g, w_in, w_out]
    if final:
        in_specs.append(_const_spec((1, D_MODEL)))
        args.append(g_final.reshape(1, D_MODEL))
    scratch = _stage_scratch(w_in.shape[1:]) + _stage_scratch(w_out.shape[1:])
    return pl.pallas_call(
        functools.partial(_ffn_body, final=final, layer=layer),
        grid=(n // tm,),
        in_specs=in_specs,
        out_specs=[row_spec, _resident_spec(xs2d.shape)],
        out_shape=[jax.ShapeDtypeStruct(x2d.shape, F32), jax.ShapeDtypeStruct(xs2d.shape, F32)],
        scratch_shapes=scratch,
        compiler_params=pltpu.CompilerParams(
            dimension_semantics=("arbitrary",), vmem_limit_bytes=VMEM_LIMIT),
        name="ffn_final" if final else "ffn",
    )(*args)


IN_CHUNKS = tuple((o, 1024) for o in range(0, 5120, 1024)) + ((5120, 512),)


def _project_in(x, g_ref, win_ref, z_ref):
    h = _rms(x, g_ref[...]).astype(BF16)
    for off, width in IN_CHUNKS:
        z_ref[:, off:off + width] = _dot(h, win_ref[:, off:off + width])


def _rotate_inplace(z_ref, cos, sin_signed, first_half):
    for blk in range(2 * QK_W // LANES):
        cols = slice(blk * LANES, (blk + 1) * LANES)
        xb = z_ref[:, cols]
        partner = jnp.where(first_half, pltpu.roll(xb, LANES - 32, 1), pltpu.roll(xb, 32, 1))
        r = xb * cos + partner * sin_signed
        if blk >= QK_W // LANES:
            r = r * (RET_QK_DIM ** -0.5)
        z_ref[:, cols] = r


def _merge_out(x, z_ref, ret, pool, wout_ref):
    ga = z_ref[:, GA_OFF:GA_OFF + D_MODEL]
    gb = z_ref[:, GB_OFF:GB_OFF + D_MODEL]
    merged = jax.nn.sigmoid(ga) * ret + jax.nn.sigmoid(gb) * pool
    return x + _dot(merged.astype(BF16), wout_ref[...])


def _swish_gate_norm(o, gate):
    o = o * lax.rsqrt(jnp.mean(o * o, axis=-1, keepdims=True) + EPS)
    return o * (gate * jax.nn.sigmoid(gate))


def _mixer_prompt_body(*refs, tm, chunk, n_alias, layer):
    (x_ref, g_ref, win_ref, wpool_ref, pscale_ref, wout_ref, cos_ref, sin_ref,
     dmask_ref, xi_ref, wk_ref, gs_ref) = refs[:12]
    outs = refs[12 + n_alias:]
    o_ref, sret_ref, spool_ref = outs[:3]
    z_ref, s_ref, ext_ref, ret_ref, pool_ref, h_ref, d_ref = outs[6:13]
    t = pl.program_id(1)
    nt = pl.num_programs(1)
    bi = pl.program_id(0)
    first = (bi == 0) & (t == 0)
    win_ref, wpool_ref, wout_ref = _convert_weights(first, layer, (win_ref, wpool_ref, wout_ref), outs[13:22])
    _emit_weights(first, (bi == pl.num_programs(0) - 1) & (t == nt - 1),
                  (win_ref, wpool_ref, wout_ref), outs[3:6], outs[22])

    @pl.when(t == 0)
    def _():
        s_ref[...] = jnp.zeros_like(s_ref)
        ext_ref[0:POOL_PAD, :] = jnp.zeros((POOL_PAD, POOL_IN), F32)

    _mixer_prompt_tile((x_ref, g_ref, win_ref, wpool_ref, pscale_ref, wout_ref, cos_ref, sin_ref, dmask_ref, xi_ref,
                        wk_ref, gs_ref, o_ref, z_ref, s_ref, ext_ref, ret_ref, pool_ref, h_ref, d_ref),
                       t * tm, tm, chunk)

    @pl.when(t == nt - 1)
    def _():
        sret_ref[...] = s_ref[...].reshape(N_PAIRS * LANES, RET_V_DIM)
        spool_ref[...] = ext_ref[POOL_PAD - POOL_BUF:POOL_PAD, :]


def _mixer_prompt_tile(tile_refs, pos0, tm, chunk):
    (x_ref, g_ref, win_ref, wpool_ref, pscale_ref, wout_ref, cos_ref, sin_ref, dmask_ref, xi_ref,
     wk_ref, gs_ref, o_ref, z_ref, s_ref, ext_ref, ret_ref, pool_ref, h_ref, d_ref) = tile_refs
    h_ref[...] = _rms(x_ref[...], g_ref[...]).astype(BF16)

    def project(off, width, act=None):
        y = _dot(h_ref[...], win_ref[:, off:off + width])
        z_ref[:, off:off + width] = y if act is None else act(y)

    project(Q_OFF, 2 * QK_W)
    project(V_OFF, V_W)
    project(U_OFF, POOL_IN)

    lane = lax.broadcasted_iota(jnp.int32, (1, LANES), 1)
    first_half = (lane % RET_QK_DIM) < (RET_QK_DIM // 2)
    _rotate_inplace(z_ref, cos_ref[...], sin_ref[...], first_half)

    ext_ref[POOL_PAD:POOL_PAD + tm, :] = z_ref[:, U_OFF:U_OFF + POOL_IN]
    pos = pos0 + lax.broadcasted_iota(jnp.int32, (tm, 1), 0)
    for g, w in enumerate(POOL_WINDOWS):
        cols = slice(g * POOL_GROUP_IN, (g + 1) * POOL_GROUP_IN)
        u = ext_ref[POOL_PAD:POOL_PAD + tm, cols]
        tot = u
        for j in range(1, w):
            tot = tot + ext_ref[POOL_PAD - j:POOL_PAD - j + tm, cols]
        cnt = jnp.minimum(pos + 1, w).astype(F32)
        d_ref[:, cols] = (tot / cnt - u).astype(BF16)
    tail = ext_ref[tm:tm + POOL_PAD, :]
    ext_ref[0:POOL_PAD, :] = tail

    def pool_matmul(g):
        ocols = slice(g * POOL_GROUP_OUT, (g + 1) * POOL_GROUP_OUT)
        y = _dot(d_ref[:, g * POOL_GROUP_IN:(g + 1) * POOL_GROUP_IN],
                 wpool_ref[g * POOL_GROUP_IN:(g + 1) * POOL_GROUP_IN, :])
        pool_ref[:, ocols] = y * pscale_ref[:, ocols]

    fill_width = 2 * LANES
    fillers = [functools.partial(project, off, fill_width, jax.nn.silu)
               for off in range(G_OFF, G_OFF + V_W, fill_width)]
    fillers += [functools.partial(project, off, fill_width, jax.nn.sigmoid)
                for off in range(GA_OFF, D_IN, fill_width)]
    fillers += [functools.partial(pool_matmul, g) for g in range(N_POOL_GROUPS)]

    head0 = (lane < RET_QK_DIM).astype(F32)
    head1 = 1.0 - head0

    n_chunks = tm // chunk
    stages = 3 * n_chunks

    def run_fillers(stage):
        for f in fillers[stage * len(fillers) // stages:(stage + 1) * len(fillers) // stages]:
            f()

    def pair_cols(off, pair, width):
        return slice(off + pair * width, off + (pair + 1) * width)

    for c in range(n_chunks):
        rows = slice(c * chunk, (c + 1) * chunk)
        scores = []
        for pair in range(N_PAIRS):
            qb = z_ref[rows, pair_cols(Q_OFF, pair, LANES)]
            kb = z_ref[rows, pair_cols(K_OFF, pair, LANES)]
            k_stack = jnp.concatenate([kb * head0, kb * head1], axis=0).astype(BF16)
            scores.append(lax.dot_general(qb.astype(BF16), k_stack, (((1,), (1,)), ((), ())),
                                          preferred_element_type=F32))
        run_fillers(3 * c)
        for pair in range(N_PAIRS):
            qx = z_ref[rows, pair_cols(Q_OFF, pair, LANES)] * xi_ref[pair]
            vb16 = z_ref[rows, pair_cols(V_OFF, pair, 2 * RET_V_DIM)].astype(BF16)
            s16 = s_ref[pair].astype(BF16)
            for j, hmask in enumerate((head0, head1)):
                h = 2 * pair + j
                scj = scores[pair][:, j * chunk:(j + 1) * chunk] * dmask_ref[h]
                lhs = jnp.concatenate([scj, qx * hmask], axis=1).astype(BF16)
                rhs = jnp.concatenate([vb16[:, j * RET_V_DIM:(j + 1) * RET_V_DIM], s16], axis=0)
                o = _dot(lhs, rhs)
                o = o * lax.rsqrt(jnp.mean(o * o, axis=-1, keepdims=True) + EPS)
                ret_ref[rows, h * RET_V_DIM:(h + 1) * RET_V_DIM] = o
        run_fillers(3 * c + 1)
        for pair in range(N_PAIRS):
            kw = (z_ref[rows, pair_cols(K_OFF, pair, LANES)] * wk_ref[pair]).astype(BF16)
            vb16 = z_ref[rows, pair_cols(V_OFF, pair, 2 * RET_V_DIM)].astype(BF16)
            upd = lax.dot_general(kw, vb16, (((0,), (0,)), ((), ())),
                                  preferred_element_type=F32)
            new_s = jnp.concatenate([upd[0:RET_QK_DIM, 0:RET_V_DIM],
                                     upd[RET_QK_DIM:, RET_V_DIM:]], axis=0)
            s_ref[pair] = s_ref[pair] * gs_ref[pair] + new_s
        run_fillers(3 * c + 2)

    n_halves = 2
    for r in range(n_halves):
        rows = slice(r * tm // n_halves, (r + 1) * tm // n_halves)
        ret = ret_ref[rows, :] * z_ref[rows, G_OFF:G_OFF + V_W]
        merged = (z_ref[rows, GA_OFF:GA_OFF + D_MODEL] * ret
                  + z_ref[rows, GB_OFF:GB_OFF + D_MODEL] * pool_ref[rows, :])
        o_ref[rows, :] = x_ref[rows, :] + _dot(merged.astype(BF16), wout_ref[...])


def _state_alias(prev, n_in, first_out):
    if prev is None:
        return [], [], {}
    specs = [pl.BlockSpec(memory_space=pl.ANY) for _ in prev]
    aliases = {n_in + k: first_out + k for k in range(len(prev))}
    return list(prev), specs, aliases


def _mixer_prompt(x, g, w_in, w_pool, pool_scale, w_out, tabs, layer, prev_states, *, tm):
    b, t_len, _ = x.shape
    depth = w_in.shape[0]
    chunk = RET_CHUNK
    assert t_len % tm == 0 and tm % chunk == 0
    cos, sin, dmask, xi, wk, gs = tabs
    row_spec = pl.BlockSpec((None, tm, D_MODEL), lambda i, j: (i, j, 0))
    tab_spec = pl.BlockSpec((tm, LANES), lambda i, j: (j, 0))
    in_specs = [
        row_spec, _layer_spec(g.shape, layer), ANY_SPEC, ANY_SPEC, _layer_spec(pool_scale.shape, layer),
        ANY_SPEC, tab_spec, tab_spec,
        _const_spec(dmask.shape), _const_spec(xi.shape), _const_spec(wk.shape), _const_spec(gs.shape),
    ]
    args = [x, g, w_in, w_pool, pool_scale, w_out, cos, sin, dmask, xi, wk, gs]
    extra, extra_specs, aliases = _state_alias(prev_states, len(args), 1)
    w_shapes = [w_in.shape[1:], w_pool.shape[1:], w_out.shape[1:]]
    out_specs = [
        row_spec,
        pl.BlockSpec((None, None, N_PAIRS * LANES, RET_V_DIM), lambda i, j: (layer, i, 0, 0)),
        pl.BlockSpec((None, None, POOL_BUF, POOL_IN), lambda i, j: (layer, i, 0, 0)),
    ] + [ANY_SPEC] * len(w_shapes)
    out_shape = [
        jax.ShapeDtypeStruct((b, t_len, D_MODEL), F32),
        jax.ShapeDtypeStruct((depth, b, N_PAIRS * LANES, RET_V_DIM), F32),
        jax.ShapeDtypeStruct((depth, b, POOL_BUF, POOL_IN), F32),
    ] + [jax.ShapeDtypeStruct(s, BF16) for s in w_shapes]
    scratch = [
        pltpu.VMEM((tm, D_IN), F32),
        pltpu.VMEM((N_PAIRS, LANES, RET_V_DIM), F32),
        pltpu.VMEM((POOL_PAD + tm, POOL_IN), F32),
        pltpu.VMEM((tm, V_W), F32),
        pltpu.VMEM((tm, D_MODEL), F32),
        pltpu.VMEM((tm, D_MODEL), BF16),
        pltpu.VMEM((tm, POOL_IN), BF16),
    ]
    for s in w_shapes:
        scratch += _stage_scratch(s)
    scratch.append(pltpu.SemaphoreType.DMA((len(w_shapes),)))
    y, sret, spool, *w16 = pl.pallas_call(
        functools.partial(_mixer_prompt_body, tm=tm, chunk=chunk, n_alias=len(extra), layer=layer),
        grid=(b, t_len // tm),
        in_specs=in_specs + extra_specs,
        out_specs=out_specs,
        out_shape=out_shape,
        scratch_shapes=scratch,
        input_output_aliases=aliases,
        compiler_params=pltpu.CompilerParams(
            dimension_semantics=("arbitrary", "arbitrary"), vmem_limit_bytes=VMEM_LIMIT),
        name="mixer_prompt",
    )(*args, *extra)
    return y, (sret, spool), tuple(w16)


def _log_gamma():
    return jnp.log1p(-jnp.exp2(-5.0 - jnp.arange(N_RET_HEADS, dtype=F32)))


def _rope_tables(pos):
    half = RET_QK_DIM // 2
    inv = jnp.power(jnp.float32(ROPE_BASE), -jnp.linspace(0.0, 1.0, half, dtype=F32))
    ang = pos[:, None] * inv[None, :]
    cos, sin = jnp.cos(ang), jnp.sin(ang)
    cos128 = jnp.tile(cos, (1, LANES // half))
    sin128 = jnp.tile(jnp.concatenate([-sin, sin], axis=1), (1, LANES // RET_QK_DIM))
    return cos128, sin128


def _pair_lanes(per_head):
    h, c = per_head.shape
    x = jnp.broadcast_to(per_head[:, :, None], (h, c, RET_QK_DIM))
    return x.reshape(N_PAIRS, 2, c, RET_QK_DIM).transpose(0, 2, 1, 3).reshape(N_PAIRS, c, LANES)


def _state_decay(chunk):
    g_c = jnp.exp(_log_gamma() * chunk)
    gs = jnp.broadcast_to(g_c[:, None, None], (N_RET_HEADS, RET_QK_DIM, RET_V_DIM))
    return gs.reshape(N_PAIRS, LANES, RET_V_DIM)


def _decay_tables(chunk):
    lg = _log_gamma()
    i = jnp.arange(chunk, dtype=F32)
    dist = i[:, None] - i[None, :]
    dmask = jnp.where(dist[None] >= 0, jnp.exp(lg[:, None, None] * jnp.maximum(dist, 0.0)[None]), 0.0)
    xi = jnp.exp(lg[:, None] * (i[None, :] + 1.0))
    wk = jnp.exp(lg[:, None] * (chunk - 1.0 - i[None, :]))
    return dmask, _pair_lanes(xi), _pair_lanes(wk), _state_decay(chunk)


def _mixer_sample_body(*refs, bb, ts, past, n_alias):
    (x_ref, g_ref, win_ref, wpool_ref, pscale_ref, wout_ref, cos_ref, sin_ref,
     xi_ref, wk_ref, lag_ref, gs_ref, ind_ref, s0_ref, pb_ref) = refs[:15]
    (o_ref, snew_ref, pnew_ref, z_ref, inter_ref, ublk_ref, pblk_ref) = refs[15 + n_alias:]
    rows = bb * ts
    x = x_ref[...]
    _project_in(x, g_ref, win_ref, z_ref)
    lane = lax.broadcasted_iota(jnp.int32, (1, LANES), 1)
    first_half = (lane % RET_QK_DIM) < (RET_QK_DIM // 2)
    _rotate_inplace(z_ref, cos_ref[...], sin_ref[...], first_half)

    q = z_ref[:, Q_OFF:Q_OFF + QK_W]
    k = z_ref[:, K_OFF:K_OFF + QK_W]
    v = z_ref[:, V_OFF:V_OFF + V_W]
    step = lax.broadcasted_iota(jnp.int32, (rows, 1), 0) % ts
    intra = None
    for d in range(ts):
        kd = k if d == 0 else pltpu.roll(k, d, 0)
        vd = v if d == 0 else pltpu.roll(v, d, 0)
        prod = jnp.where(step >= d, q * kd, 0.0).astype(BF16)
        score = _dot(prod, ind_ref[...])
        term = score * lag_ref[d:d + 1, :] * jnp.where(step >= d, vd, 0.0)
        intra = term if intra is None else intra + term

    row8 = lax.broadcasted_iota(jnp.int32, (2 * ts, 1), 0)
    top = (lax.broadcasted_iota(jnp.int32, (LANES, 1), 0) < RET_QK_DIM).astype(F32)
    bot = 1.0 - top

    def seq_pair(p, carry):
        r0 = pl.multiple_of(p * 2 * ts, 2 * ts)
        rsl = pl.ds(r0, 2 * ts)
        for pair in range(N_PAIRS):
            lanes = slice(pair * LANES, (pair + 1) * LANES)
            q8 = (z_ref[rsl, Q_OFF + pair * LANES:Q_OFF + (pair + 1) * LANES] * xi_ref[:, lanes]).astype(BF16)
            kw8 = z_ref[rsl, K_OFF + pair * LANES:K_OFF + (pair + 1) * LANES] * wk_ref[:, lanes]
            v8 = z_ref[rsl, V_OFF + pair * 2 * RET_V_DIM:V_OFF + (pair + 1) * 2 * RET_V_DIM].astype(BF16)
            outs = []
            for j in range(2):
                b = 2 * p + j
                s_pair = s0_ref[b, lanes, :]
                bd = jnp.concatenate([s_pair * top, s_pair * bot], axis=1).astype(BF16)
                outs.append(_dot(q8, bd))
                mine = (row8 >= j * ts) & (row8 < (j + 1) * ts)
                kwj = jnp.where(mine, kw8, 0.0).astype(BF16)
                upd = lax.dot_general(kwj, v8, (((0,), (0,)), ((), ())), preferred_element_type=F32)
                new_s = jnp.concatenate([upd[0:RET_QK_DIM, 0:RET_V_DIM],
                                         upd[RET_QK_DIM:, RET_V_DIM:]], axis=0)
                snew_ref[b, lanes, :] = s_pair * gs_ref[pair] + new_s
            inter_ref[rsl, pair * 2 * RET_V_DIM:(pair + 1) * 2 * RET_V_DIM] = jnp.where(
                row8 < ts, outs[0], outs[1])
        return carry

    lax.fori_loop(0, bb // 2, seq_pair, 0, unroll=8)

    keep = POOL_BUF - ts
    pnew_ref[0:keep] = pb_ref[ts:]
    for g, w in enumerate(POOL_WINDOWS):
        ublk_ref[g] = z_ref[:, U_OFF + g * POOL_GROUP_IN:U_OFF + (g + 1) * POOL_GROUP_IN]
        u_steps = [ublk_ref[g, pl.ds(t, bb, stride=ts), :] for t in range(ts)]
        for t in range(ts):
            pnew_ref[keep + t, :, g * POOL_GROUP_IN:(g + 1) * POOL_GROUP_IN] = u_steps[t]
        ds_ = []
        for t in range(ts):
            tot = None
            for j in range(w):
                idx = POOL_BUF + t - j
                if idx >= POOL_BUF:
                    term = u_steps[idx - POOL_BUF]
                else:
                    term = pb_ref[idx, :, g * POOL_GROUP_IN:(g + 1) * POOL_GROUP_IN]
                tot = term if tot is None else tot + term
            cnt = float(min(past + t + 1, w))
            ds_.append(tot / cnt - u_steps[t])
        y = _dot(jnp.concatenate(ds_, axis=0).astype(BF16),
                 wpool_ref[g * POOL_GROUP_IN:(g + 1) * POOL_GROUP_IN, :])
        y = y * pscale_ref[:, g * POOL_GROUP_OUT:(g + 1) * POOL_GROUP_OUT]
        for half in range(POOL_GROUP_OUT // LANES):
            blk = g * (POOL_GROUP_OUT // LANES) + half
            for t in range(ts):
                pblk_ref[blk, pl.ds(t, bb, stride=ts), :] = y[t * bb:(t + 1) * bb, half * LANES:(half + 1) * LANES]
    pool = jnp.concatenate([pblk_ref[blk] for blk in range(D_MODEL // LANES)], axis=1)

    o_all = intra + inter_ref[...]
    rets = []
    for h in range(N_RET_HEADS):
        hc = slice(h * RET_V_DIM, (h + 1) * RET_V_DIM)
        rets.append(_swish_gate_norm(o_all[:, hc], z_ref[:, G_OFF + h * RET_V_DIM:G_OFF + (h + 1) * RET_V_DIM]))
    ret = jnp.concatenate(rets, axis=1)
    o_ref[...] = _merge_out(x, z_ref, ret, pool, wout_ref)


def _sample_tables(ts, past, rows):
    lg = _log_gamma()
    i = jnp.arange(ts, dtype=F32)
    lag = jnp.exp(lg[:, None] * i[None, :])
    lag = jnp.repeat(lag.T, RET_V_DIM, axis=1)
    xi = jnp.exp(lg[:, None] * (i[None, :] + 1.0))
    wk = jnp.exp(lg[:, None] * (ts - 1.0 - i[None, :]))
    expand = lambda a: jnp.tile(jnp.repeat(a.T, RET_QK_DIM, axis=1), (2, 1))
    pos = (past + jnp.arange(ts, dtype=jnp.int32)).astype(F32)
    cos, sin = _rope_tables(pos)
    reps = rows // ts
    head_of_k = jnp.arange(QK_W) // RET_QK_DIM
    head_of_v = jnp.arange(V_W) // RET_V_DIM
    ind = (head_of_k[:, None] == head_of_v[None, :]).astype(BF16)
    return (jnp.tile(cos, (reps, 1)), jnp.tile(sin, (reps, 1)), expand(xi), expand(wk), lag,
            _state_decay(ts), ind)


def _mixer_sample(x2d, s0, pbuf, g, w_in, w_pool, pool_scale, w_out, tabs, layer, prev_states, *, bb, ts, past):
    n = x2d.shape[0]
    depth = s0.shape[0]
    nb = n // ts
    rows = bb * ts
    assert nb % bb == 0 and bb % 2 == 0 and (2 * ts) % 8 == 0 and ts <= POOL_BUF
    cos, sin, xi, wk, lag, gs, ind = tabs
    row_spec = pl.BlockSpec((rows, D_MODEL), lambda i: (i, 0))
    state_spec = pl.BlockSpec((None, bb, N_PAIRS * LANES, RET_V_DIM), lambda i: (layer, i, 0, 0))
    pool_spec = pl.BlockSpec((None, POOL_BUF, bb, POOL_IN), lambda i: (layer, 0, i, 0))
    in_specs = [
        row_spec, _layer_spec(g.shape, layer), _resident_spec(w_in.shape),
        _resident_spec(w_pool.shape), _layer_spec(pool_scale.shape, layer),
        _resident_spec(w_out.shape), _const_spec(cos.shape), _const_spec(sin.shape),
        _const_spec(xi.shape), _const_spec(wk.shape), _const_spec(lag.shape), _const_spec(gs.shape),
        _const_spec(ind.shape), state_spec, pool_spec,
    ]
    args = [x2d, g, w_in, w_pool, pool_scale, w_out, cos, sin, xi, wk, lag, gs, ind, s0, pbuf]
    extra, extra_specs, aliases = _state_alias(prev_states, len(args), 1)
    out_specs = [row_spec, state_spec, pool_spec]
    out_shape = [
        jax.ShapeDtypeStruct((n, D_MODEL), F32),
        jax.ShapeDtypeStruct((depth, nb, N_PAIRS * LANES, RET_V_DIM), F32),
        jax.ShapeDtypeStruct((depth, POOL_BUF, nb, POOL_IN), F32),
    ]
    scratch = [
        pltpu.VMEM((rows, D_IN), F32),
        pltpu.VMEM((rows, V_W), F32),
        pltpu.VMEM((N_POOL_GROUPS, rows, LANES), F32),
        pltpu.VMEM((D_MODEL // LANES, rows, LANES), F32),
    ]
    y, snew, pnew = pl.pallas_call(
        functools.partial(_mixer_sample_body, bb=bb, ts=ts, past=past, n_alias=len(extra)),
        grid=(nb // bb,),
        in_specs=in_specs + extra_specs,
        out_specs=out_specs,
        out_shape=out_shape,
        scratch_shapes=scratch,
        input_output_aliases=aliases,
        compiler_params=pltpu.CompilerParams(
            dimension_semantics=("arbitrary",), vmem_limit_bytes=VMEM_LIMIT),
        name="mixer_sample",
    )(*args, *extra)
    return y, (snew, pnew)


def kernel(x_prompt, x_sample, state_ret, state_pool, norm_ffn1, w_ffn1_in, w_ffn1_out, norm_mix,
           w_in, w_pool, pool_scale, w_out, norm_ffn2, w_ffn2_in, w_ffn2_out, norm_final):
    depth = w_in.shape[0]
    bp, tp, _ = x_prompt.shape
    bs, ts, _ = x_sample.shape
    wp = w_pool.reshape(depth, N_POOL_GROUPS * POOL_GROUP_IN, POOL_GROUP_OUT)

    row = lambda a: a.reshape(depth, 1, D_MODEL)
    n1, nm, n2, ps = map(row, (norm_ffn1, norm_mix, norm_ffn2, pool_scale))
    gf = norm_final.reshape(1, D_MODEL)
    last = depth - 1

    tm = 512
    tm_ffn = 1024
    pos_p = jnp.arange(tp, dtype=jnp.int32).astype(F32)
    tabs_p = _rope_tables(pos_p) + _decay_tables(RET_CHUNK)

    bb = 32
    tabs_s = _sample_tables(ts, PAST_LEN, bb * ts)
    s0_all = state_ret.reshape(depth, bs, N_PAIRS * LANES, RET_V_DIM)
    pb_all = jnp.swapaxes(state_pool, 1, 2)

    x = x_prompt.reshape(bp * tp, D_MODEL)
    xs = x_sample.reshape(bs * ts, D_MODEL)
    states_p = states_s = None
    for l in range(depth):
        x, xs = _ffn(x, xs, n1, w_ffn1_in, w_ffn1_out, l, tm=tm_ffn)
        x, states_p, (wi16, wp16, wo16) = _mixer_prompt(x.reshape(bp, tp, D_MODEL), nm, w_in, wp, ps, w_out,
                                                        tabs_p, l, states_p, tm=tm)
        xs, states_s = _mixer_sample(xs, s0_all, pb_all, nm, wi16, wp16, ps, wo16, tabs_s, l, states_s,
                                     bb=bb, ts=ts, past=PAST_LEN)
        x, xs = _ffn(x.reshape(bp * tp, D_MODEL), xs, n2, w_ffn2_in, w_ffn2_out, l,
                     gf if l == last else None, tm=tm_ffn)
    y_prompt = x.reshape(bp, tp, D_MODEL)
    y_sample = xs.reshape(bs, ts, D_MODEL)
    state_ret_prompt = states_p[0].reshape(depth, bp, N_RET_HEADS, RET_QK_DIM, RET_V_DIM)
    state_pool_prompt = states_p[1]
    state_ret_sample = states_s[0].reshape(depth, bs, N_RET_HEADS, RET_QK_DIM, RET_V_DIM)
    state_pool_sample = jnp.swapaxes(states_s[1], 1, 2)
    return (y_prompt, y_sample, state_ret_prompt, state_ret_sample, state_pool_prompt, state_pool_sample)
```

```python
import functools

import numpy as np
import jax
import jax.numpy as jnp
from jax import lax
from jax.experimental import pallas as pl
from jax.experimental.pallas import tpu as pltpu

D_MODEL = 1024
N_RET_HEADS = 8
RET_QK_DIM = 64
RET_V_DIM = 128
RET_CHUNK = 128
ROPE_BASE = 10000.0
QK_W = N_RET_HEADS * RET_QK_DIM
V_W = N_RET_HEADS * RET_V_DIM
POOL_WINDOWS = (2, 4, 8, 16)
N_POOL_GROUPS = 4
POOL_IN = 512
POOL_GROUP_IN = 128
POOL_GROUP_OUT = 256
POOL_BUF = 15
D_IN = 2 * QK_W + 2 * V_W + POOL_IN + 2 * D_MODEL
D_FF = 2816
EPS = 1e-6
PAST_LEN = 16384

Q_OFF = 0
K_OFF = QK_W
V_OFF = 2 * QK_W
G_OFF = 2 * QK_W + V_W
U_OFF = 2 * QK_W + 2 * V_W
GA_OFF = U_OFF + POOL_IN
GB_OFF = GA_OFF + D_MODEL

LANES = 128
N_PAIRS = N_RET_HEADS // 2
POOL_PAD = 16
VMEM_LIMIT = 60 * 1024 * 1024

F32 = jnp.float32
BF16 = jnp.bfloat16


def _dot(a, b):
    return jnp.dot(a, b, preferred_element_type=F32)


def _rms(x, g):
    return x * lax.rsqrt(jnp.mean(x * x, axis=-1, keepdims=True) + EPS) * g


def _const_spec(shape):
    nd = len(shape)
    return pl.BlockSpec(shape, lambda *_: (0,) * nd)


FF_CHUNKS = ((0, 1024), (1024, 1024), (2048, 768))


STAGE_BUFS = 6
STAGE_BYTES = 3 * 256 * 1024


def _stage_rows(shape):
    r, c = shape
    fits = [n for n in range(16, r + 1, 16) if r % n == 0 and n * c * 4 <= STAGE_BYTES]
    return max(fits)


def _stage_scratch(shape):
    return [pltpu.VMEM(shape, BF16), pltpu.VMEM((STAGE_BUFS, _stage_rows(shape), shape[1]), F32),
            pltpu.SemaphoreType.DMA((STAGE_BUFS,))]


def _fetch_cast(src, dst, stage, sems):
    nbuf, rows, _ = stage.shape
    n = src.shape[0] // rows

    def copy(k):
        return pltpu.make_async_copy(src.at[pl.ds(k * rows, rows), :], stage.at[k % nbuf], sems.at[k % nbuf])

    for k in range(min(nbuf, n)):
        copy(k).start()
    for k in range(n):
        copy(k).wait()
        dst[k * rows:(k + 1) * rows, :] = stage[k % nbuf].astype(BF16)
        if k + nbuf < n:
            copy(k + nbuf).start()


def _convert_weights(first, layer, srcs, scratch):
    @pl.when(first)
    def _():
        for k, src in enumerate(srcs):
            _fetch_cast(src.at[layer], *scratch[3 * k:3 * k + 3])

    return scratch[0::3]


def _emit_weights(first, last, srcs, outs, sems):
    copies = [pltpu.make_async_copy(s, o, sems.at[k]) for k, (s, o) in enumerate(zip(srcs, outs))]

    @pl.when(first)
    def _():
        for c in copies:
            c.start()

    @pl.when(last)
    def _():
        for c in copies:
            c.wait()


def _ffn_body(*refs, final, layer):
    n_in = 6 if final else 5
    x_ref, xs_ref, g_ref, win_hbm, wout_hbm = refs[:5]
    gf_ref = refs[5] if final else None
    o_ref, os_ref = refs[n_in:n_in + 2]
    g_ref = g_ref.at[pl.ds(layer, 1)]
    i = pl.program_id(0)
    win_ref, wout_ref = _convert_weights(i == 0, layer, (win_hbm, wout_hbm), refs[n_in + 2:])

    def ffn_rows(x):
        h = _rms(x, g_ref[...]).astype(BF16)
        acc = None
        for off, width in FF_CHUNKS:
            gate = _dot(h, win_ref[:, off:off + width])
            up = _dot(h, win_ref[:, D_FF + off:D_FF + off + width])
            a = (gate * jax.nn.sigmoid(gate) * up).astype(BF16)
            part = _dot(a, wout_ref[off:off + width, :])
            acc = part if acc is None else acc + part
        y = x + 0.5 * acc
        return _rms(y, gf_ref[...]) if final else y

    o_ref[...] = ffn_rows(x_ref[...])

    @pl.when(i == pl.num_programs(0) - 1)
    def _():
        os_ref[...] = ffn_rows(xs_ref[...])


def _resident_spec(shape):
    nd = len(shape)
    return pl.BlockSpec(tuple(shape), lambda *_: (0,) * nd, pipeline_mode=pl.Buffered(1))


ANY_SPEC = pl.BlockSpec(memory_space=pl.ANY)


def _ffn(x2d, xs2d, g, w_in, w_out, layer, g_final=None, *, tm):
    n = x2d.shape[0]
    assert n % tm == 0
    final = g_final is not None
    row_spec = pl.BlockSpec((tm, D_MODEL), lambda i: (i, 0))
    in_specs = [row_spec, _resident_spec(xs2d.shape), _const_spec(g.shape), ANY_SPEC, ANY_SPEC]
    args = [x2d, xs2d, g, w_in, w_out]
    if final:
        in_specs.append(_const_spec((1, D_MODEL)))
        args.append(g_final.reshape(1, D_MODEL))
    scratch = _stage_scratch(w_in.shape[1:]) + _stage_scratch(w_out.shape[1:])
    return pl.pallas_call(
        functools.partial(_ffn_body, final=final, layer=layer),
        grid=(n // tm,),
        in_specs=in_specs,
        out_specs=[row_spec, _resident_spec(xs2d.shape)],
        out_shape=[jax.ShapeDtypeStruct(x2d.shape, F32), jax.ShapeDtypeStruct(xs2d.shape, F32)],
        scratch_shapes=scratch,
        compiler_params=pltpu.CompilerParams(
            dimension_semantics=("arbitrary",), vmem_limit_bytes=VMEM_LIMIT),
        name="ffn_final" if final else "ffn",
    )(*args)


IN_CHUNKS = tuple((o, 1024) for o in range(0, 5120, 1024)) + ((5120, 512),)


def _project_in(x, g_ref, win_ref, z_ref):
    h = _rms(x, g_ref[...]).astype(BF16)
    for off, width in IN_CHUNKS:
        z_ref[:, off:off + width] = _dot(h, win_ref[:, off:off + width])


def _rotate_inplace(z_ref, cos, sin_signed, first_half):
    for blk in range(2 * QK_W // LANES):
        cols = slice(blk * LANES, (blk + 1) * LANES)
        xb = z_ref[:, cols]
        partner = jnp.where(first_half, pltpu.roll(xb, LANES - 32, 1), pltpu.roll(xb, 32, 1))
        r = xb * cos + partner * sin_signed
        if blk >= QK_W // LANES:
            r = r * (RET_QK_DIM ** -0.5)
        z_ref[:, cols] = r


def _merge_out(x, z_ref, ret, pool, wout_ref):
    ga = z_ref[:, GA_OFF:GA_OFF + D_MODEL]
    gb = z_ref[:, GB_OFF:GB_OFF + D_MODEL]
    merged = jax.nn.sigmoid(ga) * ret + jax.nn.sigmoid(gb) * pool
    return x + _dot(merged.astype(BF16), wout_ref[...])


def _swish_gate_norm(o, gate):
    o = o * lax.rsqrt(jnp.mean(o * o, axis=-1, keepdims=True) + EPS)
    return o * (gate * jax.nn.sigmoid(gate))


def _mixer_prompt_body(*refs, tm, chunk, n_alias, layer):
    (x_ref, g_ref, win_ref, wpool_ref, pscale_ref, wout_ref, cos_ref, sin_ref,
     dmask_ref, xi_ref, wk_ref, gs_ref) = refs[:12]
    outs = refs[12 + n_alias:]
    o_ref, sret_ref, spool_ref = outs[:3]
    z_ref, s_ref, ext_ref, ret_ref, pool_ref, h_ref, d_ref = outs[6:13]
    g_ref, pscale_ref = g_ref.at[pl.ds(layer, 1)], pscale_ref.at[pl.ds(layer, 1)]
    t = pl.program_id(1)
    nt = pl.num_programs(1)
    bi = pl.program_id(0)
    first = (bi == 0) & (t == 0)
    win_ref, wpool_ref, wout_ref = _convert_weights(first, layer, (win_ref, wpool_ref, wout_ref), outs[13:22])
    _emit_weights(first, (bi == pl.num_programs(0) - 1) & (t == nt - 1),
                  (win_ref, wpool_ref, wout_ref), outs[3:6], outs[22])

    @pl.when(t == 0)
    def _():
        s_ref[...] = jnp.zeros_like(s_ref)
        ext_ref[0:POOL_PAD, :] = jnp.zeros((POOL_PAD, POOL_IN), F32)

    _mixer_prompt_tile((x_ref, g_ref, win_ref, wpool_ref, pscale_ref, wout_ref, cos_ref, sin_ref, dmask_ref, xi_ref,
                        wk_ref, gs_ref, o_ref, z_ref, s_ref, ext_ref, ret_ref, pool_ref, h_ref, d_ref),
                       t * tm, tm, chunk)

    @pl.when(t == nt - 1)
    def _():
        sret_ref[...] = s_ref[...].reshape(N_PAIRS * LANES, RET_V_DIM)
        spool_ref[...] = ext_ref[POOL_PAD - POOL_BUF:POOL_PAD, :]


def _mixer_prompt_tile(tile_refs, pos0, tm, chunk):
    (x_ref, g_ref, win_ref, wpool_ref, pscale_ref, wout_ref, cos_ref, sin_ref, dmask_ref, xi_ref,
     wk_ref, gs_ref, o_ref, z_ref, s_ref, ext_ref, ret_ref, pool_ref, h_ref, d_ref) = tile_refs
    h_ref[...] = _rms(x_ref[...], g_ref[...]).astype(BF16)

    def project(off, width, act=None):
        y = _dot(h_ref[...], win_ref[:, off:off + width])
        z_ref[:, off:off + width] = y if act is None else act(y)

    project(Q_OFF, 2 * QK_W)
    project(V_OFF, V_W)
    project(U_OFF, POOL_IN)

    lane = lax.broadcasted_iota(jnp.int32, (1, LANES), 1)
    first_half = (lane % RET_QK_DIM) < (RET_QK_DIM // 2)
    _rotate_inplace(z_ref, cos_ref[...], sin_ref[...], first_half)

    ext_ref[POOL_PAD:POOL_PAD + tm, :] = z_ref[:, U_OFF:U_OFF + POOL_IN]
    pos = pos0 + lax.broadcasted_iota(jnp.int32, (tm, 1), 0)
    for g, w in enumerate(POOL_WINDOWS):
        cols = slice(g * POOL_GROUP_IN, (g + 1) * POOL_GROUP_IN)
        u = ext_ref[POOL_PAD:POOL_PAD + tm, cols]
        tot = u
        for j in range(1, w):
            tot = tot + ext_ref[POOL_PAD - j:POOL_PAD - j + tm, cols]
        cnt = jnp.minimum(pos + 1, w).astype(F32)
        d_ref[:, cols] = (tot / cnt - u).astype(BF16)
    tail = ext_ref[tm:tm + POOL_PAD, :]
    ext_ref[0:POOL_PAD, :] = tail

    def pool_matmul(g):
        ocols = slice(g * POOL_GROUP_OUT, (g + 1) * POOL_GROUP_OUT)
        y = _dot(d_ref[:, g * POOL_GROUP_IN:(g + 1) * POOL_GROUP_IN],
                 wpool_ref[g * POOL_GROUP_IN:(g + 1) * POOL_GROUP_IN, :])
        pool_ref[:, ocols] = y * pscale_ref[:, ocols]

    fill_width = 2 * LANES
    fillers = [functools.partial(project, off, fill_width, jax.nn.silu)
               for off in range(G_OFF, G_OFF + V_W, fill_width)]
    fillers += [functools.partial(project, off, fill_width, jax.nn.sigmoid)
                for off in range(GA_OFF, D_IN, fill_width)]
    fillers += [functools.partial(pool_matmul, g) for g in range(N_POOL_GROUPS)]

    head0 = (lane < RET_QK_DIM).astype(F32)
    head1 = 1.0 - head0

    n_chunks = tm // chunk
    stages = 3 * n_chunks

    def run_fillers(stage):
        for f in fillers[stage * len(fillers) // stages:(stage + 1) * len(fillers) // stages]:
            f()

    def pair_cols(off, pair, width):
        return slice(off + pair * width, off + (pair + 1) * width)

    for c in range(n_chunks):
        rows = slice(c * chunk, (c + 1) * chunk)
        scores = []
        for pair in range(N_PAIRS):
            qb = z_ref[rows, pair_cols(Q_OFF, pair, LANES)]
            kb = z_ref[rows, pair_cols(K_OFF, pair, LANES)]
            k_stack = jnp.concatenate([kb * head0, kb * head1], axis=0).astype(BF16)
            scores.append(lax.dot_general(qb.astype(BF16), k_stack, (((1,), (1,)), ((), ())),
                                          preferred_element_type=F32))
        run_fillers(3 * c)
        for pair in range(N_PAIRS):
            qx = z_ref[rows, pair_cols(Q_OFF, pair, LANES)] * xi_ref[pair]
            vb16 = z_ref[rows, pair_cols(V_OFF, pair, 2 * RET_V_DIM)].astype(BF16)
            s16 = s_ref[pair].astype(BF16)
            for j, hmask in enumerate((head0, head1)):
                h = 2 * pair + j
                scj = scores[pair][:, j * chunk:(j + 1) * chunk] * dmask_ref[h]
                lhs = jnp.concatenate([scj, qx * hmask], axis=1).astype(BF16)
                rhs = jnp.concatenate([vb16[:, j * RET_V_DIM:(j + 1) * RET_V_DIM], s16], axis=0)
                o = _dot(lhs, rhs)
                o = o * lax.rsqrt(jnp.mean(o * o, axis=-1, keepdims=True) + EPS)
                ret_ref[rows, h * RET_V_DIM:(h + 1) * RET_V_DIM] = o
        run_fillers(3 * c + 1)
        for pair in range(N_PAIRS):
            kw = (z_ref[rows, pair_cols(K_OFF, pair, LANES)] * wk_ref[pair]).astype(BF16)
            vb16 = z_ref[rows, pair_cols(V_OFF, pair, 2 * RET_V_DIM)].astype(BF16)
            upd = lax.dot_general(kw, vb16, (((0,), (0,)), ((), ())),
                                  preferred_element_type=F32)
            new_s = jnp.concatenate([upd[0:RET_QK_DIM, 0:RET_V_DIM],
                                     upd[RET_QK_DIM:, RET_V_DIM:]], axis=0)
            s_ref[pair] = s_ref[pair] * gs_ref[pair] + new_s
        run_fillers(3 * c + 2)

    n_halves = 2
    for r in range(n_halves):
        rows = slice(r * tm // n_halves, (r + 1) * tm // n_halves)
        ret = ret_ref[rows, :] * z_ref[rows, G_OFF:G_OFF + V_W]
        merged = (z_ref[rows, GA_OFF:GA_OFF + D_MODEL] * ret
                  + z_ref[rows, GB_OFF:GB_OFF + D_MODEL] * pool_ref[rows, :])
        o_ref[rows, :] = x_ref[rows, :] + _dot(merged.astype(BF16), wout_ref[...])


def _state_alias(prev, n_in, first_out):
    if prev is None:
        return [], [], {}
    specs = [pl.BlockSpec(memory_space=pl.ANY) for _ in prev]
    aliases = {n_in + k: first_out + k for k in range(len(prev))}
    return list(prev), specs, aliases


def _mixer_prompt(x, g, w_in, w_pool, pool_scale, w_out, tabs, layer, prev_states, *, tm):
    b, t_len, _ = x.shape
    depth = w_in.shape[0]
    chunk = RET_CHUNK
    assert t_len % tm == 0 and tm % chunk == 0
    cos, sin, dmask, xi, wk, gs = tabs
    row_spec = pl.BlockSpec((None, tm, D_MODEL), lambda i, j: (i, j, 0))
    tab_spec = pl.BlockSpec((tm, LANES), lambda i, j: (j, 0))
    in_specs = [
        row_spec, _const_spec(g.shape), ANY_SPEC, ANY_SPEC, _const_spec(pool_scale.shape),
        ANY_SPEC, tab_spec, tab_spec,
        _const_spec(dmask.shape), _const_spec(xi.shape), _const_spec(wk.shape), _const_spec(gs.shape),
    ]
    args = [x, g, w_in, w_pool, pool_scale, w_out, cos, sin, dmask, xi, wk, gs]
    extra, extra_specs, aliases = _state_alias(prev_states, len(args), 1)
    w_shapes = [w_in.shape[1:], w_pool.shape[1:], w_out.shape[1:]]
    out_specs = [
        row_spec,
        pl.BlockSpec((None, None, N_PAIRS * LANES, RET_V_DIM), lambda i, j: (layer, i, 0, 0)),
        pl.BlockSpec((None, None, POOL_BUF, POOL_IN), lambda i, j: (layer, i, 0, 0)),
    ] + [ANY_SPEC] * len(w_shapes)
    out_shape = [
        jax.ShapeDtypeStruct((b, t_len, D_MODEL), F32),
        jax.ShapeDtypeStruct((depth, b, N_PAIRS * LANES, RET_V_DIM), F32),
        jax.ShapeDtypeStruct((depth, b, POOL_BUF, POOL_IN), F32),
    ] + [jax.ShapeDtypeStruct(s, BF16) for s in w_shapes]
    scratch = [
        pltpu.VMEM((tm, D_IN), F32),
        pltpu.VMEM((N_PAIRS, LANES, RET_V_DIM), F32),
        pltpu.VMEM((POOL_PAD + tm, POOL_IN), F32),
        pltpu.VMEM((tm, V_W), F32),
        pltpu.VMEM((tm, D_MODEL), F32),
        pltpu.VMEM((tm, D_MODEL), BF16),
        pltpu.VMEM((tm, POOL_IN), BF16),
    ]
    for s in w_shapes:
        scratch += _stage_scratch(s)
    scratch.append(pltpu.SemaphoreType.DMA((len(w_shapes),)))
    y, sret, spool, *w16 = pl.pallas_call(
        functools.partial(_mixer_prompt_body, tm=tm, chunk=chunk, n_alias=len(extra), layer=layer),
        grid=(b, t_len // tm),
        in_specs=in_specs + extra_specs,
        out_specs=out_specs,
        out_shape=out_shape,
        scratch_shapes=scratch,
        input_output_aliases=aliases,
        compiler_params=pltpu.CompilerParams(
            dimension_semantics=("arbitrary", "arbitrary"), vmem_limit_bytes=VMEM_LIMIT),
        name="mixer_prompt",
    )(*args, *extra)
    return y, (sret, spool), tuple(w16)


def _log_gamma():
    return np.log1p(-np.exp2(-5.0 - np.arange(N_RET_HEADS, dtype=np.float64)))


def _rope_tables(pos):
    half = RET_QK_DIM // 2
    inv = np.power(ROPE_BASE, -np.linspace(0.0, 1.0, half))
    ang = pos[:, None] * inv[None, :]
    cos, sin = np.cos(ang), np.sin(ang)
    cos128 = np.tile(cos, (1, LANES // half))
    sin128 = np.tile(np.concatenate([-sin, sin], axis=1), (1, LANES // RET_QK_DIM))
    return cos128.astype(np.float32), sin128.astype(np.float32)


def _pair_lanes(per_head):
    h, c = per_head.shape
    x = np.broadcast_to(per_head[:, :, None], (h, c, RET_QK_DIM))
    x = x.reshape(N_PAIRS, 2, c, RET_QK_DIM).transpose(0, 2, 1, 3).reshape(N_PAIRS, c, LANES)
    return x.astype(np.float32)


def _state_decay(chunk):
    g_c = np.exp(_log_gamma() * chunk)
    gs = np.broadcast_to(g_c[:, None, None], (N_RET_HEADS, RET_QK_DIM, RET_V_DIM))
    return gs.reshape(N_PAIRS, LANES, RET_V_DIM).astype(np.float32)


def _decay_tables(chunk):
    lg = _log_gamma()
    i = np.arange(chunk, dtype=np.float64)
    dist = i[:, None] - i[None, :]
    dmask = np.where(dist[None] >= 0, np.exp(lg[:, None, None] * np.maximum(dist, 0.0)[None]), 0.0)
    xi = np.exp(lg[:, None] * (i[None, :] + 1.0))
    wk = np.exp(lg[:, None] * (chunk - 1.0 - i[None, :]))
    return dmask.astype(np.float32), _pair_lanes(xi), _pair_lanes(wk), _state_decay(chunk)


def _mixer_sample_body(*refs, bb, ts, past, n_alias, layer):
    (x_ref, g_ref, win_ref, wpool_ref, pscale_ref, wout_ref, cos_ref, sin_ref,
     xi_ref, wk_ref, lag_ref, gs_ref, ind_ref, s0_ref, pb_ref) = refs[:15]
    (o_ref, snew_ref, pnew_ref, z_ref, inter_ref, ublk_ref, pblk_ref) = refs[15 + n_alias:]
    g_ref, pscale_ref = g_ref.at[pl.ds(layer, 1)], pscale_ref.at[pl.ds(layer, 1)]
    rows = bb * ts
    x = x_ref[...]
    _project_in(x, g_ref, win_ref, z_ref)
    lane = lax.broadcasted_iota(jnp.int32, (1, LANES), 1)
    first_half = (lane % RET_QK_DIM) < (RET_QK_DIM // 2)
    _rotate_inplace(z_ref, cos_ref[...], sin_ref[...], first_half)

    q = z_ref[:, Q_OFF:Q_OFF + QK_W]
    k = z_ref[:, K_OFF:K_OFF + QK_W]
    v = z_ref[:, V_OFF:V_OFF + V_W]
    step = lax.broadcasted_iota(jnp.int32, (rows, 1), 0) % ts
    intra = None
    for d in range(ts):
        kd = k if d == 0 else pltpu.roll(k, d, 0)
        vd = v if d == 0 else pltpu.roll(v, d, 0)
        prod = jnp.where(step >= d, q * kd, 0.0).astype(BF16)
        score = _dot(prod, ind_ref[...])
        term = score * lag_ref[d:d + 1, :] * jnp.where(step >= d, vd, 0.0)
        intra = term if intra is None else intra + term

    row8 = lax.broadcasted_iota(jnp.int32, (2 * ts, 1), 0)
    top = (lax.broadcasted_iota(jnp.int32, (LANES, 1), 0) < RET_QK_DIM).astype(F32)
    bot = 1.0 - top

    def seq_pair(p, carry):
        r0 = pl.multiple_of(p * 2 * ts, 2 * ts)
        rsl = pl.ds(r0, 2 * ts)
        for pair in range(N_PAIRS):
            lanes = slice(pair * LANES, (pair + 1) * LANES)
            q8 = (z_ref[rsl, Q_OFF + pair * LANES:Q_OFF + (pair + 1) * LANES] * xi_ref[:, lanes]).astype(BF16)
            kw8 = z_ref[rsl, K_OFF + pair * LANES:K_OFF + (pair + 1) * LANES] * wk_ref[:, lanes]
            v8 = z_ref[rsl, V_OFF + pair * 2 * RET_V_DIM:V_OFF + (pair + 1) * 2 * RET_V_DIM].astype(BF16)
            outs = []
            for j in range(2):
                b = 2 * p + j
                s_pair = s0_ref[b, lanes, :]
                bd = jnp.concatenate([s_pair * top, s_pair * bot], axis=1).astype(BF16)
                outs.append(_dot(q8, bd))
                mine = (row8 >= j * ts) & (row8 < (j + 1) * ts)
                kwj = jnp.where(mine, kw8, 0.0).astype(BF16)
                upd = lax.dot_general(kwj, v8, (((0,), (0,)), ((), ())), preferred_element_type=F32)
                new_s = jnp.concatenate([upd[0:RET_QK_DIM, 0:RET_V_DIM],
                                         upd[RET_QK_DIM:, RET_V_DIM:]], axis=0)
                snew_ref[b, lanes, :] = s_pair * gs_ref[pair] + new_s
            inter_ref[rsl, pair * 2 * RET_V_DIM:(pair + 1) * 2 * RET_V_DIM] = jnp.where(
                row8 < ts, outs[0], outs[1])
        return carry

    lax.fori_loop(0, bb // 2, seq_pair, 0, unroll=8)

    keep = POOL_BUF - ts
    pnew_ref[0:keep] = pb_ref[ts:]
    for g, w in enumerate(POOL_WINDOWS):
        ublk_ref[g] = z_ref[:, U_OFF + g * POOL_GROUP_IN:U_OFF + (g + 1) * POOL_GROUP_IN]
        u_steps = [ublk_ref[g, pl.ds(t, bb, stride=ts), :] for t in range(ts)]
        for t in range(ts):
            pnew_ref[keep + t, :, g * POOL_GROUP_IN:(g + 1) * POOL_GROUP_IN] = u_steps[t]
        ds_ = []
        for t in range(ts):
            tot = None
            for j in range(w):
                idx = POOL_BUF + t - j
                if idx >= POOL_BUF:
                    term = u_steps[idx - POOL_BUF]
                else:
                    term = pb_ref[idx, :, g * POOL_GROUP_IN:(g + 1) * POOL_GROUP_IN]
                tot = term if tot is None else tot + term
            cnt = float(min(past + t + 1, w))
            ds_.append(tot / cnt - u_steps[t])
        y = _dot(jnp.concatenate(ds_, axis=0).astype(BF16),
                 wpool_ref[g * POOL_GROUP_IN:(g + 1) * POOL_GROUP_IN, :])
        y = y * pscale_ref[:, g * POOL_GROUP_OUT:(g + 1) * POOL_GROUP_OUT]
        for half in range(POOL_GROUP_OUT // LANES):
            blk = g * (POOL_GROUP_OUT // LANES) + half
            for t in range(ts):
                pblk_ref[blk, pl.ds(t, bb, stride=ts), :] = y[t * bb:(t + 1) * bb, half * LANES:(half + 1) * LANES]
    pool = jnp.concatenate([pblk_ref[blk] for blk in range(D_MODEL // LANES)], axis=1)

    o_all = intra + inter_ref[...]
    rets = []
    for h in range(N_RET_HEADS):
        hc = slice(h * RET_V_DIM, (h + 1) * RET_V_DIM)
        rets.append(_swish_gate_norm(o_all[:, hc], z_ref[:, G_OFF + h * RET_V_DIM:G_OFF + (h + 1) * RET_V_DIM]))
    ret = jnp.concatenate(rets, axis=1)
    o_ref[...] = _merge_out(x, z_ref, ret, pool, wout_ref)


def _sample_tables(ts, past, rows):
    lg = _log_gamma()
    i = np.arange(ts, dtype=np.float64)
    lag = np.exp(lg[:, None] * i[None, :])
    lag = np.repeat(lag.T, RET_V_DIM, axis=1)
    xi = np.exp(lg[:, None] * (i[None, :] + 1.0))
    wk = np.exp(lg[:, None] * (ts - 1.0 - i[None, :]))
    expand = lambda a: np.tile(np.repeat(a.T, RET_QK_DIM, axis=1), (2, 1)).astype(np.float32)
    cos, sin = _rope_tables(past + np.arange(ts, dtype=np.float64))
    reps = rows // ts
    head_of_k = np.arange(QK_W) // RET_QK_DIM
    head_of_v = np.arange(V_W) // RET_V_DIM
    ind = (head_of_k[:, None] == head_of_v[None, :]).astype(BF16)
    return (np.tile(cos, (reps, 1)), np.tile(sin, (reps, 1)), expand(xi), expand(wk), lag.astype(np.float32),
            _state_decay(ts), ind)


def _mixer_sample(x2d, s0, pbuf, g, w_in, w_pool, pool_scale, w_out, tabs, layer, prev_states, *, bb, ts, past):
    n = x2d.shape[0]
    depth = s0.shape[0]
    nb = n // ts
    rows = bb * ts
    assert nb % bb == 0 and bb % 2 == 0 and (2 * ts) % 8 == 0 and ts <= POOL_BUF
    cos, sin, xi, wk, lag, gs, ind = tabs
    row_spec = pl.BlockSpec((rows, D_MODEL), lambda i: (i, 0))
    state_spec = pl.BlockSpec((None, bb, N_PAIRS * LANES, RET_V_DIM), lambda i: (layer, i, 0, 0))
    pool_spec = pl.BlockSpec((None, POOL_BUF, bb, POOL_IN), lambda i: (layer, 0, i, 0))
    in_specs = [
        row_spec, _const_spec(g.shape), _resident_spec(w_in.shape),
        _resident_spec(w_pool.shape), _const_spec(pool_scale.shape),
        _resident_spec(w_out.shape), _const_spec(cos.shape), _const_spec(sin.shape),
        _const_spec(xi.shape), _const_spec(wk.shape), _const_spec(lag.shape), _const_spec(gs.shape),
        _const_spec(ind.shape), state_spec, pool_spec,
    ]
    args = [x2d, g, w_in, w_pool, pool_scale, w_out, cos, sin, xi, wk, lag, gs, ind, s0, pbuf]
    extra, extra_specs, aliases = _state_alias(prev_states, len(args), 1)
    out_specs = [row_spec, state_spec, pool_spec]
    out_shape = [
        jax.ShapeDtypeStruct((n, D_MODEL), F32),
        jax.ShapeDtypeStruct((depth, nb, N_PAIRS * LANES, RET_V_DIM), F32),
        jax.ShapeDtypeStruct((depth, POOL_BUF, nb, POOL_IN), F32),
    ]
    scratch = [
        pltpu.VMEM((rows, D_IN), F32),
        pltpu.VMEM((rows, V_W), F32),
        pltpu.VMEM((N_POOL_GROUPS, rows, LANES), F32),
        pltpu.VMEM((D_MODEL // LANES, rows, LANES), F32),
    ]
    y, snew, pnew = pl.pallas_call(
        functools.partial(_mixer_sample_body, bb=bb, ts=ts, past=past, n_alias=len(extra), layer=layer),
        grid=(nb // bb,),
        in_specs=in_specs + extra_specs,
        out_specs=out_specs,
        out_shape=out_shape,
        scratch_shapes=scratch,
        input_output_aliases=aliases,
        compiler_params=pltpu.CompilerParams(
            dimension_semantics=("arbitrary",), vmem_limit_bytes=VMEM_LIMIT),
        name="mixer_sample",
    )(*args, *extra)
    return y, (snew, pnew)


def kernel(x_prompt, x_sample, state_ret, state_pool, norm_ffn1, w_ffn1_in, w_ffn1_out, norm_mix,
           w_in, w_pool, pool_scale, w_out, norm_ffn2, w_ffn2_in, w_ffn2_out, norm_final):
    depth = w_in.shape[0]
    bp, tp, _ = x_prompt.shape
    bs, ts, _ = x_sample.shape
    wp = w_pool.reshape(depth, N_POOL_GROUPS * POOL_GROUP_IN, POOL_GROUP_OUT)

    n1, nm, n2, ps = norm_ffn1, norm_mix, norm_ffn2, pool_scale
    gf = norm_final.reshape(1, D_MODEL)
    last = depth - 1

    tm = 512
    tm_ffn = 1024
    tabs_p = _rope_tables(np.arange(tp, dtype=np.float64)) + _decay_tables(RET_CHUNK)

    bb = 32
    tabs_s = _sample_tables(ts, PAST_LEN, bb * ts)
    s0_all = state_ret.reshape(depth, bs, N_PAIRS * LANES, RET_V_DIM)
    pb_all = jnp.swapaxes(state_pool, 1, 2)

    x = x_prompt.reshape(bp * tp, D_MODEL)
    xs = x_sample.reshape(bs * ts, D_MODEL)
    states_p = states_s = None
    for l in range(depth):
        x, xs = _ffn(x, xs, n1, w_ffn1_in, w_ffn1_out, l, tm=tm_ffn)
        x, states_p, (wi16, wp16, wo16) = _mixer_prompt(x.reshape(bp, tp, D_MODEL), nm, w_in, wp, ps, w_out,
                                                        tabs_p, l, states_p, tm=tm)
        xs, states_s = _mixer_sample(xs, s0_all, pb_all, nm, wi16, wp16, ps, wo16, tabs_s, l, states_s,
                                     bb=bb, ts=ts, past=PAST_LEN)
        x, xs = _ffn(x.reshape(bp * tp, D_MODEL), xs, n2, w_ffn2_in, w_ffn2_out, l,
                     gf if l == last else None, tm=tm_ffn)
    y_prompt = x.reshape(bp, tp, D_MODEL)
    y_sample = xs.reshape(bs, ts, D_MODEL)
    state_ret_prompt = states_p[0].reshape(depth, bp, N_RET_HEADS, RET_QK_DIM, RET_V_DIM)
    state_pool_prompt = states_p[1]
    state_ret_sample = states_s[0].reshape(depth, bs, N_RET_HEADS, RET_QK_DIM, RET_V_DIM)
    state_pool_sample = jnp.swapaxes(states_s[1], 1, 2)
    return (y_prompt, y_sample, state_ret_prompt, state_ret_sample, state_pool_prompt, state_pool_sample)
```

```python
import functools

import numpy as np
import jax
import jax.numpy as jnp
from jax import lax
from jax.experimental import pallas as pl
from jax.experimental.pallas import tpu as pltpu

D_MODEL = 1024
N_RET_HEADS = 8
RET_QK_DIM = 64
RET_V_DIM = 128
RET_CHUNK = 128
ROPE_BASE = 10000.0
QK_W = N_RET_HEADS * RET_QK_DIM
V_W = N_RET_HEADS * RET_V_DIM
POOL_WINDOWS = (2, 4, 8, 16)
N_POOL_GROUPS = 4
POOL_IN = 512
POOL_GROUP_IN = 128
POOL_GROUP_OUT = 256
POOL_BUF = 15
D_IN = 2 * QK_W + 2 * V_W + POOL_IN + 2 * D_MODEL
D_FF = 2816
EPS = 1e-6
PAST_LEN = 16384

Q_OFF = 0
K_OFF = QK_W
V_OFF = 2 * QK_W
G_OFF = 2 * QK_W + V_W
U_OFF = 2 * QK_W + 2 * V_W
GA_OFF = U_OFF + POOL_IN
GB_OFF = GA_OFF + D_MODEL

LANES = 128
N_PAIRS = N_RET_HEADS // 2
POOL_PAD = 16
VMEM_LIMIT = 60 * 1024 * 1024

F32 = jnp.float32
BF16 = jnp.bfloat16


def _dot(a, b):
    return jnp.dot(a, b, preferred_element_type=F32)


def _rms(x, g):
    return x * lax.rsqrt(jnp.mean(x * x, axis=-1, keepdims=True) + EPS) * g


def _const_spec(shape):
    nd = len(shape)
    return pl.BlockSpec(shape, lambda *_: (0,) * nd)


FF_CHUNKS = ((0, 1024), (1024, 1024), (2048, 768))


STAGE_BUFS = 6
STAGE_BYTES = 3 * 256 * 1024


def _stage_rows(shape):
    r, c = shape
    fits = [n for n in range(16, r + 1, 16) if r % n == 0 and n * c * 4 <= STAGE_BYTES]
    return max(fits)


def _stage_scratch(shape):
    return [pltpu.VMEM(shape, BF16), pltpu.VMEM((STAGE_BUFS, _stage_rows(shape), shape[1]), F32),
            pltpu.SemaphoreType.DMA((STAGE_BUFS,))]


def _fetch_cast(src, dst, stage, sems):
    nbuf, rows, _ = stage.shape
    n = src.shape[0] // rows

    def copy(k):
        return pltpu.make_async_copy(src.at[pl.ds(k * rows, rows), :], stage.at[k % nbuf], sems.at[k % nbuf])

    for k in range(min(nbuf, n)):
        copy(k).start()
    for k in range(n):
        copy(k).wait()
        dst[k * rows:(k + 1) * rows, :] = stage[k % nbuf].astype(BF16)
        if k + nbuf < n:
            copy(k + nbuf).start()


def _convert_weights(first, layer, srcs, scratch):
    @pl.when(first)
    def _():
        for k, src in enumerate(srcs):
            _fetch_cast(src.at[layer], *scratch[3 * k:3 * k + 3])

    return scratch[0::3]


def _emit_weights(first, last, srcs, outs, sems):
    copies = [pltpu.make_async_copy(s, o, sems.at[k]) for k, (s, o) in enumerate(zip(srcs, outs))]

    @pl.when(first)
    def _():
        for c in copies:
            c.start()

    @pl.when(last)
    def _():
        for c in copies:
            c.wait()


def _ffn_body(layer_ref, *refs, final):
    n_in = 6 if final else 5
    x_ref, xs_ref, g_ref, win_hbm, wout_hbm = refs[:5]
    gf_ref = refs[5] if final else None
    o_ref, os_ref = refs[n_in:n_in + 2]
    layer = layer_ref[0]
    g_ref = g_ref.at[pl.ds(layer, 1)]
    i = pl.program_id(0)
    win_ref, wout_ref = _convert_weights(i == 0, layer, (win_hbm, wout_hbm), refs[n_in + 2:])

    def ffn_rows(x):
        h = _rms(x, g_ref[...]).astype(BF16)
        acc = None
        for off, width in FF_CHUNKS:
            gate = _dot(h, win_ref[:, off:off + width])
            up = _dot(h, win_ref[:, D_FF + off:D_FF + off + width])
            a = (gate * jax.nn.sigmoid(gate) * up).astype(BF16)
            part = _dot(a, wout_ref[off:off + width, :])
            acc = part if acc is None else acc + part
        y = x + 0.5 * acc
        return _rms(y, gf_ref[...]) if final else y

    o_ref[...] = ffn_rows(x_ref[...])

    @pl.when(i == pl.num_programs(0) - 1)
    def _():
        os_ref[...] = ffn_rows(xs_ref[...])


def _resident_spec(shape):
    nd = len(shape)
    return pl.BlockSpec(tuple(shape), lambda *_: (0,) * nd, pipeline_mode=pl.Buffered(1))


ANY_SPEC = pl.BlockSpec(memory_space=pl.ANY)


def _ffn(x2d, xs2d, g, w_in, w_out, layer, g_final=None, *, tm):
    n = x2d.shape[0]
    assert n % tm == 0
    final = g_final is not None
    row_spec = pl.BlockSpec((tm, D_MODEL), lambda i, *_: (i, 0))
    in_specs = [row_spec, _resident_spec(xs2d.shape), _const_spec(g.shape), ANY_SPEC, ANY_SPEC]
    args = [x2d, xs2d, g, w_in, w_out]
    if final:
        in_specs.append(_const_spec((1, D_MODEL)))
        args.append(g_final.reshape(1, D_MODEL))
    scratch = _stage_scratch(w_in.shape[1:]) + _stage_scratch(w_out.shape[1:])
    return pl.pallas_call(
        functools.partial(_ffn_body, final=final),
        grid_spec=pltpu.PrefetchScalarGridSpec(
            num_scalar_prefetch=1,
            grid=(n // tm,),
            in_specs=in_specs,
            out_specs=[row_spec, _resident_spec(xs2d.shape)],
            scratch_shapes=scratch),
        out_shape=[jax.ShapeDtypeStruct(x2d.shape, F32), jax.ShapeDtypeStruct(xs2d.shape, F32)],
        compiler_params=pltpu.CompilerParams(
            dimension_semantics=("arbitrary",), vmem_limit_bytes=VMEM_LIMIT),
        name="ffn_final" if final else "ffn",
    )(jnp.full((1,), layer, jnp.int32), *args)


IN_CHUNKS = tuple((o, 1024) for o in range(0, 5120, 1024)) + ((5120, 512),)


def _project_in(x, g_ref, win_ref, z_ref):
    h = _rms(x, g_ref[...]).astype(BF16)
    for off, width in IN_CHUNKS:
        z_ref[:, off:off + width] = _dot(h, win_ref[:, off:off + width])


def _rotate_inplace(z_ref, cos, sin_signed, first_half):
    for blk in range(2 * QK_W // LANES):
        cols = slice(blk * LANES, (blk + 1) * LANES)
        xb = z_ref[:, cols]
        partner = jnp.where(first_half, pltpu.roll(xb, LANES - 32, 1), pltpu.roll(xb, 32, 1))
        r = xb * cos + partner * sin_signed
        if blk >= QK_W // LANES:
            r = r * (RET_QK_DIM ** -0.5)
        z_ref[:, cols] = r


def _merge_out(x, z_ref, ret, pool, wout_ref):
    ga = z_ref[:, GA_OFF:GA_OFF + D_MODEL]
    gb = z_ref[:, GB_OFF:GB_OFF + D_MODEL]
    merged = jax.nn.sigmoid(ga) * ret + jax.nn.sigmoid(gb) * pool
    return x + _dot(merged.astype(BF16), wout_ref[...])


def _swish_gate_norm(o, gate):
    o = o * lax.rsqrt(jnp.mean(o * o, axis=-1, keepdims=True) + EPS)
    return o * (gate * jax.nn.sigmoid(gate))


def _mixer_prompt_body(*refs, tm, chunk, n_alias, layer):
    (x_ref, g_ref, win_ref, wpool_ref, pscale_ref, wout_ref, cos_ref, sin_ref,
     dmask_ref, xi_ref, wk_ref, gs_ref) = refs[:12]
    outs = refs[12 + n_alias:]
    o_ref, sret_ref, spool_ref = outs[:3]
    z_ref, s_ref, ext_ref, ret_ref, pool_ref, h_ref, d_ref = outs[6:13]
    g_ref, pscale_ref = g_ref.at[pl.ds(layer, 1)], pscale_ref.at[pl.ds(layer, 1)]
    t = pl.program_id(1)
    nt = pl.num_programs(1)
    bi = pl.program_id(0)
    first = (bi == 0) & (t == 0)
    win_ref, wpool_ref, wout_ref = _convert_weights(first, layer, (win_ref, wpool_ref, wout_ref), outs[13:22])
    _emit_weights(first, (bi == pl.num_programs(0) - 1) & (t == nt - 1),
                  (win_ref, wpool_ref, wout_ref), outs[3:6], outs[22])

    @pl.when(t == 0)
    def _():
        s_ref[...] = jnp.zeros_like(s_ref)
        ext_ref[0:POOL_PAD, :] = jnp.zeros((POOL_PAD, POOL_IN), F32)

    _mixer_prompt_tile((x_ref, g_ref, win_ref, wpool_ref, pscale_ref, wout_ref, cos_ref, sin_ref, dmask_ref, xi_ref,
                        wk_ref, gs_ref, o_ref, z_ref, s_ref, ext_ref, ret_ref, pool_ref, h_ref, d_ref),
                       t * tm, tm, chunk)

    @pl.when(t == nt - 1)
    def _():
        sret_ref[...] = s_ref[...].reshape(N_PAIRS * LANES, RET_V_DIM)
        spool_ref[...] = ext_ref[POOL_PAD - POOL_BUF:POOL_PAD, :]


def _mixer_prompt_tile(tile_refs, pos0, tm, chunk):
    (x_ref, g_ref, win_ref, wpool_ref, pscale_ref, wout_ref, cos_ref, sin_ref, dmask_ref, xi_ref,
     wk_ref, gs_ref, o_ref, z_ref, s_ref, ext_ref, ret_ref, pool_ref, h_ref, d_ref) = tile_refs
    h_ref[...] = _rms(x_ref[...], g_ref[...]).astype(BF16)

    def project(off, width, act=None):
        y = _dot(h_ref[...], win_ref[:, off:off + width])
        z_ref[:, off:off + width] = y if act is None else act(y)

    project(Q_OFF, 2 * QK_W)
    project(V_OFF, V_W)
    project(U_OFF, POOL_IN)

    lane = lax.broadcasted_iota(jnp.int32, (1, LANES), 1)
    first_half = (lane % RET_QK_DIM) < (RET_QK_DIM // 2)
    _rotate_inplace(z_ref, cos_ref[...], sin_ref[...], first_half)

    ext_ref[POOL_PAD:POOL_PAD + tm, :] = z_ref[:, U_OFF:U_OFF + POOL_IN]
    pos = pos0 + lax.broadcasted_iota(jnp.int32, (tm, 1), 0)
    for g, w in enumerate(POOL_WINDOWS):
        cols = slice(g * POOL_GROUP_IN, (g + 1) * POOL_GROUP_IN)
        u = ext_ref[POOL_PAD:POOL_PAD + tm, cols]
        tot = u
        for j in range(1, w):
            tot = tot + ext_ref[POOL_PAD - j:POOL_PAD - j + tm, cols]
        cnt = jnp.minimum(pos + 1, w).astype(F32)
        d_ref[:, cols] = (tot / cnt - u).astype(BF16)
    tail = ext_ref[tm:tm + POOL_PAD, :]
    ext_ref[0:POOL_PAD, :] = tail

    def pool_matmul(g):
        ocols = slice(g * POOL_GROUP_OUT, (g + 1) * POOL_GROUP_OUT)
        y = _dot(d_ref[:, g * POOL_GROUP_IN:(g + 1) * POOL_GROUP_IN],
                 wpool_ref[g * POOL_GROUP_IN:(g + 1) * POOL_GROUP_IN, :])
        pool_ref[:, ocols] = y * pscale_ref[:, ocols]

    fill_width = 2 * LANES
    fillers = [functools.partial(project, off, fill_width, jax.nn.silu)
               for off in range(G_OFF, G_OFF + V_W, fill_width)]
    fillers += [functools.partial(project, off, fill_width, jax.nn.sigmoid)
                for off in range(GA_OFF, D_IN, fill_width)]
    fillers += [functools.partial(pool_matmul, g) for g in range(N_POOL_GROUPS)]

    head0 = (lane < RET_QK_DIM).astype(F32)
    head1 = 1.0 - head0

    n_chunks = tm // chunk
    stages = 3 * n_chunks

    def run_fillers(stage):
        for f in fillers[stage * len(fillers) // stages:(stage + 1) * len(fillers) // stages]:
            f()

    def pair_cols(off, pair, width):
        return slice(off + pair * width, off + (pair + 1) * width)

    for c in range(n_chunks):
        rows = slice(c * chunk, (c + 1) * chunk)
        scores = []
        for pair in range(N_PAIRS):
            qb = z_ref[rows, pair_cols(Q_OFF, pair, LANES)]
            kb = z_ref[rows, pair_cols(K_OFF, pair, LANES)]
            k_stack = jnp.concatenate([kb * head0, kb * head1], axis=0).astype(BF16)
            scores.append(lax.dot_general(qb.astype(BF16), k_stack, (((1,), (1,)), ((), ())),
                                          preferred_element_type=F32))
        run_fillers(3 * c)
        for pair in range(N_PAIRS):
            qx = z_ref[rows, pair_cols(Q_OFF, pair, LANES)] * xi_ref[pair]
            vb16 = z_ref[rows, pair_cols(V_OFF, pair, 2 * RET_V_DIM)].astype(BF16)
            s16 = s_ref[pair].astype(BF16)
            for j, hmask in enumerate((head0, head1)):
                h = 2 * pair + j
                scj = scores[pair][:, j * chunk:(j + 1) * chunk] * dmask_ref[h]
                lhs = jnp.concatenate([scj, qx * hmask], axis=1).astype(BF16)
                rhs = jnp.concatenate([vb16[:, j * RET_V_DIM:(j + 1) * RET_V_DIM], s16], axis=0)
                o = _dot(lhs, rhs)
                o = o * lax.rsqrt(jnp.mean(o * o, axis=-1, keepdims=True) + EPS)
                ret_ref[rows, h * RET_V_DIM:(h + 1) * RET_V_DIM] = o
        run_fillers(3 * c + 1)
        for pair in range(N_PAIRS):
            kw = (z_ref[rows, pair_cols(K_OFF, pair, LANES)] * wk_ref[pair]).astype(BF16)
            vb16 = z_ref[rows, pair_cols(V_OFF, pair, 2 * RET_V_DIM)].astype(BF16)
            upd = lax.dot_general(kw, vb16, (((0,), (0,)), ((), ())),
                                  preferred_element_type=F32)
            new_s = jnp.concatenate([upd[0:RET_QK_DIM, 0:RET_V_DIM],
                                     upd[RET_QK_DIM:, RET_V_DIM:]], axis=0)
            s_ref[pair] = s_ref[pair] * gs_ref[pair] + new_s
        run_fillers(3 * c + 2)

    n_halves = 2
    for r in range(n_halves):
        rows = slice(r * tm // n_halves, (r + 1) * tm // n_halves)
        ret = ret_ref[rows, :] * z_ref[rows, G_OFF:G_OFF + V_W]
        merged = (z_ref[rows, GA_OFF:GA_OFF + D_MODEL] * ret
                  + z_ref[rows, GB_OFF:GB_OFF + D_MODEL] * pool_ref[rows, :])
        o_ref[rows, :] = x_ref[rows, :] + _dot(merged.astype(BF16), wout_ref[...])


def _state_alias(prev, n_in, first_out):
    if prev is None:
        return [], [], {}
    specs = [pl.BlockSpec(memory_space=pl.ANY) for _ in prev]
    aliases = {n_in + k: first_out + k for k in range(len(prev))}
    return list(prev), specs, aliases


def _mixer_prompt(x, g, w_in, w_pool, pool_scale, w_out, tabs, layer, prev_states, *, tm):
    b, t_len, _ = x.shape
    depth = w_in.shape[0]
    chunk = RET_CHUNK
    assert t_len % tm == 0 and tm % chunk == 0
    cos, sin, dmask, xi, wk, gs = tabs
    row_spec = pl.BlockSpec((None, tm, D_MODEL), lambda i, j: (i, j, 0))
    tab_spec = pl.BlockSpec((tm, LANES), lambda i, j: (j, 0))
    in_specs = [
        row_spec, _const_spec(g.shape), ANY_SPEC, ANY_SPEC, _const_spec(pool_scale.shape),
        ANY_SPEC, tab_spec, tab_spec,
        _const_spec(dmask.shape), _const_spec(xi.shape), _const_spec(wk.shape), _const_spec(gs.shape),
    ]
    args = [x, g, w_in, w_pool, pool_scale, w_out, cos, sin, dmask, xi, wk, gs]
    extra, extra_specs, aliases = _state_alias(prev_states, len(args), 1)
    w_shapes = [w_in.shape[1:], w_pool.shape[1:], w_out.shape[1:]]
    out_specs = [
        row_spec,
        pl.BlockSpec((None, None, N_PAIRS * LANES, RET_V_DIM), lambda i, j: (layer, i, 0, 0)),
        pl.BlockSpec((None, None, POOL_BUF, POOL_IN), lambda i, j: (layer, i, 0, 0)),
    ] + [ANY_SPEC] * len(w_shapes)
    out_shape = [
        jax.ShapeDtypeStruct((b, t_len, D_MODEL), F32),
        jax.ShapeDtypeStruct((depth, b, N_PAIRS * LANES, RET_V_DIM), F32),
        jax.ShapeDtypeStruct((depth, b, POOL_BUF, POOL_IN), F32),
    ] + [jax.ShapeDtypeStruct(s, BF16) for s in w_shapes]
    scratch = [
        pltpu.VMEM((tm, D_IN), F32),
        pltpu.VMEM((N_PAIRS, LANES, RET_V_DIM), F32),
        pltpu.VMEM((POOL_PAD + tm, POOL_IN), F32),
        pltpu.VMEM((tm, V_W), F32),
        pltpu.VMEM((tm, D_MODEL), F32),
        pltpu.VMEM((tm, D_MODEL), BF16),
        pltpu.VMEM((tm, POOL_IN), BF16),
    ]
    for s in w_shapes:
        scratch += _stage_scratch(s)
    scratch.append(pltpu.SemaphoreType.DMA((len(w_shapes),)))
    y, sret, spool, *w16 = pl.pallas_call(
        functools.partial(_mixer_prompt_body, tm=tm, chunk=chunk, n_alias=len(extra), layer=layer),
        grid=(b, t_len // tm),
        in_specs=in_specs + extra_specs,
        out_specs=out_specs,
        out_shape=out_shape,
        scratch_shapes=scratch,
        input_output_aliases=aliases,
        compiler_params=pltpu.CompilerParams(
            dimension_semantics=("arbitrary", "arbitrary"), vmem_limit_bytes=VMEM_LIMIT),
        name="mixer_prompt",
    )(*args, *extra)
    return y, (sret, spool), tuple(w16)


def _log_gamma():
    return np.log1p(-np.exp2(-5.0 - np.arange(N_RET_HEADS, dtype=np.float64)))


def _rope_tables(pos):
    half = RET_QK_DIM // 2
    inv = np.power(ROPE_BASE, -np.linspace(0.0, 1.0, half))
    ang = pos[:, None] * inv[None, :]
    cos, sin = np.cos(ang), np.sin(ang)
    cos128 = np.tile(cos, (1, LANES // half))
    sin128 = np.tile(np.concatenate([-sin, sin], axis=1), (1, LANES // RET_QK_DIM))
    return cos128.astype(np.float32), sin128.astype(np.float32)


def _pair_lanes(per_head):
    h, c = per_head.shape
    x = np.broadcast_to(per_head[:, :, None], (h, c, RET_QK_DIM))
    x = x.reshape(N_PAIRS, 2, c, RET_QK_DIM).transpose(0, 2, 1, 3).reshape(N_PAIRS, c, LANES)
    return x.astype(np.float32)


def _state_decay(chunk):
    g_c = np.exp(_log_gamma() * chunk)
    gs = np.broadcast_to(g_c[:, None, None], (N_RET_HEADS, RET_QK_DIM, RET_V_DIM))
    return gs.reshape(N_PAIRS, LANES, RET_V_DIM).astype(np.float32)


def _decay_tables(chunk):
    lg = _log_gamma()
    i = np.arange(chunk, dtype=np.float64)
    dist = i[:, None] - i[None, :]
    dmask = np.where(dist[None] >= 0, np.exp(lg[:, None, None] * np.maximum(dist, 0.0)[None]), 0.0)
    xi = np.exp(lg[:, None] * (i[None, :] + 1.0))
    wk = np.exp(lg[:, None] * (chunk - 1.0 - i[None, :]))
    return dmask.astype(np.float32), _pair_lanes(xi), _pair_lanes(wk), _state_decay(chunk)


def _mixer_sample_body(*refs, bb, ts, past, n_alias, layer):
    (x_ref, g_ref, win_ref, wpool_ref, pscale_ref, wout_ref, cos_ref, sin_ref,
     xi_ref, wk_ref, lag_ref, gs_ref, ind_ref, s0_ref, pb_ref) = refs[:15]
    (o_ref, snew_ref, pnew_ref, z_ref, inter_ref, ublk_ref, pblk_ref) = refs[15 + n_alias:]
    g_ref, pscale_ref = g_ref.at[pl.ds(layer, 1)], pscale_ref.at[pl.ds(layer, 1)]
    rows = bb * ts
    x = x_ref[...]
    _project_in(x, g_ref, win_ref, z_ref)
    lane = lax.broadcasted_iota(jnp.int32, (1, LANES), 1)
    first_half = (lane % RET_QK_DIM) < (RET_QK_DIM // 2)
    _rotate_inplace(z_ref, cos_ref[...], sin_ref[...], first_half)

    q = z_ref[:, Q_OFF:Q_OFF + QK_W]
    k = z_ref[:, K_OFF:K_OFF + QK_W]
    v = z_ref[:, V_OFF:V_OFF + V_W]
    step = lax.broadcasted_iota(jnp.int32, (rows, 1), 0) % ts
    intra = None
    for d in range(ts):
        kd = k if d == 0 else pltpu.roll(k, d, 0)
        vd = v if d == 0 else pltpu.roll(v, d, 0)
        prod = jnp.where(step >= d, q * kd, 0.0).astype(BF16)
        score = _dot(prod, ind_ref[...])
        term = score * lag_ref[d:d + 1, :] * jnp.where(step >= d, vd, 0.0)
        intra = term if intra is None else intra + term

    row8 = lax.broadcasted_iota(jnp.int32, (2 * ts, 1), 0)
    top = (lax.broadcasted_iota(jnp.int32, (LANES, 1), 0) < RET_QK_DIM).astype(F32)
    bot = 1.0 - top

    def seq_pair(p, carry):
        r0 = pl.multiple_of(p * 2 * ts, 2 * ts)
        rsl = pl.ds(r0, 2 * ts)
        for pair in range(N_PAIRS):
            lanes = slice(pair * LANES, (pair + 1) * LANES)
            q8 = (z_ref[rsl, Q_OFF + pair * LANES:Q_OFF + (pair + 1) * LANES] * xi_ref[:, lanes]).astype(BF16)
            kw8 = z_ref[rsl, K_OFF + pair * LANES:K_OFF + (pair + 1) * LANES] * wk_ref[:, lanes]
            v8 = z_ref[rsl, V_OFF + pair * 2 * RET_V_DIM:V_OFF + (pair + 1) * 2 * RET_V_DIM].astype(BF16)
            outs = []
            for j in range(2):
                b = 2 * p + j
                s_pair = s0_ref[b, lanes, :]
                bd = jnp.concatenate([s_pair * top, s_pair * bot], axis=1).astype(BF16)
                outs.append(_dot(q8, bd))
                mine = (row8 >= j * ts) & (row8 < (j + 1) * ts)
                kwj = jnp.where(mine, kw8, 0.0).astype(BF16)
                upd = lax.dot_general(kwj, v8, (((0,), (0,)), ((), ())), preferred_element_type=F32)
                new_s = jnp.concatenate([upd[0:RET_QK_DIM, 0:RET_V_DIM],
                                         upd[RET_QK_DIM:, RET_V_DIM:]], axis=0)
                snew_ref[b, lanes, :] = s_pair * gs_ref[pair] + new_s
            inter_ref[rsl, pair * 2 * RET_V_DIM:(pair + 1) * 2 * RET_V_DIM] = jnp.where(
                row8 < ts, outs[0], outs[1])
        return carry

    lax.fori_loop(0, bb // 2, seq_pair, 0, unroll=8)

    keep = POOL_BUF - ts
    pnew_ref[0:keep] = pb_ref[ts:]
    for g, w in enumerate(POOL_WINDOWS):
        ublk_ref[g] = z_ref[:, U_OFF + g * POOL_GROUP_IN:U_OFF + (g + 1) * POOL_GROUP_IN]
        u_steps = [ublk_ref[g, pl.ds(t, bb, stride=ts), :] for t in range(ts)]
        for t in range(ts):
            pnew_ref[keep + t, :, g * POOL_GROUP_IN:(g + 1) * POOL_GROUP_IN] = u_steps[t]
        ds_ = []
        for t in range(ts):
            tot = None
            for j in range(w):
                idx = POOL_BUF + t - j
                if idx >= POOL_BUF:
                    term = u_steps[idx - POOL_BUF]
                else:
                    term = pb_ref[idx, :, g * POOL_GROUP_IN:(g + 1) * POOL_GROUP_IN]
                tot = term if tot is None else tot + term
            cnt = float(min(past + t + 1, w))
            ds_.append(tot / cnt - u_steps[t])
        y = _dot(jnp.concatenate(ds_, axis=0).astype(BF16),
                 wpool_ref[g * POOL_GROUP_IN:(g + 1) * POOL_GROUP_IN, :])
        y = y * pscale_ref[:, g * POOL_GROUP_OUT:(g + 1) * POOL_GROUP_OUT]
        for half in range(POOL_GROUP_OUT // LANES):
            blk = g * (POOL_GROUP_OUT // LANES) + half
            for t in range(ts):
                pblk_ref[blk, pl.ds(t, bb, stride=ts), :] = y[t * bb:(t + 1) * bb, half * LANES:(half + 1) * LANES]
    pool = jnp.concatenate([pblk_ref[blk] for blk in range(D_MODEL // LANES)], axis=1)

    o_all = intra + inter_ref[...]
    rets = []
    for h in range(N_RET_HEADS):
        hc = slice(h * RET_V_DIM, (h + 1) * RET_V_DIM)
        rets.append(_swish_gate_norm(o_all[:, hc], z_ref[:, G_OFF + h * RET_V_DIM:G_OFF + (h + 1) * RET_V_DIM]))
    ret = jnp.concatenate(rets, axis=1)
    o_ref[...] = _merge_out(x, z_ref, ret, pool, wout_ref)


def _sample_tables(ts, past, rows):
    lg = _log_gamma()
    i = np.arange(ts, dtype=np.float64)
    lag = np.exp(lg[:, None] * i[None, :])
    lag = np.repeat(lag.T, RET_V_DIM, axis=1)
    xi = np.exp(lg[:, None] * (i[None, :] + 1.0))
    wk = np.exp(lg[:, None] * (ts - 1.0 - i[None, :]))
    expand = lambda a: np.tile(np.repeat(a.T, RET_QK_DIM, axis=1), (2, 1)).astype(np.float32)
    cos, sin = _rope_tables(past + np.arange(ts, dtype=np.float64))
    reps = rows // ts
    head_of_k = np.arange(QK_W) // RET_QK_DIM
    head_of_v = np.arange(V_W) // RET_V_DIM
    ind = (head_of_k[:, None] == head_of_v[None, :]).astype(BF16)
    return (np.tile(cos, (reps, 1)), np.tile(sin, (reps, 1)), expand(xi), expand(wk), lag.astype(np.float32),
            _state_decay(ts), ind)


def _mixer_sample(x2d, s0, pbuf, g, w_in, w_pool, pool_scale, w_out, tabs, layer, prev_states, *, bb, ts, past):
    n = x2d.shape[0]
    depth = s0.shape[0]
    nb = n // ts
    rows = bb * ts
    assert nb % bb == 0 and bb % 2 == 0 and (2 * ts) % 8 == 0 and ts <= POOL_BUF
    cos, sin, xi, wk, lag, gs, ind = tabs
    row_spec = pl.BlockSpec((rows, D_MODEL), lambda i: (i, 0))
    state_spec = pl.BlockSpec((None, bb, N_PAIRS * LANES, RET_V_DIM), lambda i: (layer, i, 0, 0))
    pool_spec = pl.BlockSpec((None, POOL_BUF, bb, POOL_IN), lambda i: (layer, 0, i, 0))
    in_specs = [
        row_spec, _const_spec(g.shape), _resident_spec(w_in.shape),
        _resident_spec(w_pool.shape), _const_spec(pool_scale.shape),
        _resident_spec(w_out.shape), _const_spec(cos.shape), _const_spec(sin.shape),
        _const_spec(xi.shape), _const_spec(wk.shape), _const_spec(lag.shape), _const_spec(gs.shape),
        _const_spec(ind.shape), state_spec, pool_spec,
    ]
    args = [x2d, g, w_in, w_pool, pool_scale, w_out, cos, sin, xi, wk, lag, gs, ind, s0, pbuf]
    extra, extra_specs, aliases = _state_alias(prev_states, len(args), 1)
    out_specs = [row_spec, state_spec, pool_spec]
    out_shape = [
        jax.ShapeDtypeStruct((n, D_MODEL), F32),
        jax.ShapeDtypeStruct((depth, nb, N_PAIRS * LANES, RET_V_DIM), F32),
        jax.ShapeDtypeStruct((depth, POOL_BUF, nb, POOL_IN), F32),
    ]
    scratch = [
        pltpu.VMEM((rows, D_IN), F32),
        pltpu.VMEM((rows, V_W), F32),
        pltpu.VMEM((N_POOL_GROUPS, rows, LANES), F32),
        pltpu.VMEM((D_MODEL // LANES, rows, LANES), F32),
    ]
    y, snew, pnew = pl.pallas_call(
        functools.partial(_mixer_sample_body, bb=bb, ts=ts, past=past, n_alias=len(extra), layer=layer),
        grid=(nb // bb,),
        in_specs=in_specs + extra_specs,
        out_specs=out_specs,
        out_shape=out_shape,
        scratch_shapes=scratch,
        input_output_aliases=aliases,
        compiler_params=pltpu.CompilerParams(
            dimension_semantics=("arbitrary",), vmem_limit_bytes=VMEM_LIMIT),
        name="mixer_sample",
    )(*args, *extra)
    return y, (snew, pnew)


def kernel(x_prompt, x_sample, state_ret, state_pool, norm_ffn1, w_ffn1_in, w_ffn1_out, norm_mix,
           w_in, w_pool, pool_scale, w_out, norm_ffn2, w_ffn2_in, w_ffn2_out, norm_final):
    depth = w_in.shape[0]
    bp, tp, _ = x_prompt.shape
    bs, ts, _ = x_sample.shape
    wp = w_pool.reshape(depth, N_POOL_GROUPS * POOL_GROUP_IN, POOL_GROUP_OUT)

    n1, nm, n2, ps = norm_ffn1, norm_mix, norm_ffn2, pool_scale
    gf = norm_final.reshape(1, D_MODEL)
    last = depth - 1

    tm = 512
    tm_ffn = 1024
    tabs_p = _rope_tables(np.arange(tp, dtype=np.float64)) + _decay_tables(RET_CHUNK)

    bb = 32
    tabs_s = _sample_tables(ts, PAST_LEN, bb * ts)
    s0_all = state_ret.reshape(depth, bs, N_PAIRS * LANES, RET_V_DIM)
    pb_all = jnp.swapaxes(state_pool, 1, 2)

    x = x_prompt.reshape(bp * tp, D_MODEL)
    xs = x_sample.reshape(bs * ts, D_MODEL)
    states_p = states_s = None
    for l in range(depth):
        x, xs = _ffn(x, xs, n1, w_ffn1_in, w_ffn1_out, l, tm=tm_ffn)
        x, states_p, (wi16, wp16, wo16) = _mixer_prompt(x.reshape(bp, tp, D_MODEL), nm, w_in, wp, ps, w_out,
                                                        tabs_p, l, states_p, tm=tm)
        xs, states_s = _mixer_sample(xs, s0_all, pb_all, nm, wi16, wp16, ps, wo16, tabs_s, l, states_s,
                                     bb=bb, ts=ts, past=PAST_LEN)
        x, xs = _ffn(x.reshape(bp * tp, D_MODEL), xs, n2, w_ffn2_in, w_ffn2_out, l,
                     gf if l == last else None, tm=tm_ffn)
    y_prompt = x.reshape(bp, tp, D_MODEL)
    y_sample = xs.reshape(bs, ts, D_MODEL)
    state_ret_prompt = states_p[0].reshape(depth, bp, N_RET_HEADS, RET_QK_DIM, RET_V_DIM)
    state_pool_prompt = states_p[1]
    state_ret_sample = states_s[0].reshape(depth, bs, N_RET_HEADS, RET_QK_DIM, RET_V_DIM)
    state_pool_sample = jnp.swapaxes(states_s[1], 1, 2)
    return (y_prompt, y_sample, state_ret_prompt, state_ret_sample, state_pool_prompt, state_pool_sample)
```

```python
import functools

import numpy as np
import jax
import jax.numpy as jnp
from jax import lax
from jax.experimental import pallas as pl
from jax.experimental.pallas import tpu as pltpu

D_MODEL = 1024
N_RET_HEADS = 8
RET_QK_DIM = 64
RET_V_DIM = 128
RET_CHUNK = 128
ROPE_BASE = 10000.0
QK_W = N_RET_HEADS * RET_QK_DIM
V_W = N_RET_HEADS * RET_V_DIM
POOL_WINDOWS = (2, 4, 8, 16)
N_POOL_GROUPS = 4
POOL_IN = 512
POOL_GROUP_IN = 128
POOL_GROUP_OUT = 256
POOL_BUF = 15
D_IN = 2 * QK_W + 2 * V_W + POOL_IN + 2 * D_MODEL
D_FF = 2816
EPS = 1e-6
PAST_LEN = 16384

Q_OFF = 0
K_OFF = QK_W
V_OFF = 2 * QK_W
G_OFF = 2 * QK_W + V_W
U_OFF = 2 * QK_W + 2 * V_W
GA_OFF = U_OFF + POOL_IN
GB_OFF = GA_OFF + D_MODEL

LANES = 128
N_PAIRS = N_RET_HEADS // 2
POOL_PAD = 16
VMEM_LIMIT = 60 * 1024 * 1024

F32 = jnp.float32
BF16 = jnp.bfloat16


def _dot(a, b):
    return jnp.dot(a, b, preferred_element_type=F32)


def _rms(x, g):
    return x * lax.rsqrt(jnp.mean(x * x, axis=-1, keepdims=True) + EPS) * g


def _const_spec(shape):
    nd = len(shape)
    return pl.BlockSpec(shape, lambda *_: (0,) * nd)


FF_CHUNKS = ((0, 1024), (1024, 1024), (2048, 768))


STAGE_BUFS = 6
STAGE_BYTES = 3 * 256 * 1024


def _stage_rows(shape):
    r, c = shape
    fits = [n for n in range(16, r + 1, 16) if r % n == 0 and n * c * 4 <= STAGE_BYTES]
    return max(fits)


def _stage_scratch(shape):
    return [pltpu.VMEM(shape, BF16), pltpu.VMEM((STAGE_BUFS, _stage_rows(shape), shape[1]), F32),
            pltpu.SemaphoreType.DMA((STAGE_BUFS,))]


def _fetch_cast(src, dst, stage, sems):
    nbuf, rows, _ = stage.shape
    n = src.shape[0] // rows

    def copy(k):
        return pltpu.make_async_copy(src.at[pl.ds(k * rows, rows), :], stage.at[k % nbuf], sems.at[k % nbuf])

    for k in range(min(nbuf, n)):
        copy(k).start()
    for k in range(n):
        copy(k).wait()
        dst[k * rows:(k + 1) * rows, :] = stage[k % nbuf].astype(BF16)
        if k + nbuf < n:
            copy(k + nbuf).start()


def _convert_weights(first, layer, srcs, scratch):
    @pl.when(first)
    def _():
        for k, src in enumerate(srcs):
            _fetch_cast(src.at[layer], *scratch[3 * k:3 * k + 3])

    return scratch[0::3]


def _emit_weights(first, last, srcs, outs, sems):
    copies = [pltpu.make_async_copy(s, o, sems.at[k]) for k, (s, o) in enumerate(zip(srcs, outs))]

    @pl.when(first)
    def _():
        for c in copies:
            c.start()

    @pl.when(last)
    def _():
        for c in copies:
            c.wait()


def _ffn_body(*refs, final, layer):
    n_in = 6 if final else 5
    x_ref, xs_ref, g_ref, win_hbm, wout_hbm = refs[:5]
    gf_ref = refs[5] if final else None
    o_ref, os_ref = refs[n_in:n_in + 2]
    g_ref = g_ref.at[pl.ds(layer, 1)]
    i = pl.program_id(0)
    win_ref, wout_ref = _convert_weights(i == 0, layer, (win_hbm, wout_hbm), refs[n_in + 2:])

    def ffn_rows(x):
        r = lax.rsqrt(jnp.mean(x * x, axis=-1, keepdims=True) + EPS)
        h = (x * g_ref[...]).astype(BF16)
        acc = None
        for off, width in FF_CHUNKS:
            gate = _dot(h, win_ref[:, off:off + width]) * r
            up = _dot(h, win_ref[:, D_FF + off:D_FF + off + width]) * r
            a = (gate * jax.nn.sigmoid(gate) * up).astype(BF16)
            part = _dot(a, wout_ref[off:off + width, :])
            acc = part if acc is None else acc + part
        y = x + 0.5 * acc
        return _rms(y, gf_ref[...]) if final else y

    o_ref[...] = ffn_rows(x_ref[...])

    @pl.when(i == pl.num_programs(0) - 1)
    def _():
        os_ref[...] = ffn_rows(xs_ref[...])


def _resident_spec(shape):
    nd = len(shape)
    return pl.BlockSpec(tuple(shape), lambda *_: (0,) * nd, pipeline_mode=pl.Buffered(1))


ANY_SPEC = pl.BlockSpec(memory_space=pl.ANY)


def _ffn(x2d, xs2d, g, w_in, w_out, layer, g_final=None, *, tm):
    n = x2d.shape[0]
    assert n % tm == 0
    final = g_final is not None
    row_spec = pl.BlockSpec((tm, D_MODEL), lambda i: (i, 0))
    in_specs = [row_spec, _resident_spec(xs2d.shape), _const_spec(g.shape), ANY_SPEC, ANY_SPEC]
    args = [x2d, xs2d, g, w_in, w_out]
    if final:
        in_specs.append(_const_spec((1, D_MODEL)))
        args.append(g_final.reshape(1, D_MODEL))
    scratch = _stage_scratch(w_in.shape[1:]) + _stage_scratch(w_out.shape[1:])
    return pl.pallas_call(
        functools.partial(_ffn_body, final=final, layer=layer),
        grid=(n // tm,),
        in_specs=in_specs,
        out_specs=[row_spec, _resident_spec(xs2d.shape)],
        out_shape=[jax.ShapeDtypeStruct(x2d.shape, F32), jax.ShapeDtypeStruct(xs2d.shape, F32)],
        scratch_shapes=scratch,
        compiler_params=pltpu.CompilerParams(
            dimension_semantics=("arbitrary",), vmem_limit_bytes=VMEM_LIMIT),
        name="ffn_final" if final else "ffn",
    )(*args)


IN_CHUNKS = tuple((o, 1024) for o in range(0, 5120, 1024)) + ((5120, 512),)


def _project_in(x, g_ref, win_ref, z_ref):
    h = _rms(x, g_ref[...]).astype(BF16)
    for off, width in IN_CHUNKS:
        z_ref[:, off:off + width] = _dot(h, win_ref[:, off:off + width])


def _rotate_inplace(z_ref, cos, sin_signed, first_half):
    for blk in range(2 * QK_W // LANES):
        cols = slice(blk * LANES, (blk + 1) * LANES)
        xb = z_ref[:, cols]
        partner = jnp.where(first_half, pltpu.roll(xb, LANES - 32, 1), pltpu.roll(xb, 32, 1))
        r = xb * cos + partner * sin_signed
        if blk >= QK_W // LANES:
            r = r * (RET_QK_DIM ** -0.5)
        z_ref[:, cols] = r


def _merge_out(x, z_ref, ret, pool, wout_ref):
    ga = z_ref[:, GA_OFF:GA_OFF + D_MODEL]
    gb = z_ref[:, GB_OFF:GB_OFF + D_MODEL]
    merged = jax.nn.sigmoid(ga) * ret + jax.nn.sigmoid(gb) * pool
    return x + _dot(merged.astype(BF16), wout_ref[...])


def _swish_gate_norm(o, gate):
    o = o * lax.rsqrt(jnp.mean(o * o, axis=-1, keepdims=True) + EPS)
    return o * (gate * jax.nn.sigmoid(gate))


def _mixer_prompt_body(*refs, tm, chunk, n_alias, layer):
    (x_ref, g_ref, win_ref, wpool_ref, pscale_ref, wout_ref, cos_ref, sin_ref,
     dmask_ref, xi_ref, wk_ref, gs_ref) = refs[:12]
    outs = refs[12 + n_alias:]
    o_ref, sret_ref, spool_ref = outs[:3]
    z_ref, s_ref, ext_ref, ret_ref, pool_ref, h_ref, d_ref = outs[6:13]
    g_ref, pscale_ref = g_ref.at[pl.ds(layer, 1)], pscale_ref.at[pl.ds(layer, 1)]
    t = pl.program_id(1)
    nt = pl.num_programs(1)
    bi = pl.program_id(0)
    first = (bi == 0) & (t == 0)
    win_ref, wpool_ref, wout_ref = _convert_weights(first, layer, (win_ref, wpool_ref, wout_ref), outs[13:22])
    _emit_weights(first, (bi == pl.num_programs(0) - 1) & (t == nt - 1),
                  (win_ref, wpool_ref, wout_ref), outs[3:6], outs[22])

    @pl.when(t == 0)
    def _():
        s_ref[...] = jnp.zeros_like(s_ref)
        ext_ref[0:POOL_PAD, :] = jnp.zeros((POOL_PAD, POOL_IN), F32)

    _mixer_prompt_tile((x_ref, g_ref, win_ref, wpool_ref, pscale_ref, wout_ref, cos_ref, sin_ref, dmask_ref, xi_ref,
                        wk_ref, gs_ref, o_ref, z_ref, s_ref, ext_ref, ret_ref, pool_ref, h_ref, d_ref),
                       t * tm, tm, chunk)

    @pl.when(t == nt - 1)
    def _():
        sret_ref[...] = s_ref[...].reshape(N_PAIRS * LANES, RET_V_DIM)
        spool_ref[...] = ext_ref[POOL_PAD - POOL_BUF:POOL_PAD, :]


def _mixer_prompt_tile(tile_refs, pos0, tm, chunk):
    (x_ref, g_ref, win_ref, wpool_ref, pscale_ref, wout_ref, cos_ref, sin_ref, dmask_ref, xi_ref,
     wk_ref, gs_ref, o_ref, z_ref, s_ref, ext_ref, ret_ref, pool_ref, h_ref, d_ref) = tile_refs
    h_ref[...] = _rms(x_ref[...], g_ref[...]).astype(BF16)

    def project(off, width, act=None):
        y = _dot(h_ref[...], win_ref[:, off:off + width])
        z_ref[:, off:off + width] = y if act is None else act(y)

    project(Q_OFF, 2 * QK_W)
    project(V_OFF, V_W)
    project(U_OFF, POOL_IN)

    lane = lax.broadcasted_iota(jnp.int32, (1, LANES), 1)
    first_half = (lane % RET_QK_DIM) < (RET_QK_DIM // 2)
    _rotate_inplace(z_ref, cos_ref[...], sin_ref[...], first_half)

    ext_ref[POOL_PAD:POOL_PAD + tm, :] = z_ref[:, U_OFF:U_OFF + POOL_IN]
    pos = pos0 + lax.broadcasted_iota(jnp.int32, (tm, 1), 0)
    for g, w in enumerate(POOL_WINDOWS):
        cols = slice(g * POOL_GROUP_IN, (g + 1) * POOL_GROUP_IN)
        u = ext_ref[POOL_PAD:POOL_PAD + tm, cols]
        tot = u
        for j in range(1, w):
            tot = tot + ext_ref[POOL_PAD - j:POOL_PAD - j + tm, cols]
        cnt = jnp.minimum(pos + 1, w).astype(F32)
        d_ref[:, cols] = (tot / cnt - u).astype(BF16)
    tail = ext_ref[tm:tm + POOL_PAD, :]
    ext_ref[0:POOL_PAD, :] = tail

    def pool_matmul(g):
        ocols = slice(g * POOL_GROUP_OUT, (g + 1) * POOL_GROUP_OUT)
        y = _dot(d_ref[:, g * POOL_GROUP_IN:(g + 1) * POOL_GROUP_IN],
                 wpool_ref[g * POOL_GROUP_IN:(g + 1) * POOL_GROUP_IN, :])
        pool_ref[:, ocols] = y * pscale_ref[:, ocols]

    fill_width = 2 * LANES
    fillers = [functools.partial(project, off, fill_width, jax.nn.silu)
               for off in range(G_OFF, G_OFF + V_W, fill_width)]
    fillers += [functools.partial(project, off, fill_width, jax.nn.sigmoid)
                for off in range(GA_OFF, D_IN, fill_width)]
    fillers += [functools.partial(pool_matmul, g) for g in range(N_POOL_GROUPS)]

    head0 = (lane < RET_QK_DIM).astype(F32)
    head1 = 1.0 - head0

    n_chunks = tm // chunk
    stages = 3 * n_chunks

    def run_fillers(stage):
        for f in fillers[stage * len(fillers) // stages:(stage + 1) * len(fillers) // stages]:
            f()

    def pair_cols(off, pair, width):
        return slice(off + pair * width, off + (pair + 1) * width)

    for c in range(n_chunks):
        rows = slice(c * chunk, (c + 1) * chunk)
        scores = []
        for pair in range(N_PAIRS):
            qb = z_ref[rows, pair_cols(Q_OFF, pair, LANES)]
            kb = z_ref[rows, pair_cols(K_OFF, pair, LANES)]
            k_stack = jnp.concatenate([kb * head0, kb * head1], axis=0).astype(BF16)
            scores.append(lax.dot_general(qb.astype(BF16), k_stack, (((1,), (1,)), ((), ())),
                                          preferred_element_type=F32))
        run_fillers(3 * c)
        for pair in range(N_PAIRS):
            qx = z_ref[rows, pair_cols(Q_OFF, pair, LANES)] * xi_ref[pair]
            vb16 = z_ref[rows, pair_cols(V_OFF, pair, 2 * RET_V_DIM)].astype(BF16)
            s16 = s_ref[pair].astype(BF16)
            for j, hmask in enumerate((head0, head1)):
                h = 2 * pair + j
                scj = scores[pair][:, j * chunk:(j + 1) * chunk] * dmask_ref[h]
                lhs = jnp.concatenate([scj, qx * hmask], axis=1).astype(BF16)
                rhs = jnp.concatenate([vb16[:, j * RET_V_DIM:(j + 1) * RET_V_DIM], s16], axis=0)
                o = _dot(lhs, rhs)
                o = o * lax.rsqrt(jnp.mean(o * o, axis=-1, keepdims=True) + EPS)
                ret_ref[rows, h * RET_V_DIM:(h + 1) * RET_V_DIM] = o
        run_fillers(3 * c + 1)
        for pair in range(N_PAIRS):
            kw = (z_ref[rows, pair_cols(K_OFF, pair, LANES)] * wk_ref[pair]).astype(BF16)
            vb16 = z_ref[rows, pair_cols(V_OFF, pair, 2 * RET_V_DIM)].astype(BF16)
            upd = lax.dot_general(kw, vb16, (((0,), (0,)), ((), ())),
                                  preferred_element_type=F32)
            new_s = jnp.concatenate([upd[0:RET_QK_DIM, 0:RET_V_DIM],
                                     upd[RET_QK_DIM:, RET_V_DIM:]], axis=0)
            s_ref[pair] = s_ref[pair] * gs_ref[pair] + new_s
        run_fillers(3 * c + 2)

    n_halves = 2
    for r in range(n_halves):
        rows = slice(r * tm // n_halves, (r + 1) * tm // n_halves)
        ret = ret_ref[rows, :] * z_ref[rows, G_OFF:G_OFF + V_W]
        merged = (z_ref[rows, GA_OFF:GA_OFF + D_MODEL] * ret
                  + z_ref[rows, GB_OFF:GB_OFF + D_MODEL] * pool_ref[rows, :])
        o_ref[rows, :] = x_ref[rows, :] + _dot(merged.astype(BF16), wout_ref[...])


def _state_alias(prev, n_in, first_out):
    if prev is None:
        return [], [], {}
    specs = [pl.BlockSpec(memory_space=pl.ANY) for _ in prev]
    aliases = {n_in + k: first_out + k for k in range(len(prev))}
    return list(prev), specs, aliases


def _mixer_prompt(x, g, w_in, w_pool, pool_scale, w_out, tabs, layer, prev_states, *, tm):
    b, t_len, _ = x.shape
    depth = w_in.shape[0]
    chunk = RET_CHUNK
    assert t_len % tm == 0 and tm % chunk == 0
    cos, sin, dmask, xi, wk, gs = tabs
    row_spec = pl.BlockSpec((None, tm, D_MODEL), lambda i, j: (i, j, 0))
    tab_spec = pl.BlockSpec((tm, LANES), lambda i, j: (j, 0))
    in_specs = [
        row_spec, _const_spec(g.shape), ANY_SPEC, ANY_SPEC, _const_spec(pool_scale.shape),
        ANY_SPEC, tab_spec, tab_spec,
        _const_spec(dmask.shape), _const_spec(xi.shape), _const_spec(wk.shape), _const_spec(gs.shape),
    ]
    args = [x, g, w_in, w_pool, pool_scale, w_out, cos, sin, dmask, xi, wk, gs]
    extra, extra_specs, aliases = _state_alias(prev_states, len(args), 1)
    w_shapes = [w_in.shape[1:], w_pool.shape[1:], w_out.shape[1:]]
    out_specs = [
        row_spec,
        pl.BlockSpec((None, None, N_PAIRS * LANES, RET_V_DIM), lambda i, j: (layer, i, 0, 0)),
        pl.BlockSpec((None, None, POOL_BUF, POOL_IN), lambda i, j: (layer, i, 0, 0)),
    ] + [ANY_SPEC] * len(w_shapes)
    out_shape = [
        jax.ShapeDtypeStruct((b, t_len, D_MODEL), F32),
        jax.ShapeDtypeStruct((depth, b, N_PAIRS * LANES, RET_V_DIM), F32),
        jax.ShapeDtypeStruct((depth, b, POOL_BUF, POOL_IN), F32),
    ] + [jax.ShapeDtypeStruct(s, BF16) for s in w_shapes]
    scratch = [
        pltpu.VMEM((tm, D_IN), F32),
        pltpu.VMEM((N_PAIRS, LANES, RET_V_DIM), F32),
        pltpu.VMEM((POOL_PAD + tm, POOL_IN), F32),
        pltpu.VMEM((tm, V_W), F32),
        pltpu.VMEM((tm, D_MODEL), F32),
        pltpu.VMEM((tm, D_MODEL), BF16),
        pltpu.VMEM((tm, POOL_IN), BF16),
    ]
    for s in w_shapes:
        scratch += _stage_scratch(s)
    scratch.append(pltpu.SemaphoreType.DMA((len(w_shapes),)))
    y, sret, spool, *w16 = pl.pallas_call(
        functools.partial(_mixer_prompt_body, tm=tm, chunk=chunk, n_alias=len(extra), layer=layer),
        grid=(b, t_len // tm),
        in_specs=in_specs + extra_specs,
        out_specs=out_specs,
        out_shape=out_shape,
        scratch_shapes=scratch,
        input_output_aliases=aliases,
        compiler_params=pltpu.CompilerParams(
            dimension_semantics=("arbitrary", "arbitrary"), vmem_limit_bytes=VMEM_LIMIT),
        name="mixer_prompt",
    )(*args, *extra)
    return y, (sret, spool), tuple(w16)


def _log_gamma():
    return np.log1p(-np.exp2(-5.0 - np.arange(N_RET_HEADS, dtype=np.float64)))


def _rope_tables(pos):
    half = RET_QK_DIM // 2
    inv = np.power(ROPE_BASE, -np.linspace(0.0, 1.0, half))
    ang = pos[:, None] * inv[None, :]
    cos, sin = np.cos(ang), np.sin(ang)
    cos128 = np.tile(cos, (1, LANES // half))
    sin128 = np.tile(np.concatenate([-sin, sin], axis=1), (1, LANES // RET_QK_DIM))
    return cos128.astype(np.float32), sin128.astype(np.float32)


def _pair_lanes(per_head):
    h, c = per_head.shape
    x = np.broadcast_to(per_head[:, :, None], (h, c, RET_QK_DIM))
    x = x.reshape(N_PAIRS, 2, c, RET_QK_DIM).transpose(0, 2, 1, 3).reshape(N_PAIRS, c, LANES)
    return x.astype(np.float32)


def _state_decay(chunk):
    g_c = np.exp(_log_gamma() * chunk)
    gs = np.broadcast_to(g_c[:, None, None], (N_RET_HEADS, RET_QK_DIM, RET_V_DIM))
    return gs.reshape(N_PAIRS, LANES, RET_V_DIM).astype(np.float32)


def _decay_tables(chunk):
    lg = _log_gamma()
    i = np.arange(chunk, dtype=np.float64)
    dist = i[:, None] - i[None, :]
    dmask = np.where(dist[None] >= 0, np.exp(lg[:, None, None] * np.maximum(dist, 0.0)[None]), 0.0)
    xi = np.exp(lg[:, None] * (i[None, :] + 1.0))
    wk = np.exp(lg[:, None] * (chunk - 1.0 - i[None, :]))
    return dmask.astype(np.float32), _pair_lanes(xi), _pair_lanes(wk), _state_decay(chunk)


def _mixer_sample_body(*refs, bb, ts, past, n_alias, layer):
    (x_ref, g_ref, win_ref, wpool_ref, pscale_ref, wout_ref, cos_ref, sin_ref,
     xi_ref, wk_ref, lag_ref, gs_ref, ind_ref, s0_ref, pb_ref) = refs[:15]
    (o_ref, snew_ref, pnew_ref, z_ref, inter_ref, ublk_ref, pblk_ref) = refs[15 + n_alias:]
    g_ref, pscale_ref = g_ref.at[pl.ds(layer, 1)], pscale_ref.at[pl.ds(layer, 1)]
    rows = bb * ts
    x = x_ref[...]
    _project_in(x, g_ref, win_ref, z_ref)
    lane = lax.broadcasted_iota(jnp.int32, (1, LANES), 1)
    first_half = (lane % RET_QK_DIM) < (RET_QK_DIM // 2)
    _rotate_inplace(z_ref, cos_ref[...], sin_ref[...], first_half)

    q = z_ref[:, Q_OFF:Q_OFF + QK_W]
    k = z_ref[:, K_OFF:K_OFF + QK_W]
    v = z_ref[:, V_OFF:V_OFF + V_W]
    step = lax.broadcasted_iota(jnp.int32, (rows, 1), 0) % ts
    intra = None
    for d in range(ts):
        kd = k if d == 0 else pltpu.roll(k, d, 0)
        vd = v if d == 0 else pltpu.roll(v, d, 0)
        prod = jnp.where(step >= d, q * kd, 0.0).astype(BF16)
        score = _dot(prod, ind_ref[...])
        term = score * lag_ref[d:d + 1, :] * jnp.where(step >= d, vd, 0.0)
        intra = term if intra is None else intra + term

    row8 = lax.broadcasted_iota(jnp.int32, (2 * ts, 1), 0)
    top = (lax.broadcasted_iota(jnp.int32, (LANES, 1), 0) < RET_QK_DIM).astype(F32)
    bot = 1.0 - top

    def seq_pair(p, carry):
        r0 = pl.multiple_of(p * 2 * ts, 2 * ts)
        rsl = pl.ds(r0, 2 * ts)
        for pair in range(N_PAIRS):
            lanes = slice(pair * LANES, (pair + 1) * LANES)
            q8 = (z_ref[rsl, Q_OFF + pair * LANES:Q_OFF + (pair + 1) * LANES] * xi_ref[:, lanes]).astype(BF16)
            kw8 = z_ref[rsl, K_OFF + pair * LANES:K_OFF + (pair + 1) * LANES] * wk_ref[:, lanes]
            v8 = z_ref[rsl, V_OFF + pair * 2 * RET_V_DIM:V_OFF + (pair + 1) * 2 * RET_V_DIM].astype(BF16)
            outs = []
            for j in range(2):
                b = 2 * p + j
                s_pair = s0_ref[b, lanes, :]
                bd = jnp.concatenate([s_pair * top, s_pair * bot], axis=1).astype(BF16)
                outs.append(_dot(q8, bd))
                mine = (row8 >= j * ts) & (row8 < (j + 1) * ts)
                kwj = jnp.where(mine, kw8, 0.0).astype(BF16)
                upd = lax.dot_general(kwj, v8, (((0,), (0,)), ((), ())), preferred_element_type=F32)
                new_s = jnp.concatenate([upd[0:RET_QK_DIM, 0:RET_V_DIM],
                                         upd[RET_QK_DIM:, RET_V_DIM:]], axis=0)
                snew_ref[b, lanes, :] = s_pair * gs_ref[pair] + new_s
            inter_ref[rsl, pair * 2 * RET_V_DIM:(pair + 1) * 2 * RET_V_DIM] = jnp.where(
                row8 < ts, outs[0], outs[1])
        return carry

    lax.fori_loop(0, bb // 2, seq_pair, 0, unroll=8)

    keep = POOL_BUF - ts
    pnew_ref[0:keep] = pb_ref[ts:]
    for g, w in enumerate(POOL_WINDOWS):
        ublk_ref[g] = z_ref[:, U_OFF + g * POOL_GROUP_IN:U_OFF + (g + 1) * POOL_GROUP_IN]
        u_steps = [ublk_ref[g, pl.ds(t, bb, stride=ts), :] for t in range(ts)]
        for t in range(ts):
            pnew_ref[keep + t, :, g * POOL_GROUP_IN:(g + 1) * POOL_GROUP_IN] = u_steps[t]
        ds_ = []
        for t in range(ts):
            tot = None
            for j in range(w):
                idx = POOL_BUF + t - j
                if idx >= POOL_BUF:
                    term = u_steps[idx - POOL_BUF]
                else:
                    term = pb_ref[idx, :, g * POOL_GROUP_IN:(g + 1) * POOL_GROUP_IN]
                tot = term if tot is None else tot + term
            cnt = float(min(past + t + 1, w))
            ds_.append(tot / cnt - u_steps[t])
        y = _dot(jnp.concatenate(ds_, axis=0).astype(BF16),
                 wpool_ref[g * POOL_GROUP_IN:(g + 1) * POOL_GROUP_IN, :])
        y = y * pscale_ref[:, g * POOL_GROUP_OUT:(g + 1) * POOL_GROUP_OUT]
        for half in range(POOL_GROUP_OUT // LANES):
            blk = g * (POOL_GROUP_OUT // LANES) + half
            for t in range(ts):
                pblk_ref[blk, pl.ds(t, bb, stride=ts), :] = y[t * bb:(t + 1) * bb, half * LANES:(half + 1) * LANES]
    pool = jnp.concatenate([pblk_ref[blk] for blk in range(D_MODEL // LANES)], axis=1)

    o_all = intra + inter_ref[...]
    rets = []
    for h in range(N_RET_HEADS):
        hc = slice(h * RET_V_DIM, (h + 1) * RET_V_DIM)
        rets.append(_swish_gate_norm(o_all[:, hc], z_ref[:, G_OFF + h * RET_V_DIM:G_OFF + (h + 1) * RET_V_DIM]))
    ret = jnp.concatenate(rets, axis=1)
    o_ref[...] = _merge_out(x, z_ref, ret, pool, wout_ref)


def _sample_tables(ts, past, rows):
    lg = _log_gamma()
    i = np.arange(ts, dtype=np.float64)
    lag = np.exp(lg[:, None] * i[None, :])
    lag = np.repeat(lag.T, RET_V_DIM, axis=1)
    xi = np.exp(lg[:, None] * (i[None, :] + 1.0))
    wk = np.exp(lg[:, None] * (ts - 1.0 - i[None, :]))
    expand = lambda a: np.tile(np.repeat(a.T, RET_QK_DIM, axis=1), (2, 1)).astype(np.float32)
    cos, sin = _rope_tables(past + np.arange(ts, dtype=np.float64))
    reps = rows // ts
    head_of_k = np.arange(QK_W) // RET_QK_DIM
    head_of_v = np.arange(V_W) // RET_V_DIM
    ind = (head_of_k[:, None] == head_of_v[None, :]).astype(BF16)
    return (np.tile(cos, (reps, 1)), np.tile(sin, (reps, 1)), expand(xi), expand(wk), lag.astype(np.float32),
            _state_decay(ts), ind)


def _mixer_sample(x2d, s0, pbuf, g, w_in, w_pool, pool_scale, w_out, tabs, layer, prev_states, *, bb, ts, past):
    n = x2d.shape[0]
    depth = s0.shape[0]
    nb = n // ts
    rows = bb * ts
    assert nb % bb == 0 and bb % 2 == 0 and (2 * ts) % 8 == 0 and ts <= POOL_BUF
    cos, sin, xi, wk, lag, gs, ind = tabs
    row_spec = pl.BlockSpec((rows, D_MODEL), lambda i: (i, 0))
    state_spec = pl.BlockSpec((None, bb, N_PAIRS * LANES, RET_V_DIM), lambda i: (layer, i, 0, 0))
    pool_spec = pl.BlockSpec((None, POOL_BUF, bb, POOL_IN), lambda i: (layer, 0, i, 0))
    in_specs = [
        row_spec, _const_spec(g.shape), _resident_spec(w_in.shape),
        _resident_spec(w_pool.shape), _const_spec(pool_scale.shape),
        _resident_spec(w_out.shape), _const_spec(cos.shape), _const_spec(sin.shape),
        _const_spec(xi.shape), _const_spec(wk.shape), _const_spec(lag.shape), _const_spec(gs.shape),
        _const_spec(ind.shape), state_spec, pool_spec,
    ]
    args = [x2d, g, w_in, w_pool, pool_scale, w_out, cos, sin, xi, wk, lag, gs, ind, s0, pbuf]
    extra, extra_specs, aliases = _state_alias(prev_states, len(args), 1)
    out_specs = [row_spec, state_spec, pool_spec]
    out_shape = [
        jax.ShapeDtypeStruct((n, D_MODEL), F32),
        jax.ShapeDtypeStruct((depth, nb, N_PAIRS * LANES, RET_V_DIM), F32),
        jax.ShapeDtypeStruct((depth, POOL_BUF, nb, POOL_IN), F32),
    ]
    scratch = [
        pltpu.VMEM((rows, D_IN), F32),
        pltpu.VMEM((rows, V_W), F32),
        pltpu.VMEM((N_POOL_GROUPS, rows, LANES), F32),
        pltpu.VMEM((D_MODEL // LANES, rows, LANES), F32),
    ]
    y, snew, pnew = pl.pallas_call(
        functools.partial(_mixer_sample_body, bb=bb, ts=ts, past=past, n_alias=len(extra), layer=layer),
        grid=(nb // bb,),
        in_specs=in_specs + extra_specs,
        out_specs=out_specs,
        out_shape=out_shape,
        scratch_shapes=scratch,
        input_output_aliases=aliases,
        compiler_params=pltpu.CompilerParams(
            dimension_semantics=("arbitrary",), vmem_limit_bytes=VMEM_LIMIT),
        name="mixer_sample",
    )(*args, *extra)
    return y, (snew, pnew)


def kernel(x_prompt, x_sample, state_ret, state_pool, norm_ffn1, w_ffn1_in, w_ffn1_out, norm_mix,
           w_in, w_pool, pool_scale, w_out, norm_ffn2, w_ffn2_in, w_ffn2_out, norm_final):
    depth = w_in.shape[0]
    bp, tp, _ = x_prompt.shape
    bs, ts, _ = x_sample.shape
    wp = w_pool.reshape(depth, N_POOL_GROUPS * POOL_GROUP_IN, POOL_GROUP_OUT)

    n1, nm, n2, ps = norm_ffn1, norm_mix, norm_ffn2, pool_scale
    gf = norm_final.reshape(1, D_MODEL)
    last = depth - 1

    tm = 512
    tm_ffn = 1024
    tabs_p = _rope_tables(np.arange(tp, dtype=np.float64)) + _decay_tables(RET_CHUNK)

    bb = 32
    tabs_s = _sample_tables(ts, PAST_LEN, bb * ts)
    s0_all = state_ret.reshape(depth, bs, N_PAIRS * LANES, RET_V_DIM)
    pb_all = jnp.swapaxes(state_pool, 1, 2)

    x = x_prompt.reshape(bp * tp, D_MODEL)
    xs = x_sample.reshape(bs * ts, D_MODEL)
    states_p = states_s = None
    for l in range(depth):
        x, xs = _ffn(x, xs, n1, w_ffn1_in, w_ffn1_out, l, tm=tm_ffn)
        x, states_p, (wi16, wp16, wo16) = _mixer_prompt(x.reshape(bp, tp, D_MODEL), nm, w_in, wp, ps, w_out,
                                                        tabs_p, l, states_p, tm=tm)
        xs, states_s = _mixer_sample(xs, s0_all, pb_all, nm, wi16, wp16, ps, wo16, tabs_s, l, states_s,
                                     bb=bb, ts=ts, past=PAST_LEN)
        x, xs = _ffn(x.reshape(bp * tp, D_MODEL), xs, n2, w_ffn2_in, w_ffn2_out, l,
                     gf if l == last else None, tm=tm_ffn)
    y_prompt = x.reshape(bp, tp, D_MODEL)
    y_sample = xs.reshape(bs, ts, D_MODEL)
    state_ret_prompt = states_p[0].reshape(depth, bp, N_RET_HEADS, RET_QK_DIM, RET_V_DIM)
    state_pool_prompt = states_p[1]
    state_ret_sample = states_s[0].reshape(depth, bs, N_RET_HEADS, RET_QK_DIM, RET_V_DIM)
    state_pool_sample = jnp.swapaxes(states_s[1], 1, 2)
    return (y_prompt, y_sample, state_ret_prompt, state_ret_sample, state_pool_prompt, state_pool_sample)
```

```python
import functools

import numpy as np
import jax
import jax.numpy as jnp
from jax import lax
from jax.experimental import pallas as pl
from jax.experimental.pallas import tpu as pltpu

D_MODEL = 1024
N_RET_HEADS = 8
RET_QK_DIM = 64
RET_V_DIM = 128
RET_CHUNK = 128
ROPE_BASE = 10000.0
QK_W = N_RET_HEADS * RET_QK_DIM
V_W = N_RET_HEADS * RET_V_DIM
POOL_WINDOWS = (2, 4, 8, 16)
N_POOL_GROUPS = 4
POOL_IN = 512
POOL_GROUP_IN = 128
POOL_GROUP_OUT = 256
POOL_BUF = 15
D_IN = 2 * QK_W + 2 * V_W + POOL_IN + 2 * D_MODEL
D_FF = 2816
EPS = 1e-6
PAST_LEN = 16384

Q_OFF = 0
K_OFF = QK_W
V_OFF = 2 * QK_W
G_OFF = 2 * QK_W + V_W
U_OFF = 2 * QK_W + 2 * V_W
GA_OFF = U_OFF + POOL_IN
GB_OFF = GA_OFF + D_MODEL

LANES = 128
N_PAIRS = N_RET_HEADS // 2
POOL_PAD = 16
VMEM_LIMIT = 60 * 1024 * 1024

F32 = jnp.float32
BF16 = jnp.bfloat16


def _dot(a, b):
    return jnp.dot(a, b, preferred_element_type=F32)


def _rms(x, g):
    return x * lax.rsqrt(jnp.mean(x * x, axis=-1, keepdims=True) + EPS) * g


def _const_spec(shape):
    nd = len(shape)
    return pl.BlockSpec(shape, lambda *_: (0,) * nd)


FF_CHUNKS = ((0, 1024), (1024, 1024), (2048, 768))


STAGE_BUFS = 6
STAGE_BYTES = 3 * 256 * 1024


def _stage_rows(shape):
    r, c = shape
    fits = [n for n in range(16, r + 1, 16) if r % n == 0 and n * c * 4 <= STAGE_BYTES]
    return max(fits)


def _stage_scratch(shape):
    return [pltpu.VMEM(shape, BF16), pltpu.VMEM((STAGE_BUFS, _stage_rows(shape), shape[1]), F32),
            pltpu.SemaphoreType.DMA((STAGE_BUFS,))]


def _fetch_cast(src, dst, stage, sems):
    nbuf, rows, _ = stage.shape
    n = src.shape[0] // rows

    def copy(k):
        return pltpu.make_async_copy(src.at[pl.ds(k * rows, rows), :], stage.at[k % nbuf], sems.at[k % nbuf])

    for k in range(min(nbuf, n)):
        copy(k).start()
    for k in range(n):
        copy(k).wait()
        dst[k * rows:(k + 1) * rows, :] = stage[k % nbuf].astype(BF16)
        if k + nbuf < n:
            copy(k + nbuf).start()


def _convert_weights(first, layer, srcs, scratch):
    @pl.when(first)
    def _():
        for k, src in enumerate(srcs):
            _fetch_cast(src.at[layer], *scratch[3 * k:3 * k + 3])

    return scratch[0::3]


def _emit_weights(first, last, srcs, outs, sems):
    copies = [pltpu.make_async_copy(s, o, sems.at[k]) for k, (s, o) in enumerate(zip(srcs, outs))]

    @pl.when(first)
    def _():
        for c in copies:
            c.start()

    @pl.when(last)
    def _():
        for c in copies:
            c.wait()


def _ffn_body(*refs, final, layer):
    n_in = 6 if final else 5
    x_ref, xs_ref, g_ref, win_hbm, wout_hbm = refs[:5]
    gf_ref = refs[5] if final else None
    o_ref, os_ref = refs[n_in:n_in + 2]
    g_ref = g_ref.at[pl.ds(layer, 1)]
    i = pl.program_id(0)
    win_ref, wout_ref = _convert_weights(i == 0, layer, (win_hbm, wout_hbm), refs[n_in + 2:])

    def ffn_rows(x):
        r = lax.rsqrt(jnp.mean(x * x, axis=-1, keepdims=True) + EPS)
        h = (x * g_ref[...]).astype(BF16)
        acc = None
        for off, width in FF_CHUNKS:
            gate = _dot(h, win_ref[:, off:off + width]) * r
            up = _dot(h, win_ref[:, D_FF + off:D_FF + off + width]) * r
            a = (gate * jax.nn.sigmoid(gate) * up).astype(BF16)
            part = _dot(a, wout_ref[off:off + width, :])
            acc = part if acc is None else acc + part
        y = x + 0.5 * acc
        return _rms(y, gf_ref[...]) if final else y

    o_ref[...] = ffn_rows(x_ref[...])

    @pl.when(i == pl.num_programs(0) - 1)
    def _():
        os_ref[...] = ffn_rows(xs_ref[...])


def _resident_spec(shape):
    nd = len(shape)
    return pl.BlockSpec(tuple(shape), lambda *_: (0,) * nd, pipeline_mode=pl.Buffered(1))


ANY_SPEC = pl.BlockSpec(memory_space=pl.ANY)


def _ffn(x2d, xs2d, g, w_in, w_out, layer, g_final=None, *, tm):
    n = x2d.shape[0]
    assert n % tm == 0
    final = g_final is not None
    row_spec = pl.BlockSpec((tm, D_MODEL), lambda i: (i, 0))
    in_specs = [row_spec, _resident_spec(xs2d.shape), _const_spec(g.shape), ANY_SPEC, ANY_SPEC]
    args = [x2d, xs2d, g, w_in, w_out]
    if final:
        in_specs.append(_const_spec((1, D_MODEL)))
        args.append(g_final.reshape(1, D_MODEL))
    scratch = _stage_scratch(w_in.shape[1:]) + _stage_scratch(w_out.shape[1:])
    return pl.pallas_call(
        functools.partial(_ffn_body, final=final, layer=layer),
        grid=(n // tm,),
        in_specs=in_specs,
        out_specs=[row_spec, _resident_spec(xs2d.shape)],
        out_shape=[jax.ShapeDtypeStruct(x2d.shape, F32), jax.ShapeDtypeStruct(xs2d.shape, F32)],
        scratch_shapes=scratch,
        compiler_params=pltpu.CompilerParams(
            dimension_semantics=("arbitrary",), vmem_limit_bytes=VMEM_LIMIT),
        name="ffn_final" if final else "ffn",
    )(*args)


IN_CHUNKS = tuple((o, 1024) for o in range(0, 5120, 1024)) + ((5120, 512),)


def _project_in(x, g_ref, win_ref, z_ref):
    h = _rms(x, g_ref[...]).astype(BF16)
    for off, width in IN_CHUNKS:
        z_ref[:, off:off + width] = _dot(h, win_ref[:, off:off + width])


def _rotate_inplace(z_ref, cos, sin_signed, first_half):
    for blk in range(2 * QK_W // LANES):
        cols = slice(blk * LANES, (blk + 1) * LANES)
        xb = z_ref[:, cols]
        partner = jnp.where(first_half, pltpu.roll(xb, LANES - 32, 1), pltpu.roll(xb, 32, 1))
        r = xb * cos + partner * sin_signed
        if blk >= QK_W // LANES:
            r = r * (RET_QK_DIM ** -0.5)
        z_ref[:, cols] = r


def _merge_out(x, z_ref, ret, pool, wout_ref):
    ga = z_ref[:, GA_OFF:GA_OFF + D_MODEL]
    gb = z_ref[:, GB_OFF:GB_OFF + D_MODEL]
    merged = jax.nn.sigmoid(ga) * ret + jax.nn.sigmoid(gb) * pool
    return x + _dot(merged.astype(BF16), wout_ref[...])


def _swish_gate_norm(o, gate):
    o = o * lax.rsqrt(jnp.mean(o * o, axis=-1, keepdims=True) + EPS)
    return o * (gate * jax.nn.sigmoid(gate))


def _mixer_prompt_body(*refs, tm, chunk, n_alias, layer):
    (x_ref, g_ref, win_ref, wpool_ref, pscale_ref, wout_ref, cos_ref, sin_ref,
     dmask_ref, xi_ref, wk_ref, gs_ref) = refs[:12]
    outs = refs[12 + n_alias:]
    o_ref, sret_ref, spool_ref = outs[:3]
    z_ref, s_ref, ext_ref, ret_ref, pool_ref, h_ref, d_ref, v_ref = outs[6:14]
    g_ref, pscale_ref = g_ref.at[pl.ds(layer, 1)], pscale_ref.at[pl.ds(layer, 1)]
    t = pl.program_id(1)
    nt = pl.num_programs(1)
    bi = pl.program_id(0)
    first = (bi == 0) & (t == 0)
    win_ref, wpool_ref, wout_ref = _convert_weights(first, layer, (win_ref, wpool_ref, wout_ref), outs[14:23])
    _emit_weights(first, (bi == pl.num_programs(0) - 1) & (t == nt - 1),
                  (win_ref, wpool_ref, wout_ref), outs[3:6], outs[23])

    @pl.when(t == 0)
    def _():
        s_ref[...] = jnp.zeros_like(s_ref)
        ext_ref[0:POOL_PAD, :] = jnp.zeros((POOL_PAD, POOL_IN), F32)

    _mixer_prompt_tile((x_ref, g_ref, win_ref, wpool_ref, pscale_ref, wout_ref, cos_ref, sin_ref, dmask_ref, xi_ref,
                        wk_ref, gs_ref, o_ref, z_ref, s_ref, ext_ref, ret_ref, pool_ref, h_ref, d_ref, v_ref),
                       t * tm, tm, chunk)

    @pl.when(t == nt - 1)
    def _():
        sret_ref[...] = s_ref[...].reshape(N_PAIRS * LANES, RET_V_DIM)
        spool_ref[...] = ext_ref[POOL_PAD - POOL_BUF:POOL_PAD, :]


def _mixer_prompt_tile(tile_refs, pos0, tm, chunk):
    (x_ref, g_ref, win_ref, wpool_ref, pscale_ref, wout_ref, cos_ref, sin_ref, dmask_ref, xi_ref,
     wk_ref, gs_ref, o_ref, z_ref, s_ref, ext_ref, ret_ref, pool_ref, h_ref, d_ref, v_ref) = tile_refs
    h_ref[...] = _rms(x_ref[...], g_ref[...]).astype(BF16)

    def project(off, width, act=None):
        y = _dot(h_ref[...], win_ref[:, off:off + width])
        z_ref[:, off:off + width] = y if act is None else act(y)

    project(Q_OFF, 2 * QK_W)
    v_ref[...] = _dot(h_ref[...], win_ref[:, V_OFF:V_OFF + V_W]).astype(BF16)
    project(U_OFF, POOL_IN)

    lane = lax.broadcasted_iota(jnp.int32, (1, LANES), 1)
    first_half = (lane % RET_QK_DIM) < (RET_QK_DIM // 2)
    _rotate_inplace(z_ref, cos_ref[...], sin_ref[...], first_half)

    ext_ref[POOL_PAD:POOL_PAD + tm, :] = z_ref[:, U_OFF:U_OFF + POOL_IN]
    pos = pos0 + lax.broadcasted_iota(jnp.int32, (tm, 1), 0)
    for g, w in enumerate(POOL_WINDOWS):
        cols = slice(g * POOL_GROUP_IN, (g + 1) * POOL_GROUP_IN)
        u = ext_ref[POOL_PAD:POOL_PAD + tm, cols]
        tot = u
        for j in range(1, w):
            tot = tot + ext_ref[POOL_PAD - j:POOL_PAD - j + tm, cols]
        cnt = jnp.minimum(pos + 1, w).astype(F32)
        d_ref[:, cols] = (tot / cnt - u).astype(BF16)
    tail = ext_ref[tm:tm + POOL_PAD, :]
    ext_ref[0:POOL_PAD, :] = tail

    def pool_matmul(g):
        ocols = slice(g * POOL_GROUP_OUT, (g + 1) * POOL_GROUP_OUT)
        y = _dot(d_ref[:, g * POOL_GROUP_IN:(g + 1) * POOL_GROUP_IN],
                 wpool_ref[g * POOL_GROUP_IN:(g + 1) * POOL_GROUP_IN, :])
        pool_ref[:, ocols] = y * pscale_ref[:, ocols]

    fill_width = 2 * LANES
    fillers = [functools.partial(project, off, fill_width, jax.nn.silu)
               for off in range(G_OFF, G_OFF + V_W, fill_width)]
    fillers += [functools.partial(project, off, fill_width, jax.nn.sigmoid)
                for off in range(GA_OFF, D_IN, fill_width)]
    fillers += [functools.partial(pool_matmul, g) for g in range(N_POOL_GROUPS)]

    head0 = (lane < RET_QK_DIM).astype(F32)
    head1 = 1.0 - head0

    n_chunks = tm // chunk
    stages = 3 * n_chunks

    def run_fillers(stage):
        for f in fillers[stage * len(fillers) // stages:(stage + 1) * len(fillers) // stages]:
            f()

    def pair_cols(off, pair, width):
        return slice(off + pair * width, off + (pair + 1) * width)

    for c in range(n_chunks):
        rows = slice(c * chunk, (c + 1) * chunk)
        scores = []
        for pair in range(N_PAIRS):
            qb = z_ref[rows, pair_cols(Q_OFF, pair, LANES)]
            kb = z_ref[rows, pair_cols(K_OFF, pair, LANES)]
            k_stack = jnp.concatenate([kb * head0, kb * head1], axis=0).astype(BF16)
            scores.append(lax.dot_general(qb.astype(BF16), k_stack, (((1,), (1,)), ((), ())),
                                          preferred_element_type=F32))
        run_fillers(3 * c)
        for pair in range(N_PAIRS):
            qx = z_ref[rows, pair_cols(Q_OFF, pair, LANES)] * xi_ref[pair]
            vb16 = v_ref[rows, pair_cols(0, pair, 2 * RET_V_DIM)]
            s16 = s_ref[pair].astype(BF16)
            for j, hmask in enumerate((head0, head1)):
                h = 2 * pair + j
                scj = scores[pair][:, j * chunk:(j + 1) * chunk] * dmask_ref[h]
                lhs = jnp.concatenate([scj, qx * hmask], axis=1).astype(BF16)
                rhs = jnp.concatenate([vb16[:, j * RET_V_DIM:(j + 1) * RET_V_DIM], s16], axis=0)
                o = _dot(lhs, rhs)
                o = o * lax.rsqrt(jnp.mean(o * o, axis=-1, keepdims=True) + EPS)
                ret_ref[rows, h * RET_V_DIM:(h + 1) * RET_V_DIM] = o
        run_fillers(3 * c + 1)
        for pair in range(N_PAIRS):
            kw = (z_ref[rows, pair_cols(K_OFF, pair, LANES)] * wk_ref[pair]).astype(BF16)
            vb16 = v_ref[rows, pair_cols(0, pair, 2 * RET_V_DIM)]
            upd = lax.dot_general(kw, vb16, (((0,), (0,)), ((), ())),
                                  preferred_element_type=F32)
            new_s = jnp.concatenate([upd[0:RET_QK_DIM, 0:RET_V_DIM],
                                     upd[RET_QK_DIM:, RET_V_DIM:]], axis=0)
            s_ref[pair] = s_ref[pair] * gs_ref[pair] + new_s
        run_fillers(3 * c + 2)

    n_halves = 2
    for r in range(n_halves):
        rows = slice(r * tm // n_halves, (r + 1) * tm // n_halves)
        ret = ret_ref[rows, :] * z_ref[rows, G_OFF:G_OFF + V_W]
        merged = (z_ref[rows, GA_OFF:GA_OFF + D_MODEL] * ret
                  + z_ref[rows, GB_OFF:GB_OFF + D_MODEL] * pool_ref[rows, :])
        o_ref[rows, :] = x_ref[rows, :] + _dot(merged.astype(BF16), wout_ref[...])


def _state_alias(prev, n_in, first_out):
    if prev is None:
        return [], [], {}
    specs = [pl.BlockSpec(memory_space=pl.ANY) for _ in prev]
    aliases = {n_in + k: first_out + k for k in range(len(prev))}
    return list(prev), specs, aliases


def _mixer_prompt(x, g, w_in, w_pool, pool_scale, w_out, tabs, layer, prev_states, *, tm):
    b, t_len, _ = x.shape
    depth = w_in.shape[0]
    chunk = RET_CHUNK
    assert t_len % tm == 0 and tm % chunk == 0
    cos, sin, dmask, xi, wk, gs = tabs
    row_spec = pl.BlockSpec((None, tm, D_MODEL), lambda i, j: (i, j, 0))
    tab_spec = pl.BlockSpec((tm, LANES), lambda i, j: (j, 0))
    in_specs = [
        row_spec, _const_spec(g.shape), ANY_SPEC, ANY_SPEC, _const_spec(pool_scale.shape),
        ANY_SPEC, tab_spec, tab_spec,
        _const_spec(dmask.shape), _const_spec(xi.shape), _const_spec(wk.shape), _const_spec(gs.shape),
    ]
    args = [x, g, w_in, w_pool, pool_scale, w_out, cos, sin, dmask, xi, wk, gs]
    extra, extra_specs, aliases = _state_alias(prev_states, len(args), 1)
    w_shapes = [w_in.shape[1:], w_pool.shape[1:], w_out.shape[1:]]
    out_specs = [
        row_spec,
        pl.BlockSpec((None, None, N_PAIRS * LANES, RET_V_DIM), lambda i, j: (layer, i, 0, 0)),
        pl.BlockSpec((None, None, POOL_BUF, POOL_IN), lambda i, j: (layer, i, 0, 0)),
    ] + [ANY_SPEC] * len(w_shapes)
    out_shape = [
        jax.ShapeDtypeStruct((b, t_len, D_MODEL), F32),
        jax.ShapeDtypeStruct((depth, b, N_PAIRS * LANES, RET_V_DIM), F32),
        jax.ShapeDtypeStruct((depth, b, POOL_BUF, POOL_IN), F32),
    ] + [jax.ShapeDtypeStruct(s, BF16) for s in w_shapes]
    scratch = [
        pltpu.VMEM((tm, D_IN), F32),
        pltpu.VMEM((N_PAIRS, LANES, RET_V_DIM), F32),
        pltpu.VMEM((POOL_PAD + tm, POOL_IN), F32),
        pltpu.VMEM((tm, V_W), F32),
        pltpu.VMEM((tm, D_MODEL), F32),
        pltpu.VMEM((tm, D_MODEL), BF16),
        pltpu.VMEM((tm, POOL_IN), BF16),
        pltpu.VMEM((tm, V_W), BF16),
    ]
    for s in w_shapes:
        scratch += _stage_scratch(s)
    scratch.append(pltpu.SemaphoreType.DMA((len(w_shapes),)))
    y, sret, spool, *w16 = pl.pallas_call(
        functools.partial(_mixer_prompt_body, tm=tm, chunk=chunk, n_alias=len(extra), layer=layer),
        grid=(b, t_len // tm),
        in_specs=in_specs + extra_specs,
        out_specs=out_specs,
        out_shape=out_shape,
        scratch_shapes=scratch,
        input_output_aliases=aliases,
        compiler_params=pltpu.CompilerParams(
            dimension_semantics=("arbitrary", "arbitrary"), vmem_limit_bytes=VMEM_LIMIT),
        name="mixer_prompt",
    )(*args, *extra)
    return y, (sret, spool), tuple(w16)


def _log_gamma():
    return np.log1p(-np.exp2(-5.0 - np.arange(N_RET_HEADS, dtype=np.float64)))


def _rope_tables(pos):
    half = RET_QK_DIM // 2
    inv = np.power(ROPE_BASE, -np.linspace(0.0, 1.0, half))
    ang = pos[:, None] * inv[None, :]
    cos, sin = np.cos(ang), np.sin(ang)
    cos128 = np.tile(cos, (1, LANES // half))
    sin128 = np.tile(np.concatenate([-sin, sin], axis=1), (1, LANES // RET_QK_DIM))
    return cos128.astype(np.float32), sin128.astype(np.float32)


def _pair_lanes(per_head):
    h, c = per_head.shape
    x = np.broadcast_to(per_head[:, :, None], (h, c, RET_QK_DIM))
    x = x.reshape(N_PAIRS, 2, c, RET_QK_DIM).transpose(0, 2, 1, 3).reshape(N_PAIRS, c, LANES)
    return x.astype(np.float32)


def _state_decay(chunk):
    g_c = np.exp(_log_gamma() * chunk)
    gs = np.broadcast_to(g_c[:, None, None], (N_RET_HEADS, RET_QK_DIM, RET_V_DIM))
    return gs.reshape(N_PAIRS, LANES, RET_V_DIM).astype(np.float32)


def _decay_tables(chunk):
    lg = _log_gamma()
    i = np.arange(chunk, dtype=np.float64)
    dist = i[:, None] - i[None, :]
    dmask = np.where(dist[None] >= 0, np.exp(lg[:, None, None] * np.maximum(dist, 0.0)[None]), 0.0)
    xi = np.exp(lg[:, None] * (i[None, :] + 1.0))
    wk = np.exp(lg[:, None] * (chunk - 1.0 - i[None, :]))
    return dmask.astype(np.float32), _pair_lanes(xi), _pair_lanes(wk), _state_decay(chunk)


def _mixer_sample_body(*refs, bb, ts, past, n_alias, layer):
    (x_ref, g_ref, win_ref, wpool_ref, pscale_ref, wout_ref, cos_ref, sin_ref,
     xi_ref, wk_ref, lag_ref, gs_ref, ind_ref, s0_ref, pb_ref) = refs[:15]
    (o_ref, snew_ref, pnew_ref, z_ref, inter_ref, ublk_ref, pblk_ref) = refs[15 + n_alias:]
    g_ref, pscale_ref = g_ref.at[pl.ds(layer, 1)], pscale_ref.at[pl.ds(layer, 1)]
    rows = bb * ts
    x = x_ref[...]
    _project_in(x, g_ref, win_ref, z_ref)
    lane = lax.broadcasted_iota(jnp.int32, (1, LANES), 1)
    first_half = (lane % RET_QK_DIM) < (RET_QK_DIM // 2)
    _rotate_inplace(z_ref, cos_ref[...], sin_ref[...], first_half)

    q = z_ref[:, Q_OFF:Q_OFF + QK_W]
    k = z_ref[:, K_OFF:K_OFF + QK_W]
    v = z_ref[:, V_OFF:V_OFF + V_W]
    step = lax.broadcasted_iota(jnp.int32, (rows, 1), 0) % ts
    intra = None
    for d in range(ts):
        kd = k if d == 0 else pltpu.roll(k, d, 0)
        vd = v if d == 0 else pltpu.roll(v, d, 0)
        prod = jnp.where(step >= d, q * kd, 0.0).astype(BF16)
        score = _dot(prod, ind_ref[...])
        term = score * lag_ref[d:d + 1, :] * jnp.where(step >= d, vd, 0.0)
        intra = term if intra is None else intra + term

    row8 = lax.broadcasted_iota(jnp.int32, (2 * ts, 1), 0)
    top = (lax.broadcasted_iota(jnp.int32, (LANES, 1), 0) < RET_QK_DIM).astype(F32)
    bot = 1.0 - top

    def seq_pair(p, carry):
        r0 = pl.multiple_of(p * 2 * ts, 2 * ts)
        rsl = pl.ds(r0, 2 * ts)
        for pair in range(N_PAIRS):
            lanes = slice(pair * LANES, (pair + 1) * LANES)
            q8 = (z_ref[rsl, Q_OFF + pair * LANES:Q_OFF + (pair + 1) * LANES] * xi_ref[:, lanes]).astype(BF16)
            kw8 = z_ref[rsl, K_OFF + pair * LANES:K_OFF + (pair + 1) * LANES] * wk_ref[:, lanes]
            v8 = z_ref[rsl, V_OFF + pair * 2 * RET_V_DIM:V_OFF + (pair + 1) * 2 * RET_V_DIM].astype(BF16)
            outs = []
            for j in range(2):
                b = 2 * p + j
                s_pair = s0_ref[b, lanes, :]
                bd = jnp.concatenate([s_pair * top, s_pair * bot], axis=1).astype(BF16)
                outs.append(_dot(q8, bd))
                mine = (row8 >= j * ts) & (row8 < (j + 1) * ts)
                kwj = jnp.where(mine, kw8, 0.0).astype(BF16)
                upd = lax.dot_general(kwj, v8, (((0,), (0,)), ((), ())), preferred_element_type=F32)
                new_s = jnp.concatenate([upd[0:RET_QK_DIM, 0:RET_V_DIM],
                                         upd[RET_QK_DIM:, RET_V_DIM:]], axis=0)
                snew_ref[b, lanes, :] = s_pair * gs_ref[pair] + new_s
            inter_ref[rsl, pair * 2 * RET_V_DIM:(pair + 1) * 2 * RET_V_DIM] = jnp.where(
                row8 < ts, outs[0], outs[1])
        return carry

    lax.fori_loop(0, bb // 2, seq_pair, 0, unroll=8)

    keep = POOL_BUF - ts
    pnew_ref[0:keep] = pb_ref[ts:]
    for g, w in enumerate(POOL_WINDOWS):
        ublk_ref[g] = z_ref[:, U_OFF + g * POOL_GROUP_IN:U_OFF + (g + 1) * POOL_GROUP_IN]
        u_steps = [ublk_ref[g, pl.ds(t, bb, stride=ts), :] for t in range(ts)]
        for t in range(ts):
            pnew_ref[keep + t, :, g * POOL_GROUP_IN:(g + 1) * POOL_GROUP_IN] = u_steps[t]
        ds_ = []
        for t in range(ts):
            tot = None
            for j in range(w):
                idx = POOL_BUF + t - j
                if idx >= POOL_BUF:
                    term = u_steps[idx - POOL_BUF]
                else:
                    term = pb_ref[idx, :, g * POOL_GROUP_IN:(g + 1) * POOL_GROUP_IN]
                tot = term if tot is None else tot + term
            cnt = float(min(past + t + 1, w))
            ds_.append(tot / cnt - u_steps[t])
        y = _dot(jnp.concatenate(ds_, axis=0).astype(BF16),
                 wpool_ref[g * POOL_GROUP_IN:(g + 1) * POOL_GROUP_IN, :])
        y = y * pscale_ref[:, g * POOL_GROUP_OUT:(g + 1) * POOL_GROUP_OUT]
        for half in range(POOL_GROUP_OUT // LANES):
            blk = g * (POOL_GROUP_OUT // LANES) + half
            for t in range(ts):
                pblk_ref[blk, pl.ds(t, bb, stride=ts), :] = y[t * bb:(t + 1) * bb, half * LANES:(half + 1) * LANES]
    pool = jnp.concatenate([pblk_ref[blk] for blk in range(D_MODEL // LANES)], axis=1)

    o_all = intra + inter_ref[...]
    rets = []
    for h in range(N_RET_HEADS):
        hc = slice(h * RET_V_DIM, (h + 1) * RET_V_DIM)
        rets.append(_swish_gate_norm(o_all[:, hc], z_ref[:, G_OFF + h * RET_V_DIM:G_OFF + (h + 1) * RET_V_DIM]))
    ret = jnp.concatenate(rets, axis=1)
    o_ref[...] = _merge_out(x, z_ref, ret, pool, wout_ref)


def _sample_tables(ts, past, rows):
    lg = _log_gamma()
    i = np.arange(ts, dtype=np.float64)
    lag = np.exp(lg[:, None] * i[None, :])
    lag = np.repeat(lag.T, RET_V_DIM, axis=1)
    xi = np.exp(lg[:, None] * (i[None, :] + 1.0))
    wk = np.exp(lg[:, None] * (ts - 1.0 - i[None, :]))
    expand = lambda a: np.tile(np.repeat(a.T, RET_QK_DIM, axis=1), (2, 1)).astype(np.float32)
    cos, sin = _rope_tables(past + np.arange(ts, dtype=np.float64))
    reps = rows // ts
    head_of_k = np.arange(QK_W) // RET_QK_DIM
    head_of_v = np.arange(V_W) // RET_V_DIM
    ind = (head_of_k[:, None] == head_of_v[None, :]).astype(BF16)
    return (np.tile(cos, (reps, 1)), np.tile(sin, (reps, 1)), expand(xi), expand(wk), lag.astype(np.float32),
            _state_decay(ts), ind)


def _mixer_sample(x2d, s0, pbuf, g, w_in, w_pool, pool_scale, w_out, tabs, layer, prev_states, *, bb, ts, past):
    n = x2d.shape[0]
    depth = s0.shape[0]
    nb = n // ts
    rows = bb * ts
    assert nb % bb == 0 and bb % 2 == 0 and (2 * ts) % 8 == 0 and ts <= POOL_BUF
    cos, sin, xi, wk, lag, gs, ind = tabs
    row_spec = pl.BlockSpec((rows, D_MODEL), lambda i: (i, 0))
    state_spec = pl.BlockSpec((None, bb, N_PAIRS * LANES, RET_V_DIM), lambda i: (layer, i, 0, 0))
    pool_spec = pl.BlockSpec((None, POOL_BUF, bb, POOL_IN), lambda i: (layer, 0, i, 0))
    in_specs = [
        row_spec, _const_spec(g.shape), _resident_spec(w_in.shape),
        _resident_spec(w_pool.shape), _const_spec(pool_scale.shape),
        _resident_spec(w_out.shape), _const_spec(cos.shape), _const_spec(sin.shape),
        _const_spec(xi.shape), _const_spec(wk.shape), _const_spec(lag.shape), _const_spec(gs.shape),
        _const_spec(ind.shape), state_spec, pool_spec,
    ]
    args = [x2d, g, w_in, w_pool, pool_scale, w_out, cos, sin, xi, wk, lag, gs, ind, s0, pbuf]
    extra, extra_specs, aliases = _state_alias(prev_states, len(args), 1)
    out_specs = [row_spec, state_spec, pool_spec]
    out_shape = [
        jax.ShapeDtypeStruct((n, D_MODEL), F32),
        jax.ShapeDtypeStruct((depth, nb, N_PAIRS * LANES, RET_V_DIM), F32),
        jax.ShapeDtypeStruct((depth, POOL_BUF, nb, POOL_IN), F32),
    ]
    scratch = [
        pltpu.VMEM((rows, D_IN), F32),
        pltpu.VMEM((rows, V_W), F32),
        pltpu.VMEM((N_POOL_GROUPS, rows, LANES), F32),
        pltpu.VMEM((D_MODEL // LANES, rows, LANES), F32),
    ]
    y, snew, pnew = pl.pallas_call(
        functools.partial(_mixer_sample_body, bb=bb, ts=ts, past=past, n_alias=len(extra), layer=layer),
        grid=(nb // bb,),
        in_specs=in_specs + extra_specs,
        out_specs=out_specs,
        out_shape=out_shape,
        scratch_shapes=scratch,
        input_output_aliases=aliases,
        compiler_params=pltpu.CompilerParams(
            dimension_semantics=("arbitrary",), vmem_limit_bytes=VMEM_LIMIT),
        name="mixer_sample",
    )(*args, *extra)
    return y, (snew, pnew)


def kernel(x_prompt, x_sample, state_ret, state_pool, norm_ffn1, w_ffn1_in, w_ffn1_out, norm_mix,
           w_in, w_pool, pool_scale, w_out, norm_ffn2, w_ffn2_in, w_ffn2_out, norm_final):
    depth = w_in.shape[0]
    bp, tp, _ = x_prompt.shape
    bs, ts, _ = x_sample.shape
    wp = w_pool.reshape(depth, N_POOL_GROUPS * POOL_GROUP_IN, POOL_GROUP_OUT)

    n1, nm, n2, ps = norm_ffn1, norm_mix, norm_ffn2, pool_scale
    gf = norm_final.reshape(1, D_MODEL)
    last = depth - 1

    tm = 512
    tm_ffn = 1024
    tabs_p = _rope_tables(np.arange(tp, dtype=np.float64)) + _decay_tables(RET_CHUNK)

    bb = 32
    tabs_s = _sample_tables(ts, PAST_LEN, bb * ts)
    s0_all = state_ret.reshape(depth, bs, N_PAIRS * LANES, RET_V_DIM)
    pb_all = jnp.swapaxes(state_pool, 1, 2)

    x = x_prompt.reshape(bp * tp, D_MODEL)
    xs = x_sample.reshape(bs * ts, D_MODEL)
    states_p = states_s = None
    for l in range(depth):
        x, xs = _ffn(x, xs, n1, w_ffn1_in, w_ffn1_out, l, tm=tm_ffn)
        x, states_p, (wi16, wp16, wo16) = _mixer_prompt(x.reshape(bp, tp, D_MODEL), nm, w_in, wp, ps, w_out,
                                                        tabs_p, l, states_p, tm=tm)
        xs, states_s = _mixer_sample(xs, s0_all, pb_all, nm, wi16, wp16, ps, wo16, tabs_s, l, states_s,
                                     bb=bb, ts=ts, past=PAST_LEN)
        x, xs = _ffn(x.reshape(bp * tp, D_MODEL), xs, n2, w_ffn2_in, w_ffn2_out, l,
                     gf if l == last else None, tm=tm_ffn)
    y_prompt = x.reshape(bp, tp, D_MODEL)
    y_sample = xs.reshape(bs, ts, D_MODEL)
    state_ret_prompt = states_p[0].reshape(depth, bp, N_RET_HEADS, RET_QK_DIM, RET_V_DIM)
    state_pool_prompt = states_p[1]
    state_ret_sample = states_s[0].reshape(depth, bs, N_RET_HEADS, RET_QK_DIM, RET_V_DIM)
    state_pool_sample = jnp.swapaxes(states_s[1], 1, 2)
    return (y_prompt, y_sample, state_ret_prompt, state_ret_sample, state_pool_prompt, state_pool_sample)
```

```python
import functools

import numpy as np
import jax
import jax.numpy as jnp
from jax import lax
from jax.experimental import pallas as pl
from jax.experimental.pallas import tpu as pltpu

D_MODEL = 1024
N_RET_HEADS = 8
RET_QK_DIM = 64
RET_V_DIM = 128
RET_CHUNK = 128
ROPE_BASE = 10000.0
QK_W = N_RET_HEADS * RET_QK_DIM
V_W = N_RET_HEADS * RET_V_DIM
POOL_WINDOWS = (2, 4, 8, 16)
N_POOL_GROUPS = 4
POOL_IN = 512
POOL_GROUP_IN = 128
POOL_GROUP_OUT = 256
POOL_BUF = 15
D_IN = 2 * QK_W + 2 * V_W + POOL_IN + 2 * D_MODEL
D_FF = 2816
EPS = 1e-6
PAST_LEN = 16384

Q_OFF = 0
K_OFF = QK_W
V_OFF = 2 * QK_W
G_OFF = 2 * QK_W + V_W
U_OFF = 2 * QK_W + 2 * V_W
GA_OFF = U_OFF + POOL_IN
GB_OFF = GA_OFF + D_MODEL

LANES = 128
N_PAIRS = N_RET_HEADS // 2
POOL_PAD = 16
VMEM_LIMIT = 60 * 1024 * 1024

F32 = jnp.float32
BF16 = jnp.bfloat16


def _dot(a, b):
    return jnp.dot(a, b, preferred_element_type=F32)


def _rms(x, g):
    return x * lax.rsqrt(jnp.mean(x * x, axis=-1, keepdims=True) + EPS) * g


def _const_spec(shape):
    nd = len(shape)
    return pl.BlockSpec(shape, lambda *_: (0,) * nd)


FF_CHUNKS = ((0, 1024), (1024, 1024), (2048, 768))


STAGE_BUFS = 6
STAGE_BYTES = 3 * 256 * 1024


def _stage_rows(shape):
    r, c = shape
    fits = [n for n in range(16, r + 1, 16) if r % n == 0 and n * c * 4 <= STAGE_BYTES]
    return max(fits)


def _stage_scratch(shape):
    return [pltpu.VMEM(shape, BF16), pltpu.VMEM((STAGE_BUFS, _stage_rows(shape), shape[1]), F32),
            pltpu.SemaphoreType.DMA((STAGE_BUFS,))]


def _fetch_cast(src, dst, stage, sems):
    nbuf, rows, _ = stage.shape
    n = src.shape[0] // rows

    def copy(k):
        return pltpu.make_async_copy(src.at[pl.ds(k * rows, rows), :], stage.at[k % nbuf], sems.at[k % nbuf])

    for k in range(min(nbuf, n)):
        copy(k).start(priority=k % 2)
    for k in range(n):
        copy(k).wait()
        dst[k * rows:(k + 1) * rows, :] = stage[k % nbuf].astype(BF16)
        if k + nbuf < n:
            copy(k + nbuf).start(priority=(k + nbuf) % 2)


def _convert_weights(first, layer, srcs, scratch):
    @pl.when(first)
    def _():
        for k, src in enumerate(srcs):
            _fetch_cast(src.at[layer], *scratch[3 * k:3 * k + 3])

    return scratch[0::3]


def _emit_weights(first, last, srcs, outs, sems):
    copies = [pltpu.make_async_copy(s, o, sems.at[k]) for k, (s, o) in enumerate(zip(srcs, outs))]

    @pl.when(first)
    def _():
        for c in copies:
            c.start()

    @pl.when(last)
    def _():
        for c in copies:
            c.wait()


def _ffn_body(*refs, final, layer):
    n_in = 6 if final else 5
    x_ref, xs_ref, g_ref, win_hbm, wout_hbm = refs[:5]
    gf_ref = refs[5] if final else None
    o_ref, os_ref = refs[n_in:n_in + 2]
    g_ref = g_ref.at[pl.ds(layer, 1)]
    i = pl.program_id(0)
    win_ref, wout_ref = _convert_weights(i == 0, layer, (win_hbm, wout_hbm), refs[n_in + 2:])

    def ffn_rows(x):
        r = lax.rsqrt(jnp.mean(x * x, axis=-1, keepdims=True) + EPS)
        h = (x * g_ref[...]).astype(BF16)
        acc = None
        for off, width in FF_CHUNKS:
            gate = _dot(h, win_ref[:, off:off + width]) * r
            up = _dot(h, win_ref[:, D_FF + off:D_FF + off + width]) * r
            a = (gate * jax.nn.sigmoid(gate) * up).astype(BF16)
            part = _dot(a, wout_ref[off:off + width, :])
            acc = part if acc is None else acc + part
        y = x + 0.5 * acc
        return _rms(y, gf_ref[...]) if final else y

    o_ref[...] = ffn_rows(x_ref[...])

    @pl.when(i == pl.num_programs(0) - 1)
    def _():
        os_ref[...] = ffn_rows(xs_ref[...])


def _resident_spec(shape):
    nd = len(shape)
    return pl.BlockSpec(tuple(shape), lambda *_: (0,) * nd, pipeline_mode=pl.Buffered(1))


ANY_SPEC = pl.BlockSpec(memory_space=pl.ANY)


def _ffn(x2d, xs2d, g, w_in, w_out, layer, g_final=None, *, tm):
    n = x2d.shape[0]
    assert n % tm == 0
    final = g_final is not None
    row_spec = pl.BlockSpec((tm, D_MODEL), lambda i: (i, 0))
    in_specs = [row_spec, _resident_spec(xs2d.shape), _const_spec(g.shape), ANY_SPEC, ANY_SPEC]
    args = [x2d, xs2d, g, w_in, w_out]
    if final:
        in_specs.append(_const_spec((1, D_MODEL)))
        args.append(g_final.reshape(1, D_MODEL))
    scratch = _stage_scratch(w_in.shape[1:]) + _stage_scratch(w_out.shape[1:])
    return pl.pallas_call(
        functools.partial(_ffn_body, final=final, layer=layer),
        grid=(n // tm,),
        in_specs=in_specs,
        out_specs=[row_spec, _resident_spec(xs2d.shape)],
        out_shape=[jax.ShapeDtypeStruct(x2d.shape, F32), jax.ShapeDtypeStruct(xs2d.shape, F32)],
        scratch_shapes=scratch,
        compiler_params=pltpu.CompilerParams(
            dimension_semantics=("arbitrary",), vmem_limit_bytes=VMEM_LIMIT),
        name="ffn_final" if final else "ffn",
    )(*args)


IN_CHUNKS = tuple((o, 1024) for o in range(0, 5120, 1024)) + ((5120, 512),)


def _project_in(x, g_ref, win_ref, z_ref):
    h = _rms(x, g_ref[...]).astype(BF16)
    for off, width in IN_CHUNKS:
        z_ref[:, off:off + width] = _dot(h, win_ref[:, off:off + width])


def _rotate_inplace(z_ref, cos, sin_signed, first_half):
    for blk in range(2 * QK_W // LANES):
        cols = slice(blk * LANES, (blk + 1) * LANES)
        xb = z_ref[:, cols]
        partner = jnp.where(first_half, pltpu.roll(xb, LANES - 32, 1), pltpu.roll(xb, 32, 1))
        r = xb * cos + partner * sin_signed
        if blk >= QK_W // LANES:
            r = r * (RET_QK_DIM ** -0.5)
        z_ref[:, cols] = r


def _merge_out(x, z_ref, ret, pool, wout_ref):
    ga = z_ref[:, GA_OFF:GA_OFF + D_MODEL]
    gb = z_ref[:, GB_OFF:GB_OFF + D_MODEL]
    merged = jax.nn.sigmoid(ga) * ret + jax.nn.sigmoid(gb) * pool
    return x + _dot(merged.astype(BF16), wout_ref[...])


def _swish_gate_norm(o, gate):
    o = o * lax.rsqrt(jnp.mean(o * o, axis=-1, keepdims=True) + EPS)
    return o * (gate * jax.nn.sigmoid(gate))


def _mixer_prompt_body(*refs, tm, chunk, n_alias, layer):
    (x_ref, g_ref, win_ref, wpool_ref, pscale_ref, wout_ref, cos_ref, sin_ref,
     dmask_ref, xi_ref, wk_ref, gs_ref) = refs[:12]
    outs = refs[12 + n_alias:]
    o_ref, sret_ref, spool_ref = outs[:3]
    z_ref, s_ref, ext_ref, ret_ref, pool_ref, h_ref, d_ref, v_ref = outs[6:14]
    g_ref, pscale_ref = g_ref.at[pl.ds(layer, 1)], pscale_ref.at[pl.ds(layer, 1)]
    t = pl.program_id(1)
    nt = pl.num_programs(1)
    bi = pl.program_id(0)
    first = (bi == 0) & (t == 0)
    win_ref, wpool_ref, wout_ref = _convert_weights(first, layer, (win_ref, wpool_ref, wout_ref), outs[14:23])
    _emit_weights(first, (bi == pl.num_programs(0) - 1) & (t == nt - 1),
                  (win_ref, wpool_ref, wout_ref), outs[3:6], outs[23])

    @pl.when(t == 0)
    def _():
        s_ref[...] = jnp.zeros_like(s_ref)
        ext_ref[0:POOL_PAD, :] = jnp.zeros((POOL_PAD, POOL_IN), F32)

    _mixer_prompt_tile((x_ref, g_ref, win_ref, wpool_ref, pscale_ref, wout_ref, cos_ref, sin_ref, dmask_ref, xi_ref,
                        wk_ref, gs_ref, o_ref, z_ref, s_ref, ext_ref, ret_ref, pool_ref, h_ref, d_ref, v_ref),
                       t * tm, tm, chunk)

    @pl.when(t == nt - 1)
    def _():
        sret_ref[...] = s_ref[...].reshape(N_PAIRS * LANES, RET_V_DIM)
        spool_ref[...] = ext_ref[POOL_PAD - POOL_BUF:POOL_PAD, :]


def _mixer_prompt_tile(tile_refs, pos0, tm, chunk):
    (x_ref, g_ref, win_ref, wpool_ref, pscale_ref, wout_ref, cos_ref, sin_ref, dmask_ref, xi_ref,
     wk_ref, gs_ref, o_ref, z_ref, s_ref, ext_ref, ret_ref, pool_ref, h_ref, d_ref, v_ref) = tile_refs
    h_ref[...] = _rms(x_ref[...], g_ref[...]).astype(BF16)

    def project(off, width, act=None):
        y = _dot(h_ref[...], win_ref[:, off:off + width])
        z_ref[:, off:off + width] = y if act is None else act(y)

    project(Q_OFF, 2 * QK_W)
    v_ref[...] = _dot(h_ref[...], win_ref[:, V_OFF:V_OFF + V_W]).astype(BF16)
    project(U_OFF, POOL_IN)

    lane = lax.broadcasted_iota(jnp.int32, (1, LANES), 1)
    first_half = (lane % RET_QK_DIM) < (RET_QK_DIM // 2)
    _rotate_inplace(z_ref, cos_ref[...], sin_ref[...], first_half)

    ext_ref[POOL_PAD:POOL_PAD + tm, :] = z_ref[:, U_OFF:U_OFF + POOL_IN]
    pos = pos0 + lax.broadcasted_iota(jnp.int32, (tm, 1), 0)
    for g, w in enumerate(POOL_WINDOWS):
        cols = slice(g * POOL_GROUP_IN, (g + 1) * POOL_GROUP_IN)
        u = ext_ref[POOL_PAD:POOL_PAD + tm, cols]
        tot = u
        for j in range(1, w):
            tot = tot + ext_ref[POOL_PAD - j:POOL_PAD - j + tm, cols]
        cnt = jnp.minimum(pos + 1, w).astype(F32)
        d_ref[:, cols] = (tot / cnt - u).astype(BF16)
    tail = ext_ref[tm:tm + POOL_PAD, :]
    ext_ref[0:POOL_PAD, :] = tail

    def pool_matmul(g):
        ocols = slice(g * POOL_GROUP_OUT, (g + 1) * POOL_GROUP_OUT)
        y = _dot(d_ref[:, g * POOL_GROUP_IN:(g + 1) * POOL_GROUP_IN],
                 wpool_ref[g * POOL_GROUP_IN:(g + 1) * POOL_GROUP_IN, :])
        pool_ref[:, ocols] = y * pscale_ref[:, ocols]

    fill_width = 2 * LANES
    fillers = [functools.partial(project, off, fill_width, jax.nn.silu)
               for off in range(G_OFF, G_OFF + V_W, fill_width)]
    fillers += [functools.partial(project, off, fill_width, jax.nn.sigmoid)
                for off in range(GA_OFF, D_IN, fill_width)]
    fillers += [functools.partial(pool_matmul, g) for g in range(N_POOL_GROUPS)]

    head0 = (lane < RET_QK_DIM).astype(F32)
    head1 = 1.0 - head0

    n_chunks = tm // chunk
    stages = 3 * n_chunks

    def run_fillers(stage):
        for f in fillers[stage * len(fillers) // stages:(stage + 1) * len(fillers) // stages]:
            f()

    def pair_cols(off, pair, width):
        return slice(off + pair * width, off + (pair + 1) * width)

    for c in range(n_chunks):
        rows = slice(c * chunk, (c + 1) * chunk)
        scores = []
        for pair in range(N_PAIRS):
            qb = z_ref[rows, pair_cols(Q_OFF, pair, LANES)]
            kb = z_ref[rows, pair_cols(K_OFF, pair, LANES)]
            k_stack = jnp.concatenate([kb * head0, kb * head1], axis=0).astype(BF16)
            scores.append(lax.dot_general(qb.astype(BF16), k_stack, (((1,), (1,)), ((), ())),
                                          preferred_element_type=F32))
        run_fillers(3 * c)
        for pair in range(N_PAIRS):
            qx = z_ref[rows, pair_cols(Q_OFF, pair, LANES)] * xi_ref[pair]
            vb16 = v_ref[rows, pair_cols(0, pair, 2 * RET_V_DIM)]
            s16 = s_ref[pair].astype(BF16)
            for j, hmask in enumerate((head0, head1)):
                h = 2 * pair + j
                scj = scores[pair][:, j * chunk:(j + 1) * chunk] * dmask_ref[h]
                lhs = jnp.concatenate([scj, qx * hmask], axis=1).astype(BF16)
                rhs = jnp.concatenate([vb16[:, j * RET_V_DIM:(j + 1) * RET_V_DIM], s16], axis=0)
                o = _dot(lhs, rhs)
                o = o * lax.rsqrt(jnp.mean(o * o, axis=-1, keepdims=True) + EPS)
                ret_ref[rows, h * RET_V_DIM:(h + 1) * RET_V_DIM] = o
        run_fillers(3 * c + 1)
        for pair in range(N_PAIRS):
            kw = (z_ref[rows, pair_cols(K_OFF, pair, LANES)] * wk_ref[pair]).astype(BF16)
            vb16 = v_ref[rows, pair_cols(0, pair, 2 * RET_V_DIM)]
            upd = lax.dot_general(kw, vb16, (((0,), (0,)), ((), ())),
                                  preferred_element_type=F32)
            new_s = jnp.concatenate([upd[0:RET_QK_DIM, 0:RET_V_DIM],
                                     upd[RET_QK_DIM:, RET_V_DIM:]], axis=0)
            s_ref[pair] = s_ref[pair] * gs_ref[pair] + new_s
        run_fillers(3 * c + 2)

    n_halves = 2
    for r in range(n_halves):
        rows = slice(r * tm // n_halves, (r + 1) * tm // n_halves)
        ret = ret_ref[rows, :] * z_ref[rows, G_OFF:G_OFF + V_W]
        merged = (z_ref[rows, GA_OFF:GA_OFF + D_MODEL] * ret
                  + z_ref[rows, GB_OFF:GB_OFF + D_MODEL] * pool_ref[rows, :])
        o_ref[rows, :] = x_ref[rows, :] + _dot(merged.astype(BF16), wout_ref[...])


def _state_alias(prev, n_in, first_out):
    if prev is None:
        return [], [], {}
    specs = [pl.BlockSpec(memory_space=pl.ANY) for _ in prev]
    aliases = {n_in + k: first_out + k for k in range(len(prev))}
    return list(prev), specs, aliases


def _mixer_prompt(x, g, w_in, w_pool, pool_scale, w_out, tabs, layer, prev_states, *, tm):
    b, t_len, _ = x.shape
    depth = w_in.shape[0]
    chunk = RET_CHUNK
    assert t_len % tm == 0 and tm % chunk == 0
    cos, sin, dmask, xi, wk, gs = tabs
    row_spec = pl.BlockSpec((None, tm, D_MODEL), lambda i, j: (i, j, 0))
    tab_spec = pl.BlockSpec((tm, LANES), lambda i, j: (j, 0))
    in_specs = [
        row_spec, _const_spec(g.shape), ANY_SPEC, ANY_SPEC, _const_spec(pool_scale.shape),
        ANY_SPEC, tab_spec, tab_spec,
        _const_spec(dmask.shape), _const_spec(xi.shape), _const_spec(wk.shape), _const_spec(gs.shape),
    ]
    args = [x, g, w_in, w_pool, pool_scale, w_out, cos, sin, dmask, xi, wk, gs]
    extra, extra_specs, aliases = _state_alias(prev_states, len(args), 1)
    w_shapes = [w_in.shape[1:], w_pool.shape[1:], w_out.shape[1:]]
    out_specs = [
        row_spec,
        pl.BlockSpec((None, None, N_PAIRS * LANES, RET_V_DIM), lambda i, j: (layer, i, 0, 0)),
        pl.BlockSpec((None, None, POOL_BUF, POOL_IN), lambda i, j: (layer, i, 0, 0)),
    ] + [ANY_SPEC] * len(w_shapes)
    out_shape = [
        jax.ShapeDtypeStruct((b, t_len, D_MODEL), F32),
        jax.ShapeDtypeStruct((depth, b, N_PAIRS * LANES, RET_V_DIM), F32),
        jax.ShapeDtypeStruct((depth, b, POOL_BUF, POOL_IN), F32),
    ] + [jax.ShapeDtypeStruct(s, BF16) for s in w_shapes]
    scratch = [
        pltpu.VMEM((tm, D_IN), F32),
        pltpu.VMEM((N_PAIRS, LANES, RET_V_DIM), F32),
        pltpu.VMEM((POOL_PAD + tm, POOL_IN), F32),
        pltpu.VMEM((tm, V_W), F32),
        pltpu.VMEM((tm, D_MODEL), F32),
        pltpu.VMEM((tm, D_MODEL), BF16),
        pltpu.VMEM((tm, POOL_IN), BF16),
        pltpu.VMEM((tm, V_W), BF16),
    ]
    for s in w_shapes:
        scratch += _stage_scratch(s)
    scratch.append(pltpu.SemaphoreType.DMA((len(w_shapes),)))
    y, sret, spool, *w16 = pl.pallas_call(
        functools.partial(_mixer_prompt_body, tm=tm, chunk=chunk, n_alias=len(extra), layer=layer),
        grid=(b, t_len // tm),
        in_specs=in_specs + extra_specs,
        out_specs=out_specs,
        out_shape=out_shape,
        scratch_shapes=scratch,
        input_output_aliases=aliases,
        compiler_params=pltpu.CompilerParams(
            dimension_semantics=("arbitrary", "arbitrary"), vmem_limit_bytes=VMEM_LIMIT),
        name="mixer_prompt",
    )(*args, *extra)
    return y, (sret, spool), tuple(w16)


def _log_gamma():
    return np.log1p(-np.exp2(-5.0 - np.arange(N_RET_HEADS, dtype=np.float64)))


def _rope_tables(pos):
    half = RET_QK_DIM // 2
    inv = np.power(ROPE_BASE, -np.linspace(0.0, 1.0, half))
    ang = pos[:, None] * inv[None, :]
    cos, sin = np.cos(ang), np.sin(ang)
    cos128 = np.tile(cos, (1, LANES // half))
    sin128 = np.tile(np.concatenate([-sin, sin], axis=1), (1, LANES // RET_QK_DIM))
    return cos128.astype(np.float32), sin128.astype(np.float32)


def _pair_lanes(per_head):
    h, c = per_head.shape
    x = np.broadcast_to(per_head[:, :, None], (h, c, RET_QK_DIM))
    x = x.reshape(N_PAIRS, 2, c, RET_QK_DIM).transpose(0, 2, 1, 3).reshape(N_PAIRS, c, LANES)
    return x.astype(np.float32)


def _state_decay(chunk):
    g_c = np.exp(_log_gamma() * chunk)
    gs = np.broadcast_to(g_c[:, None, None], (N_RET_HEADS, RET_QK_DIM, RET_V_DIM))
    return gs.reshape(N_PAIRS, LANES, RET_V_DIM).astype(np.float32)


def _decay_tables(chunk):
    lg = _log_gamma()
    i = np.arange(chunk, dtype=np.float64)
    dist = i[:, None] - i[None, :]
    dmask = np.where(dist[None] >= 0, np.exp(lg[:, None, None] * np.maximum(dist, 0.0)[None]), 0.0)
    xi = np.exp(lg[:, None] * (i[None, :] + 1.0))
    wk = np.exp(lg[:, None] * (chunk - 1.0 - i[None, :]))
    return dmask.astype(np.float32), _pair_lanes(xi), _pair_lanes(wk), _state_decay(chunk)


def _mixer_sample_body(*refs, bb, ts, past, n_alias, layer):
    (x_ref, g_ref, win_ref, wpool_ref, pscale_ref, wout_ref, cos_ref, sin_ref,
     xi_ref, wk_ref, lag_ref, gs_ref, ind_ref, s0_ref, pb_ref) = refs[:15]
    (o_ref, snew_ref, pnew_ref, z_ref, inter_ref, ublk_ref, pblk_ref) = refs[15 + n_alias:]
    g_ref, pscale_ref = g_ref.at[pl.ds(layer, 1)], pscale_ref.at[pl.ds(layer, 1)]
    rows = bb * ts
    x = x_ref[...]
    _project_in(x, g_ref, win_ref, z_ref)
    lane = lax.broadcasted_iota(jnp.int32, (1, LANES), 1)
    first_half = (lane % RET_QK_DIM) < (RET_QK_DIM // 2)
    _rotate_inplace(z_ref, cos_ref[...], sin_ref[...], first_half)

    q = z_ref[:, Q_OFF:Q_OFF + QK_W]
    k = z_ref[:, K_OFF:K_OFF + QK_W]
    v = z_ref[:, V_OFF:V_OFF + V_W]
    step = lax.broadcasted_iota(jnp.int32, (rows, 1), 0) % ts
    intra = None
    for d in range(ts):
        kd = k if d == 0 else pltpu.roll(k, d, 0)
        vd = v if d == 0 else pltpu.roll(v, d, 0)
        prod = jnp.where(step >= d, q * kd, 0.0).astype(BF16)
        score = _dot(prod, ind_ref[...])
        term = score * lag_ref[d:d + 1, :] * jnp.where(step >= d, vd, 0.0)
        intra = term if intra is None else intra + term

    row8 = lax.broadcasted_iota(jnp.int32, (2 * ts, 1), 0)
    top = (lax.broadcasted_iota(jnp.int32, (LANES, 1), 0) < RET_QK_DIM).astype(F32)
    bot = 1.0 - top

    def seq_pair(p, carry):
        r0 = pl.multiple_of(p * 2 * ts, 2 * ts)
        rsl = pl.ds(r0, 2 * ts)
        for pair in range(N_PAIRS):
            lanes = slice(pair * LANES, (pair + 1) * LANES)
            q8 = (z_ref[rsl, Q_OFF + pair * LANES:Q_OFF + (pair + 1) * LANES] * xi_ref[:, lanes]).astype(BF16)
            kw8 = z_ref[rsl, K_OFF + pair * LANES:K_OFF + (pair + 1) * LANES] * wk_ref[:, lanes]
            v8 = z_ref[rsl, V_OFF + pair * 2 * RET_V_DIM:V_OFF + (pair + 1) * 2 * RET_V_DIM].astype(BF16)
            outs = []
            for j in range(2):
                b = 2 * p + j
                s_pair = s0_ref[b, lanes, :]
                bd = jnp.concatenate([s_pair * top, s_pair * bot], axis=1).astype(BF16)
                outs.append(_dot(q8, bd))
                mine = (row8 >= j * ts) & (row8 < (j + 1) * ts)
                kwj = jnp.where(mine, kw8, 0.0).astype(BF16)
                upd = lax.dot_general(kwj, v8, (((0,), (0,)), ((), ())), preferred_element_type=F32)
                new_s = jnp.concatenate([upd[0:RET_QK_DIM, 0:RET_V_DIM],
                                         upd[RET_QK_DIM:, RET_V_DIM:]], axis=0)
                snew_ref[b, lanes, :] = s_pair * gs_ref[pair] + new_s
            inter_ref[rsl, pair * 2 * RET_V_DIM:(pair + 1) * 2 * RET_V_DIM] = jnp.where(
                row8 < ts, outs[0], outs[1])
        return carry

    lax.fori_loop(0, bb // 2, seq_pair, 0, unroll=8)

    keep = POOL_BUF - ts
    pnew_ref[0:keep] = pb_ref[ts:]
    for g, w in enumerate(POOL_WINDOWS):
        ublk_ref[g] = z_ref[:, U_OFF + g * POOL_GROUP_IN:U_OFF + (g + 1) * POOL_GROUP_IN]
        u_steps = [ublk_ref[g, pl.ds(t, bb, stride=ts), :] for t in range(ts)]
        for t in range(ts):
            pnew_ref[keep + t, :, g * POOL_GROUP_IN:(g + 1) * POOL_GROUP_IN] = u_steps[t]
        ds_ = []
        for t in range(ts):
            tot = None
            for j in range(w):
                idx = POOL_BUF + t - j
                if idx >= POOL_BUF:
                    term = u_steps[idx - POOL_BUF]
                else:
                    term = pb_ref[idx, :, g * POOL_GROUP_IN:(g + 1) * POOL_GROUP_IN]
                tot = term if tot is None else tot + term
            cnt = float(min(past + t + 1, w))
            ds_.append(tot / cnt - u_steps[t])
        y = _dot(jnp.concatenate(ds_, axis=0).astype(BF16),
                 wpool_ref[g * POOL_GROUP_IN:(g + 1) * POOL_GROUP_IN, :])
        y = y * pscale_ref[:, g * POOL_GROUP_OUT:(g + 1) * POOL_GROUP_OUT]
        for half in range(POOL_GROUP_OUT // LANES):
            blk = g * (POOL_GROUP_OUT // LANES) + half
            for t in range(ts):
                pblk_ref[blk, pl.ds(t, bb, stride=ts), :] = y[t * bb:(t + 1) * bb, half * LANES:(half + 1) * LANES]
    pool = jnp.concatenate([pblk_ref[blk] for blk in range(D_MODEL // LANES)], axis=1)

    o_all = intra + inter_ref[...]
    rets = []
    for h in range(N_RET_HEADS):
        hc = slice(h * RET_V_DIM, (h + 1) * RET_V_DIM)
        rets.append(_swish_gate_norm(o_all[:, hc], z_ref[:, G_OFF + h * RET_V_DIM:G_OFF + (h + 1) * RET_V_DIM]))
    ret = jnp.concatenate(rets, axis=1)
    o_ref[...] = _merge_out(x, z_ref, ret, pool, wout_ref)


def _sample_tables(ts, past, rows):
    lg = _log_gamma()
    i = np.arange(ts, dtype=np.float64)
    lag = np.exp(lg[:, None] * i[None, :])
    lag = np.repeat(lag.T, RET_V_DIM, axis=1)
    xi = np.exp(lg[:, None] * (i[None, :] + 1.0))
    wk = np.exp(lg[:, None] * (ts - 1.0 - i[None, :]))
    expand = lambda a: np.tile(np.repeat(a.T, RET_QK_DIM, axis=1), (2, 1)).astype(np.float32)
    cos, sin = _rope_tables(past + np.arange(ts, dtype=np.float64))
    reps = rows // ts
    head_of_k = np.arange(QK_W) // RET_QK_DIM
    head_of_v = np.arange(V_W) // RET_V_DIM
    ind = (head_of_k[:, None] == head_of_v[None, :]).astype(BF16)
    return (np.tile(cos, (reps, 1)), np.tile(sin, (reps, 1)), expand(xi), expand(wk), lag.astype(np.float32),
            _state_decay(ts), ind)


def _mixer_sample(x2d, s0, pbuf, g, w_in, w_pool, pool_scale, w_out, tabs, layer, prev_states, *, bb, ts, past):
    n = x2d.shape[0]
    depth = s0.shape[0]
    nb = n // ts
    rows = bb * ts
    assert nb % bb == 0 and bb % 2 == 0 and (2 * ts) % 8 == 0 and ts <= POOL_BUF
    cos, sin, xi, wk, lag, gs, ind = tabs
    row_spec = pl.BlockSpec((rows, D_MODEL), lambda i: (i, 0))
    state_spec = pl.BlockSpec((None, bb, N_PAIRS * LANES, RET_V_DIM), lambda i: (layer, i, 0, 0))
    pool_spec = pl.BlockSpec((None, POOL_BUF, bb, POOL_IN), lambda i: (layer, 0, i, 0))
    in_specs = [
        row_spec, _const_spec(g.shape), _resident_spec(w_in.shape),
        _resident_spec(w_pool.shape), _const_spec(pool_scale.shape),
        _resident_spec(w_out.shape), _const_spec(cos.shape), _const_spec(sin.shape),
        _const_spec(xi.shape), _const_spec(wk.shape), _const_spec(lag.shape), _const_spec(gs.shape),
        _const_spec(ind.shape), state_spec, pool_spec,
    ]
    args = [x2d, g, w_in, w_pool, pool_scale, w_out, cos, sin, xi, wk, lag, gs, ind, s0, pbuf]
    extra, extra_specs, aliases = _state_alias(prev_states, len(args), 1)
    out_specs = [row_spec, state_spec, pool_spec]
    out_shape = [
        jax.ShapeDtypeStruct((n, D_MODEL), F32),
        jax.ShapeDtypeStruct((depth, nb, N_PAIRS * LANES, RET_V_DIM), F32),
        jax.ShapeDtypeStruct((depth, POOL_BUF, nb, POOL_IN), F32),
    ]
    scratch = [
        pltpu.VMEM((rows, D_IN), F32),
        pltpu.VMEM((rows, V_W), F32),
        pltpu.VMEM((N_POOL_GROUPS, rows, LANES), F32),
        pltpu.VMEM((D_MODEL // LANES, rows, LANES), F32),
    ]
    y, snew, pnew = pl.pallas_call(
        functools.partial(_mixer_sample_body, bb=bb, ts=ts, past=past, n_alias=len(extra), layer=layer),
        grid=(nb // bb,),
        in_specs=in_specs + extra_specs,
        out_specs=out_specs,
        out_shape=out_shape,
        scratch_shapes=scratch,
        input_output_aliases=aliases,
        compiler_params=pltpu.CompilerParams(
            dimension_semantics=("arbitrary",), vmem_limit_bytes=VMEM_LIMIT),
        name="mixer_sample",
    )(*args, *extra)
    return y, (snew, pnew)


def kernel(x_prompt, x_sample, state_ret, state_pool, norm_ffn1, w_ffn1_in, w_ffn1_out, norm_mix,
           w_in, w_pool, pool_scale, w_out, norm_ffn2, w_ffn2_in, w_ffn2_out, norm_final):
    depth = w_in.shape[0]
    bp, tp, _ = x_prompt.shape
    bs, ts, _ = x_sample.shape
    wp = w_pool.reshape(depth, N_POOL_GROUPS * POOL_GROUP_IN, POOL_GROUP_OUT)

    n1, nm, n2, ps = norm_ffn1, norm_mix, norm_ffn2, pool_scale
    gf = norm_final.reshape(1, D_MODEL)
    last = depth - 1

    tm = 512
    tm_ffn = 1024
    tabs_p = _rope_tables(np.arange(tp, dtype=np.float64)) + _decay_tables(RET_CHUNK)

    bb = 32
    tabs_s = _sample_tables(ts, PAST_LEN, bb * ts)
    s0_all = state_ret.reshape(depth, bs, N_PAIRS * LANES, RET_V_DIM)
    pb_all = jnp.swapaxes(state_pool, 1, 2)

    x = x_prompt.reshape(bp * tp, D_MODEL)
    xs = x_sample.reshape(bs * ts, D_MODEL)
    states_p = states_s = None
    for l in range(depth):
        x, xs = _ffn(x, xs, n1, w_ffn1_in, w_ffn1_out, l, tm=tm_ffn)
        x, states_p, (wi16, wp16, wo16) = _mixer_prompt(x.reshape(bp, tp, D_MODEL), nm, w_in, wp, ps, w_out,
                                                        tabs_p, l, states_p, tm=tm)
        xs, states_s = _mixer_sample(xs, s0_all, pb_all, nm, wi16, wp16, ps, wo16, tabs_s, l, states_s,
                                     bb=bb, ts=ts, past=PAST_LEN)
        x, xs = _ffn(x.reshape(bp * tp, D_MODEL), xs, n2, w_ffn2_in, w_ffn2_out, l,
                     gf if l == last else None, tm=tm_ffn)
    y_prompt = x.reshape(bp, tp, D_MODEL)
    y_sample = xs.reshape(bs, ts, D_MODEL)
    state_ret_prompt = states_p[0].reshape(depth, bp, N_RET_HEADS, RET_QK_DIM, RET_V_DIM)
    state_pool_prompt = states_p[1]
    state_ret_sample = states_s[0].reshape(depth, bs, N_RET_HEADS, RET_QK_DIM, RET_V_DIM)
    state_pool_sample = jnp.swapaxes(states_s[1], 1, 2)
    return (y_prompt, y_sample, state_ret_prompt, state_ret_sample, state_pool_prompt, state_pool_sample)
```
